```python
import jax, jax.numpy as jnp
from jax import lax
import numpy as np

D_MODEL = 2048
BATCH = 2
SEQ = 4096
DEPTH = 2
DEC_BATCH = 128
DEC_SEQ = 8
PAST_LEN = 2048
PAGE_SIZE = 128

N_A = DEPTH // 2
N_B = DEPTH - N_A
HEAD_DIM = 128
MEM_LEN = 256
MEM_HEADS = 4
MEM_WIDTH = MEM_HEADS * HEAD_DIM
RWKV_WIDTH = D_MODEL - MEM_WIDTH
RWKV_HEAD_DIM = 64
RWKV_HEADS = RWKV_WIDTH // RWKV_HEAD_DIM
DECAY_LORA = 96
AAA_LORA = 96
GATE_LORA = 256
GN_EPS = 64e-5
NSA_WIDTH = D_MODEL - MEM_WIDTH
NSA_HEADS = NSA_WIDTH // HEAD_DIM
NSA_KV = 2
NSA_GROUP = NSA_HEADS // NSA_KV
CMP_BLOCK = 32
CMP_STRIDE = 16
SLC_BLOCK = 64
N_SELECT = 16
WINDOW = 512
Q_BLOCK = 128
D_FF = 4 * D_MODEL
ROPE_THETA = 10000.0
NORM_EPS = 1e-6
NEG_INF = -1e30
FORCE_SCORE = 1e9

kernel_name = "yoco_rwkv7_nsa_memory_decoder_step"

f32 = jnp.float32


def rmsnorm(x, g):
    xf = x.astype(f32)
    y = xf * lax.rsqrt(jnp.mean(xf * xf, axis=-1, keepdims=True) + NORM_EPS)
    return (y * g.astype(f32)).astype(x.dtype)


def rope(x, pos):
    half = HEAD_DIM // 2
    inv = jnp.power(ROPE_THETA, -jnp.arange(half, dtype=f32) / half)
    ang = pos.astype(f32)[:, None] * inv[None, :]
    cos = jnp.cos(ang)[:, None, :]
    sin = jnp.sin(ang)[:, None, :]
    xf = x.astype(f32)
    x1, x2 = xf[..., :half], xf[..., half:]
    return jnp.concatenate([x1 * cos - x2 * sin, x2 * cos + x1 * sin], axis=-1).astype(x.dtype)


def mem_attention(q, mk, mv):
    s = jnp.einsum('bthd,bmhd->bhtm', q, mk).astype(f32) * (HEAD_DIM ** -0.5)
    p = jax.nn.softmax(s, axis=-1).astype(mv.dtype)
    o = jnp.einsum('bhtm,bmhd->bthd', p, mv)
    return o.reshape(q.shape[0], q.shape[1], MEM_WIDTH)


def mem_kv(mem, g, wk, wv):
    m = rmsnorm(mem, g)
    B, M = mem.shape[:2]
    return ((m @ wk).reshape(B, M, MEM_HEADS, HEAD_DIM), (m @ wv).reshape(B, M, MEM_HEADS, HEAD_DIM))


def rwkv_mem_mixer(xn, x_prev, S0, mk, mv, W, a):
    B, T, _ = xn.shape
    xs = jnp.concatenate([x_prev[:, None, :].astype(xn.dtype), xn], axis=1)
    proj = xs @ W['w_in_a'][a]
    cur, prev = proj[:, 1:], proj[:, :-1]
    c_rkv = cur[..., :3 * RWKV_WIDTH]
    rkv = c_rkv + (prev[..., :3 * RWKV_WIDTH] - c_rkv) * W['mu_rkv'][a]
    r, k, v = jnp.split(rkv, 3, axis=-1)
    mq = cur[..., 3 * RWKV_WIDTH:].reshape(B, T, MEM_HEADS, HEAD_DIM)
    xx = xs[:, :-1] - xn
    mu = W['mu_wag'][a]
    xw = xn + xx * mu[0]
    xa = xn + xx * mu[1]
    xg = xn + xx * mu[2]
    w_raw = (W['w0'][a] + jnp.tanh(xw @ W['w_decay1'][a]) @ W['w_decay2'][a]).astype(f32)
    decay = jnp.exp(-jnp.exp(-jax.nn.softplus(-w_raw) - 0.5))
    a_rate = jax.nn.sigmoid((W['a0'][a] + (xa @ W['w_aaa1'][a]) @ W['w_aaa2'][a]).astype(f32))
    g = jax.nn.sigmoid(xg @ W['w_gate1'][a]) @ W['w_gate2'][a]
    shp = (B, T, RWKV_HEADS, RWKV_HEAD_DIM)
    r_h = r.astype(f32).reshape(shp)
    k_h = k.astype(f32).reshape(shp)
    v_h = v.astype(f32).reshape(shp)
    a_h = a_rate.reshape(shp)
    w_h = decay.reshape(shp)
    kk = k_h * W['k_k'][a].astype(f32).reshape(RWKV_HEADS, RWKV_HEAD_DIM)
    kk = kk * lax.rsqrt(jnp.sum(kk * kk, axis=-1, keepdims=True) + 1e-12)
    k_h = k_h * (1.0 + (a_h - 1.0) * W['k_a'][a].astype(f32).reshape(RWKV_HEADS, RWKV_HEAD_DIM))

    def step(S, inp):
        r_t, w_t, k_t, v_t, kk_t, a_t = inp
        Skk = jnp.einsum('bhvk,bhk->bhv', S, kk_t)
        S = S * w_t[:, :, None, :] - Skk[..., None] * (kk_t * a_t)[:, :, None, :] + v_t[..., None] * k_t[:, :, None, :]
        return S, jnp.einsum('bhvk,bhk->bhv', S, r_t)

    seq = (jnp.moveaxis(r_h, 1, 0), jnp.moveaxis(w_h, 1, 0), jnp.moveaxis(k_h, 1, 0),
           jnp.moveaxis(v_h, 1, 0), jnp.moveaxis(kk, 1, 0), jnp.moveaxis(a_h, 1, 0))
    S_T, ys = lax.scan(step, S0.astype(f32), seq)
    y = jnp.moveaxis(ys, 0, 1)
    m = jnp.mean(y, axis=-1, keepdims=True)
    var = jnp.mean(jnp.square(y - m), axis=-1, keepdims=True)
    y = ((y - m) * lax.rsqrt(var + GN_EPS)).reshape(B, T, RWKV_WIDTH)
    y = y * W['lnx_w'][a].astype(f32) + W['lnx_b'][a].astype(f32)
    bonus = jnp.sum(r_h * k_h * W['r_k'][a].astype(f32), axis=-1, keepdims=True) * v_h
    o_rwkv = ((y + bonus.reshape(B, T, RWKV_WIDTH)) * g.astype(f32)).astype(xn.dtype)
    o_mem = mem_attention(mq, mk, mv)
    out = jnp.concatenate([o_rwkv, o_mem], axis=-1) @ W['w_out_a'][a]
    return out, S_T.astype(S0.dtype), xn[:, -1]


def shared_kv(h, pos, g_kv, w_kv):
    B, T, _ = h.shape
    kv = (rmsnorm(h, g_kv) @ w_kv).reshape(B, T, 6, NSA_KV, HEAD_DIM)
    return (kv[:, :, 0], kv[:, :, 1], rope(kv[:, :, 2], pos), kv[:, :, 3], rope(kv[:, :, 4], pos), kv[:, :, 5])


def compress(rows, pos_emb, w1, w2):
    B, L = rows.shape[:2]
    xin = rows.transpose(0, 2, 1, 3).reshape(B * NSA_KV, L, HEAD_DIM)
    hid = lax.conv_general_dilated(xin, w1, window_strides=(CMP_STRIDE,), padding='VALID',
                                   dimension_numbers=('NWC', 'WIO', 'NWC'))
    hid = hid + jnp.einsum('sd,sde->e', pos_emb, w1)
    out = jax.nn.gelu(hid) @ w2
    n = out.shape[1]
    return out.reshape(B, NSA_KV, n, HEAD_DIM).transpose(0, 2, 1, 3)


def to_blocks(t, n_slc):
    B = t.shape[0]
    return t.reshape(B, n_slc, SLC_BLOCK, NSA_KV, HEAD_DIM).transpose(0, 3, 1, 2, 4)


def nsa_attend(q, q_rot, pos, ck, cv, k_blk, v_blk, wk, wv, wpos):
    B, Tq = q.shape[:2]
    n_cmp = ck.shape[1]
    n_slc = k_blk.shape[2]
    n_sel = min(N_SELECT, n_slc)
    scale = HEAD_DIM ** -0.5
    qg = q.reshape(B, Tq, NSA_KV, NSA_GROUP, HEAD_DIM)
    qr = q_rot.reshape(B, Tq, NSA_KV, NSA_GROUP, HEAD_DIM)
    cmp_end = jnp.arange(n_cmp) * CMP_STRIDE + (CMP_BLOCK - 1)
    vis_c = cmp_end[None, :] <= pos[:, None]
    s = jnp.einsum('btkgd,bnkd->bkgtn', qg, ck).astype(f32) * scale
    p_c = jax.nn.softmax(jnp.where(vis_c, s, NEG_INF), axis=-1) * vis_c
    o_cmp = jnp.einsum('bkgtn,bnkd->btkgd', p_c.astype(cv.dtype), cv)
    cmp_start = cmp_end - (CMP_BLOCK - 1)
    slc_start = jnp.arange(n_slc) * SLC_BLOCK
    overlap = ((cmp_start[:, None] < slc_start[None, :] + SLC_BLOCK)
               & (cmp_end[:, None] >= slc_start[None, :])).astype(f32)
    imp = jnp.einsum('bkgtn,nj->bktj', p_c, overlap)
    cur_blk = pos // SLC_BLOCK
    j = jnp.arange(n_slc)
    causal_b = slc_start[None, :] <= pos[:, None]
    forced = (j[None, :] == 0) | (j[None, :] == cur_blk[:, None]) | (j[None, :] == cur_blk[:, None] - 1)
    score = jnp.where(causal_b, jnp.where(forced, FORCE_SCORE, imp), -FORCE_SCORE)
    _, sel = lax.top_k(score, n_sel)
    sel_flat = sel.reshape(B, NSA_KV, Tq * n_sel)
    take = jax.vmap(jax.vmap(lambda blk, idx: blk[idx]))
    gk = take(k_blk, sel_flat).reshape(B, NSA_KV, Tq, n_sel, SLC_BLOCK, HEAD_DIM)
    gv = take(v_blk, sel_flat).reshape(B, NSA_KV, Tq, n_sel, SLC_BLOCK, HEAD_DIM)
    key_pos = sel[..., None] * SLC_BLOCK + jnp.arange(SLC_BLOCK)
    vis_s = key_pos <= pos[:, None, None]
    s = jnp.einsum('btkgd,bktnsd->bkgtns', qr, gk).astype(f32) * scale
    s = jnp.where(vis_s[:, :, None], s, NEG_INF).reshape(B, NSA_KV, NSA_GROUP, Tq, n_sel * SLC_BLOCK)
    p = jax.nn.softmax(s, axis=-1).reshape(B, NSA_KV, NSA_GROUP, Tq, n_sel, SLC_BLOCK)
    o_slc = jnp.einsum('bkgtns,bktnsd->btkgd', p.astype(gv.dtype), gv)
    vis_w = (wpos[None, :] <= pos[:, None]) & (pos[:, None] - wpos[None, :] < WINDOW)
    s = jnp.einsum('btkgd,blkd->bkgtl', qr, wk).astype(f32) * scale
    p = jax.nn.softmax(jnp.where(vis_w, s, NEG_INF), axis=-1)
    o_win = jnp.einsum('bkgtl,blkd->btkgd', p.astype(wv.dtype), wv)
    shp = (B, Tq, NSA_HEADS, HEAD_DIM)
    return o_cmp.reshape(shp), o_slc.reshape(shp), o_win.reshape(shp)


def nsa_prompt(q, q_rot, ck_c, cv_c, sk, sv, wk, wv):
    B, T = q.shape[:2]
    n_slc = T // SLC_BLOCK
    k_blk, v_blk = to_blocks(sk, n_slc), to_blocks(sv, n_slc)
    padw = ((0, 0), (WINDOW, 0), (0, 0), (0, 0))
    wk_pad, wv_pad = jnp.pad(wk, padw), jnp.pad(wv, padw)

    def block(i):
        start = i * Q_BLOCK
        pos = start + jnp.arange(Q_BLOCK)
        wpos = start - WINDOW + jnp.arange(WINDOW + Q_BLOCK)
        qb = lax.dynamic_slice_in_dim(q, start, Q_BLOCK, axis=1)
        qrb = lax.dynamic_slice_in_dim(q_rot, start, Q_BLOCK, axis=1)
        wkb = lax.dynamic_slice_in_dim(wk_pad, start, WINDOW + Q_BLOCK, axis=1)
        wvb = lax.dynamic_slice_in_dim(wv_pad, start, WINDOW + Q_BLOCK, axis=1)
        return nsa_attend(qb, qrb, pos, ck_c, cv_c, k_blk, v_blk, wkb, wvb, wpos)

    oc, os_, ow = lax.map(block, jnp.arange(T // Q_BLOCK))
    merge = lambda o: jnp.moveaxis(o, 0, 1).reshape(B, T, NSA_HEADS, HEAD_DIM)
    return merge(oc), merge(os_), merge(ow)


def nsa_sample(q, q_rot, pos, ck_c, cv_c, sk, sv, wk, wv, wpos):
    B, L = sk.shape[:2]
    n_slc = -(-L // SLC_BLOCK)
    padk = ((0, 0), (0, n_slc * SLC_BLOCK - L), (0, 0), (0, 0))
    k_blk, v_blk = to_blocks(jnp.pad(sk, padk), n_slc), to_blocks(jnp.pad(sv, padk), n_slc)

    def one(i):
        qi = lax.dynamic_slice_in_dim(q, i, 1, axis=1)
        qri = lax.dynamic_slice_in_dim(q_rot, i, 1, axis=1)
        pi = lax.dynamic_slice_in_dim(pos, i, 1)
        return nsa_attend(qi, qri, pi, ck_c, cv_c, k_blk, v_blk, wk, wv, wpos)

    oc, os_, ow = lax.map(one, jnp.arange(q.shape[1]))
    merge = lambda o: jnp.moveaxis(o[:, :, 0], 0, 1)
    return merge(oc), merge(os_), merge(ow)


def nsa_mem_mixer(xn, pos, mk, mv, attend, w_in, w_out):
    B, T, _ = xn.shape
    proj = xn @ w_in
    q = proj[..., :NSA_WIDTH].reshape(B, T, NSA_HEADS, HEAD_DIM)
    mq = proj[..., NSA_WIDTH:NSA_WIDTH + MEM_WIDTH].reshape(B, T, MEM_HEADS, HEAD_DIM)
    gate = jax.nn.sigmoid(proj[..., NSA_WIDTH + MEM_WIDTH:].astype(f32)).reshape(B, T, 3, NSA_HEADS, 1)
    oc, os_, ow = attend(q, rope(q, pos))
    o = gate[:, :, 0] * oc + gate[:, :, 1] * os_ + gate[:, :, 2] * ow
    o = o.astype(xn.dtype).reshape(B, T, NSA_WIDTH)
    o_mem = mem_attention(mq, mk, mv)
    return jnp.concatenate([o, o_mem], axis=-1) @ w_out


def trunk(x, pos, mem_k, mem_v, shift0, wkv0, kv_source, W):
    h = x
    shifts, states = [], []
    attend, kv_rows = None, None
    for l in range(DEPTH):
        hn = rmsnorm(h, W['g_mix_pre'][l])
        if l < N_A:
            out, S, xl = rwkv_mem_mixer(hn, shift0[l], wkv0[l], mem_k[l], mem_v[l], W, l)
            shifts.append(xl)
            states.append(S)
        else:
            if l == N_A:
                attend, kv_rows = kv_source(h)
            b = l - N_A
            out = nsa_mem_mixer(hn, pos, mem_k[l], mem_v[l], attend, W['w_in_b'][b], W['w_out_b'][b])
        h = h + rmsnorm(out, W['g_mix_post'][l])
        f = rmsnorm(h, W['g_ffn_pre'][l])
        f = jnp.square(jax.nn.relu(f @ W['w_ff1'][l])) @ W['w_ff2'][l]
        h = h + rmsnorm(f, W['g_ffn_post'][l])
    return h, jnp.stack(shifts), jnp.stack(states), kv_rows


def gather_pages(pool, page_table):
    rows = pool[page_table]
    return rows.reshape(page_table.shape[0], -1, pool.shape[2], pool.shape[3])


def setup_inputs(seed: int = 0) -> dict:
    key = jax.random.key(seed)
    ks = iter(jax.random.split(key, 64))

    def nrm(shape, scale):
        return jax.random.normal(next(ks), shape, jnp.float32) * scale

    def gain(shape):
        return 1.0 + nrm(shape, 0.05)

    def unif(shape, lo, hi):
        return jax.random.uniform(next(ks), shape, jnp.float32, lo, hi)

    n_pages = PAST_LEN // PAGE_SIZE
    n_used = DEC_BATCH * n_pages
    n_pool = n_used + max(1, n_used // 4)
    page_table = jax.random.permutation(next(ks), n_pool)[:n_used].reshape(DEC_BATCH, n_pages).astype(jnp.int32)
    win_buf = min(WINDOW, PAST_LEN)
    pshape = (n_pool, PAGE_SIZE, NSA_KV, HEAD_DIM)
    ds = D_MODEL ** -0.5
    return {
        'x_prompt': nrm((BATCH, SEQ, D_MODEL), 1.0),
        'x_sample': nrm((DEC_BATCH, DEC_SEQ, D_MODEL), 1.0),
        'mem_prompt': nrm((BATCH, MEM_LEN, D_MODEL), 1.0),
        'cache_mem_k': nrm((DEPTH, DEC_BATCH, MEM_LEN, MEM_HEADS, HEAD_DIM), 1.0),
        'cache_mem_v': nrm((DEPTH, DEC_BATCH, MEM_LEN, MEM_HEADS, HEAD_DIM), 1.0),
        'state_wkv': nrm((N_A, DEC_BATCH, RWKV_HEADS, RWKV_HEAD_DIM, RWKV_HEAD_DIM), 0.3),
        'state_shift': nrm((N_A, DEC_BATCH, D_MODEL), 1.0),
        'cache_cmp_k': nrm(pshape, 1.0),
        'cache_cmp_v': nrm(pshape, 1.0),
        'cache_slc_k': nrm(pshape, 1.0),
        'cache_slc_v': nrm(pshape, 1.0),
        'cache_win_k': nrm((DEC_BATCH, win_buf, NSA_KV, HEAD_DIM), 1.0),
        'cache_win_v': nrm((DEC_BATCH, win_buf, NSA_KV, HEAD_DIM), 1.0),
        'page_table': page_table,
        'g_mix_pre': gain((DEPTH, D_MODEL)),
        'g_mix_post': gain((DEPTH, D_MODEL)),
        'g_ffn_pre': gain((DEPTH, D_MODEL)),
        'g_ffn_post': gain((DEPTH, D_MODEL)),
        'g_mem': gain((DEPTH, D_MODEL)),
        'w_mem_k': nrm((DEPTH, D_MODEL, MEM_WIDTH), ds),
        'w_mem_v': nrm((DEPTH, D_MODEL, MEM_WIDTH), ds),
        'w_in_a': nrm((N_A, D_MODEL, 3 * RWKV_WIDTH + MEM_WIDTH), ds),
        'mu_rkv': unif((N_A, 3 * RWKV_WIDTH), 0.0, 1.0),
        'mu_wag': unif((N_A, 3, D_MODEL), 0.0, 1.0),
        'w0': unif((N_A, RWKV_WIDTH), -6.0, 1.0),
        'w_decay1': nrm((N_A, D_MODEL, DECAY_LORA), ds),
        'w_decay2': nrm((N_A, DECAY_LORA, RWKV_WIDTH), 0.1 * DECAY_LORA ** -0.5),
        'a0': nrm((N_A, RWKV_WIDTH), 0.1),
        'w_aaa1': nrm((N_A, D_MODEL, AAA_LORA), ds),
        'w_aaa2': nrm((N_A, AAA_LORA, RWKV_WIDTH), 0.1 * AAA_LORA ** -0.5),
        'w_gate1': nrm((N_A, D_MODEL, GATE_LORA), ds),
        'w_gate2': nrm((N_A, GATE_LORA, RWKV_WIDTH), GATE_LORA ** -0.5),
        'k_k': 0.85 + nrm((N_A, RWKV_WIDTH), 0.05),
        'k_a': gain((N_A, RWKV_WIDTH)),
        'r_k': nrm((N_A, RWKV_HEADS, RWKV_HEAD_DIM), 0.1),
        'lnx_w': gain((N_A, RWKV_WIDTH)),
        'lnx_b': nrm((N_A, RWKV_WIDTH), 0.01),
        'w_out_a': nrm((N_A, RWKV_WIDTH + MEM_WIDTH, D_MODEL), ds),
        'g_kv': gain((D_MODEL,)),
        'w_kv': nrm((D_MODEL, 6 * NSA_KV * HEAD_DIM), ds),
        'cmp_pos': nrm((2, CMP_BLOCK, HEAD_DIM), 0.1),
        'cmp_w1': nrm((2, CMP_BLOCK, HEAD_DIM, HEAD_DIM), (CMP_BLOCK * HEAD_DIM) ** -0.5),
        'cmp_w2': nrm((2, HEAD_DIM, HEAD_DIM), HEAD_DIM ** -0.5),
        'w_in_b': nrm((N_B, D_MODEL, NSA_WIDTH + MEM_WIDTH + 3 * NSA_HEADS), ds),
        'w_out_b': nrm((N_B, NSA_WIDTH + MEM_WIDTH, D_MODEL), ds),
        'w_ff1': nrm((DEPTH, D_MODEL, D_FF), ds),
        'w_ff2': nrm((DEPTH, D_FF, D_MODEL), D_FF ** -0.5),
    }


def reference(x_prompt, x_sample, mem_prompt, cache_mem_k, cache_mem_v, state_wkv, state_shift,
              cache_cmp_k, cache_cmp_v, cache_slc_k, cache_slc_v, cache_win_k, cache_win_v, page_table,
              g_mix_pre, g_mix_post, g_ffn_pre, g_ffn_post, g_mem, w_mem_k, w_mem_v,
              w_in_a, mu_rkv, mu_wag, w0, w_decay1, w_decay2, a0, w_aaa1, w_aaa2, w_gate1, w_gate2,
              k_k, k_a, r_k, lnx_w, lnx_b, w_out_a,
              g_kv, w_kv, cmp_pos, cmp_w1, cmp_w2, w_in_b, w_out_b, w_ff1, w_ff2):
    W = dict(g_mix_pre=g_mix_pre, g_mix_post=g_mix_post, g_ffn_pre=g_ffn_pre, g_ffn_post=g_ffn_post,
             w_in_a=w_in_a, mu_rkv=mu_rkv, mu_wag=mu_wag, w0=w0, w_decay1=w_decay1, w_decay2=w_decay2,
             a0=a0, w_aaa1=w_aaa1, w_aaa2=w_aaa2, w_gate1=w_gate1, w_gate2=w_gate2, k_k=k_k, k_a=k_a,
             r_k=r_k, lnx_w=lnx_w, lnx_b=lnx_b, w_out_a=w_out_a, w_in_b=w_in_b, w_out_b=w_out_b,
             w_ff1=w_ff1, w_ff2=w_ff2)

    B_p, T_p = x_prompt.shape[:2]
    pos_p = jnp.arange(T_p, dtype=jnp.int32)
    mkv = [mem_kv(mem_prompt, g_mem[l], w_mem_k[l], w_mem_v[l]) for l in range(DEPTH)]
    mem_k_p = jnp.stack([m[0] for m in mkv])
    mem_v_p = jnp.stack([m[1] for m in mkv])

    def prompt_kv(h):
        ck, cv, sk, sv, wk, wv = shared_kv(h, pos_p, g_kv, w_kv)
        ck_c = compress(ck, cmp_pos[0], cmp_w1[0], cmp_w2[0])
        cv_c = compress(cv, cmp_pos[1], cmp_w1[1], cmp_w2[1])
        n_keep = min(WINDOW, T_p)

        def attend(q, q_rot):
            return nsa_prompt(q, q_rot, ck_c, cv_c, sk, sv, wk, wv)
        return attend, (ck, cv, sk, sv, wk[:, T_p - n_keep:], wv[:, T_p - n_keep:])

    shift0_p = jnp.zeros((N_A, B_p, D_MODEL), x_prompt.dtype)
    wkv0_p = jnp.zeros((N_A, B_p, RWKV_HEADS, RWKV_HEAD_DIM, RWKV_HEAD_DIM), x_prompt.dtype)
    y_p, shift_p, wkv_p, rows_p = trunk(x_prompt, pos_p, mem_k_p, mem_v_p, shift0_p, wkv0_p, prompt_kv, W)
    cmp_k_p, cmp_v_p, slc_k_p, slc_v_p, win_k_p, win_v_p = rows_p

    T_s = x_sample.shape[1]
    pos_s = PAST_LEN + jnp.arange(T_s, dtype=jnp.int32)

    def sample_kv(h):
        ck, cv, sk, sv, wk, wv = shared_kv(h, pos_s, g_kv, w_kv)
        ck_all = jnp.concatenate([gather_pages(cache_cmp_k, page_table), ck], axis=1)
        cv_all = jnp.concatenate([gather_pages(cache_cmp_v, page_table), cv], axis=1)
        sk_all = jnp.concatenate([gather_pages(cache_slc_k, page_table), sk], axis=1)
        sv_all = jnp.concatenate([gather_pages(cache_slc_v, page_table), sv], axis=1)
        ck_c = compress(ck_all, cmp_pos[0], cmp_w1[0], cmp_w2[0])
        cv_c = compress(cv_all, cmp_pos[1], cmp_w1[1], cmp_w2[1])
        n_buf = cache_win_k.shape[1]
        wk_all = jnp.concatenate([cache_win_k, wk], axis=1)
        wv_all = jnp.concatenate([cache_win_v, wv], axis=1)
        wpos = PAST_LEN - n_buf + jnp.arange(n_buf + T_s, dtype=jnp.int32)

        def attend(q, q_rot):
            return nsa_sample(q, q_rot, pos_s, ck_c, cv_c, sk_all, sv_all, wk_all, wv_all, wpos)
        return attend, (ck, cv, sk, sv, wk_all[:, T_s:], wv_all[:, T_s:])

    y_s, shift_s, wkv_s, rows_s = trunk(x_sample, pos_s, cache_mem_k, cache_mem_v, state_shift, state_wkv, sample_kv, W)
    cmp_k_s, cmp_v_s, slc_k_s, slc_v_s, win_k_s, win_v_s = rows_s

    return (y_p, y_s, mem_k_p, mem_v_p, wkv_p, shift_p, cmp_k_p, cmp_v_p, slc_k_p, slc_v_p, win_k_p, win_v_p,
            wkv_s, shift_s, cmp_k_s, cmp_v_s, slc_k_s, slc_v_s, win_k_s, win_v_s)
```

```python
import functools

import jax
import jax.numpy as jnp
from jax import lax
from jax.experimental import pallas as pl
from jax.experimental.pallas import tpu as pltpu

f32 = jnp.float32
bf16 = jnp.bfloat16

LANES = 128
VMEM_LIMIT_BYTES = 56 * 1024 * 1024

HEAD_DIM = 128
MEM_HEADS = 4
MEM_WIDTH = MEM_HEADS * HEAD_DIM
RWKV_HEAD_DIM = 64
GN_EPS = 64e-5
NSA_KV = 2
NSA_GROUP = 6
NSA_HEADS = NSA_KV * NSA_GROUP
NSA_WIDTH = NSA_HEADS * HEAD_DIM
KV_COLS = NSA_KV * HEAD_DIM
CMP_BLOCK = 32
CMP_STRIDE = 16
SLC_BLOCK = 64
SLC_SHIFT = 6
N_SELECT = 16
WINDOW = 512
Q_BLOCK = 128
ROPE_THETA = 10000.0
NORM_EPS = 1e-6
NEG_INF = -1e30
FORCE_SCORE = 1e9
PAGE_SIZE = 128

ROW_TILE = 512
SLC_CHUNK = 512
SEL_LANES = 128


def _params(*sem):
    return pltpu.CompilerParams(dimension_semantics=sem, vmem_limit_bytes=VMEM_LIMIT_BYTES)


def _rms(x, g):
    return x * lax.rsqrt(jnp.mean(x * x, axis=-1, keepdims=True) + NORM_EPS) * g


def _sigmoid(x):
    return 1.0 / (1.0 + jnp.exp(-x))


def _rmsnorm_kernel(x_ref, g_ref, o_ref):
    o_ref[...] = _rms(x_ref[...], g_ref[...])


def rmsnorm(x, g):
    m, d = x.shape
    tm = min(ROW_TILE, m)
    return pl.pallas_call(
        _rmsnorm_kernel,
        grid=(m // tm,),
        in_specs=[pl.BlockSpec((tm, d), lambda i: (i, 0)), pl.BlockSpec((1, d), lambda i: (0, 0))],
        out_specs=pl.BlockSpec((tm, d), lambda i: (i, 0)),
        out_shape=jax.ShapeDtypeStruct((m, d), f32),
        compiler_params=_params("parallel"),
        name="rmsnorm",
    )(x, g.reshape(1, d))


def _mm_kernel(x_ref, g_ref, w_ref, o_ref, xn_ref, *, norm):
    @pl.when(pl.program_id(1) == 0)
    def _():
        x = x_ref[...]
        if norm:
            x = _rms(x, g_ref[...])
        xn_ref[...] = x.astype(bf16)

    o_ref[...] = jnp.dot(xn_ref[...], w_ref[...], preferred_element_type=f32)


def matmul(x, w, g=None, tn=512):
    m, k = x.shape
    n = w.shape[1]
    tm = min(ROW_TILE, m)
    tn = min(tn, n)
    assert m % tm == 0 and n % tn == 0, (m, n, tm, tn)
    gg = jnp.ones((1, k), f32) if g is None else g.reshape(1, k)
    return pl.pallas_call(
        functools.partial(_mm_kernel, norm=g is not None),
        grid=(m // tm, n // tn),
        in_specs=[pl.BlockSpec((tm, k), lambda i, j: (i, 0)),
                  pl.BlockSpec((1, k), lambda i, j: (0, 0)),
                  pl.BlockSpec((k, tn), lambda i, j: (0, j))],
        out_specs=pl.BlockSpec((tm, tn), lambda i, j: (i, j)),
        out_shape=jax.ShapeDtypeStruct((m, n), f32),
        scratch_shapes=[pltpu.VMEM((tm, k), bf16)],
        compiler_params=_params("parallel", "arbitrary"),
        name="matmul",
    )(x, gg, w.astype(bf16))


def _out_proj_kernel(o_ref, w_ref, h_ref, g_ref, y_ref):
    acc = jnp.dot(o_ref[...].astype(bf16), w_ref[...], preferred_element_type=f32)
    y_ref[...] = h_ref[...] + _rms(acc, g_ref[...])


def out_proj_residual(o, w, h, g):
    m, k = o.shape
    d = w.shape[1]
    tm = min(ROW_TILE, m)
    return pl.pallas_call(
        _out_proj_kernel,
        grid=(m // tm,),
        in_specs=[pl.BlockSpec((tm, k), lambda i: (i, 0)),
                  pl.BlockSpec((k, d), lambda i: (0, 0)),
                  pl.BlockSpec((tm, d), lambda i: (i, 0)),
                  pl.BlockSpec((1, d), lambda i: (0, 0))],
        out_specs=pl.BlockSpec((tm, d), lambda i: (i, 0)),
        out_shape=jax.ShapeDtypeStruct((m, d), f32),
        compiler_params=_params("parallel"),
        name="out_proj",
    )(o, w.astype(bf16), h, g.reshape(1, d))


def _ffn_kernel(h_ref, gpre_ref, w1_ref, w2_ref, gpost_ref, y_ref, xn_ref, acc_ref):
    j = pl.program_id(1)

    @pl.when(j == 0)
    def _():
        xn_ref[...] = _rms(h_ref[...], gpre_ref[...]).astype(bf16)
        acc_ref[...] = jnp.zeros_like(acc_ref)

    u = jnp.dot(xn_ref[...], w1_ref[...], preferred_element_type=f32)
    u = jnp.square(jnp.maximum(u, 0.0))
    acc_ref[...] += jnp.dot(u.astype(bf16), w2_ref[...], preferred_element_type=f32)

    @pl.when(j == pl.num_programs(1) - 1)
    def _():
        y_ref[...] = h_ref[...] + _rms(acc_ref[...], gpost_ref[...])


def ffn_residual(h, g_pre, w1, w2, g_post, tf=512):
    m, d = h.shape
    dff = w1.shape[1]
    tm = min(ROW_TILE, m)
    return pl.pallas_call(
        _ffn_kernel,
        grid=(m // tm, dff // tf),
        in_specs=[pl.BlockSpec((tm, d), lambda i, j: (i, 0)),
                  pl.BlockSpec((1, d), lambda i, j: (0, 0)),
                  pl.BlockSpec((d, tf), lambda i, j: (0, j)),
                  pl.BlockSpec((tf, d), lambda i, j: (j, 0)),
                  pl.BlockSpec((1, d), lambda i, j: (0, 0))],
        out_specs=pl.BlockSpec((tm, d), lambda i, j: (i, 0)),
        out_shape=jax.ShapeDtypeStruct((m, d), f32),
        scratch_shapes=[pltpu.VMEM((tm, d), bf16), pltpu.VMEM((tm, d), f32)],
        compiler_params=_params("parallel", "arbitrary"),
        name="ffn",
    )(h, g_pre.reshape(1, d), w1.astype(bf16), w2.astype(bf16), g_post.reshape(1, d))


def _rope_tile(x, c2, s2):
    return x * c2 + pltpu.roll(x, HEAD_DIM // 2, 1) * s2


def _rope_heads(q, c2, s2):
    return jnp.concatenate([_rope_tile(q[:, g * HEAD_DIM:(g + 1) * HEAD_DIM], c2, s2)
                            for g in range(NSA_GROUP)], axis=0)


def _kv_proj_kernel(h_ref, g_ref, w_ref, c2_ref, s2_ref, o_ref, xn_ref):
    j = pl.program_id(1)

    @pl.when(j == 0)
    def _():
        xn_ref[...] = _rms(h_ref[...], g_ref[...]).astype(bf16)

    acc = jnp.dot(xn_ref[...], w_ref[...], preferred_element_type=f32)
    rotated = (j == 2) | (j == 4)

    @pl.when(rotated)
    def _():
        c2, s2 = c2_ref[...], s2_ref[...]
        for kv in range(NSA_KV):
            sl = slice(kv * HEAD_DIM, (kv + 1) * HEAD_DIM)
            o_ref[:, sl] = _rope_tile(acc[:, sl], c2, s2)

    @pl.when(jnp.logical_not(rotated))
    def _():
        o_ref[...] = acc


def kv_proj(h, g, w, c2, s2):
    m, d = h.shape
    n = w.shape[1]
    tm = min(ROW_TILE, m)
    ntab = c2.shape[0] // tm
    return pl.pallas_call(
        _kv_proj_kernel,
        grid=(m // tm, n // KV_COLS),
        in_specs=[pl.BlockSpec((tm, d), lambda i, j: (i, 0)),
                  pl.BlockSpec((1, d), lambda i, j: (0, 0)),
                  pl.BlockSpec((d, KV_COLS), lambda i, j: (0, j)),
                  pl.BlockSpec((tm, HEAD_DIM), lambda i, j: (i % ntab, 0)),
                  pl.BlockSpec((tm, HEAD_DIM), lambda i, j: (i % ntab, 0))],
        out_specs=pl.BlockSpec((tm, KV_COLS), lambda i, j: (i, j)),
        out_shape=jax.ShapeDtypeStruct((m, n), f32),
        scratch_shapes=[pltpu.VMEM((tm, d), bf16)],
        compiler_params=_params("parallel", "arbitrary"),
        name="kv_proj",
    )(h, g.reshape(1, d), w.astype(bf16), c2, s2)


def _mem_attn_kernel(q_ref, k_ref, v_ref, o_ref):
    scale = HEAD_DIM ** -0.5
    for hd in range(MEM_HEADS):
        sl = slice(hd * HEAD_DIM, (hd + 1) * HEAD_DIM)
        q = q_ref[0, :, sl].astype(bf16)
        k = k_ref[0, :, sl].astype(bf16)
        s = lax.dot_general(q, k, (((1,), (1,)), ((), ())), preferred_element_type=f32) * scale
        s = s - jnp.max(s, axis=-1, keepdims=True)
        e = jnp.exp(s)
        p = e / jnp.sum(e, axis=-1, keepdims=True)
        o_ref[0, :, sl] = jnp.dot(p.astype(bf16), v_ref[0, :, sl].astype(bf16), preferred_element_type=f32)


def mem_attention(q, mk, mv):
    b, t, w = q.shape
    mlen = mk.shape[1]
    tq = min(ROW_TILE, t)
    return pl.pallas_call(
        _mem_attn_kernel,
        grid=(b, t // tq),
        in_specs=[pl.BlockSpec((1, tq, w), lambda i, j: (i, j, 0)),
                  pl.BlockSpec((1, mlen, w), lambda i, j: (i, 0, 0)),
                  pl.BlockSpec((1, mlen, w), lambda i, j: (i, 0, 0))],
        out_specs=pl.BlockSpec((1, tq, w), lambda i, j: (i, j, 0)),
        out_shape=jax.ShapeDtypeStruct((b, t, w), f32),
        compiler_params=_params("parallel", "arbitrary"),
        name="mem_attn",
    )(q, mk, mv)


def _wkv_kernel(r_ref, w_ref, k_ref, kk_ref, b_ref, vt_ref, s0_ref, yt_ref, st_ref, s_scr, *, nb, nh, tb):
    ti = pl.program_id(1)

    @pl.when(ti == 0)
    def _():
        s_scr[...] = s0_ref[...]

    lane = lax.broadcasted_iota(jnp.int32, (RWKV_HEAD_DIM, tb), 1)

    def step(t, carry):
        onehot = lane == t
        for ib in range(nb):
            for h in range(nh):
                s = s_scr[ib, h]
                kk = kk_ref[ib, h, pl.ds(t, 1), :]
                w = w_ref[ib, h, pl.ds(t, 1), :]
                bb = b_ref[ib, h, pl.ds(t, 1), :]
                k = k_ref[ib, h, pl.ds(t, 1), :]
                r = r_ref[ib, h, pl.ds(t, 1), :]
                vcol = jnp.sum(jnp.where(onehot, vt_ref[ib, h], 0.0), axis=1, keepdims=True)
                z = jnp.sum(s * kk, axis=1, keepdims=True)
                s = s * w - z * bb + vcol * k
                y = jnp.sum(s * r, axis=1, keepdims=True)
                s_scr[ib, h] = s
                yt_ref[ib, h] = jnp.where(onehot, y, yt_ref[ib, h])
        return carry

    lax.fori_loop(0, tb, step, 0)

    @pl.when(ti == pl.num_programs(1) - 1)
    def _():
        st_ref[...] = s_scr[...]


def wkv_scan(r, w, k, kk, b, vt, s0, nb=2):
    bsz, nh, t, n = r.shape
    tb = min(LANES, t)
    row = pl.BlockSpec((nb, nh, tb, n), lambda i, j: (i, 0, j, 0))
    col = pl.BlockSpec((nb, nh, n, tb), lambda i, j: (i, 0, 0, j))
    st = pl.BlockSpec((nb, nh, n, n), lambda i, j: (i, 0, 0, 0))
    return pl.pallas_call(
        functools.partial(_wkv_kernel, nb=nb, nh=nh, tb=tb),
        grid=(bsz // nb, t // tb),
        in_specs=[row, row, row, row, row, col, st],
        out_specs=[col, st],
        out_shape=[jax.ShapeDtypeStruct((bsz, nh, n, t), f32), jax.ShapeDtypeStruct((bsz, nh, n, n), f32)],
        scratch_shapes=[pltpu.VMEM((nb, nh, n, n), f32)],
        compiler_params=_params("parallel", "arbitrary"),
        name="wkv_scan",
    )(r, w, k, kk, b, vt, s0)


def _gelu_tanh(x):
    return 0.5 * x * (1.0 + jnp.tanh(0.7978845608028654 * (x + 0.044715 * x * x * x)))


def _compress_rows(x_ref, n_chunks, pos_ref, w1_ref, w2_ref, n_valid):
    half = CMP_STRIDE * HEAD_DIM
    x2 = jnp.concatenate(
        [x_ref[pl.ds(s, n_chunks, stride=CMP_STRIDE), :] for s in range(CMP_STRIDE)],
        axis=1).astype(bf16)
    pa = jnp.dot(x2, w1_ref[:half, :], preferred_element_type=f32)
    pb = jnp.dot(x2, w1_ref[half:, :], preferred_element_type=f32)
    posterm = jnp.dot(pos_ref[...], w1_ref[...], preferred_element_type=f32)[0:1, :]
    hid = pa + pltpu.roll(pb, n_chunks - 1, 0) + posterm
    out = jnp.dot(_gelu_tanh(hid).astype(bf16), w2_ref[...], preferred_element_type=f32)
    rows = lax.broadcasted_iota(jnp.int32, out.shape, 0)
    return jnp.where(rows < n_valid, out, 0.0)


def _compress_prompt_kernel(k_ref, v_ref, posk_ref, w1k_ref, w2k_ref, posv_ref, w1v_ref, w2v_ref,
                            ok_ref, ov_ref, *, n_chunks, n_valid):
    ok_ref[0, 0] = _compress_rows(k_ref.at[0], n_chunks, posk_ref, w1k_ref, w2k_ref, n_valid)
    ov_ref[0, 0] = _compress_rows(v_ref.at[0], n_chunks, posv_ref, w1v_ref, w2v_ref, n_valid)


def _cmp_weights(cmp_pos, cmp_w1, cmp_w2):
    ws = []
    for i in range(2):
        pos = jnp.broadcast_to(cmp_pos[i].reshape(1, CMP_BLOCK * HEAD_DIM), (8, CMP_BLOCK * HEAD_DIM)).astype(bf16)
        ws += [pos, cmp_w1[i].reshape(CMP_BLOCK * HEAD_DIM, HEAD_DIM).astype(bf16), cmp_w2[i].astype(bf16)]
    return ws


_CMP_WEIGHT_SHAPES = [(8, CMP_BLOCK * HEAD_DIM), (CMP_BLOCK * HEAD_DIM, HEAD_DIM), (HEAD_DIM, HEAD_DIM)] * 2


def compress_prompt(kv, cmp_pos, cmp_w1, cmp_w2):
    b, t, _ = kv.shape
    n_chunks = t // CMP_STRIDE
    n_valid = (t - CMP_BLOCK) // CMP_STRIDE + 1
    out = jax.ShapeDtypeStruct((b, NSA_KV, n_chunks, HEAD_DIM), f32)
    ospec = pl.BlockSpec((1, 1, n_chunks, HEAD_DIM), lambda i, kv: (i, kv, 0, 0))
    wspecs = [pl.BlockSpec(s, lambda i, kv: (0, 0)) for s in _CMP_WEIGHT_SHAPES]
    return pl.pallas_call(
        functools.partial(_compress_prompt_kernel, n_chunks=n_chunks, n_valid=n_valid),
        grid=(b, NSA_KV),
        in_specs=[pl.BlockSpec((1, t, HEAD_DIM), lambda i, kv: (i, 0, kv)),
                  pl.BlockSpec((1, t, HEAD_DIM), lambda i, kv: (i, 0, NSA_KV + kv))] + wspecs,
        out_specs=[ospec, ospec],
        out_shape=[out, out],
        compiler_params=_params("parallel", "parallel"),
        name="compress_prompt",
    )(kv, kv, *_cmp_weights(cmp_pos, cmp_w1, cmp_w2))


def _compress_paged_kernel(pt_ref, k_ref, v_ref, posk_ref, w1k_ref, w2k_ref, posv_ref, w1v_ref, w2v_ref,
                           ok_ref, ov_ref, k_scr, v_scr, *, n_chunks, n_valid):
    p = pl.program_id(1)
    row0 = pl.multiple_of(p * PAGE_SIZE, PAGE_SIZE)
    for kv in range(NSA_KV):
        k_scr[kv, pl.ds(row0, PAGE_SIZE), :] = k_ref[0, :, kv * HEAD_DIM:(kv + 1) * HEAD_DIM]
        v_scr[kv, pl.ds(row0, PAGE_SIZE), :] = v_ref[0, :, kv * HEAD_DIM:(kv + 1) * HEAD_DIM]

    @pl.when(p == pl.num_programs(1) - 1)
    def _():
        for kv in range(NSA_KV):
            ok_ref[0, kv] = _compress_rows(k_scr.at[kv], n_chunks, posk_ref, w1k_ref, w2k_ref, n_valid)
            ov_ref[0, kv] = _compress_rows(v_scr.at[kv], n_chunks, posv_ref, w1v_ref, w2v_ref, n_valid)


def compress_paged(pool_k, pool_v, page_table, n_valid, cmp_pos, cmp_w1, cmp_w2):
    b, n_pages = page_table.shape
    length = n_pages * PAGE_SIZE
    n_chunks = length // CMP_STRIDE
    out = jax.ShapeDtypeStruct((b, NSA_KV, n_chunks, HEAD_DIM), f32)
    ospec = pl.BlockSpec((1, NSA_KV, n_chunks, HEAD_DIM), lambda i, p, pt: (i, 0, 0, 0))
    page = pl.BlockSpec((1, PAGE_SIZE, KV_COLS), lambda i, p, pt: (pt[i, p], 0, 0))
    wspecs = [pl.BlockSpec(s, lambda i, p, pt: (0, 0)) for s in _CMP_WEIGHT_SHAPES]
    return pl.pallas_call(
        functools.partial(_compress_paged_kernel, n_chunks=n_chunks, n_valid=n_valid),
        grid_spec=pltpu.PrefetchScalarGridSpec(
            num_scalar_prefetch=1,
            grid=(b, n_pages),
            in_specs=[page, page] + wspecs,
            out_specs=[ospec, ospec],
            scratch_shapes=[pltpu.VMEM((NSA_KV, length, HEAD_DIM), f32),
                            pltpu.VMEM((NSA_KV, length, HEAD_DIM), f32)]),
        out_shape=[out, out],
        compiler_params=_params("parallel", "arbitrary"),
        name="compress_paged",
    )(page_table, pool_k, pool_v, *_cmp_weights(cmp_pos, cmp_w1, cmp_w2))


def _stack_heads(x):
    return jnp.concatenate([x[:, g * HEAD_DIM:(g + 1) * HEAD_DIM] for g in range(NSA_GROUP)], axis=0)


def _dot_nt(a, b):
    return lax.dot_general(a, b, (((1,), (1,)), ((), ())), preferred_element_type=f32)


def _compressed_branch(qc, ckc, cvc, pos_rows, n_cmp):
    s = _dot_nt(qc, ckc) * (HEAD_DIM ** -0.5)
    n = lax.broadcasted_iota(jnp.int32, s.shape, 1)
    vis = (n * CMP_STRIDE + (CMP_BLOCK - 1) <= pos_rows) & (n < n_cmp)
    s = jnp.where(vis, s, NEG_INF)
    m = jnp.max(s, axis=-1, keepdims=True)
    e = jnp.where(vis, jnp.exp(s - m), 0.0)
    denom = jnp.sum(e, axis=-1, keepdims=True)
    p = e / jnp.maximum(denom, 1e-30)
    return jnp.dot(p.astype(bf16), cvc, preferred_element_type=f32), p


def _select_blocks(p, tq, pos0, n_slc, n_j):
    psum = p[0:tq]
    for g in range(1, NSA_GROUP):
        psum = psum + p[g * tq:(g + 1) * tq]
    if tq < SEL_LANES:
        psum = jnp.concatenate([psum, jnp.zeros((SEL_LANES - tq, psum.shape[1]), f32)], axis=0)
    n_c = p.shape[1]
    j = lax.broadcasted_iota(jnp.int32, (n_j, n_c), 0)
    cs = lax.broadcasted_iota(jnp.int32, (n_j, n_c), 1) * CMP_STRIDE
    overlap = jnp.where((cs < j * SLC_BLOCK + SLC_BLOCK) & (cs + (CMP_BLOCK - 1) >= j * SLC_BLOCK), 1.0, 0.0)
    imp_t = lax.dot_general(overlap, psum, (((1,), (1,)), ((), ())),
                            preferred_element_type=f32, precision=lax.Precision.HIGHEST)
    j = lax.broadcasted_iota(jnp.int32, imp_t.shape, 0)
    pos_t = pos0 + lax.broadcasted_iota(jnp.int32, imp_t.shape, 1)
    cur = lax.shift_right_logical(pos_t, SLC_SHIFT)
    causal = j * SLC_BLOCK <= pos_t
    forced = (j == 0) | (j == cur) | (j == cur - 1)
    score = jnp.where(causal, jnp.where(forced, FORCE_SCORE, imp_t), -FORCE_SCORE)
    score = jnp.where(j < n_slc, score, -2.0 * FORCE_SCORE)
    rank = jnp.zeros(imp_t.shape, f32)
    for jp in range(n_slc):
        row = score[jp:jp + 1, :]
        ahead = (row > score) | ((row == score) & (j > jp))
        rank = rank + jnp.where(ahead, 1.0, 0.0)
    sel_t = jnp.where(rank < min(N_SELECT, n_slc), 1.0, 0.0)
    return sel_t.T[0:tq]


def _expand_sel(sel_rows, key0, n_keys):
    nj = sel_rows.shape[1]
    j = lax.broadcasted_iota(jnp.int32, (nj, n_keys), 0)
    kpos = key0 + lax.broadcasted_iota(jnp.int32, (nj, n_keys), 1)
    e = jnp.where(lax.shift_right_logical(kpos, SLC_SHIFT) == j, 1.0, 0.0).astype(bf16)
    return jnp.dot(sel_rows.astype(bf16), e, preferred_element_type=f32)


def _online_update(m, l, acc, s, valid, v):
    s = jnp.where(valid, s, NEG_INF)
    m_new = jnp.maximum(m, jnp.max(s, axis=-1, keepdims=True))
    alpha = jnp.exp(m - m_new)
    p = jnp.where(valid, jnp.exp(s - m_new), 0.0)
    l = alpha * l + jnp.sum(p, axis=-1, keepdims=True)
    acc = alpha * acc + jnp.dot(p.astype(bf16), v, preferred_element_type=f32)
    return m_new, l, acc


def _window_branch(qr, wk, wv, kpos0, n_keys_valid, pos_rows, n_phantom=None):
    s = _dot_nt(qr, wk) * (HEAD_DIM ** -0.5)
    lane = lax.broadcasted_iota(jnp.int32, s.shape, 1)
    kpos = kpos0 + lane
    valid = (kpos <= pos_rows) & (pos_rows - kpos < WINDOW) & (lane < n_keys_valid)
    s = jnp.where(valid, s, NEG_INF)
    m = jnp.max(s, axis=-1, keepdims=True)
    if n_phantom is not None:
        m = jnp.where(n_phantom > 0.0, jnp.maximum(m, 0.0), m)
    e = jnp.where(valid, jnp.exp(s - m), 0.0)
    denom = jnp.sum(e, axis=-1, keepdims=True)
    if n_phantom is not None:
        denom = denom + n_phantom * jnp.exp(-m)
    return jnp.dot((e / denom).astype(bf16), wv, preferred_element_type=f32)


def _gated_sum(gate, col0, tq, o_cmp, o_slc, o_win, g):
    r = slice(g * tq, (g + 1) * tq)
    c, s, w = col0[0] + g, col0[1] + g, col0[2] + g
    return gate[:, c:c + 1] * o_cmp[r] + gate[:, s:s + 1] * o_slc[r] + gate[:, w:w + 1] * o_win[r]


def _nsa_prompt_kernel(q_ref, c2_ref, s2_ref, gate_ref, ckc_ref, cvc_ref, sk_ref, sv_ref, wk_ref, wv_ref,
                       o_ref, *, n_cmp, n_slc):
    i = pl.program_id(2)
    tq = Q_BLOCK
    rows = NSA_GROUP * tq
    q0 = i * tq
    scale = HEAD_DIM ** -0.5
    q = q_ref[0]
    qc = _stack_heads(q).astype(bf16)
    qr = _rope_heads(q, c2_ref[...], s2_ref[...]).astype(bf16)
    t_row = lax.broadcasted_iota(jnp.int32, (rows, 1), 0)
    pos_rows = q0 + jnp.bitwise_and(t_row, tq - 1)

    o_cmp, p = _compressed_branch(qc, ckc_ref[0, 0].astype(bf16), cvc_ref[0, 0].astype(bf16), pos_rows, n_cmp)
    sel = _select_blocks(p, tq, q0, n_slc, n_slc)
    sel_rows = jnp.concatenate([sel] * NSA_GROUP, axis=0)

    def slc_step(c, carry):
        m, l, acc = carry
        k0 = pl.multiple_of(c * SLC_CHUNK, SLC_CHUNK)
        s = _dot_nt(qr, sk_ref[0, pl.ds(k0, SLC_CHUNK), :]) * scale
        kpos = k0 + lax.broadcasted_iota(jnp.int32, s.shape, 1)
        valid = (_expand_sel(sel_rows, k0, SLC_CHUNK) > 0.5) & (kpos <= pos_rows)
        return _online_update(m, l, acc, s, valid, sv_ref[0, pl.ds(k0, SLC_CHUNK), :])

    n_chunks = (q0 + tq - 1) // SLC_CHUNK + 1
    init = (jnp.full((rows, 1), NEG_INF, f32), jnp.zeros((rows, 1), f32), jnp.zeros((rows, HEAD_DIM), f32))
    _, l, acc = lax.fori_loop(0, n_chunks, slc_step, init)
    o_slc = acc / l

    span = WINDOW + tq
    w0 = pl.multiple_of(jnp.maximum(q0 - WINDOW, 0), tq)
    n_phantom = jnp.maximum(WINDOW - 1 - pos_rows, 0).astype(f32)
    o_win = _window_branch(qr, wk_ref[0, pl.ds(w0, span), :], wv_ref[0, pl.ds(w0, span), :], w0, span, pos_rows,
                           n_phantom)

    gate = _sigmoid(gate_ref[0, 0])
    cols = (0, NSA_GROUP, 2 * NSA_GROUP)
    for g in range(NSA_GROUP):
        o_ref[0, :, g * HEAD_DIM:(g + 1) * HEAD_DIM] = _gated_sum(gate, cols, tq, o_cmp, o_slc, o_win, g)


def nsa_prompt(q, gate, ckc, cvc, kv_bf, c2, s2, n_cmp):
    b, t, _ = q.shape
    assert t % SLC_CHUNK == 0 and t >= WINDOW + Q_BLOCK
    n_slc = t // SLC_BLOCK
    gw = NSA_GROUP * HEAD_DIM
    kvcol = lambda c: pl.BlockSpec((1, t, HEAD_DIM), lambda bi, kv, i, c=c: (bi, 0, c * NSA_KV + kv))
    cmp_spec = pl.BlockSpec((1, 1, ckc.shape[2], HEAD_DIM), lambda bi, kv, i: (bi, kv, 0, 0))
    return pl.pallas_call(
        functools.partial(_nsa_prompt_kernel, n_cmp=n_cmp, n_slc=n_slc),
        grid=(b, NSA_KV, t // Q_BLOCK),
        in_specs=[pl.BlockSpec((1, Q_BLOCK, gw), lambda bi, kv, i: (bi, i, kv)),
                  pl.BlockSpec((Q_BLOCK, HEAD_DIM), lambda bi, kv, i: (i, 0)),
                  pl.BlockSpec((Q_BLOCK, HEAD_DIM), lambda bi, kv, i: (i, 0)),
                  pl.BlockSpec((1, 1, Q_BLOCK, 3 * NSA_GROUP), lambda bi, kv, i: (bi, kv, i, 0)),
                  cmp_spec, cmp_spec, kvcol(2), kvcol(3), kvcol(4), kvcol(5)],
        out_specs=pl.BlockSpec((1, Q_BLOCK, gw), lambda bi, kv, i: (bi, i, kv)),
        out_shape=jax.ShapeDtypeStruct((b, t, NSA_WIDTH), f32),
        compiler_params=_params("parallel", "parallel", "arbitrary"),
        name="nsa_prompt",
    )(q, c2, s2, gate, ckc, cvc, kv_bf, kv_bf, kv_bf, kv_bf)


def _pad_rows(x, n):
    return jnp.concatenate([x, jnp.zeros((n - x.shape[0], x.shape[1]), x.dtype)], axis=0)


def _nsa_sample_kernel(pt_ref, q_ref, c2_ref, s2_ref, gate_ref, ckc_ref, cvc_ref, pk_ref, pv_ref,
                       new_ref, wink_ref, winv_ref, o_ref,
                       qr_scr, sel_scr, ocmp_scr, owin_scr, m_scr, l_scr, acc_scr, *, n_cmp, n_slc, past, tq):
    p = pl.program_id(1)
    rows = NSA_GROUP * tq
    scale = HEAD_DIM ** -0.5
    t_row = lax.broadcasted_iota(jnp.int32, (rows, 1), 0)
    pos_rows = past + jnp.bitwise_and(t_row, tq - 1)
    n_j = sel_scr.shape[2]
    lw = wink_ref.shape[1]
    new_col = lambda branch, kv: slice((branch * NSA_KV + kv) * HEAD_DIM, (branch * NSA_KV + kv + 1) * HEAD_DIM)

    @pl.when(p == 0)
    def _():
        c2, s2 = c2_ref[...], s2_ref[...]
        for kv in range(NSA_KV):
            q = q_ref[0, :, kv * NSA_GROUP * HEAD_DIM:(kv + 1) * NSA_GROUP * HEAD_DIM]
            qc = _stack_heads(q).astype(bf16)
            qr = _rope_heads(q, c2, s2).astype(bf16)
            qr_scr[kv] = qr
            o_cmp, pc = _compressed_branch(qc, ckc_ref[0, kv].astype(bf16), cvc_ref[0, kv].astype(bf16),
                                           pos_rows, n_cmp)
            ocmp_scr[kv] = o_cmp
            sel = _select_blocks(pc, tq, past, n_slc, n_j)
            sel_scr[kv] = jnp.concatenate([sel] * NSA_GROUP, axis=0)
            wk = jnp.concatenate([wink_ref[0, :, kv * HEAD_DIM:(kv + 1) * HEAD_DIM],
                                  _pad_rows(new_ref[0, :, new_col(4, kv)], LANES)], axis=0).astype(bf16)
            wv = jnp.concatenate([winv_ref[0, :, kv * HEAD_DIM:(kv + 1) * HEAD_DIM],
                                  _pad_rows(new_ref[0, :, new_col(5, kv)], LANES)], axis=0).astype(bf16)
            owin_scr[kv] = _window_branch(qr, wk, wv, past - lw, lw + tq, pos_rows)
        m_scr[...] = jnp.full(m_scr.shape, NEG_INF, f32)
        l_scr[...] = jnp.zeros(l_scr.shape, f32)
        acc_scr[...] = jnp.zeros(acc_scr.shape, f32)

    k0 = p * PAGE_SIZE
    for kv in range(NSA_KV):
        ksl = slice(kv * HEAD_DIM, (kv + 1) * HEAD_DIM)
        s = _dot_nt(qr_scr[kv], pk_ref[0, :, ksl].astype(bf16)) * scale
        kpos = k0 + lax.broadcasted_iota(jnp.int32, s.shape, 1)
        valid = (_expand_sel(sel_scr[kv], k0, PAGE_SIZE) > 0.5) & (kpos <= pos_rows)
        m, l, acc = _online_update(m_scr[kv], l_scr[kv], acc_scr[kv], s, valid, pv_ref[0, :, ksl].astype(bf16))
        m_scr[kv], l_scr[kv], acc_scr[kv] = m, l, acc

    @pl.when(p == pl.num_programs(1) - 1)
    def _():
        gate = _sigmoid(gate_ref[0])
        jb = past // SLC_BLOCK
        for kv in range(NSA_KV):
            nk = _pad_rows(new_ref[0, :, new_col(2, kv)], LANES).astype(bf16)
            nv = _pad_rows(new_ref[0, :, new_col(3, kv)], LANES).astype(bf16)
            s = _dot_nt(qr_scr[kv], nk) * scale
            lane = lax.broadcasted_iota(jnp.int32, s.shape, 1)
            valid = (sel_scr[kv][:, jb:jb + 1] > 0.5) & (past + lane <= pos_rows) & (lane < tq)
            _, l, acc = _online_update(m_scr[kv], l_scr[kv], acc_scr[kv], s, valid, nv)
            o_slc = acc / l
            cols = tuple(br * NSA_HEADS + kv * NSA_GROUP for br in range(3))
            for g in range(NSA_GROUP):
                hd = kv * NSA_GROUP + g
                o_ref[0, :, hd * HEAD_DIM:(hd + 1) * HEAD_DIM] = _gated_sum(
                    gate, cols, tq, ocmp_scr[kv], o_slc, owin_scr[kv], g)


def nsa_sample(q, gate, ckc, cvc, pool_k, pool_v, page_table, new_kv, win_k, win_v, c2, s2, n_cmp, past):
    b, tq, _ = q.shape
    n_pages = page_table.shape[1]
    assert past == n_pages * PAGE_SIZE and past % SLC_BLOCK == 0 and tq <= SLC_BLOCK
    n_slc = past // SLC_BLOCK + 1
    n_j = -(-n_slc // SLC_BLOCK) * SLC_BLOCK
    rows = NSA_GROUP * tq
    lw = win_k.shape[1]
    per_b = lambda shape: pl.BlockSpec((1,) + shape, lambda i, p, pt: (i,) + (0,) * len(shape))
    tab = pl.BlockSpec((tq, HEAD_DIM), lambda i, p, pt: (0, 0))
    page = pl.BlockSpec((1, PAGE_SIZE, KV_COLS), lambda i, p, pt: (pt[i, p], 0, 0))
    return pl.pallas_call(
        functools.partial(_nsa_sample_kernel, n_cmp=n_cmp, n_slc=n_slc, past=past, tq=tq),
        grid_spec=pltpu.PrefetchScalarGridSpec(
            num_scalar_prefetch=1,
            grid=(b, n_pages),
            in_specs=[per_b((tq, NSA_WIDTH)), tab, tab, per_b((tq, 3 * NSA_HEADS)),
                      per_b((NSA_KV, ckc.shape[2], HEAD_DIM)), per_b((NSA_KV, ckc.shape[2], HEAD_DIM)),
                      page, page, per_b((tq, new_kv.shape[2])), per_b((lw, KV_COLS)), per_b((lw, KV_COLS))],
            out_specs=per_b((tq, NSA_WIDTH)),
            scratch_shapes=[pltpu.VMEM((NSA_KV, rows, HEAD_DIM), bf16),
                            pltpu.VMEM((NSA_KV, rows, n_j), f32),
                            pltpu.VMEM((NSA_KV, rows, HEAD_DIM), f32),
                            pltpu.VMEM((NSA_KV, rows, HEAD_DIM), f32),
                            pltpu.VMEM((NSA_KV, rows, 1), f32),
                            pltpu.VMEM((NSA_KV, rows, 1), f32),
                            pltpu.VMEM((NSA_KV, rows, HEAD_DIM), f32)]),
        out_shape=jax.ShapeDtypeStruct((b, tq, NSA_WIDTH), f32),
        compiler_params=_params("parallel", "arbitrary"),
        name="nsa_sample",
    )(page_table, q, c2, s2, gate, ckc, cvc, pool_k, pool_v, new_kv, win_k, win_v)


def _rope_tables(pos):
    half = HEAD_DIM // 2
    inv = jnp.power(ROPE_THETA, -jnp.arange(half, dtype=f32) / half)
    ang = pos.astype(f32)[:, None] * inv[None, :]
    cos, sin = jnp.cos(ang), jnp.sin(ang)
    return jnp.concatenate([cos, cos], axis=-1), jnp.concatenate([-sin, sin], axis=-1)


def _pad_cols(w, n):
    return jnp.pad(w, ((0, 0), (0, n - w.shape[1])))


def _pad_to_rows(x, n):
    return jnp.pad(x, ((0, n - x.shape[0]), (0, 0)))


def _shift_tokens(first, seq):
    return jnp.concatenate([first[:, None, :], seq[:, :-1]], axis=1)


def rwkv_mem_layer(h, x_prev, s0, mk, mv, P):
    b, t, d = h.shape
    m = b * t
    h2 = h.reshape(m, d)
    hn2 = rmsnorm(h2, P['g_mix_pre'])
    hn = hn2.reshape(b, t, d)
    rw = P['w0'].shape[0]
    nh = rw // RWKV_HEAD_DIM

    proj = matmul(hn2, P['w_in_a']).reshape(b, t, -1)
    rows8 = -(-b // 8) * 8
    p0 = matmul(_pad_to_rows(x_prev, rows8), P['w_in_a'])[:b]
    cur = proj[..., :3 * rw]
    prev = _shift_tokens(p0[:, :3 * rw], cur)
    rkv = cur + (prev - cur) * P['mu_rkv']
    r, k, v = jnp.split(rkv, 3, axis=-1)
    mq = proj[..., 3 * rw:]

    xx = _shift_tokens(x_prev, hn) - hn
    mu = P['mu_wag']
    lora = lambda x, w1: matmul(x.reshape(m, d), _pad_cols(w1, -(-w1.shape[1] // LANES) * LANES))
    second = lambda x, w2: matmul(x, _pad_to_rows(w2, x.shape[1])).reshape(b, t, rw)
    w_raw = P['w0'] + second(jnp.tanh(lora(hn + xx * mu[0], P['w_decay1'])), P['w_decay2'])
    decay = jnp.exp(-jnp.exp(-jax.nn.softplus(-w_raw) - 0.5))
    a_rate = jax.nn.sigmoid(P['a0'] + second(lora(hn + xx * mu[1], P['w_aaa1']), P['w_aaa2']))
    gate = second(jax.nn.sigmoid(lora(hn + xx * mu[2], P['w_gate1'])), P['w_gate2'])

    heads = lambda x: x.reshape(b, t, nh, RWKV_HEAD_DIM)
    r_h, k_h, v_h, a_h, w_h = heads(r), heads(k), heads(v), heads(a_rate), heads(decay)
    kk = k_h * P['k_k'].reshape(nh, RWKV_HEAD_DIM)
    kk = kk * lax.rsqrt(jnp.sum(kk * kk, axis=-1, keepdims=True) + 1e-12)
    k_h = k_h * (1.0 + (a_h - 1.0) * P['k_a'].reshape(nh, RWKV_HEAD_DIM))

    tmaj = lambda x: jnp.transpose(x, (0, 2, 1, 3))
    yt, s_t = wkv_scan(tmaj(r_h), tmaj(w_h), tmaj(k_h), tmaj(kk), tmaj(kk * a_h),
                       jnp.transpose(v_h, (0, 2, 3, 1)), s0)
    y = jnp.transpose(yt, (0, 3, 1, 2))
    mean = jnp.mean(y, axis=-1, keepdims=True)
    var = jnp.mean(jnp.square(y - mean), axis=-1, keepdims=True)
    y = ((y - mean) * lax.rsqrt(var + GN_EPS)).reshape(b, t, rw)
    y = y * P['lnx_w'] + P['lnx_b']
    bonus = jnp.sum(r_h * k_h * P['r_k'], axis=-1, keepdims=True) * v_h
    o_rwkv = (y + bonus.reshape(b, t, rw)) * gate

    o_mem = mem_attention(mq, mk, mv)
    o = jnp.concatenate([o_rwkv, o_mem], axis=-1).reshape(m, d)
    h2 = out_proj_residual(o, P['w_out_a'], h2, P['g_mix_post'])
    h2 = ffn_residual(h2, P['g_ffn_pre'], P['w_ff1'], P['w_ff2'], P['g_ffn_post'])
    return h2.reshape(b, t, d), s_t, hn[:, -1]


def nsa_mem_layer(h, mk, mv, P, attend):
    b, t, d = h.shape
    m = b * t
    h2 = h.reshape(m, d)
    w_in = P['w_in_b']
    n_in = w_in.shape[1]
    proj = matmul(h2, _pad_cols(w_in, -(-n_in // 512) * 512), g=P['g_mix_pre']).reshape(b, t, -1)
    q = proj[..., :NSA_WIDTH]
    mq = proj[..., NSA_WIDTH:NSA_WIDTH + MEM_WIDTH]
    gate = proj[..., NSA_WIDTH + MEM_WIDTH:n_in]
    o_nsa = attend(q, gate)
    o_mem = mem_attention(mq, mk, mv)
    o = jnp.concatenate([o_nsa, o_mem], axis=-1).reshape(m, d)
    h2 = out_proj_residual(o, P['w_out_b'], h2, P['g_mix_post'])
    h2 = ffn_residual(h2, P['g_ffn_pre'], P['w_ff1'], P['w_ff2'], P['g_ffn_post'])
    return h2.reshape(b, t, d)


def _split_kv_rows(kv):
    b, t, _ = kv.shape
    rows = kv.reshape(b, t, 6, NSA_KV, HEAD_DIM)
    return [rows[:, :, i] for i in range(6)]


def kernel(x_prompt, x_sample, mem_prompt, cache_mem_k, cache_mem_v, state_wkv, state_shift, cache_cmp_k, cache_cmp_v, cache_slc_k, cache_slc_v, cache_win_k, cache_win_v, page_table, g_mix_pre, g_mix_post, g_ffn_pre, g_ffn_post, g_mem, w_mem_k, w_mem_v, w_in_a, mu_rkv, mu_wag, w0, w_decay1, w_decay2, a0, w_aaa1, w_aaa2, w_gate1, w_gate2, k_k, k_a, r_k, lnx_w, lnx_b, w_out_a, g_kv, w_kv, cmp_pos, cmp_w1, cmp_w2, w_in_b, w_out_b, w_ff1, w_ff2):
    bp, tp, d = x_prompt.shape
    bs, ts, _ = x_sample.shape
    depth = g_mix_pre.shape[0]
    assert depth == 2 and w_in_a.shape[0] == 1 and w_in_b.shape[0] == 1
    n_pages = page_table.shape[1]
    past = n_pages * PAGE_SIZE
    mem_len = mem_prompt.shape[1]

    P0 = dict(g_mix_pre=g_mix_pre[0], g_mix_post=g_mix_post[0], g_ffn_pre=g_ffn_pre[0], g_ffn_post=g_ffn_post[0],
              w_in_a=w_in_a[0], mu_rkv=mu_rkv[0], mu_wag=mu_wag[0], w0=w0[0], w_decay1=w_decay1[0],
              w_decay2=w_decay2[0], a0=a0[0], w_aaa1=w_aaa1[0], w_aaa2=w_aaa2[0], w_gate1=w_gate1[0],
              w_gate2=w_gate2[0], k_k=k_k[0], k_a=k_a[0], r_k=r_k[0], lnx_w=lnx_w[0], lnx_b=lnx_b[0],
              w_out_a=w_out_a[0], w_ff1=w_ff1[0], w_ff2=w_ff2[0])
    P1 = dict(g_mix_pre=g_mix_pre[1], g_mix_post=g_mix_post[1], g_ffn_pre=g_ffn_pre[1], g_ffn_post=g_ffn_post[1],
              w_in_b=w_in_b[0], w_out_b=w_out_b[0], w_ff1=w_ff1[1], w_ff2=w_ff2[1])

    mem2 = mem_prompt.reshape(bp * mem_len, d)
    mkv = [matmul(mem2, jnp.concatenate([w_mem_k[l], w_mem_v[l]], axis=1), g=g_mem[l]).reshape(bp, mem_len, -1)
           for l in range(depth)]
    mem_k_p = jnp.stack([x[..., :MEM_WIDTH] for x in mkv])
    mem_v_p = jnp.stack([x[..., MEM_WIDTH:] for x in mkv])

    nh = w0.shape[1] // RWKV_HEAD_DIM
    shift0 = jnp.zeros((bp, d), f32)
    wkv0 = jnp.zeros((bp, nh, RWKV_HEAD_DIM, RWKV_HEAD_DIM), f32)
    h, wkv_p, shift_p = rwkv_mem_layer(x_prompt, shift0, wkv0, mem_k_p[0], mem_v_p[0], P0)

    c2p, s2p = _rope_tables(jnp.arange(tp, dtype=jnp.int32))
    kv_p = kv_proj(h.reshape(bp * tp, d), g_kv, w_kv, c2p, s2p).reshape(bp, tp, -1)
    ckc, cvc = compress_prompt(kv_p, cmp_pos, cmp_w1, cmp_w2)
    n_cmp_p = (tp - CMP_BLOCK) // CMP_STRIDE + 1

    def attend_prompt(q, gate):
        g = gate.reshape(bp, tp, 3, NSA_KV, NSA_GROUP).transpose(0, 3, 1, 2, 4).reshape(bp, NSA_KV, tp, 3 * NSA_GROUP)
        return nsa_prompt(q, g, ckc, cvc, kv_p.astype(bf16), c2p, s2p, n_cmp_p)

    y_p = nsa_mem_layer(h, mem_k_p[1], mem_v_p[1], P1, attend_prompt)
    cmp_k_p, cmp_v_p, slc_k_p, slc_v_p, win_k_p, win_v_p = _split_kv_rows(kv_p)
    n_keep = min(WINDOW, tp)
    win_k_p, win_v_p = win_k_p[:, tp - n_keep:], win_v_p[:, tp - n_keep:]

    mk_s = cache_mem_k.reshape(depth, bs, mem_len, MEM_WIDTH)
    mv_s = cache_mem_v.reshape(depth, bs, mem_len, MEM_WIDTH)
    h, wkv_s, shift_s = rwkv_mem_layer(x_sample, state_shift[0], state_wkv[0], mk_s[0], mv_s[0], P0)

    c2s, s2s = _rope_tables(past + jnp.arange(ts, dtype=jnp.int32))
    kv_s = kv_proj(h.reshape(bs * ts, d), g_kv, w_kv, jnp.tile(c2s, (bs, 1)), jnp.tile(s2s, (bs, 1)))
    kv_s = kv_s.reshape(bs, ts, -1)
    n_cmp_s = (past + ts - CMP_BLOCK) // CMP_STRIDE + 1
    assert (n_cmp_s - 1) * CMP_STRIDE + CMP_BLOCK <= past
    n_pool = cache_cmp_k.shape[0]
    pool = lambda x: x.reshape(n_pool, PAGE_SIZE, KV_COLS)
    ckc_s, cvc_s = compress_paged(pool(cache_cmp_k), pool(cache_cmp_v), page_table, n_cmp_s, cmp_pos, cmp_w1, cmp_w2)
    lw = cache_win_k.shape[1]
    win_k2, win_v2 = cache_win_k.reshape(bs, lw, KV_COLS), cache_win_v.reshape(bs, lw, KV_COLS)

    def attend_sample(q, gate):
        return nsa_sample(q, gate, ckc_s, cvc_s, pool(cache_slc_k), pool(cache_slc_v), page_table, kv_s,
                          win_k2, win_v2, c2s, s2s, n_cmp_s, past)

    y_s = nsa_mem_layer(h, mk_s[1], mv_s[1], P1, attend_sample)
    cmp_k_s, cmp_v_s, slc_k_s, slc_v_s, wk_new, wv_new = _split_kv_rows(kv_s)
    win_k_s = jnp.concatenate([cache_win_k, wk_new], axis=1)[:, ts:]
    win_v_s = jnp.concatenate([cache_win_v, wv_new], axis=1)[:, ts:]

    return (y_p, y_s, mem_k_p.reshape(depth, bp, mem_len, MEM_HEADS, HEAD_DIM),
            mem_v_p.reshape(depth, bp, mem_len, MEM_HEADS, HEAD_DIM),
            wkv_p[None], shift_p[None], cmp_k_p, cmp_v_p, slc_k_p, slc_v_p, win_k_p, win_v_p,
            wkv_s[None], shift_s[None], cmp_k_s, cmp_v_s, slc_k_s, slc_v_s, win_k_s, win_v_s)
```

```python
import functools

import jax
import jax.numpy as jnp
from jax import lax
from jax.experimental import pallas as pl
from jax.experimental.pallas import tpu as pltpu

f32 = jnp.float32
bf16 = jnp.bfloat16

LANES = 128
VMEM_LIMIT_BYTES = 56 * 1024 * 1024

HEAD_DIM = 128
MEM_HEADS = 4
MEM_WIDTH = MEM_HEADS * HEAD_DIM
RWKV_HEAD_DIM = 64
GN_EPS = 64e-5
NSA_KV = 2
NSA_GROUP = 6
NSA_HEADS = NSA_KV * NSA_GROUP
NSA_WIDTH = NSA_HEADS * HEAD_DIM
KV_COLS = NSA_KV * HEAD_DIM
CMP_BLOCK = 32
CMP_STRIDE = 16
SLC_BLOCK = 64
SLC_SHIFT = 6
N_SELECT = 16
WINDOW = 512
Q_BLOCK = 128
ROPE_THETA = 10000.0
NORM_EPS = 1e-6
NEG_INF = -1e30
FORCE_SCORE = 1e9
PAGE_SIZE = 128

ROW_TILE = 512
SLC_CHUNK = 512
SEL_LANES = 128


def _params(*sem):
    return pltpu.CompilerParams(dimension_semantics=sem, vmem_limit_bytes=VMEM_LIMIT_BYTES)


def _rms(x, g):
    return x * lax.rsqrt(jnp.mean(x * x, axis=-1, keepdims=True) + NORM_EPS) * g


def _sigmoid(x):
    return 1.0 / (1.0 + jnp.exp(-x))


def _rmsnorm_kernel(x_ref, g_ref, o_ref):
    o_ref[...] = _rms(x_ref[...], g_ref[...])


def rmsnorm(x, g):
    m, d = x.shape
    tm = min(ROW_TILE, m)
    return pl.pallas_call(
        _rmsnorm_kernel,
        grid=(m // tm,),
        in_specs=[pl.BlockSpec((tm, d), lambda i: (i, 0)), pl.BlockSpec((1, d), lambda i: (0, 0))],
        out_specs=pl.BlockSpec((tm, d), lambda i: (i, 0)),
        out_shape=jax.ShapeDtypeStruct((m, d), f32),
        compiler_params=_params("parallel"),
        name="rmsnorm",
    )(x, g.reshape(1, d))


def _mm_kernel(x_ref, g_ref, w_ref, o_ref, xn_ref, *, norm):
    @pl.when(pl.program_id(1) == 0)
    def _():
        x = x_ref[...]
        if norm:
            x = _rms(x, g_ref[...])
        xn_ref[...] = x.astype(bf16)

    o_ref[...] = jnp.dot(xn_ref[...], w_ref[...], preferred_element_type=f32)


def matmul(x, w, g=None, tn=512):
    m, k = x.shape
    n = w.shape[1]
    tm = min(ROW_TILE, m)
    tn = min(tn, n)
    assert m % tm == 0 and n % tn == 0, (m, n, tm, tn)
    gg = jnp.ones((1, k), f32) if g is None else g.reshape(1, k)
    return pl.pallas_call(
        functools.partial(_mm_kernel, norm=g is not None),
        grid=(m // tm, n // tn),
        in_specs=[pl.BlockSpec((tm, k), lambda i, j: (i, 0)),
                  pl.BlockSpec((1, k), lambda i, j: (0, 0)),
                  pl.BlockSpec((k, tn), lambda i, j: (0, j))],
        out_specs=pl.BlockSpec((tm, tn), lambda i, j: (i, j)),
        out_shape=jax.ShapeDtypeStruct((m, n), f32),
        scratch_shapes=[pltpu.VMEM((tm, k), bf16)],
        compiler_params=_params("parallel", "arbitrary"),
        name="matmul",
    )(x, gg, w.astype(bf16))


def _out_proj_kernel(o_ref, w_ref, h_ref, g_ref, y_ref):
    acc = jnp.dot(o_ref[...].astype(bf16), w_ref[...], preferred_element_type=f32)
    y_ref[...] = h_ref[...] + _rms(acc, g_ref[...])


def out_proj_residual(o, w, h, g):
    m, k = o.shape
    d = w.shape[1]
    tm = min(ROW_TILE, m)
    return pl.pallas_call(
        _out_proj_kernel,
        grid=(m // tm,),
        in_specs=[pl.BlockSpec((tm, k), lambda i: (i, 0)),
                  pl.BlockSpec((k, d), lambda i: (0, 0)),
                  pl.BlockSpec((tm, d), lambda i: (i, 0)),
                  pl.BlockSpec((1, d), lambda i: (0, 0))],
        out_specs=pl.BlockSpec((tm, d), lambda i: (i, 0)),
        out_shape=jax.ShapeDtypeStruct((m, d), f32),
        compiler_params=_params("parallel"),
        name="out_proj",
    )(o, w.astype(bf16), h, g.reshape(1, d))


def _ffn_kernel(h_ref, gpre_ref, w1_ref, w2_ref, gpost_ref, y_ref, xn_ref, acc_ref):
    j = pl.program_id(1)

    @pl.when(j == 0)
    def _():
        xn_ref[...] = _rms(h_ref[...], gpre_ref[...]).astype(bf16)
        acc_ref[...] = jnp.zeros_like(acc_ref)

    u = jnp.dot(xn_ref[...], w1_ref[...], preferred_element_type=f32)
    u = jnp.square(jnp.maximum(u, 0.0))
    acc_ref[...] += jnp.dot(u.astype(bf16), w2_ref[...], preferred_element_type=f32)

    @pl.when(j == pl.num_programs(1) - 1)
    def _():
        y_ref[...] = h_ref[...] + _rms(acc_ref[...], gpost_ref[...])


def ffn_residual(h, g_pre, w1, w2, g_post, tf=512):
    m, d = h.shape
    dff = w1.shape[1]
    tm = min(ROW_TILE, m)
    return pl.pallas_call(
        _ffn_kernel,
        grid=(m // tm, dff // tf),
        in_specs=[pl.BlockSpec((tm, d), lambda i, j: (i, 0)),
                  pl.BlockSpec((1, d), lambda i, j: (0, 0)),
                  pl.BlockSpec((d, tf), lambda i, j: (0, j)),
                  pl.BlockSpec((tf, d), lambda i, j: (j, 0)),
                  pl.BlockSpec((1, d), lambda i, j: (0, 0))],
        out_specs=pl.BlockSpec((tm, d), lambda i, j: (i, 0)),
        out_shape=jax.ShapeDtypeStruct((m, d), f32),
        scratch_shapes=[pltpu.VMEM((tm, d), bf16), pltpu.VMEM((tm, d), f32)],
        compiler_params=_params("parallel", "arbitrary"),
        name="ffn",
    )(h, g_pre.reshape(1, d), w1.astype(bf16), w2.astype(bf16), g_post.reshape(1, d))


def _rope_tile(x, c2, s2):
    return x * c2 + pltpu.roll(x, HEAD_DIM // 2, 1) * s2


def _rope_heads(q, c2, s2):
    return jnp.concatenate([_rope_tile(q[:, g * HEAD_DIM:(g + 1) * HEAD_DIM], c2, s2)
                            for g in range(NSA_GROUP)], axis=0)


def _kv_proj_kernel(h_ref, g_ref, w_ref, c2_ref, s2_ref, o_ref, xn_ref):
    j = pl.program_id(1)

    @pl.when(j == 0)
    def _():
        xn_ref[...] = _rms(h_ref[...], g_ref[...]).astype(bf16)

    acc = jnp.dot(xn_ref[...], w_ref[...], preferred_element_type=f32)
    rotated = (j == 2) | (j == 4)

    @pl.when(rotated)
    def _():
        c2, s2 = c2_ref[...], s2_ref[...]
        for kv in range(NSA_KV):
            sl = slice(kv * HEAD_DIM, (kv + 1) * HEAD_DIM)
            o_ref[:, sl] = _rope_tile(acc[:, sl], c2, s2)

    @pl.when(jnp.logical_not(rotated))
    def _():
        o_ref[...] = acc


def kv_proj(h, g, w, c2, s2):
    m, d = h.shape
    n = w.shape[1]
    tm = min(ROW_TILE, m)
    ntab = c2.shape[0] // tm
    return pl.pallas_call(
        _kv_proj_kernel,
        grid=(m // tm, n // KV_COLS),
        in_specs=[pl.BlockSpec((tm, d), lambda i, j: (i, 0)),
                  pl.BlockSpec((1, d), lambda i, j: (0, 0)),
                  pl.BlockSpec((d, KV_COLS), lambda i, j: (0, j)),
                  pl.BlockSpec((tm, HEAD_DIM), lambda i, j: (i % ntab, 0)),
                  pl.BlockSpec((tm, HEAD_DIM), lambda i, j: (i % ntab, 0))],
        out_specs=pl.BlockSpec((tm, KV_COLS), lambda i, j: (i, j)),
        out_shape=jax.ShapeDtypeStruct((m, n), f32),
        scratch_shapes=[pltpu.VMEM((tm, d), bf16)],
        compiler_params=_params("parallel", "arbitrary"),
        name="kv_proj",
    )(h, g.reshape(1, d), w.astype(bf16), c2, s2)


def _mem_attn_kernel(q_ref, k_ref, v_ref, o_ref):
    scale = HEAD_DIM ** -0.5
    for hd in range(MEM_HEADS):
        sl = slice(hd * HEAD_DIM, (hd + 1) * HEAD_DIM)
        q = q_ref[0, :, sl].astype(bf16)
        k = k_ref[0, :, sl].astype(bf16)
        s = lax.dot_general(q, k, (((1,), (1,)), ((), ())), preferred_element_type=f32) * scale
        s = s - jnp.max(s, axis=-1, keepdims=True)
        e = jnp.exp(s)
        p = e / jnp.sum(e, axis=-1, keepdims=True)
        o_ref[0, :, sl] = jnp.dot(p.astype(bf16), v_ref[0, :, sl].astype(bf16), preferred_element_type=f32)


def mem_attention(q, mk, mv):
    b, t, w = q.shape
    mlen = mk.shape[1]
    tq = min(ROW_TILE, t)
    return pl.pallas_call(
        _mem_attn_kernel,
        grid=(b, t // tq),
        in_specs=[pl.BlockSpec((1, tq, w), lambda i, j: (i, j, 0)),
                  pl.BlockSpec((1, mlen, w), lambda i, j: (i, 0, 0)),
                  pl.BlockSpec((1, mlen, w), lambda i, j: (i, 0, 0))],
        out_specs=pl.BlockSpec((1, tq, w), lambda i, j: (i, j, 0)),
        out_shape=jax.ShapeDtypeStruct((b, t, w), f32),
        compiler_params=_params("parallel", "arbitrary"),
        name="mem_attn",
    )(q, mk, mv)


WKV_QUAD = 4
WKV_LANES = WKV_QUAD * RWKV_HEAD_DIM
WKV_TB = 64


def _wkv_kernel(r_ref, w_ref, k_ref, kk_ref, b_ref, v_ref, s0_ref, yt_ref, st_ref,
                s_scr, lhs_scr, vd_scr, yl_scr, *, nb, nq, tb):
    n = RWKV_HEAD_DIM
    ti = pl.program_id(1)

    @pl.when(ti == 0)
    def _():
        for ib in range(nb):
            s_scr[ib * nq * n:(ib + 1) * nq * n, :] = s0_ref[ib]

    yt_ref[...] = jnp.zeros(yt_ref.shape, f32)
    ri = lax.broadcasted_iota(jnp.int32, (WKV_LANES, WKV_LANES), 0)
    ci = lax.broadcasted_iota(jnp.int32, (WKV_LANES, WKV_LANES), 1)
    ones_blk = jnp.where(lax.shift_right_logical(ri, 6) == lax.shift_right_logical(ci, 6), 1.0, 0.0).astype(bf16)
    ones_blk2 = jnp.concatenate([ones_blk, ones_blk], axis=0)
    eye_rep = jnp.where(lax.broadcasted_iota(jnp.int32, (n, WKV_LANES), 0)
                        == jnp.bitwise_and(lax.broadcasted_iota(jnp.int32, (n, WKV_LANES), 1), n - 1), 1.0, 0.0)
    step_lane = jnp.bitwise_and(lax.broadcasted_iota(jnp.int32, (nq * n, WKV_LANES), 1), n - 1)
    tiles = [(ib, q) for ib in range(nb) for q in range(nq)]

    def step(t, carry):
        row = lambda ref, ib, q: ref[ib, pl.ds(t, 1), q * WKV_LANES:(q + 1) * WKV_LANES]
        for ib, q in tiles:
            rs = slice((ib * nq + q) * n, (ib * nq + q + 1) * n)
            prod = s_scr[rs, :] * row(kk_ref, ib, q)
            hi = prod.astype(bf16)
            lhs_scr[rs, 0:WKV_LANES] = hi
            lhs_scr[rs, WKV_LANES:2 * WKV_LANES] = (prod - hi.astype(f32)).astype(bf16)
            vd_scr[rs, :] = (eye_rep * row(v_ref, ib, q)).astype(bf16)
        z = jnp.dot(lhs_scr[...], ones_blk2, preferred_element_type=f32)
        vcol = jnp.dot(vd_scr[...], ones_blk, preferred_element_type=f32)
        for ib, q in tiles:
            rs = slice((ib * nq + q) * n, (ib * nq + q + 1) * n)
            s = s_scr[rs, :] * row(w_ref, ib, q) - z[rs] * row(b_ref, ib, q) + vcol[rs] * row(k_ref, ib, q)
            s_scr[rs, :] = s
            yl_scr[rs, :] = (s * row(r_ref, ib, q)).astype(bf16)
        y = jnp.dot(yl_scr[...], ones_blk, preferred_element_type=f32)
        for ib in range(nb):
            rs = slice(ib * nq * n, (ib + 1) * nq * n)
            yt_ref[ib, 0] = jnp.where(step_lane == t, y[rs], yt_ref[ib, 0])
        return carry

    lax.fori_loop(0, tb, step, 0)

    @pl.when(ti == pl.num_programs(1) - 1)
    def _():
        for ib in range(nb):
            st_ref[ib] = s_scr[ib * nq * n:(ib + 1) * nq * n, :]


def wkv_scan(r, w, k, kk, b, v, s0, nb=2):
    bsz, t, width = r.shape
    n = RWKV_HEAD_DIM
    nh = width // n
    nq = nh // WKV_QUAD
    tb = min(WKV_TB, t)
    nblk = t // tb
    to_tiles = lambda s: s.reshape(bsz, nq, WKV_QUAD, n, n).transpose(0, 1, 3, 2, 4).reshape(bsz, nq * n, WKV_LANES)
    row = pl.BlockSpec((nb, tb, width), lambda i, j: (i, j, 0))
    st = pl.BlockSpec((nb, nq * n, WKV_LANES), lambda i, j: (i, 0, 0))
    rows_all = nb * nq * n
    yt, s_t = pl.pallas_call(
        functools.partial(_wkv_kernel, nb=nb, nq=nq, tb=tb),
        grid=(bsz // nb, nblk),
        in_specs=[row, row, row, row, row, row, st],
        out_specs=[pl.BlockSpec((nb, 1, nq * n, WKV_LANES), lambda i, j: (i, j, 0, 0)), st],
        out_shape=[jax.ShapeDtypeStruct((bsz, nblk, nq * n, WKV_LANES), f32),
                   jax.ShapeDtypeStruct((bsz, nq * n, WKV_LANES), f32)],
        scratch_shapes=[pltpu.VMEM((rows_all, WKV_LANES), f32),
                        pltpu.VMEM((rows_all, 2 * WKV_LANES), bf16),
                        pltpu.VMEM((rows_all, WKV_LANES), bf16),
                        pltpu.VMEM((rows_all, WKV_LANES), bf16)],
        compiler_params=_params("parallel", "arbitrary"),
        name="wkv_scan",
    )(r, w, k, kk, b, v, to_tiles(s0))
    y = yt.reshape(bsz, nblk, nq, n, WKV_QUAD, n).transpose(0, 1, 5, 2, 4, 3)[:, :, :tb].reshape(bsz, t, width)
    s_t = s_t.reshape(bsz, nq, n, WKV_QUAD, n).transpose(0, 1, 3, 2, 4).reshape(bsz, nh, n, n)
    return y, s_t


def _gelu_tanh(x):
    return 0.5 * x * (1.0 + jnp.tanh(0.7978845608028654 * (x + 0.044715 * x * x * x)))


def _chunk_rows(x_ref, n_chunks, row0=0, row_stride=1):
    return jnp.concatenate(
        [x_ref[pl.ds(row0 + s * row_stride, n_chunks, stride=CMP_STRIDE * row_stride), :]
         for s in range(CMP_STRIDE)], axis=1).astype(bf16)


def _compress_rows(x2, pos_ref, w1_ref, w2_ref, n_valid):
    half = CMP_STRIDE * HEAD_DIM
    n_chunks = x2.shape[0]
    pa = jnp.dot(x2, w1_ref[:half, :], preferred_element_type=f32)
    pb = jnp.dot(x2, w1_ref[half:, :], preferred_element_type=f32)
    posterm = jnp.dot(pos_ref[...], w1_ref[...], preferred_element_type=f32)[0:1, :]
    hid = pa + pltpu.roll(pb, n_chunks - 1, 0) + posterm
    out = jnp.dot(_gelu_tanh(hid).astype(bf16), w2_ref[...], preferred_element_type=f32)
    rows = lax.broadcasted_iota(jnp.int32, out.shape, 0)
    return jnp.where(rows < n_valid, out, 0.0)


def _compress_prompt_kernel(k_ref, v_ref, posk_ref, w1k_ref, w2k_ref, posv_ref, w1v_ref, w2v_ref,
                            ok_ref, ov_ref, *, n_chunks, n_valid):
    ok_ref[0, 0] = _compress_rows(_chunk_rows(k_ref.at[0], n_chunks), posk_ref, w1k_ref, w2k_ref, n_valid)
    ov_ref[0, 0] = _compress_rows(_chunk_rows(v_ref.at[0], n_chunks), posv_ref, w1v_ref, w2v_ref, n_valid)


def _cmp_weights(cmp_pos, cmp_w1, cmp_w2):
    ws = []
    for i in range(2):
        pos = jnp.broadcast_to(cmp_pos[i].reshape(1, CMP_BLOCK * HEAD_DIM), (8, CMP_BLOCK * HEAD_DIM)).astype(bf16)
        ws += [pos, cmp_w1[i].reshape(CMP_BLOCK * HEAD_DIM, HEAD_DIM).astype(bf16), cmp_w2[i].astype(bf16)]
    return ws


_CMP_WEIGHT_SHAPES = [(8, CMP_BLOCK * HEAD_DIM), (CMP_BLOCK * HEAD_DIM, HEAD_DIM), (HEAD_DIM, HEAD_DIM)] * 2


def compress_prompt(kv, cmp_pos, cmp_w1, cmp_w2):
    b, t, _ = kv.shape
    n_chunks = t // CMP_STRIDE
    n_valid = (t - CMP_BLOCK) // CMP_STRIDE + 1
    out = jax.ShapeDtypeStruct((b, NSA_KV, n_chunks, HEAD_DIM), f32)
    ospec = pl.BlockSpec((1, 1, n_chunks, HEAD_DIM), lambda i, kv: (i, kv, 0, 0))
    wspecs = [pl.BlockSpec(s, lambda i, kv: (0, 0)) for s in _CMP_WEIGHT_SHAPES]
    return pl.pallas_call(
        functools.partial(_compress_prompt_kernel, n_chunks=n_chunks, n_valid=n_valid),
        grid=(b, NSA_KV),
        in_specs=[pl.BlockSpec((1, t, HEAD_DIM), lambda i, kv: (i, 0, kv)),
                  pl.BlockSpec((1, t, HEAD_DIM), lambda i, kv: (i, 0, NSA_KV + kv))] + wspecs,
        out_specs=[ospec, ospec],
        out_shape=[out, out],
        compiler_params=_params("parallel", "parallel"),
        name="compress_prompt",
    )(kv, kv, *_cmp_weights(cmp_pos, cmp_w1, cmp_w2))


PAGE_ROWS = PAGE_SIZE * NSA_KV


def _page_specs(n_pages):
    return [pl.BlockSpec((PAGE_ROWS, HEAD_DIM), lambda i, pt, p=p: (pt[i, p], 0)) for p in range(n_pages)]


def _compress_paged_kernel(pt_ref, *refs, n_pages, n_valid):
    k_pages, v_pages = refs[:n_pages], refs[n_pages:2 * n_pages]
    posk_ref, w1k_ref, w2k_ref, posv_ref, w1v_ref, w2v_ref, ok_ref, ov_ref = refs[2 * n_pages:]
    per_page = PAGE_SIZE // CMP_STRIDE
    for kv in range(NSA_KV):
        chunks = lambda pages: jnp.concatenate([_chunk_rows(pg, per_page, kv, NSA_KV) for pg in pages], axis=0)
        ok_ref[0, kv] = _compress_rows(chunks(k_pages), posk_ref, w1k_ref, w2k_ref, n_valid)
        ov_ref[0, kv] = _compress_rows(chunks(v_pages), posv_ref, w1v_ref, w2v_ref, n_valid)


def compress_paged(pool_k, pool_v, page_table, n_valid, cmp_pos, cmp_w1, cmp_w2):
    b, n_pages = page_table.shape
    n_chunks = n_pages * PAGE_SIZE // CMP_STRIDE
    out = jax.ShapeDtypeStruct((b, NSA_KV, n_chunks, HEAD_DIM), f32)
    ospec = pl.BlockSpec((1, NSA_KV, n_chunks, HEAD_DIM), lambda i, pt: (i, 0, 0, 0))
    wspecs = [pl.BlockSpec(s, lambda i, pt: (0, 0)) for s in _CMP_WEIGHT_SHAPES]
    return pl.pallas_call(
        functools.partial(_compress_paged_kernel, n_pages=n_pages, n_valid=n_valid),
        grid_spec=pltpu.PrefetchScalarGridSpec(
            num_scalar_prefetch=1,
            grid=(b,),
            in_specs=_page_specs(n_pages) * 2 + wspecs,
            out_specs=[ospec, ospec]),
        out_shape=[out, out],
        compiler_params=_params("parallel"),
        name="compress_paged",
    )(page_table, *([pool_k] * n_pages), *([pool_v] * n_pages), *_cmp_weights(cmp_pos, cmp_w1, cmp_w2))


def _stack_heads(x):
    return jnp.concatenate([x[:, g * HEAD_DIM:(g + 1) * HEAD_DIM] for g in range(NSA_GROUP)], axis=0)


def _dot_nt(a, b):
    return lax.dot_general(a, b, (((1,), (1,)), ((), ())), preferred_element_type=f32)


def _compressed_branch(qc, ckc, cvc, pos_rows, n_cmp):
    s = _dot_nt(qc, ckc) * (HEAD_DIM ** -0.5)
    n = lax.broadcasted_iota(jnp.int32, s.shape, 1)
    vis = (n * CMP_STRIDE + (CMP_BLOCK - 1) <= pos_rows) & (n < n_cmp)
    s = jnp.where(vis, s, NEG_INF)
    m = jnp.max(s, axis=-1, keepdims=True)
    e = jnp.where(vis, jnp.exp(s - m), 0.0)
    denom = jnp.sum(e, axis=-1, keepdims=True)
    p = e / jnp.maximum(denom, 1e-30)
    return jnp.dot(p.astype(bf16), cvc, preferred_element_type=f32), p


def _select_blocks(p, tq, pos0, n_slc, n_j):
    psum = p[0:tq]
    for g in range(1, NSA_GROUP):
        psum = psum + p[g * tq:(g + 1) * tq]
    if tq < SEL_LANES:
        psum = jnp.concatenate([psum, jnp.zeros((SEL_LANES - tq, psum.shape[1]), f32)], axis=0)
    n_c = p.shape[1]
    j = lax.broadcasted_iota(jnp.int32, (n_j, n_c), 0)
    cs = lax.broadcasted_iota(jnp.int32, (n_j, n_c), 1) * CMP_STRIDE
    overlap = jnp.where((cs < j * SLC_BLOCK + SLC_BLOCK) & (cs + (CMP_BLOCK - 1) >= j * SLC_BLOCK), 1.0, 0.0)
    imp_t = lax.dot_general(overlap, psum, (((1,), (1,)), ((), ())),
                            preferred_element_type=f32, precision=lax.Precision.HIGHEST)
    j = lax.broadcasted_iota(jnp.int32, imp_t.shape, 0)
    pos_t = pos0 + lax.broadcasted_iota(jnp.int32, imp_t.shape, 1)
    cur = lax.shift_right_logical(pos_t, SLC_SHIFT)
    causal = j * SLC_BLOCK <= pos_t
    forced = (j == 0) | (j == cur) | (j == cur - 1)
    score = jnp.where(causal, jnp.where(forced, FORCE_SCORE, imp_t), -FORCE_SCORE)
    score = jnp.where(j < n_slc, score, -2.0 * FORCE_SCORE)
    rank = jnp.zeros(imp_t.shape, f32)
    for jp in range(n_slc):
        row = score[jp:jp + 1, :]
        ahead = (row > score) | ((row == score) & (j > jp))
        rank = rank + jnp.where(ahead, 1.0, 0.0)
    sel_t = jnp.where(rank < min(N_SELECT, n_slc), 1.0, 0.0)
    return sel_t.T[0:tq]


def _expand_sel(sel_rows, key0, n_keys):
    nj = sel_rows.shape[1]
    j = lax.broadcasted_iota(jnp.int32, (nj, n_keys), 0)
    kpos = key0 + lax.broadcasted_iota(jnp.int32, (nj, n_keys), 1)
    e = jnp.where(lax.shift_right_logical(kpos, SLC_SHIFT) == j, 1.0, 0.0).astype(bf16)
    return jnp.dot(sel_rows.astype(bf16), e, preferred_element_type=f32)


def _online_update(m, l, acc, s, valid, v):
    s = jnp.where(valid, s, NEG_INF)
    m_new = jnp.maximum(m, jnp.max(s, axis=-1, keepdims=True))
    alpha = jnp.exp(m - m_new)
    p = jnp.where(valid, jnp.exp(s - m_new), 0.0)
    l = alpha * l + jnp.sum(p, axis=-1, keepdims=True)
    acc = alpha * acc + jnp.dot(p.astype(bf16), v, preferred_element_type=f32)
    return m_new, l, acc


def _window_branch(qr, wk, wv, kpos0, n_keys_valid, pos_rows, n_phantom=None):
    s = _dot_nt(qr, wk) * (HEAD_DIM ** -0.5)
    lane = lax.broadcasted_iota(jnp.int32, s.shape, 1)
    kpos = kpos0 + lane
    valid = (kpos <= pos_rows) & (pos_rows - kpos < WINDOW) & (lane < n_keys_valid)
    s = jnp.where(valid, s, NEG_INF)
    m = jnp.max(s, axis=-1, keepdims=True)
    if n_phantom is not None:
        m = jnp.where(n_phantom > 0.0, jnp.maximum(m, 0.0), m)
    e = jnp.where(valid, jnp.exp(s - m), 0.0)
    denom = jnp.sum(e, axis=-1, keepdims=True)
    if n_phantom is not None:
        denom = denom + n_phantom * jnp.exp(-m)
    return jnp.dot((e / denom).astype(bf16), wv, preferred_element_type=f32)


def _gated_sum(gate, col0, tq, o_cmp, o_slc, o_win, g):
    r = slice(g * tq, (g + 1) * tq)
    c, s, w = col0[0] + g, col0[1] + g, col0[2] + g
    return gate[:, c:c + 1] * o_cmp[r] + gate[:, s:s + 1] * o_slc[r] + gate[:, w:w + 1] * o_win[r]


def _nsa_prompt_kernel(q_ref, c2_ref, s2_ref, gate_ref, ckc_ref, cvc_ref, sk_ref, sv_ref, wk_ref, wv_ref,
                       o_ref, *, n_cmp, n_slc):
    i = pl.program_id(2)
    tq = Q_BLOCK
    rows = NSA_GROUP * tq
    q0 = i * tq
    scale = HEAD_DIM ** -0.5
    q = q_ref[0]
    qc = _stack_heads(q).astype(bf16)
    qr = _rope_heads(q, c2_ref[...], s2_ref[...]).astype(bf16)
    t_row = lax.broadcasted_iota(jnp.int32, (rows, 1), 0)
    pos_rows = q0 + jnp.bitwise_and(t_row, tq - 1)

    o_cmp, p = _compressed_branch(qc, ckc_ref[0, 0].astype(bf16), cvc_ref[0, 0].astype(bf16), pos_rows, n_cmp)
    sel = _select_blocks(p, tq, q0, n_slc, n_slc)
    sel_rows = jnp.concatenate([sel] * NSA_GROUP, axis=0)

    def slc_step(c, carry):
        m, l, acc = carry
        k0 = pl.multiple_of(c * SLC_CHUNK, SLC_CHUNK)
        s = _dot_nt(qr, sk_ref[0, pl.ds(k0, SLC_CHUNK), :]) * scale
        kpos = k0 + lax.broadcasted_iota(jnp.int32, s.shape, 1)
        valid = (_expand_sel(sel_rows, k0, SLC_CHUNK) > 0.5) & (kpos <= pos_rows)
        return _online_update(m, l, acc, s, valid, sv_ref[0, pl.ds(k0, SLC_CHUNK), :])

    n_chunks = (q0 + tq - 1) // SLC_CHUNK + 1
    init = (jnp.full((rows, 1), NEG_INF, f32), jnp.zeros((rows, 1), f32), jnp.zeros((rows, HEAD_DIM), f32))
    _, l, acc = lax.fori_loop(0, n_chunks, slc_step, init)
    o_slc = acc / l

    span = WINDOW + tq
    w0 = pl.multiple_of(jnp.maximum(q0 - WINDOW, 0), tq)
    n_phantom = jnp.maximum(WINDOW - 1 - pos_rows, 0).astype(f32)
    o_win = _window_branch(qr, wk_ref[0, pl.ds(w0, span), :], wv_ref[0, pl.ds(w0, span), :], w0, span, pos_rows,
                           n_phantom)

    gate = _sigmoid(gate_ref[0, 0])
    cols = (0, NSA_GROUP, 2 * NSA_GROUP)
    for g in range(NSA_GROUP):
        o_ref[0, :, g * HEAD_DIM:(g + 1) * HEAD_DIM] = _gated_sum(gate, cols, tq, o_cmp, o_slc, o_win, g)


def nsa_prompt(q, gate, ckc, cvc, kv_bf, c2, s2, n_cmp):
    b, t, _ = q.shape
    assert t % SLC_CHUNK == 0 and t >= WINDOW + Q_BLOCK
    n_slc = t // SLC_BLOCK
    gw = NSA_GROUP * HEAD_DIM
    kvcol = lambda c: pl.BlockSpec((1, t, HEAD_DIM), lambda bi, kv, i, c=c: (bi, 0, c * NSA_KV + kv))
    cmp_spec = pl.BlockSpec((1, 1, ckc.shape[2], HEAD_DIM), lambda bi, kv, i: (bi, kv, 0, 0))
    return pl.pallas_call(
        functools.partial(_nsa_prompt_kernel, n_cmp=n_cmp, n_slc=n_slc),
        grid=(b, NSA_KV, t // Q_BLOCK),
        in_specs=[pl.BlockSpec((1, Q_BLOCK, gw), lambda bi, kv, i: (bi, i, kv)),
                  pl.BlockSpec((Q_BLOCK, HEAD_DIM), lambda bi, kv, i: (i, 0)),
                  pl.BlockSpec((Q_BLOCK, HEAD_DIM), lambda bi, kv, i: (i, 0)),
                  pl.BlockSpec((1, 1, Q_BLOCK, 3 * NSA_GROUP), lambda bi, kv, i: (bi, kv, i, 0)),
                  cmp_spec, cmp_spec, kvcol(2), kvcol(3), kvcol(4), kvcol(5)],
        out_specs=pl.BlockSpec((1, Q_BLOCK, gw), lambda bi, kv, i: (bi, i, kv)),
        out_shape=jax.ShapeDtypeStruct((b, t, NSA_WIDTH), f32),
        compiler_params=_params("parallel", "parallel", "arbitrary"),
        name="nsa_prompt",
    )(q, c2, s2, gate, ckc, cvc, kv_bf, kv_bf, kv_bf, kv_bf)


def _pad_rows(x, n):
    return jnp.concatenate([x, jnp.zeros((n - x.shape[0], x.shape[1]), x.dtype)], axis=0)


def _nsa_sample_kernel(pt_ref, *refs, n_pages, n_cmp, n_slc, n_j, past, tq):
    k_pages, v_pages = refs[:n_pages], refs[n_pages:2 * n_pages]
    q_ref, c2_ref, s2_ref, gate_ref, ckc_ref, cvc_ref, new_ref, wink_ref, winv_ref, o_ref = refs[2 * n_pages:]
    rows = NSA_GROUP * tq
    t_row = lax.broadcasted_iota(jnp.int32, (rows, 1), 0)
    pos_rows = past + jnp.bitwise_and(t_row, tq - 1)
    lw = wink_ref.shape[1]
    new_col = lambda branch, kv: slice((branch * NSA_KV + kv) * HEAD_DIM, (branch * NSA_KV + kv + 1) * HEAD_DIM)
    c2, s2 = c2_ref[...], s2_ref[...]
    gate = _sigmoid(gate_ref[0])
    for kv in range(NSA_KV):
        q = q_ref[0, :, kv * NSA_GROUP * HEAD_DIM:(kv + 1) * NSA_GROUP * HEAD_DIM]
        qc = _stack_heads(q).astype(bf16)
        qr = _rope_heads(q, c2, s2).astype(bf16)
        o_cmp, pc = _compressed_branch(qc, ckc_ref[0, kv].astype(bf16), cvc_ref[0, kv].astype(bf16), pos_rows, n_cmp)
        sel = _select_blocks(pc, tq, past, n_slc, n_j)
        sel_rows = jnp.concatenate([sel] * NSA_GROUP, axis=0)

        paged = lambda pages: [pg[pl.ds(kv, PAGE_SIZE, stride=NSA_KV), :] for pg in pages]
        sk = jnp.concatenate(paged(k_pages) + [_pad_rows(new_ref[0, :, new_col(2, kv)], LANES)], axis=0).astype(bf16)
        sv = jnp.concatenate(paged(v_pages) + [_pad_rows(new_ref[0, :, new_col(3, kv)], LANES)], axis=0).astype(bf16)
        s = _dot_nt(qr, sk) * (HEAD_DIM ** -0.5)
        kpos = lax.broadcasted_iota(jnp.int32, s.shape, 1)
        valid = (_expand_sel(sel_rows, 0, s.shape[1]) > 0.5) & (kpos <= pos_rows)
        s = jnp.where(valid, s, NEG_INF)
        e = jnp.where(valid, jnp.exp(s - jnp.max(s, axis=-1, keepdims=True)), 0.0)
        o_slc = jnp.dot((e / jnp.sum(e, axis=-1, keepdims=True)).astype(bf16), sv, preferred_element_type=f32)

        wk = jnp.concatenate([wink_ref[0, :, kv * HEAD_DIM:(kv + 1) * HEAD_DIM],
                              _pad_rows(new_ref[0, :, new_col(4, kv)], LANES)], axis=0).astype(bf16)
        wv = jnp.concatenate([winv_ref[0, :, kv * HEAD_DIM:(kv + 1) * HEAD_DIM],
                              _pad_rows(new_ref[0, :, new_col(5, kv)], LANES)], axis=0).astype(bf16)
        o_win = _window_branch(qr, wk, wv, past - lw, lw + tq, pos_rows)

        cols = tuple(br * NSA_HEADS + kv * NSA_GROUP for br in range(3))
        for g in range(NSA_GROUP):
            hd = kv * NSA_GROUP + g
            o_ref[0, :, hd * HEAD_DIM:(hd + 1) * HEAD_DIM] = _gated_sum(gate, cols, tq, o_cmp, o_slc, o_win, g)


def nsa_sample(q, gate, ckc, cvc, pool_k, pool_v, page_table, new_kv, win_k, win_v, c2, s2, n_cmp, past):
    b, tq, _ = q.shape
    n_pages = page_table.shape[1]
    assert past == n_pages * PAGE_SIZE and past % SLC_BLOCK == 0 and tq <= SLC_BLOCK
    n_slc = past // SLC_BLOCK + 1
    n_j = -(-n_slc // SLC_BLOCK) * SLC_BLOCK
    lw = win_k.shape[1]
    per_b = lambda shape: pl.BlockSpec((1,) + shape, lambda i, pt: (i,) + (0,) * len(shape))
    tab = pl.BlockSpec((tq, HEAD_DIM), lambda i, pt: (0, 0))
    return pl.pallas_call(
        functools.partial(_nsa_sample_kernel, n_pages=n_pages, n_cmp=n_cmp, n_slc=n_slc, n_j=n_j, past=past, tq=tq),
        grid_spec=pltpu.PrefetchScalarGridSpec(
            num_scalar_prefetch=1,
            grid=(b,),
            in_specs=_page_specs(n_pages) * 2 + [
                per_b((tq, NSA_WIDTH)), tab, tab, per_b((tq, 3 * NSA_HEADS)),
                per_b((NSA_KV, ckc.shape[2], HEAD_DIM)), per_b((NSA_KV, ckc.shape[2], HEAD_DIM)),
                per_b((tq, new_kv.shape[2])), per_b((lw, KV_COLS)), per_b((lw, KV_COLS))],
            out_specs=per_b((tq, NSA_WIDTH))),
        out_shape=jax.ShapeDtypeStruct((b, tq, NSA_WIDTH), f32),
        compiler_params=_params("parallel"),
        name="nsa_sample",
    )(page_table, *([pool_k] * n_pages), *([pool_v] * n_pages), q, c2, s2, gate, ckc, cvc, new_kv, win_k, win_v)


def _rope_tables(pos):
    half = HEAD_DIM // 2
    inv = jnp.power(ROPE_THETA, -jnp.arange(half, dtype=f32) / half)
    ang = pos.astype(f32)[:, None] * inv[None, :]
    cos, sin = jnp.cos(ang), jnp.sin(ang)
    return jnp.concatenate([cos, cos], axis=-1), jnp.concatenate([-sin, sin], axis=-1)


def _pad_cols(w, n):
    return jnp.pad(w, ((0, 0), (0, n - w.shape[1])))


def _pad_to_rows(x, n):
    return jnp.pad(x, ((0, n - x.shape[0]), (0, 0)))


def _shift_tokens(first, seq):
    return jnp.concatenate([first[:, None, :], seq[:, :-1]], axis=1)


def rwkv_mem_layer(h, x_prev, s0, mk, mv, P):
    b, t, d = h.shape
    m = b * t
    h2 = h.reshape(m, d)
    hn2 = rmsnorm(h2, P['g_mix_pre'])
    hn = hn2.reshape(b, t, d)
    rw = P['w0'].shape[0]
    nh = rw // RWKV_HEAD_DIM

    proj = matmul(hn2, P['w_in_a']).reshape(b, t, -1)
    rows8 = -(-b // 8) * 8
    p0 = matmul(_pad_to_rows(x_prev, rows8), P['w_in_a'])[:b]
    cur = proj[..., :3 * rw]
    prev = _shift_tokens(p0[:, :3 * rw], cur)
    rkv = cur + (prev - cur) * P['mu_rkv']
    r, k, v = jnp.split(rkv, 3, axis=-1)
    mq = proj[..., 3 * rw:]

    xx = _shift_tokens(x_prev, hn) - hn
    mu = P['mu_wag']
    lora = lambda x, w1: matmul(x.reshape(m, d), _pad_cols(w1, -(-w1.shape[1] // LANES) * LANES))
    second = lambda x, w2: matmul(x, _pad_to_rows(w2, x.shape[1])).reshape(b, t, rw)
    w_raw = P['w0'] + second(jnp.tanh(lora(hn + xx * mu[0], P['w_decay1'])), P['w_decay2'])
    decay = jnp.exp(-jnp.exp(-jax.nn.softplus(-w_raw) - 0.5))
    a_rate = jax.nn.sigmoid(P['a0'] + second(lora(hn + xx * mu[1], P['w_aaa1']), P['w_aaa2']))
    gate = second(jax.nn.sigmoid(lora(hn + xx * mu[2], P['w_gate1'])), P['w_gate2'])

    heads = lambda x: x.reshape(b, t, nh, RWKV_HEAD_DIM)
    r_h, k_h, v_h, a_h, w_h = heads(r), heads(k), heads(v), heads(a_rate), heads(decay)
    kk = k_h * P['k_k'].reshape(nh, RWKV_HEAD_DIM)
    kk = kk * lax.rsqrt(jnp.sum(kk * kk, axis=-1, keepdims=True) + 1e-12)
    k_h = k_h * (1.0 + (a_h - 1.0) * P['k_a'].reshape(nh, RWKV_HEAD_DIM))

    flat = lambda x: x.reshape(b, t, rw)
    y, s_t = wkv_scan(r, decay, flat(k_h), flat(kk), flat(kk * a_h), v, s0)
    y = heads(y)
    mean = jnp.mean(y, axis=-1, keepdims=True)
    var = jnp.mean(jnp.square(y - mean), axis=-1, keepdims=True)
    y = ((y - mean) * lax.rsqrt(var + GN_EPS)).reshape(b, t, rw)
    y = y * P['lnx_w'] + P['lnx_b']
    bonus = jnp.sum(r_h * k_h * P['r_k'], axis=-1, keepdims=True) * v_h
    o_rwkv = (y + bonus.reshape(b, t, rw)) * gate

    o_mem = mem_attention(mq, mk, mv)
    o = jnp.concatenate([o_rwkv, o_mem], axis=-1).reshape(m, d)
    h2 = out_proj_residual(o, P['w_out_a'], h2, P['g_mix_post'])
    h2 = ffn_residual(h2, P['g_ffn_pre'], P['w_ff1'], P['w_ff2'], P['g_ffn_post'])
    return h2.reshape(b, t, d), s_t, hn[:, -1]


def nsa_mem_layer(h, mk, mv, P, attend):
    b, t, d = h.shape
    m = b * t
    h2 = h.reshape(m, d)
    w_in = P['w_in_b']
    n_in = w_in.shape[1]
    proj = matmul(h2, _pad_cols(w_in, -(-n_in // 512) * 512), g=P['g_mix_pre']).reshape(b, t, -1)
    q = proj[..., :NSA_WIDTH]
    mq = proj[..., NSA_WIDTH:NSA_WIDTH + MEM_WIDTH]
    gate = proj[..., NSA_WIDTH + MEM_WIDTH:n_in]
    o_nsa = attend(q, gate)
    o_mem = mem_attention(mq, mk, mv)
    o = jnp.concatenate([o_nsa, o_mem], axis=-1).reshape(m, d)
    h2 = out_proj_residual(o, P['w_out_b'], h2, P['g_mix_post'])
    h2 = ffn_residual(h2, P['g_ffn_pre'], P['w_ff1'], P['w_ff2'], P['g_ffn_post'])
    return h2.reshape(b, t, d)


def _split_kv_rows(kv):
    b, t, _ = kv.shape
    rows = kv.reshape(b, t, 6, NSA_KV, HEAD_DIM)
    return [rows[:, :, i] for i in range(6)]


def kernel(x_prompt, x_sample, mem_prompt, cache_mem_k, cache_mem_v, state_wkv, state_shift, cache_cmp_k, cache_cmp_v, cache_slc_k, cache_slc_v, cache_win_k, cache_win_v, page_table, g_mix_pre, g_mix_post, g_ffn_pre, g_ffn_post, g_mem, w_mem_k, w_mem_v, w_in_a, mu_rkv, mu_wag, w0, w_decay1, w_decay2, a0, w_aaa1, w_aaa2, w_gate1, w_gate2, k_k, k_a, r_k, lnx_w, lnx_b, w_out_a, g_kv, w_kv, cmp_pos, cmp_w1, cmp_w2, w_in_b, w_out_b, w_ff1, w_ff2):
    bp, tp, d = x_prompt.shape
    bs, ts, _ = x_sample.shape
    depth = g_mix_pre.shape[0]
    assert depth == 2 and w_in_a.shape[0] == 1 and w_in_b.shape[0] == 1
    n_pages = page_table.shape[1]
    past = n_pages * PAGE_SIZE
    mem_len = mem_prompt.shape[1]

    P0 = dict(g_mix_pre=g_mix_pre[0], g_mix_post=g_mix_post[0], g_ffn_pre=g_ffn_pre[0], g_ffn_post=g_ffn_post[0],
              w_in_a=w_in_a[0], mu_rkv=mu_rkv[0], mu_wag=mu_wag[0], w0=w0[0], w_decay1=w_decay1[0],
              w_decay2=w_decay2[0], a0=a0[0], w_aaa1=w_aaa1[0], w_aaa2=w_aaa2[0], w_gate1=w_gate1[0],
              w_gate2=w_gate2[0], k_k=k_k[0], k_a=k_a[0], r_k=r_k[0], lnx_w=lnx_w[0], lnx_b=lnx_b[0],
              w_out_a=w_out_a[0], w_ff1=w_ff1[0], w_ff2=w_ff2[0])
    P1 = dict(g_mix_pre=g_mix_pre[1], g_mix_post=g_mix_post[1], g_ffn_pre=g_ffn_pre[1], g_ffn_post=g_ffn_post[1],
              w_in_b=w_in_b[0], w_out_b=w_out_b[0], w_ff1=w_ff1[1], w_ff2=w_ff2[1])

    mem2 = mem_prompt.reshape(bp * mem_len, d)
    mkv = [matmul(mem2, jnp.concatenate([w_mem_k[l], w_mem_v[l]], axis=1), g=g_mem[l]).reshape(bp, mem_len, -1)
           for l in range(depth)]
    mem_k_p = jnp.stack([x[..., :MEM_WIDTH] for x in mkv])
    mem_v_p = jnp.stack([x[..., MEM_WIDTH:] for x in mkv])

    nh = w0.shape[1] // RWKV_HEAD_DIM
    shift0 = jnp.zeros((bp, d), f32)
    wkv0 = jnp.zeros((bp, nh, RWKV_HEAD_DIM, RWKV_HEAD_DIM), f32)
    h, wkv_p, shift_p = rwkv_mem_layer(x_prompt, shift0, wkv0, mem_k_p[0], mem_v_p[0], P0)

    c2p, s2p = _rope_tables(jnp.arange(tp, dtype=jnp.int32))
    kv_p = kv_proj(h.reshape(bp * tp, d), g_kv, w_kv, c2p, s2p).reshape(bp, tp, -1)
    ckc, cvc = compress_prompt(kv_p, cmp_pos, cmp_w1, cmp_w2)
    n_cmp_p = (tp - CMP_BLOCK) // CMP_STRIDE + 1

    def attend_prompt(q, gate):
        g = gate.reshape(bp, tp, 3, NSA_KV, NSA_GROUP).transpose(0, 3, 1, 2, 4).reshape(bp, NSA_KV, tp, 3 * NSA_GROUP)
        return nsa_prompt(q, g, ckc, cvc, kv_p.astype(bf16), c2p, s2p, n_cmp_p)

    y_p = nsa_mem_layer(h, mem_k_p[1], mem_v_p[1], P1, attend_prompt)
    cmp_k_p, cmp_v_p, slc_k_p, slc_v_p, win_k_p, win_v_p = _split_kv_rows(kv_p)
    n_keep = min(WINDOW, tp)
    win_k_p, win_v_p = win_k_p[:, tp - n_keep:], win_v_p[:, tp - n_keep:]

    mk_s = cache_mem_k.reshape(depth, bs, mem_len, MEM_WIDTH)
    mv_s = cache_mem_v.reshape(depth, bs, mem_len, MEM_WIDTH)
    h, wkv_s, shift_s = rwkv_mem_layer(x_sample, state_shift[0], state_wkv[0], mk_s[0], mv_s[0], P0)

    c2s, s2s = _rope_tables(past + jnp.arange(ts, dtype=jnp.int32))
    kv_s = kv_proj(h.reshape(bs * ts, d), g_kv, w_kv, jnp.tile(c2s, (bs, 1)), jnp.tile(s2s, (bs, 1)))
    kv_s = kv_s.reshape(bs, ts, -1)
    n_cmp_s = (past + ts - CMP_BLOCK) // CMP_STRIDE + 1
    assert (n_cmp_s - 1) * CMP_STRIDE + CMP_BLOCK <= past
    n_pool = cache_cmp_k.shape[0]
    pool = lambda x: x.reshape(n_pool * PAGE_ROWS, HEAD_DIM)
    ckc_s, cvc_s = compress_paged(pool(cache_cmp_k), pool(cache_cmp_v), page_table, n_cmp_s, cmp_pos, cmp_w1, cmp_w2)
    lw = cache_win_k.shape[1]
    win_k2, win_v2 = cache_win_k.reshape(bs, lw, KV_COLS), cache_win_v.reshape(bs, lw, KV_COLS)

    def attend_sample(q, gate):
        return nsa_sample(q, gate, ckc_s, cvc_s, pool(cache_slc_k), pool(cache_slc_v), page_table, kv_s,
                          win_k2, win_v2, c2s, s2s, n_cmp_s, past)

    y_s = nsa_mem_layer(h, mk_s[1], mv_s[1], P1, attend_sample)
    cmp_k_s, cmp_v_s, slc_k_s, slc_v_s, wk_new, wv_new = _split_kv_rows(kv_s)
    win_k_s = jnp.concatenate([cache_win_k, wk_new], axis=1)[:, ts:]
    win_v_s = jnp.concatenate([cache_win_v, wv_new], axis=1)[:, ts:]

    return (y_p, y_s, mem_k_p.reshape(depth, bp, mem_len, MEM_HEADS, HEAD_DIM),
            mem_v_p.reshape(depth, bp, mem_len, MEM_HEADS, HEAD_DIM),
            wkv_p[None], shift_p[None], cmp_k_p, cmp_v_p, slc_k_p, slc_v_p, win_k_p, win_v_p,
            wkv_s[None], shift_s[None], cmp_k_s, cmp_v_s, slc_k_s, slc_v_s, win_k_s, win_v_s)
```

```python
import functools

import jax
import jax.numpy as jnp
from jax import lax
from jax.experimental import pallas as pl
from jax.experimental.pallas import tpu as pltpu

f32 = jnp.float32
bf16 = jnp.bfloat16

LANES = 128
VMEM_LIMIT_BYTES = 56 * 1024 * 1024

HEAD_DIM = 128
MEM_HEADS = 4
MEM_WIDTH = MEM_HEADS * HEAD_DIM
RWKV_HEAD_DIM = 64
GN_EPS = 64e-5
NSA_KV = 2
NSA_GROUP = 6
NSA_HEADS = NSA_KV * NSA_GROUP
NSA_WIDTH = NSA_HEADS * HEAD_DIM
KV_COLS = NSA_KV * HEAD_DIM
CMP_BLOCK = 32
CMP_STRIDE = 16
SLC_BLOCK = 64
SLC_SHIFT = 6
N_SELECT = 16
WINDOW = 512
Q_BLOCK = 128
ROPE_THETA = 10000.0
NORM_EPS = 1e-6
NEG_INF = -1e30
FORCE_SCORE = 1e9
PAGE_SIZE = 128

ROW_TILE = 512
SLC_CHUNK = 512
SEL_LANES = 128


def _params(*sem):
    return pltpu.CompilerParams(dimension_semantics=sem, vmem_limit_bytes=VMEM_LIMIT_BYTES)


def _rms(x, g):
    return x * lax.rsqrt(jnp.mean(x * x, axis=-1, keepdims=True) + NORM_EPS) * g


def _sigmoid(x):
    return 1.0 / (1.0 + jnp.exp(-x))


def _rmsnorm_kernel(x_ref, g_ref, o_ref):
    o_ref[...] = _rms(x_ref[...], g_ref[...])


def rmsnorm(x, g):
    m, d = x.shape
    tm = min(ROW_TILE, m)
    return pl.pallas_call(
        _rmsnorm_kernel,
        grid=(m // tm,),
        in_specs=[pl.BlockSpec((tm, d), lambda i: (i, 0)), pl.BlockSpec((1, d), lambda i: (0, 0))],
        out_specs=pl.BlockSpec((tm, d), lambda i: (i, 0)),
        out_shape=jax.ShapeDtypeStruct((m, d), f32),
        compiler_params=_params("parallel"),
        name="rmsnorm",
    )(x, g.reshape(1, d))


def _mm_kernel(x_ref, g_ref, w_ref, o_ref, xn_ref, *, norm):
    @pl.when(pl.program_id(1) == 0)
    def _():
        x = x_ref[...]
        if norm:
            x = _rms(x, g_ref[...])
        xn_ref[...] = x.astype(bf16)

    o_ref[...] = jnp.dot(xn_ref[...], w_ref[...], preferred_element_type=f32)


def matmul(x, w, g=None, tn=512):
    m, k = x.shape
    n = w.shape[1]
    tm = min(ROW_TILE, m)
    tn = min(tn, n)
    assert m % tm == 0 and n % tn == 0, (m, n, tm, tn)
    gg = jnp.ones((1, k), f32) if g is None else g.reshape(1, k)
    return pl.pallas_call(
        functools.partial(_mm_kernel, norm=g is not None),
        grid=(m // tm, n // tn),
        in_specs=[pl.BlockSpec((tm, k), lambda i, j: (i, 0)),
                  pl.BlockSpec((1, k), lambda i, j: (0, 0)),
                  pl.BlockSpec((k, tn), lambda i, j: (0, j))],
        out_specs=pl.BlockSpec((tm, tn), lambda i, j: (i, j)),
        out_shape=jax.ShapeDtypeStruct((m, n), f32),
        scratch_shapes=[pltpu.VMEM((tm, k), bf16)],
        compiler_params=_params("parallel", "arbitrary"),
        name="matmul",
    )(x, gg, w.astype(bf16))


def _out_proj_kernel(oa_ref, ob_ref, w_ref, h_ref, g_ref, y_ref):
    ka = oa_ref.shape[1]
    acc = jnp.dot(oa_ref[...].astype(bf16), w_ref[:ka, :], preferred_element_type=f32)
    acc += jnp.dot(ob_ref[...].astype(bf16), w_ref[ka:, :], preferred_element_type=f32)
    y_ref[...] = h_ref[...] + _rms(acc, g_ref[...])


def out_proj_residual(oa, ob, ob_col, w, h, g):
    m, ka = oa.shape
    d = w.shape[1]
    kb = w.shape[0] - ka
    tm = min(ROW_TILE, m)
    return pl.pallas_call(
        _out_proj_kernel,
        grid=(m // tm,),
        in_specs=[pl.BlockSpec((tm, ka), lambda i: (i, 0)),
                  pl.BlockSpec((tm, kb), lambda i: (i, ob_col)),
                  pl.BlockSpec((ka + kb, d), lambda i: (0, 0)),
                  pl.BlockSpec((tm, d), lambda i: (i, 0)),
                  pl.BlockSpec((1, d), lambda i: (0, 0))],
        out_specs=pl.BlockSpec((tm, d), lambda i: (i, 0)),
        out_shape=jax.ShapeDtypeStruct((m, d), f32),
        compiler_params=_params("parallel"),
        name="out_proj",
    )(oa, ob, w.astype(bf16), h, g.reshape(1, d))


def _ffn_kernel(h_ref, gpre_ref, w1_ref, w2_ref, gpost_ref, y_ref, xn_ref, acc_ref):
    j = pl.program_id(1)

    @pl.when(j == 0)
    def _():
        xn_ref[...] = _rms(h_ref[...], gpre_ref[...]).astype(bf16)
        acc_ref[...] = jnp.zeros_like(acc_ref)

    u = jnp.dot(xn_ref[...], w1_ref[...], preferred_element_type=f32)
    u = jnp.square(jnp.maximum(u, 0.0))
    acc_ref[...] += jnp.dot(u.astype(bf16), w2_ref[...], preferred_element_type=f32)

    @pl.when(j == pl.num_programs(1) - 1)
    def _():
        y_ref[...] = h_ref[...] + _rms(acc_ref[...], gpost_ref[...])


def ffn_residual(h, g_pre, w1, w2, g_post, tf=512):
    m, d = h.shape
    dff = w1.shape[1]
    tm = min(ROW_TILE, m)
    return pl.pallas_call(
        _ffn_kernel,
        grid=(m // tm, dff // tf),
        in_specs=[pl.BlockSpec((tm, d), lambda i, j: (i, 0)),
                  pl.BlockSpec((1, d), lambda i, j: (0, 0)),
                  pl.BlockSpec((d, tf), lambda i, j: (0, j)),
                  pl.BlockSpec((tf, d), lambda i, j: (j, 0)),
                  pl.BlockSpec((1, d), lambda i, j: (0, 0))],
        out_specs=pl.BlockSpec((tm, d), lambda i, j: (i, 0)),
        out_shape=jax.ShapeDtypeStruct((m, d), f32),
        scratch_shapes=[pltpu.VMEM((tm, d), bf16), pltpu.VMEM((tm, d), f32)],
        compiler_params=_params("parallel", "arbitrary"),
        name="ffn",
    )(h, g_pre.reshape(1, d), w1.astype(bf16), w2.astype(bf16), g_post.reshape(1, d))


def _rope_tile(x, c2, s2):
    return x * c2 + pltpu.roll(x, HEAD_DIM // 2, 1) * s2


def _rope_heads(q, c2, s2):
    return jnp.concatenate([_rope_tile(q[:, g * HEAD_DIM:(g + 1) * HEAD_DIM], c2, s2)
                            for g in range(NSA_GROUP)], axis=0)


N_KV_BRANCH = 6


def _kv_proj_kernel(h_ref, g_ref, w_ref, c2_ref, s2_ref, *refs):
    outs, bf_ref, xn_ref = refs[:N_KV_BRANCH], refs[N_KV_BRANCH], refs[N_KV_BRANCH + 1]
    j = pl.program_id(1)

    @pl.when(j == 0)
    def _():
        xn_ref[...] = _rms(h_ref[...], g_ref[...]).astype(bf16)

    acc = jnp.dot(xn_ref[...], w_ref[...], preferred_element_type=f32)
    for br in range(N_KV_BRANCH):
        @pl.when(j == br)
        def _(br=br):
            if br in (2, 4):
                c2, s2 = c2_ref[...], s2_ref[...]
                val = jnp.concatenate([_rope_tile(acc[:, kv * HEAD_DIM:(kv + 1) * HEAD_DIM], c2, s2)
                                       for kv in range(NSA_KV)], axis=1)
            else:
                val = acc
            outs[br][...] = val
            bf_ref[...] = val.astype(bf16)


def kv_proj(h, g, w, c2, s2):
    m, d = h.shape
    n = w.shape[1]
    assert n == N_KV_BRANCH * KV_COLS
    tm = min(ROW_TILE, m)
    ntab = c2.shape[0] // tm
    res = pl.pallas_call(
        _kv_proj_kernel,
        grid=(m // tm, N_KV_BRANCH),
        in_specs=[pl.BlockSpec((tm, d), lambda i, j: (i, 0)),
                  pl.BlockSpec((1, d), lambda i, j: (0, 0)),
                  pl.BlockSpec((d, KV_COLS), lambda i, j: (0, j)),
                  pl.BlockSpec((tm, HEAD_DIM), lambda i, j: (i % ntab, 0)),
                  pl.BlockSpec((tm, HEAD_DIM), lambda i, j: (i % ntab, 0))],
        out_specs=[pl.BlockSpec((tm, KV_COLS), lambda i, j: (i, 0))] * N_KV_BRANCH
                  + [pl.BlockSpec((tm, KV_COLS), lambda i, j: (i, j))],
        out_shape=[jax.ShapeDtypeStruct((m, KV_COLS), f32)] * N_KV_BRANCH + [jax.ShapeDtypeStruct((m, n), bf16)],
        scratch_shapes=[pltpu.VMEM((tm, d), bf16)],
        compiler_params=_params("parallel", "arbitrary"),
        name="kv_proj",
    )(h, g.reshape(1, d), w.astype(bf16), c2, s2)
    return res[:N_KV_BRANCH], res[N_KV_BRANCH]


def _mem_attn_kernel(q_ref, k_ref, v_ref, o_ref):
    scale = HEAD_DIM ** -0.5
    for hd in range(MEM_HEADS):
        sl = slice(hd * HEAD_DIM, (hd + 1) * HEAD_DIM)
        q = q_ref[0, :, sl].astype(bf16)
        k = k_ref[0, :, sl].astype(bf16)
        s = lax.dot_general(q, k, (((1,), (1,)), ((), ())), preferred_element_type=f32) * scale
        s = s - jnp.max(s, axis=-1, keepdims=True)
        e = jnp.exp(s)
        p = e / jnp.sum(e, axis=-1, keepdims=True)
        o_ref[0, :, sl] = jnp.dot(p.astype(bf16), v_ref[0, :, sl].astype(bf16), preferred_element_type=f32)


def mem_attention(q, mk, mv, q_col=0):
    b, t, _ = q.shape
    w = MEM_WIDTH
    mlen = mk.shape[1]
    tq = min(ROW_TILE, t)
    return pl.pallas_call(
        _mem_attn_kernel,
        grid=(b, t // tq),
        in_specs=[pl.BlockSpec((1, tq, w), lambda i, j: (i, j, q_col)),
                  pl.BlockSpec((1, mlen, w), lambda i, j: (i, 0, 0)),
                  pl.BlockSpec((1, mlen, w), lambda i, j: (i, 0, 0))],
        out_specs=pl.BlockSpec((1, tq, w), lambda i, j: (i, j, 0)),
        out_shape=jax.ShapeDtypeStruct((b, t, w), f32),
        compiler_params=_params("parallel", "arbitrary"),
        name="mem_attn",
    )(q, mk, mv)


WKV_QUAD = 4
WKV_LANES = WKV_QUAD * RWKV_HEAD_DIM
WKV_TB = 64


def _wkv_kernel(r_ref, w_ref, k_ref, kk_ref, b_ref, v_ref, s0_ref, yt_ref, st_ref,
                s_scr, lhs_scr, vd_scr, yl_scr, *, nb, nq, tb):
    n = RWKV_HEAD_DIM
    ti = pl.program_id(1)

    @pl.when(ti == 0)
    def _():
        for ib in range(nb):
            s_scr[ib * nq * n:(ib + 1) * nq * n, :] = s0_ref[ib]

    yt_ref[...] = jnp.zeros(yt_ref.shape, f32)
    ri = lax.broadcasted_iota(jnp.int32, (WKV_LANES, WKV_LANES), 0)
    ci = lax.broadcasted_iota(jnp.int32, (WKV_LANES, WKV_LANES), 1)
    ones_blk = jnp.where(lax.shift_right_logical(ri, 6) == lax.shift_right_logical(ci, 6), 1.0, 0.0).astype(bf16)
    ones_blk2 = jnp.concatenate([ones_blk, ones_blk], axis=0)
    eye_rep = jnp.where(lax.broadcasted_iota(jnp.int32, (n, WKV_LANES), 0)
                        == jnp.bitwise_and(lax.broadcasted_iota(jnp.int32, (n, WKV_LANES), 1), n - 1), 1.0, 0.0)
    step_lane = jnp.bitwise_and(lax.broadcasted_iota(jnp.int32, (nq * n, WKV_LANES), 1), n - 1)
    tiles = [(ib, q) for ib in range(nb) for q in range(nq)]

    def step(t, carry):
        row = lambda ref, ib, q: ref[ib, pl.ds(t, 1), q * WKV_LANES:(q + 1) * WKV_LANES]
        for ib, q in tiles:
            rs = slice((ib * nq + q) * n, (ib * nq + q + 1) * n)
            prod = s_scr[rs, :] * row(kk_ref, ib, q)
            hi = prod.astype(bf16)
            lhs_scr[rs, 0:WKV_LANES] = hi
            lhs_scr[rs, WKV_LANES:2 * WKV_LANES] = (prod - hi.astype(f32)).astype(bf16)
            vd_scr[rs, :] = (eye_rep * row(v_ref, ib, q)).astype(bf16)
        z = jnp.dot(lhs_scr[...], ones_blk2, preferred_element_type=f32)
        vcol = jnp.dot(vd_scr[...], ones_blk, preferred_element_type=f32)
        for ib, q in tiles:
            rs = slice((ib * nq + q) * n, (ib * nq + q + 1) * n)
            s = s_scr[rs, :] * row(w_ref, ib, q) - z[rs] * row(b_ref, ib, q) + vcol[rs] * row(k_ref, ib, q)
            s_scr[rs, :] = s
            yl_scr[rs, :] = (s * row(r_ref, ib, q)).astype(bf16)
        y = jnp.dot(yl_scr[...], ones_blk, preferred_element_type=f32)
        for ib in range(nb):
            rs = slice(ib * nq * n, (ib + 1) * nq * n)
            yt_ref[ib, 0] = jnp.where(step_lane == t, y[rs], yt_ref[ib, 0])
        return carry

    lax.fori_loop(0, tb, step, 0, unroll=4)

    @pl.when(ti == pl.num_programs(1) - 1)
    def _():
        for ib in range(nb):
            st_ref[ib] = s_scr[ib * nq * n:(ib + 1) * nq * n, :]


def wkv_scan(r, w, k, kk, b, v, s0, nb=2):
    bsz, t, width = r.shape
    n = RWKV_HEAD_DIM
    nh = width // n
    nq = nh // WKV_QUAD
    tb = min(WKV_TB, t)
    nblk = t // tb
    to_tiles = lambda s: s.reshape(bsz, nq, WKV_QUAD, n, n).transpose(0, 1, 3, 2, 4).reshape(bsz, nq * n, WKV_LANES)
    row = pl.BlockSpec((nb, tb, width), lambda i, j: (i, j, 0))
    st = pl.BlockSpec((nb, nq * n, WKV_LANES), lambda i, j: (i, 0, 0))
    rows_all = nb * nq * n
    yt, s_t = pl.pallas_call(
        functools.partial(_wkv_kernel, nb=nb, nq=nq, tb=tb),
        grid=(bsz // nb, nblk),
        in_specs=[row, row, row, row, row, row, st],
        out_specs=[pl.BlockSpec((nb, 1, nq * n, WKV_LANES), lambda i, j: (i, j, 0, 0)), st],
        out_shape=[jax.ShapeDtypeStruct((bsz, nblk, nq * n, WKV_LANES), f32),
                   jax.ShapeDtypeStruct((bsz, nq * n, WKV_LANES), f32)],
        scratch_shapes=[pltpu.VMEM((rows_all, WKV_LANES), f32),
                        pltpu.VMEM((rows_all, 2 * WKV_LANES), bf16),
                        pltpu.VMEM((rows_all, WKV_LANES), bf16),
                        pltpu.VMEM((rows_all, WKV_LANES), bf16)],
        compiler_params=_params("parallel", "arbitrary"),
        name="wkv_scan",
    )(r, w, k, kk, b, v, to_tiles(s0))
    y = yt.reshape(bsz, nblk, nq, n, WKV_QUAD, n).transpose(0, 1, 5, 2, 4, 3)[:, :, :tb].reshape(bsz, t, width)
    s_t = s_t.reshape(bsz, nq, n, WKV_QUAD, n).transpose(0, 1, 3, 2, 4).reshape(bsz, nh, n, n)
    return y, s_t


def _gelu_tanh(x):
    return 0.5 * x * (1.0 + jnp.tanh(0.7978845608028654 * (x + 0.044715 * x * x * x)))


def _chunk_rows(x_ref, n_chunks, row0=0, row_stride=1):
    return jnp.concatenate(
        [x_ref[pl.ds(row0 + s * row_stride, n_chunks, stride=CMP_STRIDE * row_stride), :]
         for s in range(CMP_STRIDE)], axis=1).astype(bf16)


def _compress_rows(x2, pos_ref, w1_ref, w2_ref, n_valid):
    half = CMP_STRIDE * HEAD_DIM
    n_chunks = x2.shape[0]
    pa = jnp.dot(x2, w1_ref[:half, :], preferred_element_type=f32)
    pb = jnp.dot(x2, w1_ref[half:, :], preferred_element_type=f32)
    posterm = jnp.dot(pos_ref[...], w1_ref[...], preferred_element_type=f32)[0:1, :]
    hid = pa + pltpu.roll(pb, n_chunks - 1, 0) + posterm
    out = jnp.dot(_gelu_tanh(hid).astype(bf16), w2_ref[...], preferred_element_type=f32)
    rows = lax.broadcasted_iota(jnp.int32, out.shape, 0)
    return jnp.where(rows < n_valid, out, 0.0)


def _compress_prompt_kernel(k_ref, v_ref, posk_ref, w1k_ref, w2k_ref, posv_ref, w1v_ref, w2v_ref,
                            ok_ref, ov_ref, *, n_chunks, n_valid):
    ok_ref[0, 0] = _compress_rows(_chunk_rows(k_ref.at[0], n_chunks), posk_ref, w1k_ref, w2k_ref, n_valid)
    ov_ref[0, 0] = _compress_rows(_chunk_rows(v_ref.at[0], n_chunks), posv_ref, w1v_ref, w2v_ref, n_valid)


def _cmp_weights(cmp_pos, cmp_w1, cmp_w2):
    ws = []
    for i in range(2):
        pos = jnp.broadcast_to(cmp_pos[i].reshape(1, CMP_BLOCK * HEAD_DIM), (8, CMP_BLOCK * HEAD_DIM)).astype(bf16)
        ws += [pos, cmp_w1[i].reshape(CMP_BLOCK * HEAD_DIM, HEAD_DIM).astype(bf16), cmp_w2[i].astype(bf16)]
    return ws


_CMP_WEIGHT_SHAPES = [(8, CMP_BLOCK * HEAD_DIM), (CMP_BLOCK * HEAD_DIM, HEAD_DIM), (HEAD_DIM, HEAD_DIM)] * 2


def compress_prompt(ck, cv, cmp_pos, cmp_w1, cmp_w2):
    b, t, _ = ck.shape
    n_chunks = t // CMP_STRIDE
    n_valid = (t - CMP_BLOCK) // CMP_STRIDE + 1
    out = jax.ShapeDtypeStruct((b, NSA_KV, n_chunks, HEAD_DIM), f32)
    ospec = pl.BlockSpec((1, 1, n_chunks, HEAD_DIM), lambda i, kv: (i, kv, 0, 0))
    wspecs = [pl.BlockSpec(s, lambda i, kv: (0, 0)) for s in _CMP_WEIGHT_SHAPES]
    return pl.pallas_call(
        functools.partial(_compress_prompt_kernel, n_chunks=n_chunks, n_valid=n_valid),
        grid=(b, NSA_KV),
        in_specs=[pl.BlockSpec((1, t, HEAD_DIM), lambda i, kv: (i, 0, kv)),
                  pl.BlockSpec((1, t, HEAD_DIM), lambda i, kv: (i, 0, kv))] + wspecs,
        out_specs=[ospec, ospec],
        out_shape=[out, out],
        compiler_params=_params("parallel", "parallel"),
        name="compress_prompt",
    )(ck, cv, *_cmp_weights(cmp_pos, cmp_w1, cmp_w2))


PAGE_ROWS = PAGE_SIZE * NSA_KV


def _page_specs(n_pages):
    return [pl.BlockSpec((PAGE_ROWS, HEAD_DIM), lambda i, pt, p=p: (pt[i, p], 0)) for p in range(n_pages)]


def _compress_paged_kernel(pt_ref, *refs, n_pages, n_valid):
    k_pages, v_pages = refs[:n_pages], refs[n_pages:2 * n_pages]
    posk_ref, w1k_ref, w2k_ref, posv_ref, w1v_ref, w2v_ref, ok_ref, ov_ref = refs[2 * n_pages:]
    per_page = PAGE_SIZE // CMP_STRIDE
    for kv in range(NSA_KV):
        chunks = lambda pages: jnp.concatenate([_chunk_rows(pg, per_page, kv, NSA_KV) for pg in pages], axis=0)
        ok_ref[0, kv] = _compress_rows(chunks(k_pages), posk_ref, w1k_ref, w2k_ref, n_valid)
        ov_ref[0, kv] = _compress_rows(chunks(v_pages), posv_ref, w1v_ref, w2v_ref, n_valid)


def compress_paged(pool_k, pool_v, page_table, n_valid, cmp_pos, cmp_w1, cmp_w2):
    b, n_pages = page_table.shape
    n_chunks = n_pages * PAGE_SIZE // CMP_STRIDE
    out = jax.ShapeDtypeStruct((b, NSA_KV, n_chunks, HEAD_DIM), f32)
    ospec = pl.BlockSpec((1, NSA_KV, n_chunks, HEAD_DIM), lambda i, pt: (i, 0, 0, 0))
    wspecs = [pl.BlockSpec(s, lambda i, pt: (0, 0)) for s in _CMP_WEIGHT_SHAPES]
    return pl.pallas_call(
        functools.partial(_compress_paged_kernel, n_pages=n_pages, n_valid=n_valid),
        grid_spec=pltpu.PrefetchScalarGridSpec(
            num_scalar_prefetch=1,
            grid=(b,),
            in_specs=_page_specs(n_pages) * 2 + wspecs,
            out_specs=[ospec, ospec]),
        out_shape=[out, out],
        compiler_params=_params("parallel"),
        name="compress_paged",
    )(page_table, *([pool_k] * n_pages), *([pool_v] * n_pages), *_cmp_weights(cmp_pos, cmp_w1, cmp_w2))


def _stack_heads(x):
    return jnp.concatenate([x[:, g * HEAD_DIM:(g + 1) * HEAD_DIM] for g in range(NSA_GROUP)], axis=0)


def _dot_nt(a, b):
    return lax.dot_general(a, b, (((1,), (1,)), ((), ())), preferred_element_type=f32)


def _compressed_branch(qc, ckc, cvc, pos_rows, n_cmp):
    s = _dot_nt(qc, ckc)
    n = lax.broadcasted_iota(jnp.int32, s.shape, 1)
    vis = (n * CMP_STRIDE + (CMP_BLOCK - 1) <= pos_rows) & (n < n_cmp)
    s = jnp.where(vis, s, NEG_INF)
    m = jnp.max(s, axis=-1, keepdims=True)
    e = jnp.where(vis, jnp.exp(s - m), 0.0)
    denom = jnp.sum(e, axis=-1, keepdims=True)
    p = e / jnp.maximum(denom, 1e-30)
    return jnp.dot(p.astype(bf16), cvc, preferred_element_type=f32), p


def _select_blocks(p, tq, pos0, n_slc, n_j):
    psum = p[0:tq]
    for g in range(1, NSA_GROUP):
        psum = psum + p[g * tq:(g + 1) * tq]
    if tq < SEL_LANES:
        psum = jnp.concatenate([psum, jnp.zeros((SEL_LANES - tq, psum.shape[1]), f32)], axis=0)
    n_c = p.shape[1]
    j = lax.broadcasted_iota(jnp.int32, (n_j, n_c), 0)
    cs = lax.broadcasted_iota(jnp.int32, (n_j, n_c), 1) * CMP_STRIDE
    overlap = jnp.where((cs < j * SLC_BLOCK + SLC_BLOCK) & (cs + (CMP_BLOCK - 1) >= j * SLC_BLOCK), 1.0, 0.0)
    imp_t = lax.dot_general(overlap, psum, (((1,), (1,)), ((), ())),
                            preferred_element_type=f32, precision=lax.Precision.HIGHEST)
    j = lax.broadcasted_iota(jnp.int32, imp_t.shape, 0)
    pos_t = pos0 + lax.broadcasted_iota(jnp.int32, imp_t.shape, 1)
    cur = lax.shift_right_logical(pos_t, SLC_SHIFT)
    causal = j * SLC_BLOCK <= pos_t
    forced = (j == 0) | (j == cur) | (j == cur - 1)
    score = jnp.where(causal, jnp.where(forced, FORCE_SCORE, imp_t), -FORCE_SCORE)
    score = jnp.where(j < n_slc, score, -2.0 * FORCE_SCORE)
    rank = jnp.zeros(imp_t.shape, f32)
    for jp in range(n_slc):
        row = score[jp:jp + 1, :]
        ahead = (row > score) | ((row == score) & (j > jp))
        rank = rank + jnp.where(ahead, 1.0, 0.0)
    sel_t = jnp.where(rank < min(N_SELECT, n_slc), 1.0, 0.0)
    return sel_t.T[0:tq]


def _selection_bias(sel_rows, key0, n_keys):
    nj = sel_rows.shape[1]
    j = lax.broadcasted_iota(jnp.int32, (nj, n_keys), 0)
    kpos = key0 + lax.broadcasted_iota(jnp.int32, (nj, n_keys), 1)
    e = jnp.where(lax.shift_right_logical(kpos, SLC_SHIFT) == j, 1.0, 0.0).astype(bf16)
    return jnp.dot(jnp.where(sel_rows > 0.5, 0.0, NEG_INF).astype(bf16), e, preferred_element_type=f32)


def _window_branch(qr, wk, wv, kpos0, n_keys_valid, pos_rows, n_phantom=None):
    s = _dot_nt(qr, wk)
    lane = lax.broadcasted_iota(jnp.int32, s.shape, 1)
    kpos = kpos0 + lane
    valid = (kpos <= pos_rows) & (pos_rows - kpos < WINDOW) & (lane < n_keys_valid)
    s = jnp.where(valid, s, NEG_INF)
    m = jnp.max(s, axis=-1, keepdims=True)
    if n_phantom is not None:
        m = jnp.where(n_phantom > 0.0, jnp.maximum(m, 0.0), m)
    e = jnp.where(valid, jnp.exp(s - m), 0.0)
    denom = jnp.sum(e, axis=-1, keepdims=True)
    if n_phantom is not None:
        denom = denom + n_phantom * jnp.exp(-m)
    return jnp.dot((e / denom).astype(bf16), wv, preferred_element_type=f32)


def _gated_sum(gate, cols, tq, o_cmp, o_slc, o_win, g):
    r = slice(g * tq, (g + 1) * tq)
    c, s, w = cols[0] + g, cols[1] + g, cols[2] + g
    return gate[:, c:c + 1] * o_cmp[r] + gate[:, s:s + 1] * o_slc[r] + gate[:, w:w + 1] * o_win[r]


QK_SCALE = HEAD_DIM ** -0.5


def _nsa_prompt_kernel(q_ref, c2_ref, s2_ref, gate_ref, ckc_ref, cvc_ref, sk_ref, sv_ref, wk_ref, wv_ref,
                       o_ref, *, n_cmp, n_slc):
    i = pl.program_id(2)
    tq = Q_BLOCK
    rows = NSA_GROUP * tq
    q0 = i * tq
    q = q_ref[0] * QK_SCALE
    qc = _stack_heads(q).astype(bf16)
    qr = _rope_heads(q, c2_ref[...], s2_ref[...]).astype(bf16)
    t_row = lax.broadcasted_iota(jnp.int32, (rows, 1), 0)
    pos_rows = q0 + jnp.bitwise_and(t_row, tq - 1)

    o_cmp, p = _compressed_branch(qc, ckc_ref[0, 0].astype(bf16), cvc_ref[0, 0].astype(bf16), pos_rows, n_cmp)
    sel = _select_blocks(p, tq, q0, n_slc, n_slc)
    sel_rows = jnp.concatenate([sel] * NSA_GROUP, axis=0)

    def slc_step(c, carry, causal):
        m, l, acc = carry
        k0 = pl.multiple_of(c * SLC_CHUNK, SLC_CHUNK)
        s = _dot_nt(qr, sk_ref[0, pl.ds(k0, SLC_CHUNK), :]) + _selection_bias(sel_rows, k0, SLC_CHUNK)
        if causal:
            kpos = k0 + lax.broadcasted_iota(jnp.int32, s.shape, 1)
            s = jnp.where(kpos <= pos_rows, s, NEG_INF)
        m_new = jnp.maximum(m, jnp.max(s, axis=-1, keepdims=True))
        alpha = jnp.exp(m - m_new)
        e = jnp.exp(s - m_new)
        l = alpha * l + jnp.sum(e, axis=-1, keepdims=True)
        acc = alpha * acc + jnp.dot(e.astype(bf16), sv_ref[0, pl.ds(k0, SLC_CHUNK), :], preferred_element_type=f32)
        return m_new, l, acc

    c_last = q0 // SLC_CHUNK
    init = (jnp.full((rows, 1), NEG_INF, f32), jnp.zeros((rows, 1), f32), jnp.zeros((rows, HEAD_DIM), f32))
    carry = lax.fori_loop(0, c_last, functools.partial(slc_step, causal=False), init)
    _, l, acc = slc_step(c_last, carry, causal=True)
    o_slc = acc / l

    span = WINDOW + tq
    w0 = pl.multiple_of(jnp.maximum(q0 - WINDOW, 0), tq)
    n_phantom = jnp.maximum(WINDOW - 1 - pos_rows, 0).astype(f32)
    o_win = _window_branch(qr, wk_ref[0, pl.ds(w0, span), :], wv_ref[0, pl.ds(w0, span), :], w0, span, pos_rows,
                           n_phantom)

    gate = _sigmoid(gate_ref[0])
    cols = (0, NSA_GROUP, 2 * NSA_GROUP)
    for g in range(NSA_GROUP):
        o_ref[0, :, g * HEAD_DIM:(g + 1) * HEAD_DIM] = _gated_sum(gate, cols, tq, o_cmp, o_slc, o_win, g)


def nsa_prompt(proj, gate_col, ckc, cvc, kv_bf, c2, s2, n_cmp):
    b, t, _ = proj.shape
    assert t % SLC_CHUNK == 0 and t >= WINDOW + Q_BLOCK
    n_slc = t // SLC_BLOCK
    gw = NSA_GROUP * HEAD_DIM
    kvcol = lambda c: pl.BlockSpec((1, t, HEAD_DIM), lambda bi, kv, i, c=c: (bi, 0, c * NSA_KV + kv))
    cmp_spec = pl.BlockSpec((1, 1, ckc.shape[2], HEAD_DIM), lambda bi, kv, i: (bi, kv, 0, 0))
    return pl.pallas_call(
        functools.partial(_nsa_prompt_kernel, n_cmp=n_cmp, n_slc=n_slc),
        grid=(b, NSA_KV, t // Q_BLOCK),
        in_specs=[pl.BlockSpec((1, Q_BLOCK, gw), lambda bi, kv, i: (bi, i, kv)),
                  pl.BlockSpec((Q_BLOCK, HEAD_DIM), lambda bi, kv, i: (i, 0)),
                  pl.BlockSpec((Q_BLOCK, HEAD_DIM), lambda bi, kv, i: (i, 0)),
                  pl.BlockSpec((1, Q_BLOCK, LANES), lambda bi, kv, i: (bi, i, gate_col + kv)),
                  cmp_spec, cmp_spec, kvcol(2), kvcol(3), kvcol(4), kvcol(5)],
        out_specs=pl.BlockSpec((1, Q_BLOCK, gw), lambda bi, kv, i: (bi, i, kv)),
        out_shape=jax.ShapeDtypeStruct((b, t, NSA_WIDTH), f32),
        compiler_params=_params("parallel", "parallel", "arbitrary"),
        name="nsa_prompt",
    )(proj, c2, s2, proj, ckc, cvc, kv_bf, kv_bf, kv_bf, kv_bf)


def _pad_rows(x, n):
    return jnp.concatenate([x, jnp.zeros((n - x.shape[0], x.shape[1]), x.dtype)], axis=0)


def _nsa_sample_kernel(pt_ref, *refs, n_pages, n_cmp, n_slc, n_j, past, tq):
    k_pages, v_pages = refs[:n_pages], refs[n_pages:2 * n_pages]
    (q_ref, c2_ref, s2_ref, gate_ref, ckc_ref, cvc_ref, nsk_ref, nsv_ref, nwk_ref, nwv_ref,
     wink_ref, winv_ref, o_ref) = refs[2 * n_pages:]
    rows = NSA_GROUP * tq
    t_row = lax.broadcasted_iota(jnp.int32, (rows, 1), 0)
    pos_rows = past + jnp.bitwise_and(t_row, tq - 1)
    lw = wink_ref.shape[1]
    c2, s2 = c2_ref[...], s2_ref[...]
    gate = _sigmoid(gate_ref[0])
    for kv in range(NSA_KV):
        ksl = slice(kv * HEAD_DIM, (kv + 1) * HEAD_DIM)
        q = q_ref[0, :, kv * NSA_GROUP * HEAD_DIM:(kv + 1) * NSA_GROUP * HEAD_DIM] * QK_SCALE
        qc = _stack_heads(q).astype(bf16)
        qr = _rope_heads(q, c2, s2).astype(bf16)
        o_cmp, pc = _compressed_branch(qc, ckc_ref[0, kv].astype(bf16), cvc_ref[0, kv].astype(bf16), pos_rows, n_cmp)
        sel = _select_blocks(pc, tq, past, n_slc, n_j)
        sel_rows = jnp.concatenate([sel] * NSA_GROUP, axis=0)

        paged = lambda pages: [pg[pl.ds(kv, PAGE_SIZE, stride=NSA_KV), :] for pg in pages]
        sk = jnp.concatenate(paged(k_pages) + [_pad_rows(nsk_ref[0, :, ksl], LANES)], axis=0).astype(bf16)
        sv = jnp.concatenate(paged(v_pages) + [_pad_rows(nsv_ref[0, :, ksl], LANES)], axis=0).astype(bf16)
        s = _dot_nt(qr, sk) + _selection_bias(sel_rows, 0, sk.shape[0])
        kpos = lax.broadcasted_iota(jnp.int32, s.shape, 1)
        s = jnp.where(kpos <= pos_rows, s, NEG_INF)
        e = jnp.exp(s - jnp.max(s, axis=-1, keepdims=True))
        o_slc = jnp.dot((e / jnp.sum(e, axis=-1, keepdims=True)).astype(bf16), sv, preferred_element_type=f32)

        wk = jnp.concatenate([wink_ref[0, :, ksl], _pad_rows(nwk_ref[0, :, ksl], LANES)], axis=0).astype(bf16)
        wv = jnp.concatenate([winv_ref[0, :, ksl], _pad_rows(nwv_ref[0, :, ksl], LANES)], axis=0).astype(bf16)
        o_win = _window_branch(qr, wk, wv, past - lw, lw + tq, pos_rows)

        cols = tuple(kv * LANES + br * NSA_GROUP for br in range(3))
        for g in range(NSA_GROUP):
            hd = kv * NSA_GROUP + g
            o_ref[0, :, hd * HEAD_DIM:(hd + 1) * HEAD_DIM] = _gated_sum(gate, cols, tq, o_cmp, o_slc, o_win, g)


def nsa_sample(proj, gate_col, ckc, cvc, pool_k, pool_v, page_table, new_rows, win_k, win_v, c2, s2, n_cmp, past):
    b, tq, _ = proj.shape
    n_pages = page_table.shape[1]
    assert past == n_pages * PAGE_SIZE and past % SLC_BLOCK == 0 and tq <= SLC_BLOCK
    n_slc = past // SLC_BLOCK + 1
    n_j = -(-n_slc // SLC_BLOCK) * SLC_BLOCK
    lw = win_k.shape[1]
    per_b = lambda shape: pl.BlockSpec((1,) + shape, lambda i, pt: (i,) + (0,) * len(shape))
    tab = pl.BlockSpec((tq, HEAD_DIM), lambda i, pt: (0, 0))
    return pl.pallas_call(
        functools.partial(_nsa_sample_kernel, n_pages=n_pages, n_cmp=n_cmp, n_slc=n_slc, n_j=n_j, past=past, tq=tq),
        grid_spec=pltpu.PrefetchScalarGridSpec(
            num_scalar_prefetch=1,
            grid=(b,),
            in_specs=_page_specs(n_pages) * 2 + [
                per_b((tq, NSA_WIDTH)), tab, tab,
                pl.BlockSpec((1, tq, NSA_KV * LANES), lambda i, pt: (i, 0, gate_col // NSA_KV)),
                per_b((NSA_KV, ckc.shape[2], HEAD_DIM)), per_b((NSA_KV, ckc.shape[2], HEAD_DIM))]
                + [per_b((tq, KV_COLS))] * 4 + [per_b((lw, KV_COLS)), per_b((lw, KV_COLS))],
            out_specs=per_b((tq, NSA_WIDTH))),
        out_shape=jax.ShapeDtypeStruct((b, tq, NSA_WIDTH), f32),
        compiler_params=_params("parallel"),
        name="nsa_sample",
    )(page_table, *([pool_k] * n_pages), *([pool_v] * n_pages), proj, c2, s2, proj, ckc, cvc, *new_rows, win_k, win_v)


def _prev_rows(x, tile, halo_ref, first_ref, seq_len):
    tm, c = x.shape
    prev = pltpu.roll(x, 1, 0)
    row = lax.broadcasted_iota(jnp.int32, (tm, 1), 0)
    if seq_len >= tm:
        tiles_per_seq = seq_len // tm
        first = first_ref[pl.ds(tile // tiles_per_seq, 1), :]
        edge = jnp.where(tile % tiles_per_seq == 0, first, halo_ref[7:8, :])
        return jnp.where(row == 0, edge, prev)
    pieces = []
    for j in range(tm // seq_len):
        pieces.append(jnp.broadcast_to(first_ref[j:j + 1, :], (8, c)))
        if seq_len > 8:
            pieces.append(jnp.zeros((seq_len - 8, c), f32))
    return jnp.where(jnp.bitwise_and(row, seq_len - 1) == 0, jnp.concatenate(pieces, axis=0), prev)


def _shift_specs(m, c, tm, seq_len, n_seq, col=None):
    cb = (lambda *g: 0) if col is None else col
    tile = pl.BlockSpec((tm, c), lambda *g: (g[0], cb(*g)))
    halo = pl.BlockSpec((8, c), lambda *g: (jnp.maximum(g[0] * (tm // 8) - 1, 0), cb(*g)))
    if seq_len >= tm:
        first = pl.BlockSpec((-(-n_seq // 8) * 8, c), lambda *g: (0, cb(*g)))
    else:
        first = pl.BlockSpec((tm // seq_len, c), lambda *g: (g[0], cb(*g)))
    return tile, halo, first


def _pad_first(first, seq_len, tm):
    return _pad_to_rows(first, -(-first.shape[0] // 8) * 8) if seq_len >= tm else first


def _head_sums(x):
    ri = lax.broadcasted_iota(jnp.int32, (WKV_LANES, WKV_LANES), 0)
    ci = lax.broadcasted_iota(jnp.int32, (WKV_LANES, WKV_LANES), 1)
    ones_blk = jnp.where(lax.shift_right_logical(ri, 6) == lax.shift_right_logical(ci, 6), 1.0, 0.0).astype(bf16)
    hi = x.astype(bf16)
    lo = (x - hi.astype(f32)).astype(bf16)
    out = []
    for c in range(x.shape[1] // WKV_LANES):
        sl = slice(c * WKV_LANES, (c + 1) * WKV_LANES)
        out.append(jnp.dot(hi[:, sl], ones_blk, preferred_element_type=f32)
                   + jnp.dot(lo[:, sl], ones_blk, preferred_element_type=f32))
    return jnp.concatenate(out, axis=1)


def _lora_kernel(hn_ref, halo_ref, first_ref, mu_ref, wd1_ref, wa1_ref, wg1_ref, wd2_ref, wa2_ref, wg2_ref,
                 w0_ref, a0_ref, decay_ref, a_ref, g_ref, *, seq_len):
    hn = hn_ref[...]
    xx = _prev_rows(hn, pl.program_id(0), halo_ref, first_ref, seq_len) - hn
    mix = lambda r: (hn + xx * mu_ref[r:r + 1, :]).astype(bf16)
    dot = lambda x, w_ref: jnp.dot(x, w_ref[...], preferred_element_type=f32)
    w_raw = w0_ref[...] + dot(jnp.tanh(dot(mix(0), wd1_ref)).astype(bf16), wd2_ref)
    softplus = jnp.maximum(-w_raw, 0.0) + jnp.log(1.0 + jnp.exp(-jnp.abs(w_raw)))
    decay_ref[...] = jnp.exp(-jnp.exp(-softplus - 0.5))
    a_ref[...] = _sigmoid(a0_ref[...] + dot(dot(mix(1), wa1_ref).astype(bf16), wa2_ref))
    g_ref[...] = dot(_sigmoid(dot(mix(2), wg1_ref)).astype(bf16), wg2_ref)


def rwkv_lora(hn, x_prev, seq_len, P):
    m, d = hn.shape
    rw = P['w0'].shape[0]
    tm = min(ROW_TILE, m)
    pad128 = lambda w: _pad_cols(w, -(-w.shape[1] // LANES) * LANES).astype(bf16)
    w1s = [pad128(P[k]) for k in ('w_decay1', 'w_aaa1', 'w_gate1')]
    w2s = [_pad_to_rows(P[k], w1.shape[1]).astype(bf16) for k, w1 in zip(('w_decay2', 'w_aaa2', 'w_gate2'), w1s)]
    full = lambda x: pl.BlockSpec(x.shape, lambda i: (0, 0))
    mu = _pad_to_rows(P['mu_wag'], 8)
    vecs = [P['w0'].reshape(1, rw), P['a0'].reshape(1, rw)]
    out = jax.ShapeDtypeStruct((m, rw), f32)
    ospec = pl.BlockSpec((tm, rw), lambda i: (i, 0))
    return pl.pallas_call(
        functools.partial(_lora_kernel, seq_len=seq_len),
        grid=(m // tm,),
        in_specs=list(_shift_specs(m, d, tm, seq_len, x_prev.shape[0])) + [full(mu)]
                 + [full(w) for w in w1s + w2s + vecs],
        out_specs=[ospec] * 3,
        out_shape=[out] * 3,
        compiler_params=_params("parallel"),
        name="rwkv_lora",
    )(hn, hn, _pad_first(x_prev, seq_len, tm), mu, *w1s, *w2s, *vecs)


def _prep_kernel(p_ref, halo_ref, first_ref, mu_ref, a_ref, kkw_ref, kaw_ref,
                 r_ref, k_ref, kk_ref, b_ref, v_ref, *, seq_len):
    j = pl.program_id(1)
    cur = p_ref[...]
    x = cur + (_prev_rows(cur, pl.program_id(0), halo_ref, first_ref, seq_len) - cur) * mu_ref[...]

    @pl.when(j == 0)
    def _():
        r_ref[...] = x

    @pl.when(j == 1)
    def _():
        a = a_ref[...]
        kk = x * kkw_ref[...]
        kk = kk * lax.rsqrt(_head_sums(kk * kk) + 1e-12)
        kk_ref[...] = kk
        b_ref[...] = kk * a
        k_ref[...] = x * (1.0 + (a - 1.0) * kaw_ref[...])

    @pl.when(j == 2)
    def _():
        v_ref[...] = x


def rwkv_prep(proj, p0, a_rate, seq_len, P):
    m = proj.shape[0]
    rw = a_rate.shape[1]
    tm = min(ROW_TILE, m)
    col = lambda i, j: j
    row1 = lambda v: v.reshape(1, -1)
    out = jax.ShapeDtypeStruct((m, rw), f32)
    ospec = pl.BlockSpec((tm, rw), lambda i, j: (i, 0))
    return pl.pallas_call(
        functools.partial(_prep_kernel, seq_len=seq_len),
        grid=(m // tm, 3),
        in_specs=list(_shift_specs(m, rw, tm, seq_len, p0.shape[0], col))
                 + [pl.BlockSpec((1, rw), lambda i, j: (0, j)),
                    pl.BlockSpec((tm, rw), lambda i, j: (i, 0)),
                    pl.BlockSpec((1, rw), lambda i, j: (0, 0)),
                    pl.BlockSpec((1, rw), lambda i, j: (0, 0))],
        out_specs=[ospec] * 5,
        out_shape=[out] * 5,
        compiler_params=_params("parallel", "arbitrary"),
        name="rwkv_prep",
    )(proj, proj, _pad_first(p0, seq_len, tm), row1(P['mu_rkv']), a_rate, row1(P['k_k']), row1(P['k_a']))


def _rwkv_out_kernel(y_ref, r_ref, k_ref, v_ref, g_ref, om_ref, h_ref, lw_ref, lb_ref, rk_ref, w_ref, gp_ref, o_ref):
    inv_n = 1.0 / RWKV_HEAD_DIM
    y = y_ref[...]
    d = y - _head_sums(y) * inv_n
    var = _head_sums(d * d) * inv_n
    yn = d * lax.rsqrt(var + GN_EPS) * lw_ref[...] + lb_ref[...]
    bonus = _head_sums(r_ref[...] * k_ref[...] * rk_ref[...]) * v_ref[...]
    o = ((yn + bonus) * g_ref[...]).astype(bf16)
    rw = o.shape[1]
    acc = jnp.dot(o, w_ref[:rw, :], preferred_element_type=f32)
    acc += jnp.dot(om_ref[...].astype(bf16), w_ref[rw:, :], preferred_element_type=f32)
    o_ref[...] = h_ref[...] + _rms(acc, gp_ref[...])


def rwkv_out(y, r, k, v, gate, o_mem, h, P):
    m, rw = y.shape
    d = h.shape[1]
    tm = min(ROW_TILE // 2, m)
    wide = pl.BlockSpec((tm, rw), lambda i: (i, 0))
    vec = pl.BlockSpec((1, rw), lambda i: (0, 0))
    return pl.pallas_call(
        _rwkv_out_kernel,
        grid=(m // tm,),
        in_specs=[wide] * 5 + [pl.BlockSpec((tm, o_mem.shape[1]), lambda i: (i, 0)),
                               pl.BlockSpec((tm, d), lambda i: (i, 0)), vec, vec, vec,
                               pl.BlockSpec((rw + o_mem.shape[1], d), lambda i: (0, 0)),
                               pl.BlockSpec((1, d), lambda i: (0, 0))],
        out_specs=pl.BlockSpec((tm, d), lambda i: (i, 0)),
        out_shape=jax.ShapeDtypeStruct((m, d), f32),
        compiler_params=_params("parallel"),
        name="rwkv_out",
    )(y, r, k, v, gate, o_mem, h, P['lnx_w'].reshape(1, rw), P['lnx_b'].reshape(1, rw), P['r_k'].reshape(1, rw),
      P['w_out_a'].astype(bf16), P['g_mix_post'].reshape(1, d))


def _rope_tables(pos):
    half = HEAD_DIM // 2
    inv = jnp.power(ROPE_THETA, -jnp.arange(half, dtype=f32) / half)
    ang = pos.astype(f32)[:, None] * inv[None, :]
    cos, sin = jnp.cos(ang), jnp.sin(ang)
    return jnp.concatenate([cos, cos], axis=-1), jnp.concatenate([-sin, sin], axis=-1)


def _pad_cols(w, n):
    return jnp.pad(w, ((0, 0), (0, n - w.shape[1])))


def _pad_to_rows(x, n):
    return jnp.pad(x, ((0, n - x.shape[0]), (0, 0)))


def rwkv_mem_layer(h, x_prev, s0, mk, mv, P):
    b, t, d = h.shape
    m = b * t
    assert t & (t - 1) == 0 and t % 8 == 0 and (t % ROW_TILE == 0 or ROW_TILE % t == 0)
    h2 = h.reshape(m, d)
    hn = rmsnorm(h2, P['g_mix_pre'])
    rw = P['w0'].shape[0]

    proj = matmul(hn, P['w_in_a'])
    p0 = matmul(_pad_to_rows(x_prev, -(-b // 8) * 8), P['w_in_a'])[:b]
    decay, a_rate, gate = rwkv_lora(hn, x_prev, t, P)
    r, k, kk, kb, v = rwkv_prep(proj, p0, a_rate, t, P)
    as3 = lambda x: x.reshape(b, t, -1)
    y, s_t = wkv_scan(as3(r), as3(decay), as3(k), as3(kk), as3(kb), as3(v), s0)
    o_mem = mem_attention(as3(proj), mk, mv, q_col=3 * rw // MEM_WIDTH)
    h2 = rwkv_out(y.reshape(m, rw), r, k, v, gate, o_mem.reshape(m, MEM_WIDTH), h2, P)
    h2 = ffn_residual(h2, P['g_ffn_pre'], P['w_ff1'], P['w_ff2'], P['g_ffn_post'])
    return h2.reshape(b, t, d), s_t, hn.reshape(b, t, d)[:, -1]


GATE_COL = (NSA_WIDTH + MEM_WIDTH) // LANES


def _nsa_in_weights(w_in):
    base = NSA_WIDTH + MEM_WIDTH
    gates = w_in[:, base:base + 3 * NSA_HEADS].reshape(-1, 3, NSA_KV, NSA_GROUP)
    blocks = [_pad_cols(gates[:, :, kv].reshape(-1, 3 * NSA_GROUP), LANES) for kv in range(NSA_KV)]
    return jnp.concatenate([w_in[:, :base]] + blocks, axis=1)


def nsa_mem_layer(h, mk, mv, P, attend):
    b, t, d = h.shape
    m = b * t
    h2 = h.reshape(m, d)
    w_in = _nsa_in_weights(P['w_in_b'])
    proj = matmul(h2, w_in, g=P['g_mix_pre'], tn=w_in.shape[1] // 3).reshape(b, t, -1)
    o_nsa = attend(proj)
    o_mem = mem_attention(proj, mk, mv, q_col=NSA_WIDTH // MEM_WIDTH)
    h2 = out_proj_residual(o_nsa.reshape(m, NSA_WIDTH), o_mem.reshape(m, MEM_WIDTH), 0, P['w_out_b'], h2,
                           P['g_mix_post'])
    h2 = ffn_residual(h2, P['g_ffn_pre'], P['w_ff1'], P['w_ff2'], P['g_ffn_post'])
    return h2.reshape(b, t, d)


def kernel(x_prompt, x_sample, mem_prompt, cache_mem_k, cache_mem_v, state_wkv, state_shift, cache_cmp_k, cache_cmp_v, cache_slc_k, cache_slc_v, cache_win_k, cache_win_v, page_table, g_mix_pre, g_mix_post, g_ffn_pre, g_ffn_post, g_mem, w_mem_k, w_mem_v, w_in_a, mu_rkv, mu_wag, w0, w_decay1, w_decay2, a0, w_aaa1, w_aaa2, w_gate1, w_gate2, k_k, k_a, r_k, lnx_w, lnx_b, w_out_a, g_kv, w_kv, cmp_pos, cmp_w1, cmp_w2, w_in_b, w_out_b, w_ff1, w_ff2):
    bp, tp, d = x_prompt.shape
    bs, ts, _ = x_sample.shape
    depth = g_mix_pre.shape[0]
    assert depth == 2 and w_in_a.shape[0] == 1 and w_in_b.shape[0] == 1
    n_pages = page_table.shape[1]
    past = n_pages * PAGE_SIZE
    mem_len = mem_prompt.shape[1]

    P0 = dict(g_mix_pre=g_mix_pre[0], g_mix_post=g_mix_post[0], g_ffn_pre=g_ffn_pre[0], g_ffn_post=g_ffn_post[0],
              w_in_a=w_in_a[0], mu_rkv=mu_rkv[0], mu_wag=mu_wag[0], w0=w0[0], w_decay1=w_decay1[0],
              w_decay2=w_decay2[0], a0=a0[0], w_aaa1=w_aaa1[0], w_aaa2=w_aaa2[0], w_gate1=w_gate1[0],
              w_gate2=w_gate2[0], k_k=k_k[0], k_a=k_a[0], r_k=r_k[0], lnx_w=lnx_w[0], lnx_b=lnx_b[0],
              w_out_a=w_out_a[0], w_ff1=w_ff1[0], w_ff2=w_ff2[0])
    P1 = dict(g_mix_pre=g_mix_pre[1], g_mix_post=g_mix_post[1], g_ffn_pre=g_ffn_pre[1], g_ffn_post=g_ffn_post[1],
              w_in_b=w_in_b[0], w_out_b=w_out_b[0], w_ff1=w_ff1[1], w_ff2=w_ff2[1])
    rows4 = lambda x, bsz: x.reshape(bsz, -1, NSA_KV, HEAD_DIM)

    mem2 = mem_prompt.reshape(bp * mem_len, d)
    mkv = [matmul(mem2, jnp.concatenate([w_mem_k[l], w_mem_v[l]], axis=1), g=g_mem[l]).reshape(bp, mem_len, -1)
           for l in range(depth)]
    mem_k_p = jnp.stack([x[..., :MEM_WIDTH] for x in mkv])
    mem_v_p = jnp.stack([x[..., MEM_WIDTH:] for x in mkv])

    nh = w0.shape[1] // RWKV_HEAD_DIM
    shift0 = jnp.zeros((bp, d), f32)
    wkv0 = jnp.zeros((bp, nh, RWKV_HEAD_DIM, RWKV_HEAD_DIM), f32)
    h, wkv_p, shift_p = rwkv_mem_layer(x_prompt, shift0, wkv0, mem_k_p[0], mem_v_p[0], P0)

    c2p, s2p = _rope_tables(jnp.arange(tp, dtype=jnp.int32))
    rows_p, kv_bf = kv_proj(h.reshape(bp * tp, d), g_kv, w_kv, c2p, s2p)
    as_p = lambda x: x.reshape(bp, tp, -1)
    ckc, cvc = compress_prompt(as_p(rows_p[0]), as_p(rows_p[1]), cmp_pos, cmp_w1, cmp_w2)
    n_cmp_p = (tp - CMP_BLOCK) // CMP_STRIDE + 1

    def attend_prompt(proj):
        return nsa_prompt(proj, GATE_COL, ckc, cvc, as_p(kv_bf), c2p, s2p, n_cmp_p)

    y_p = nsa_mem_layer(h, mem_k_p[1], mem_v_p[1], P1, attend_prompt)
    cmp_k_p, cmp_v_p, slc_k_p, slc_v_p, win_k_p, win_v_p = [rows4(x, bp) for x in rows_p]
    n_keep = min(WINDOW, tp)
    win_k_p, win_v_p = win_k_p[:, tp - n_keep:], win_v_p[:, tp - n_keep:]

    mk_s = cache_mem_k.reshape(depth, bs, mem_len, MEM_WIDTH)
    mv_s = cache_mem_v.reshape(depth, bs, mem_len, MEM_WIDTH)
    h, wkv_s, shift_s = rwkv_mem_layer(x_sample, state_shift[0], state_wkv[0], mk_s[0], mv_s[0], P0)

    c2s, s2s = _rope_tables(past + jnp.arange(ts, dtype=jnp.int32))
    rows_s, _ = kv_proj(h.reshape(bs * ts, d), g_kv, w_kv, jnp.tile(c2s, (bs, 1)), jnp.tile(s2s, (bs, 1)))
    as_s = lambda x: x.reshape(bs, ts, -1)
    n_cmp_s = (past + ts - CMP_BLOCK) // CMP_STRIDE + 1
    assert (n_cmp_s - 1) * CMP_STRIDE + CMP_BLOCK <= past
    n_pool = cache_cmp_k.shape[0]
    pool = lambda x: x.reshape(n_pool * PAGE_ROWS, HEAD_DIM)
    ckc_s, cvc_s = compress_paged(pool(cache_cmp_k), pool(cache_cmp_v), page_table, n_cmp_s, cmp_pos, cmp_w1, cmp_w2)
    lw = cache_win_k.shape[1]
    win_k2, win_v2 = cache_win_k.reshape(bs, lw, KV_COLS), cache_win_v.reshape(bs, lw, KV_COLS)

    def attend_sample(proj):
        return nsa_sample(proj, GATE_COL, ckc_s, cvc_s, pool(cache_slc_k), pool(cache_slc_v), page_table,
                          [as_s(x) for x in rows_s[2:]], win_k2, win_v2, c2s, s2s, n_cmp_s, past)

    y_s = nsa_mem_layer(h, mk_s[1], mv_s[1], P1, attend_sample)
    cmp_k_s, cmp_v_s, slc_k_s, slc_v_s, wk_new, wv_new = [rows4(x, bs) for x in rows_s]
    win_k_s = jnp.concatenate([cache_win_k, wk_new], axis=1)[:, ts:]
    win_v_s = jnp.concatenate([cache_win_v, wv_new], axis=1)[:, ts:]

    return (y_p, y_s, mem_k_p.reshape(depth, bp, mem_len, MEM_HEADS, HEAD_DIM),
            mem_v_p.reshape(depth, bp, mem_len, MEM_HEADS, HEAD_DIM),
            wkv_p[None], shift_p[None], cmp_k_p, cmp_v_p, slc_k_p, slc_v_p, win_k_p, win_v_p,
            wkv_s[None], shift_s[None], cmp_k_s, cmp_v_s, slc_k_s, slc_v_s, win_k_s, win_v_s)
```

```python
import functools

import jax
import jax.numpy as jnp
from jax import lax
from jax.experimental import pallas as pl
from jax.experimental.pallas import tpu as pltpu

f32 = jnp.float32
bf16 = jnp.bfloat16

LANES = 128
VMEM_LIMIT_BYTES = 56 * 1024 * 1024

HEAD_DIM = 128
MEM_HEADS = 4
MEM_WIDTH = MEM_HEADS * HEAD_DIM
RWKV_HEAD_DIM = 64
GN_EPS = 64e-5
NSA_KV = 2
NSA_GROUP = 6
NSA_HEADS = NSA_KV * NSA_GROUP
NSA_WIDTH = NSA_HEADS * HEAD_DIM
KV_COLS = NSA_KV * HEAD_DIM
CMP_BLOCK = 32
CMP_STRIDE = 16
SLC_BLOCK = 64
SLC_SHIFT = 6
N_SELECT = 16
WINDOW = 512
Q_BLOCK = 128
ROPE_THETA = 10000.0
NORM_EPS = 1e-6
NEG_INF = -1e30
FORCE_SCORE = 1e9
PAGE_SIZE = 128

ROW_TILE = 512
FFN_ROW_TILE = 1024
SLC_CHUNK = 512
SEL_LANES = 128


def _params(*sem):
    return pltpu.CompilerParams(dimension_semantics=sem, vmem_limit_bytes=VMEM_LIMIT_BYTES)


def _rms(x, g):
    return x * lax.rsqrt(jnp.mean(x * x, axis=-1, keepdims=True) + NORM_EPS) * g


def _sigmoid(x):
    return 1.0 / (1.0 + jnp.exp(-x))


def _rmsnorm_kernel(x_ref, g_ref, o_ref):
    o_ref[...] = _rms(x_ref[...], g_ref[...])


def rmsnorm(x, g):
    m, d = x.shape
    tm = min(ROW_TILE, m)
    return pl.pallas_call(
        _rmsnorm_kernel,
        grid=(m // tm,),
        in_specs=[pl.BlockSpec((tm, d), lambda i: (i, 0)), pl.BlockSpec((1, d), lambda i: (0, 0))],
        out_specs=pl.BlockSpec((tm, d), lambda i: (i, 0)),
        out_shape=jax.ShapeDtypeStruct((m, d), f32),
        compiler_params=_params("parallel"),
        name="rmsnorm",
    )(x, g.reshape(1, d))


def _mm_kernel(x_ref, g_ref, w_ref, o_ref, xn_ref, *, norm):
    @pl.when(pl.program_id(1) == 0)
    def _():
        x = x_ref[...]
        if norm:
            x = _rms(x, g_ref[...])
        xn_ref[...] = x.astype(bf16)

    o_ref[...] = jnp.dot(xn_ref[...], w_ref[...], preferred_element_type=f32)


def matmul(x, w, g=None, tn=512):
    m, k = x.shape
    n = w.shape[1]
    tm = min(ROW_TILE, m)
    tn = min(tn, n)
    assert m % tm == 0 and n % tn == 0, (m, n, tm, tn)
    gg = jnp.ones((1, k), f32) if g is None else g.reshape(1, k)
    return pl.pallas_call(
        functools.partial(_mm_kernel, norm=g is not None),
        grid=(m // tm, n // tn),
        in_specs=[pl.BlockSpec((tm, k), lambda i, j: (i, 0)),
                  pl.BlockSpec((1, k), lambda i, j: (0, 0)),
                  pl.BlockSpec((k, tn), lambda i, j: (0, j))],
        out_specs=pl.BlockSpec((tm, tn), lambda i, j: (i, j)),
        out_shape=jax.ShapeDtypeStruct((m, n), f32),
        scratch_shapes=[pltpu.VMEM((tm, k), bf16)],
        compiler_params=_params("parallel", "arbitrary"),
        name="matmul",
    )(x, gg, w.astype(bf16))


def _out_proj_kernel(oa_ref, ob_ref, w_ref, h_ref, g_ref, y_ref):
    ka = oa_ref.shape[1]
    acc = jnp.dot(oa_ref[...].astype(bf16), w_ref[:ka, :], preferred_element_type=f32)
    acc += jnp.dot(ob_ref[...].astype(bf16), w_ref[ka:, :], preferred_element_type=f32)
    y_ref[...] = h_ref[...] + _rms(acc, g_ref[...])


def out_proj_residual(oa, ob, ob_col, w, h, g):
    m, ka = oa.shape
    d = w.shape[1]
    kb = w.shape[0] - ka
    tm = min(ROW_TILE, m)
    return pl.pallas_call(
        _out_proj_kernel,
        grid=(m // tm,),
        in_specs=[pl.BlockSpec((tm, ka), lambda i: (i, 0)),
                  pl.BlockSpec((tm, kb), lambda i: (i, ob_col)),
                  pl.BlockSpec((ka + kb, d), lambda i: (0, 0)),
                  pl.BlockSpec((tm, d), lambda i: (i, 0)),
                  pl.BlockSpec((1, d), lambda i: (0, 0))],
        out_specs=pl.BlockSpec((tm, d), lambda i: (i, 0)),
        out_shape=jax.ShapeDtypeStruct((m, d), f32),
        compiler_params=_params("parallel"),
        name="out_proj",
    )(oa, ob, w.astype(bf16), h, g.reshape(1, d))


def _ffn_kernel(h_ref, gpre_ref, w1_ref, w2_ref, gpost_ref, y_ref, xn_ref):
    j = pl.program_id(1)

    @pl.when(j == 0)
    def _():
        xn_ref[...] = _rms(h_ref[...], gpre_ref[...]).astype(bf16)
        y_ref[...] = jnp.zeros_like(y_ref)

    u = jnp.dot(xn_ref[...], w1_ref[...], preferred_element_type=f32)
    u = jnp.square(jnp.maximum(u, 0.0))
    y_ref[...] += jnp.dot(u.astype(bf16), w2_ref[...], preferred_element_type=f32)

    @pl.when(j == pl.num_programs(1) - 1)
    def _():
        y_ref[...] = h_ref[...] + _rms(y_ref[...], gpost_ref[...])


def ffn_residual(h, g_pre, w1, w2, g_post, tf=512):
    m, d = h.shape
    dff = w1.shape[1]
    tm = min(FFN_ROW_TILE, m)
    return pl.pallas_call(
        _ffn_kernel,
        grid=(m // tm, dff // tf),
        in_specs=[pl.BlockSpec((tm, d), lambda i, j: (i, 0)),
                  pl.BlockSpec((1, d), lambda i, j: (0, 0)),
                  pl.BlockSpec((d, tf), lambda i, j: (0, j)),
                  pl.BlockSpec((tf, d), lambda i, j: (j, 0)),
                  pl.BlockSpec((1, d), lambda i, j: (0, 0))],
        out_specs=pl.BlockSpec((tm, d), lambda i, j: (i, 0)),
        out_shape=jax.ShapeDtypeStruct((m, d), f32),
        scratch_shapes=[pltpu.VMEM((tm, d), bf16)],
        compiler_params=_params("parallel", "arbitrary"),
        name="ffn",
    )(h, g_pre.reshape(1, d), w1.astype(bf16), w2.astype(bf16), g_post.reshape(1, d))


def _rope_tile(x, c2, s2):
    return x * c2 + pltpu.roll(x, HEAD_DIM // 2, 1) * s2


def _rope_heads(q, c2, s2):
    return jnp.concatenate([_rope_tile(q[:, g * HEAD_DIM:(g + 1) * HEAD_DIM], c2, s2)
                            for g in range(NSA_GROUP)], axis=0)


N_KV_BRANCH = 6


def _kv_proj_kernel(h_ref, g_ref, w_ref, c2_ref, s2_ref, *refs):
    outs, bf_ref, xn_ref = refs[:N_KV_BRANCH], refs[N_KV_BRANCH], refs[N_KV_BRANCH + 1]
    j = pl.program_id(1)

    @pl.when(j == 0)
    def _():
        xn_ref[...] = _rms(h_ref[...], g_ref[...]).astype(bf16)

    acc = jnp.dot(xn_ref[...], w_ref[...], preferred_element_type=f32)
    for br in range(N_KV_BRANCH):
        @pl.when(j == br)
        def _(br=br):
            if br in (2, 4):
                c2, s2 = c2_ref[...], s2_ref[...]
                val = jnp.concatenate([_rope_tile(acc[:, kv * HEAD_DIM:(kv + 1) * HEAD_DIM], c2, s2)
                                       for kv in range(NSA_KV)], axis=1)
            else:
                val = acc
            outs[br][...] = val
            bf_ref[...] = val.astype(bf16)


def kv_proj(h, g, w, c2, s2):
    m, d = h.shape
    n = w.shape[1]
    assert n == N_KV_BRANCH * KV_COLS
    tm = min(ROW_TILE, m)
    ntab = c2.shape[0] // tm
    res = pl.pallas_call(
        _kv_proj_kernel,
        grid=(m // tm, N_KV_BRANCH),
        in_specs=[pl.BlockSpec((tm, d), lambda i, j: (i, 0)),
                  pl.BlockSpec((1, d), lambda i, j: (0, 0)),
                  pl.BlockSpec((d, KV_COLS), lambda i, j: (0, j)),
                  pl.BlockSpec((tm, HEAD_DIM), lambda i, j: (i % ntab, 0)),
                  pl.BlockSpec((tm, HEAD_DIM), lambda i, j: (i % ntab, 0))],
        out_specs=[pl.BlockSpec((tm, KV_COLS), lambda i, j: (i, 0))] * N_KV_BRANCH
                  + [pl.BlockSpec((tm, KV_COLS), lambda i, j: (i, j))],
        out_shape=[jax.ShapeDtypeStruct((m, KV_COLS), f32)] * N_KV_BRANCH + [jax.ShapeDtypeStruct((m, n), bf16)],
        scratch_shapes=[pltpu.VMEM((tm, d), bf16)],
        compiler_params=_params("parallel", "arbitrary"),
        name="kv_proj",
    )(h, g.reshape(1, d), w.astype(bf16), c2, s2)
    return res[:N_KV_BRANCH], res[N_KV_BRANCH]


def _mem_attn_kernel(q_ref, k_ref, v_ref, o_ref, *, heads_on_rows):
    scale = HEAD_DIM ** -0.5
    for hd in range(MEM_HEADS):
        sl = slice(hd * HEAD_DIM, (hd + 1) * HEAD_DIM)
        if heads_on_rows:
            mlen = k_ref.shape[0] // MEM_HEADS
            k = k_ref[pl.ds(hd, mlen, stride=MEM_HEADS), :]
            v = v_ref[pl.ds(hd, mlen, stride=MEM_HEADS), :]
        else:
            k, v = k_ref[0, :, sl], v_ref[0, :, sl]
        q = (q_ref[0, :, sl] * scale).astype(bf16)
        s = lax.dot_general(q, k.astype(bf16), (((1,), (1,)), ((), ())), preferred_element_type=f32)
        e = jnp.exp(s - jnp.max(s, axis=-1, keepdims=True))
        p = e / jnp.sum(e, axis=-1, keepdims=True)
        o_ref[0, :, sl] = jnp.dot(p.astype(bf16), v.astype(bf16), preferred_element_type=f32)


def mem_attention(q, q_col, mk, mv, k_col=0, v_col=0, cached=None):
    b, t, _ = q.shape
    w = MEM_WIDTH
    tq = min(ROW_TILE, t)
    if cached is None:
        mlen = mk.shape[1]
        kspec = pl.BlockSpec((1, mlen, w), lambda i, j: (i, 0, k_col))
        vspec = pl.BlockSpec((1, mlen, w), lambda i, j: (i, 0, v_col))
    else:
        layer, mlen = cached
        kspec = vspec = pl.BlockSpec((mlen * MEM_HEADS, HEAD_DIM), lambda i, j: (layer * b + i, 0))
    return pl.pallas_call(
        functools.partial(_mem_attn_kernel, heads_on_rows=cached is not None),
        grid=(b, t // tq),
        in_specs=[pl.BlockSpec((1, tq, w), lambda i, j: (i, j, q_col)), kspec, vspec],
        out_specs=pl.BlockSpec((1, tq, w), lambda i, j: (i, j, 0)),
        out_shape=jax.ShapeDtypeStruct((b, t, w), f32),
        compiler_params=_params("parallel", "arbitrary"),
        name="mem_attn",
    )(q, mk, mv)


WKV_QUAD = 4
WKV_LANES = WKV_QUAD * RWKV_HEAD_DIM
WKV_TB = 64


def _wkv_kernel(r_ref, w_ref, k_ref, kk_ref, b_ref, v_ref, s0_ref, yt_ref, st_ref,
                s_scr, lhs_scr, vd_scr, yl_scr, *, nb, nq, tb):
    n = RWKV_HEAD_DIM
    ti = pl.program_id(1)

    @pl.when(ti == 0)
    def _():
        for ib in range(nb):
            s_scr[ib * nq * n:(ib + 1) * nq * n, :] = s0_ref[ib]

    yt_ref[...] = jnp.zeros(yt_ref.shape, f32)
    ri = lax.broadcasted_iota(jnp.int32, (WKV_LANES, WKV_LANES), 0)
    ci = lax.broadcasted_iota(jnp.int32, (WKV_LANES, WKV_LANES), 1)
    ones_blk = jnp.where(lax.shift_right_logical(ri, 6) == lax.shift_right_logical(ci, 6), 1.0, 0.0).astype(bf16)
    ones_blk2 = jnp.concatenate([ones_blk, ones_blk], axis=0)
    eye_rep = jnp.where(lax.broadcasted_iota(jnp.int32, (n, WKV_LANES), 0)
                        == jnp.bitwise_and(lax.broadcasted_iota(jnp.int32, (n, WKV_LANES), 1), n - 1), 1.0, 0.0)
    step_lane = jnp.bitwise_and(lax.broadcasted_iota(jnp.int32, (nq * n, WKV_LANES), 1), n - 1)
    tiles = [(ib, q) for ib in range(nb) for q in range(nq)]

    def step(t, carry):
        row = lambda ref, ib, q: ref[ib, pl.ds(t, 1), q * WKV_LANES:(q + 1) * WKV_LANES]
        for ib, q in tiles:
            rs = slice((ib * nq + q) * n, (ib * nq + q + 1) * n)
            prod = s_scr[rs, :] * row(kk_ref, ib, q)
            hi = prod.astype(bf16)
            lhs_scr[rs, 0:WKV_LANES] = hi
            lhs_scr[rs, WKV_LANES:2 * WKV_LANES] = (prod - hi.astype(f32)).astype(bf16)
            vd_scr[rs, :] = (eye_rep * row(v_ref, ib, q)).astype(bf16)
        z = jnp.dot(lhs_scr[...], ones_blk2, preferred_element_type=f32)
        vcol = jnp.dot(vd_scr[...], ones_blk, preferred_element_type=f32)
        for ib, q in tiles:
            rs = slice((ib * nq + q) * n, (ib * nq + q + 1) * n)
            s = s_scr[rs, :] * row(w_ref, ib, q) - z[rs] * row(b_ref, ib, q) + vcol[rs] * row(k_ref, ib, q)
            s_scr[rs, :] = s
            yl_scr[rs, :] = (s * row(r_ref, ib, q)).astype(bf16)
        y = jnp.dot(yl_scr[...], ones_blk, preferred_element_type=f32)
        for ib in range(nb):
            rs = slice(ib * nq * n, (ib + 1) * nq * n)
            yt_ref[ib, 0] = jnp.where(step_lane == t, y[rs], yt_ref[ib, 0])
        return carry

    lax.fori_loop(0, tb, step, 0, unroll=4)

    @pl.when(ti == pl.num_programs(1) - 1)
    def _():
        for ib in range(nb):
            st_ref[ib] = s_scr[ib * nq * n:(ib + 1) * nq * n, :]


def wkv_scan(r, w, k, kk, b, v, s0, nb=2):
    bsz, t, width = r.shape
    n = RWKV_HEAD_DIM
    nh = width // n
    nq = nh // WKV_QUAD
    tb = min(WKV_TB, t)
    nblk = t // tb
    to_tiles = lambda s: s.reshape(bsz, nq, WKV_QUAD, n, n).transpose(0, 1, 3, 2, 4).reshape(bsz, nq * n, WKV_LANES)
    row = pl.BlockSpec((nb, tb, width), lambda i, j: (i, j, 0))
    st = pl.BlockSpec((nb, nq * n, WKV_LANES), lambda i, j: (i, 0, 0))
    rows_all = nb * nq * n
    yt, s_t = pl.pallas_call(
        functools.partial(_wkv_kernel, nb=nb, nq=nq, tb=tb),
        grid=(bsz // nb, nblk),
        in_specs=[row, row, row, row, row, row, st],
        out_specs=[pl.BlockSpec((nb, 1, nq * n, WKV_LANES), lambda i, j: (i, j, 0, 0)), st],
        out_shape=[jax.ShapeDtypeStruct((bsz, nblk, nq * n, WKV_LANES), f32),
                   jax.ShapeDtypeStruct((bsz, nq * n, WKV_LANES), f32)],
        scratch_shapes=[pltpu.VMEM((rows_all, WKV_LANES), f32),
                        pltpu.VMEM((rows_all, 2 * WKV_LANES), bf16),
                        pltpu.VMEM((rows_all, WKV_LANES), bf16),
                        pltpu.VMEM((rows_all, WKV_LANES), bf16)],
        compiler_params=_params("parallel", "arbitrary"),
        name="wkv_scan",
    )(r, w, k, kk, b, v, to_tiles(s0))
    y = yt.reshape(bsz, nblk, nq, n, WKV_QUAD, n).transpose(0, 1, 5, 2, 4, 3)[:, :, :tb].reshape(bsz, t, width)
    s_t = s_t.reshape(bsz, nq, n, WKV_QUAD, n).transpose(0, 1, 3, 2, 4).reshape(bsz, nh, n, n)
    return y, s_t


def _gelu_tanh(x):
    return 0.5 * x * (1.0 + jnp.tanh(0.7978845608028654 * (x + 0.044715 * x * x * x)))


def _chunk_rows(x_ref, n_chunks, row0=0, row_stride=1):
    return jnp.concatenate(
        [x_ref[pl.ds(row0 + s * row_stride, n_chunks, stride=CMP_STRIDE * row_stride), :]
         for s in range(CMP_STRIDE)], axis=1).astype(bf16)


def _compress_rows(x2, pos_ref, w1_ref, w2_ref, n_valid, n_heads):
    rows = x2.shape[0]
    n_chunks = rows // n_heads
    pab = jnp.dot(x2, w1_ref[...], preferred_element_type=f32)
    pos = jnp.dot(pos_ref[...], w1_ref[...], preferred_element_type=f32)
    posterm = pos[0:1, :HEAD_DIM] + pos[1:2, HEAD_DIM:]
    hid = pab[:, :HEAD_DIM] + pltpu.roll(pab[:, HEAD_DIM:], rows - 1, 0) + posterm
    out = jnp.dot(_gelu_tanh(hid).astype(bf16), w2_ref[...], preferred_element_type=f32)
    n = jnp.bitwise_and(lax.broadcasted_iota(jnp.int32, out.shape, 0), n_chunks - 1)
    return jnp.where(n < n_valid, out, 0.0)


def _compress_prompt_kernel(k_ref, v_ref, posk_ref, w1k_ref, w2k_ref, posv_ref, w1v_ref, w2v_ref,
                            ok_ref, ov_ref, *, n_chunks, n_valid):
    ok_ref[0, 0] = _compress_rows(_chunk_rows(k_ref.at[0], n_chunks), posk_ref, w1k_ref, w2k_ref, n_valid, 1)
    ov_ref[0, 0] = _compress_rows(_chunk_rows(v_ref.at[0], n_chunks), posv_ref, w1v_ref, w2v_ref, n_valid, 1)


def _cmp_weights(cmp_pos, cmp_w1, cmp_w2):
    ws = []
    for i in range(2):
        half = CMP_STRIDE * HEAD_DIM
        pos = _pad_to_rows(cmp_pos[i].reshape(2, half), 8).astype(bf16)
        w1 = cmp_w1[i].reshape(2, half, HEAD_DIM)
        ws += [pos, jnp.concatenate([w1[0], w1[1]], axis=1).astype(bf16), cmp_w2[i].astype(bf16)]
    return ws


_CMP_WEIGHT_SHAPES = [(8, CMP_STRIDE * HEAD_DIM), (CMP_STRIDE * HEAD_DIM, 2 * HEAD_DIM), (HEAD_DIM, HEAD_DIM)] * 2


def compress_prompt(ck, cv, cmp_pos, cmp_w1, cmp_w2):
    b, t, _ = ck.shape
    n_chunks = t // CMP_STRIDE
    n_valid = (t - CMP_BLOCK) // CMP_STRIDE + 1
    out = jax.ShapeDtypeStruct((b, NSA_KV, n_chunks, HEAD_DIM), f32)
    ospec = pl.BlockSpec((1, 1, n_chunks, HEAD_DIM), lambda i, kv: (i, kv, 0, 0))
    wspecs = [pl.BlockSpec(s, lambda i, kv: (0, 0)) for s in _CMP_WEIGHT_SHAPES]
    return pl.pallas_call(
        functools.partial(_compress_prompt_kernel, n_chunks=n_chunks, n_valid=n_valid),
        grid=(b, NSA_KV),
        in_specs=[pl.BlockSpec((1, t, HEAD_DIM), lambda i, kv: (i, 0, kv)),
                  pl.BlockSpec((1, t, HEAD_DIM), lambda i, kv: (i, 0, kv))] + wspecs,
        out_specs=[ospec, ospec],
        out_shape=[out, out],
        compiler_params=_params("parallel", "parallel"),
        name="compress_prompt",
    )(ck, cv, *_cmp_weights(cmp_pos, cmp_w1, cmp_w2))


PAGE_ROWS = PAGE_SIZE * NSA_KV


def _page_specs(n_pages):
    return [pl.BlockSpec((PAGE_ROWS, HEAD_DIM), lambda i, pt, p=p: (pt[i, p], 0)) for p in range(n_pages)]


def _compress_paged_kernel(pt_ref, *refs, n_pages, n_valid):
    k_pages, v_pages = refs[:n_pages], refs[n_pages:2 * n_pages]
    posk_ref, w1k_ref, w2k_ref, posv_ref, w1v_ref, w2v_ref, ok_ref, ov_ref = refs[2 * n_pages:]
    per_page = PAGE_SIZE // CMP_STRIDE
    n_chunks = n_pages * per_page
    chunks = lambda pages: jnp.concatenate([_chunk_rows(pg, per_page, kv, NSA_KV)
                                            for kv in range(NSA_KV) for pg in pages], axis=0)
    ok = _compress_rows(chunks(k_pages), posk_ref, w1k_ref, w2k_ref, n_valid, NSA_KV)
    ov = _compress_rows(chunks(v_pages), posv_ref, w1v_ref, w2v_ref, n_valid, NSA_KV)
    for kv in range(NSA_KV):
        ok_ref[0, kv] = ok[kv * n_chunks:(kv + 1) * n_chunks]
        ov_ref[0, kv] = ov[kv * n_chunks:(kv + 1) * n_chunks]


def compress_paged(pool_k, pool_v, page_table, n_valid, cmp_pos, cmp_w1, cmp_w2):
    b, n_pages = page_table.shape
    n_chunks = n_pages * PAGE_SIZE // CMP_STRIDE
    out = jax.ShapeDtypeStruct((b, NSA_KV, n_chunks, HEAD_DIM), f32)
    ospec = pl.BlockSpec((1, NSA_KV, n_chunks, HEAD_DIM), lambda i, pt: (i, 0, 0, 0))
    wspecs = [pl.BlockSpec(s, lambda i, pt: (0, 0)) for s in _CMP_WEIGHT_SHAPES]
    return pl.pallas_call(
        functools.partial(_compress_paged_kernel, n_pages=n_pages, n_valid=n_valid),
        grid_spec=pltpu.PrefetchScalarGridSpec(
            num_scalar_prefetch=1,
            grid=(b,),
            in_specs=_page_specs(n_pages) * 2 + wspecs,
            out_specs=[ospec, ospec]),
        out_shape=[out, out],
        compiler_params=_params("parallel"),
        name="compress_paged",
    )(page_table, *([pool_k] * n_pages), *([pool_v] * n_pages), *_cmp_weights(cmp_pos, cmp_w1, cmp_w2))


def _stack_heads(x):
    return jnp.concatenate([x[:, g * HEAD_DIM:(g + 1) * HEAD_DIM] for g in range(NSA_GROUP)], axis=0)


def _dot_nt(a, b):
    return lax.dot_general(a, b, (((1,), (1,)), ((), ())), preferred_element_type=f32)


def _compressed_branch(qc, ckc, cvc, pos_rows, n_cmp):
    s = _dot_nt(qc, ckc)
    n = lax.broadcasted_iota(jnp.int32, s.shape, 1)
    vis = (n * CMP_STRIDE + (CMP_BLOCK - 1) <= pos_rows) & (n < n_cmp)
    s = jnp.where(vis, s, NEG_INF)
    m = jnp.max(s, axis=-1, keepdims=True)
    e = jnp.where(vis, jnp.exp(s - m), 0.0)
    denom = jnp.sum(e, axis=-1, keepdims=True)
    p = e / jnp.maximum(denom, 1e-30)
    return jnp.dot(p.astype(bf16), cvc, preferred_element_type=f32), p


def _select_blocks(p, tq, pos0, n_slc, n_j):
    psum = p[0:tq]
    for g in range(1, NSA_GROUP):
        psum = psum + p[g * tq:(g + 1) * tq]
    if tq < SEL_LANES:
        psum = jnp.concatenate([psum, jnp.zeros((SEL_LANES - tq, psum.shape[1]), f32)], axis=0)
    n_c = p.shape[1]
    j = lax.broadcasted_iota(jnp.int32, (n_j, n_c), 0)
    cs = lax.broadcasted_iota(jnp.int32, (n_j, n_c), 1) * CMP_STRIDE
    overlap = jnp.where((cs < j * SLC_BLOCK + SLC_BLOCK) & (cs + (CMP_BLOCK - 1) >= j * SLC_BLOCK), 1.0, 0.0)
    imp_t = lax.dot_general(overlap, psum, (((1,), (1,)), ((), ())),
                            preferred_element_type=f32, precision=lax.Precision.HIGHEST)
    j = lax.broadcasted_iota(jnp.int32, imp_t.shape, 0)
    pos_t = pos0 + lax.broadcasted_iota(jnp.int32, imp_t.shape, 1)
    cur = lax.shift_right_logical(pos_t, SLC_SHIFT)
    causal = j * SLC_BLOCK <= pos_t
    forced = (j == 0) | (j == cur) | (j == cur - 1)
    score = jnp.where(causal, jnp.where(forced, FORCE_SCORE, imp_t), -FORCE_SCORE)
    score = jnp.where(j < n_slc, score, -2.0 * FORCE_SCORE)
    rank = jnp.zeros(imp_t.shape, f32)
    for jp in range(n_slc):
        row = score[jp:jp + 1, :]
        ahead = (row > score) | ((row == score) & (j > jp))
        rank = rank + jnp.where(ahead, 1.0, 0.0)
    sel_t = jnp.where(rank < min(N_SELECT, n_slc), 1.0, 0.0)
    return sel_t.T[0:tq]


def _selection_bias(sel_rows, key0, n_keys):
    nj = sel_rows.shape[1]
    j = lax.broadcasted_iota(jnp.int32, (nj, n_keys), 0)
    kpos = key0 + lax.broadcasted_iota(jnp.int32, (nj, n_keys), 1)
    e = jnp.where(lax.shift_right_logical(kpos, SLC_SHIFT) == j, 1.0, 0.0).astype(bf16)
    return jnp.dot(jnp.where(sel_rows > 0.5, 0.0, NEG_INF).astype(bf16), e, preferred_element_type=f32)


def _window_branch(qr, wk, wv, kpos0, n_keys_valid, pos_rows, n_phantom=None):
    s = _dot_nt(qr, wk)
    lane = lax.broadcasted_iota(jnp.int32, s.shape, 1)
    kpos = kpos0 + lane
    valid = (kpos <= pos_rows) & (pos_rows - kpos < WINDOW) & (lane < n_keys_valid)
    s = jnp.where(valid, s, NEG_INF)
    m = jnp.max(s, axis=-1, keepdims=True)
    if n_phantom is not None:
        m = jnp.where(n_phantom > 0.0, jnp.maximum(m, 0.0), m)
    e = jnp.where(valid, jnp.exp(s - m), 0.0)
    denom = jnp.sum(e, axis=-1, keepdims=True)
    if n_phantom is not None:
        denom = denom + n_phantom * jnp.exp(-m)
    return jnp.dot((e / denom).astype(bf16), wv, preferred_element_type=f32)


def _gated_sum(gate, cols, tq, o_cmp, o_slc, o_win, g):
    r = slice(g * tq, (g + 1) * tq)
    c, s, w = cols[0] + g, cols[1] + g, cols[2] + g
    return gate[:, c:c + 1] * o_cmp[r] + gate[:, s:s + 1] * o_slc[r] + gate[:, w:w + 1] * o_win[r]


QK_SCALE = HEAD_DIM ** -0.5


def _nsa_prompt_kernel(q_ref, c2_ref, s2_ref, gate_ref, ckc_ref, cvc_ref, sk_ref, sv_ref, wk_ref, wv_ref,
                       o_ref, *, n_cmp, n_slc):
    i = pl.program_id(2)
    tq = Q_BLOCK
    rows = NSA_GROUP * tq
    q0 = i * tq
    q = q_ref[0] * QK_SCALE
    qc = _stack_heads(q).astype(bf16)
    qr = _rope_heads(q, c2_ref[...], s2_ref[...]).astype(bf16)
    t_row = lax.broadcasted_iota(jnp.int32, (rows, 1), 0)
    pos_rows = q0 + jnp.bitwise_and(t_row, tq - 1)

    o_cmp, p = _compressed_branch(qc, ckc_ref[0, 0].astype(bf16), cvc_ref[0, 0].astype(bf16), pos_rows, n_cmp)
    sel = _select_blocks(p, tq, q0, n_slc, n_slc)
    sel_rows = jnp.concatenate([sel] * NSA_GROUP, axis=0)

    def slc_step(c, carry, causal):
        m, l, acc = carry
        k0 = pl.multiple_of(c * SLC_CHUNK, SLC_CHUNK)
        s = _dot_nt(qr, sk_ref[0, pl.ds(k0, SLC_CHUNK), :]) + _selection_bias(sel_rows, k0, SLC_CHUNK)
        if causal:
            kpos = k0 + lax.broadcasted_iota(jnp.int32, s.shape, 1)
            s = jnp.where(kpos <= pos_rows, s, NEG_INF)
        m_new = jnp.maximum(m, jnp.max(s, axis=-1, keepdims=True))
        alpha = jnp.exp(m - m_new)
        e = jnp.exp(s - m_new)
        l = alpha * l + jnp.sum(e, axis=-1, keepdims=True)
        acc = alpha * acc + jnp.dot(e.astype(bf16), sv_ref[0, pl.ds(k0, SLC_CHUNK), :], preferred_element_type=f32)
        return m_new, l, acc

    c_last = q0 // SLC_CHUNK
    init = (jnp.full((rows, 1), NEG_INF, f32), jnp.zeros((rows, 1), f32), jnp.zeros((rows, HEAD_DIM), f32))
    carry = lax.fori_loop(0, c_last, functools.partial(slc_step, causal=False), init)
    _, l, acc = slc_step(c_last, carry, causal=True)
    o_slc = acc / l

    span = WINDOW + tq
    w0 = pl.multiple_of(jnp.maximum(q0 - WINDOW, 0), tq)
    n_phantom = jnp.maximum(WINDOW - 1 - pos_rows, 0).astype(f32)
    o_win = _window_branch(qr, wk_ref[0, pl.ds(w0, span), :], wv_ref[0, pl.ds(w0, span), :], w0, span, pos_rows,
                           n_phantom)

    gate = _sigmoid(gate_ref[0])
    first_kv = pl.program_id(1) == 0
    for g in range(NSA_GROUP):
        head = [_gated_sum(gate, tuple(br * NSA_HEADS + kv * NSA_GROUP for br in range(3)), tq, o_cmp, o_slc, o_win, g)
                for kv in range(NSA_KV)]
        o_ref[0, :, g * HEAD_DIM:(g + 1) * HEAD_DIM] = jnp.where(first_kv, head[0], head[1])


def nsa_prompt(proj, gate_col, ckc, cvc, kv_bf, c2, s2, n_cmp):
    b, t, _ = proj.shape
    assert t % SLC_CHUNK == 0 and t >= WINDOW + Q_BLOCK
    n_slc = t // SLC_BLOCK
    gw = NSA_GROUP * HEAD_DIM
    kvcol = lambda c: pl.BlockSpec((1, t, HEAD_DIM), lambda bi, kv, i, c=c: (bi, 0, c * NSA_KV + kv))
    cmp_spec = pl.BlockSpec((1, 1, ckc.shape[2], HEAD_DIM), lambda bi, kv, i: (bi, kv, 0, 0))
    return pl.pallas_call(
        functools.partial(_nsa_prompt_kernel, n_cmp=n_cmp, n_slc=n_slc),
        grid=(b, NSA_KV, t // Q_BLOCK),
        in_specs=[pl.BlockSpec((1, Q_BLOCK, gw), lambda bi, kv, i: (bi, i, kv)),
                  pl.BlockSpec((Q_BLOCK, HEAD_DIM), lambda bi, kv, i: (i, 0)),
                  pl.BlockSpec((Q_BLOCK, HEAD_DIM), lambda bi, kv, i: (i, 0)),
                  pl.BlockSpec((1, Q_BLOCK, LANES), lambda bi, kv, i: (bi, i, gate_col)),
                  cmp_spec, cmp_spec, kvcol(2), kvcol(3), kvcol(4), kvcol(5)],
        out_specs=pl.BlockSpec((1, Q_BLOCK, gw), lambda bi, kv, i: (bi, i, kv)),
        out_shape=jax.ShapeDtypeStruct((b, t, NSA_WIDTH), f32),
        compiler_params=_params("parallel", "parallel", "arbitrary"),
        name="nsa_prompt",
    )(proj, c2, s2, proj, ckc, cvc, kv_bf, kv_bf, kv_bf, kv_bf)


def _pad_rows(x, n):
    return jnp.concatenate([x, jnp.zeros((n - x.shape[0], x.shape[1]), x.dtype)], axis=0)


def _nsa_sample_kernel(pt_ref, *refs, n_pages, n_cmp, n_slc, n_j, past, tq):
    k_pages, v_pages = refs[:n_pages], refs[n_pages:2 * n_pages]
    (q_ref, c2_ref, s2_ref, gate_ref, ckc_ref, cvc_ref, nsk_ref, nsv_ref, nwk_ref, nwv_ref,
     wink_ref, winv_ref, o_ref) = refs[2 * n_pages:]
    rows = NSA_GROUP * tq
    t_row = lax.broadcasted_iota(jnp.int32, (rows, 1), 0)
    pos_rows = past + jnp.bitwise_and(t_row, tq - 1)
    lw = wink_ref.shape[0] // NSA_KV
    c2, s2 = c2_ref[...], s2_ref[...]
    gate = _sigmoid(gate_ref[0])
    for kv in range(NSA_KV):
        ksl = slice(kv * HEAD_DIM, (kv + 1) * HEAD_DIM)
        q = q_ref[0, :, kv * NSA_GROUP * HEAD_DIM:(kv + 1) * NSA_GROUP * HEAD_DIM] * QK_SCALE
        qc = _stack_heads(q).astype(bf16)
        qr = _rope_heads(q, c2, s2).astype(bf16)
        o_cmp, pc = _compressed_branch(qc, ckc_ref[0, kv].astype(bf16), cvc_ref[0, kv].astype(bf16), pos_rows, n_cmp)
        sel = _select_blocks(pc, tq, past, n_slc, n_j)
        sel_rows = jnp.concatenate([sel] * NSA_GROUP, axis=0)

        paged = lambda pages: [pg[pl.ds(kv, PAGE_SIZE, stride=NSA_KV), :] for pg in pages]
        sk = jnp.concatenate(paged(k_pages) + [_pad_rows(nsk_ref[0, :, ksl], LANES)], axis=0).astype(bf16)
        sv = jnp.concatenate(paged(v_pages) + [_pad_rows(nsv_ref[0, :, ksl], LANES)], axis=0).astype(bf16)
        s = _dot_nt(qr, sk) + _selection_bias(sel_rows, 0, sk.shape[0])
        kpos = lax.broadcasted_iota(jnp.int32, s.shape, 1)
        s = jnp.where(kpos <= pos_rows, s, NEG_INF)
        e = jnp.exp(s - jnp.max(s, axis=-1, keepdims=True))
        o_slc = jnp.dot((e / jnp.sum(e, axis=-1, keepdims=True)).astype(bf16), sv, preferred_element_type=f32)

        cached = lambda ref: ref[pl.ds(kv, lw, stride=NSA_KV), :]
        wk = jnp.concatenate([cached(wink_ref), _pad_rows(nwk_ref[0, :, ksl], LANES)], axis=0).astype(bf16)
        wv = jnp.concatenate([cached(winv_ref), _pad_rows(nwv_ref[0, :, ksl], LANES)], axis=0).astype(bf16)
        o_win = _window_branch(qr, wk, wv, past - lw, lw + tq, pos_rows)

        cols = tuple(br * NSA_HEADS + kv * NSA_GROUP for br in range(3))
        for g in range(NSA_GROUP):
            hd = kv * NSA_GROUP + g
            o_ref[0, :, hd * HEAD_DIM:(hd + 1) * HEAD_DIM] = _gated_sum(gate, cols, tq, o_cmp, o_slc, o_win, g)


def nsa_sample(proj, gate_col, ckc, cvc, pool_k, pool_v, page_table, new_rows, win_k, win_v, c2, s2, n_cmp, past):
    b, tq, _ = proj.shape
    n_pages = page_table.shape[1]
    assert past == n_pages * PAGE_SIZE and past % SLC_BLOCK == 0 and tq <= SLC_BLOCK
    n_slc = past // SLC_BLOCK + 1
    n_j = -(-n_slc // SLC_BLOCK) * SLC_BLOCK
    lw2 = win_k.shape[0] // b
    per_b = lambda shape: pl.BlockSpec((1,) + shape, lambda i, pt: (i,) + (0,) * len(shape))
    tab = pl.BlockSpec((tq, HEAD_DIM), lambda i, pt: (0, 0))
    win = pl.BlockSpec((lw2, HEAD_DIM), lambda i, pt: (i, 0))
    return pl.pallas_call(
        functools.partial(_nsa_sample_kernel, n_pages=n_pages, n_cmp=n_cmp, n_slc=n_slc, n_j=n_j, past=past, tq=tq),
        grid_spec=pltpu.PrefetchScalarGridSpec(
            num_scalar_prefetch=1,
            grid=(b,),
            in_specs=_page_specs(n_pages) * 2 + [
                per_b((tq, NSA_WIDTH)), tab, tab,
                pl.BlockSpec((1, tq, LANES), lambda i, pt: (i, 0, gate_col)),
                per_b((NSA_KV, ckc.shape[2], HEAD_DIM)), per_b((NSA_KV, ckc.shape[2], HEAD_DIM))]
                + [per_b((tq, KV_COLS))] * 4 + [win, win],
            out_specs=per_b((tq, NSA_WIDTH))),
        out_shape=jax.ShapeDtypeStruct((b, tq, NSA_WIDTH), f32),
        compiler_params=_params("parallel"),
        name="nsa_sample",
    )(page_table, *([pool_k] * n_pages), *([pool_v] * n_pages), proj, c2, s2, proj, ckc, cvc, *new_rows, win_k, win_v)


def _prev_rows(x, tile, halo_ref, first_ref, seq_len):
    tm, c = x.shape
    prev = pltpu.roll(x, 1, 0)
    row = lax.broadcasted_iota(jnp.int32, (tm, 1), 0)
    if seq_len >= tm:
        tiles_per_seq = seq_len // tm
        first = first_ref[pl.ds(tile // tiles_per_seq, 1), :]
        edge = jnp.where(tile % tiles_per_seq == 0, first, halo_ref[7:8, :])
        return jnp.where(row == 0, edge, prev)
    pieces = []
    for j in range(tm // seq_len):
        pieces.append(jnp.broadcast_to(first_ref[j:j + 1, :], (8, c)))
        if seq_len > 8:
            pieces.append(jnp.zeros((seq_len - 8, c), f32))
    return jnp.where(jnp.bitwise_and(row, seq_len - 1) == 0, jnp.concatenate(pieces, axis=0), prev)


def _shift_specs(m, c, tm, seq_len, n_seq, col=None):
    cb = (lambda *g: 0) if col is None else col
    tile = pl.BlockSpec((tm, c), lambda *g: (g[0], cb(*g)))
    halo = pl.BlockSpec((8, c), lambda *g: (jnp.maximum(g[0] * (tm // 8) - 1, 0), cb(*g)))
    if seq_len >= tm:
        first = pl.BlockSpec((-(-n_seq // 8) * 8, c), lambda *g: (0, cb(*g)))
    else:
        first = pl.BlockSpec((tm // seq_len, c), lambda *g: (g[0], cb(*g)))
    return tile, halo, first


def _pad_first(first, seq_len, tm):
    return _pad_to_rows(first, -(-first.shape[0] // 8) * 8) if seq_len >= tm else first


def _head_sums(x):
    ri = lax.broadcasted_iota(jnp.int32, (WKV_LANES, WKV_LANES), 0)
    ci = lax.broadcasted_iota(jnp.int32, (WKV_LANES, WKV_LANES), 1)
    ones_blk = jnp.where(lax.shift_right_logical(ri, 6) == lax.shift_right_logical(ci, 6), 1.0, 0.0).astype(bf16)
    hi = x.astype(bf16)
    lo = (x - hi.astype(f32)).astype(bf16)
    out = []
    for c in range(x.shape[1] // WKV_LANES):
        sl = slice(c * WKV_LANES, (c + 1) * WKV_LANES)
        out.append(jnp.dot(hi[:, sl], ones_blk, preferred_element_type=f32)
                   + jnp.dot(lo[:, sl], ones_blk, preferred_element_type=f32))
    return jnp.concatenate(out, axis=1)


def _lora_kernel(hn_ref, halo_ref, first_ref, mu_ref, wd1_ref, wa1_ref, wg1_ref, wd2_ref, wa2_ref, wg2_ref,
                 w0_ref, a0_ref, decay_ref, a_ref, g_ref, *, seq_len):
    hn = hn_ref[...]
    xx = _prev_rows(hn, pl.program_id(0), halo_ref, first_ref, seq_len) - hn
    mix = lambda r: (hn + xx * mu_ref[r:r + 1, :]).astype(bf16)
    dot = lambda x, w_ref: jnp.dot(x, w_ref[...], preferred_element_type=f32)
    w_raw = w0_ref[...] + dot(jnp.tanh(dot(mix(0), wd1_ref)).astype(bf16), wd2_ref)
    softplus = jnp.maximum(-w_raw, 0.0) + jnp.log(1.0 + jnp.exp(-jnp.abs(w_raw)))
    decay_ref[...] = jnp.exp(-jnp.exp(-softplus - 0.5))
    a_ref[...] = _sigmoid(a0_ref[...] + dot(dot(mix(1), wa1_ref).astype(bf16), wa2_ref))
    g_ref[...] = dot(_sigmoid(dot(mix(2), wg1_ref)).astype(bf16), wg2_ref)


def rwkv_lora(hn, x_prev, seq_len, P):
    m, d = hn.shape
    rw = P['w0'].shape[0]
    tm = min(ROW_TILE, m)
    pad128 = lambda w: _pad_cols(w, -(-w.shape[1] // LANES) * LANES).astype(bf16)
    w1s = [pad128(P[k]) for k in ('w_decay1', 'w_aaa1', 'w_gate1')]
    w2s = [_pad_to_rows(P[k], w1.shape[1]).astype(bf16) for k, w1 in zip(('w_decay2', 'w_aaa2', 'w_gate2'), w1s)]
    full = lambda x: pl.BlockSpec(x.shape, lambda i: (0, 0))
    mu = _pad_to_rows(P['mu_wag'], 8)
    vecs = [P['w0'].reshape(1, rw), P['a0'].reshape(1, rw)]
    out = jax.ShapeDtypeStruct((m, rw), f32)
    ospec = pl.BlockSpec((tm, rw), lambda i: (i, 0))
    return pl.pallas_call(
        functools.partial(_lora_kernel, seq_len=seq_len),
        grid=(m // tm,),
        in_specs=list(_shift_specs(m, d, tm, seq_len, x_prev.shape[0])) + [full(mu)]
                 + [full(w) for w in w1s + w2s + vecs],
        out_specs=[ospec] * 3,
        out_shape=[out] * 3,
        compiler_params=_params("parallel"),
        name="rwkv_lora",
    )(hn, hn, _pad_first(x_prev, seq_len, tm), mu, *w1s, *w2s, *vecs)


def _prep_kernel(p_ref, halo_ref, first_ref, mu_ref, a_ref, kkw_ref, kaw_ref,
                 r_ref, k_ref, kk_ref, b_ref, v_ref, *, seq_len):
    j = pl.program_id(1)
    cur = p_ref[...]
    x = cur + (_prev_rows(cur, pl.program_id(0), halo_ref, first_ref, seq_len) - cur) * mu_ref[...]

    @pl.when(j == 0)
    def _():
        r_ref[...] = x

    @pl.when(j == 1)
    def _():
        a = a_ref[...]
        kk = x * kkw_ref[...]
        kk = kk * lax.rsqrt(_head_sums(kk * kk) + 1e-12)
        kk_ref[...] = kk
        b_ref[...] = kk * a
        k_ref[...] = x * (1.0 + (a - 1.0) * kaw_ref[...])

    @pl.when(j == 2)
    def _():
        v_ref[...] = x


def rwkv_prep(proj, p0, a_rate, seq_len, P):
    m = proj.shape[0]
    rw = a_rate.shape[1]
    tm = min(ROW_TILE, m)
    col = lambda i, j: j
    row1 = lambda v: v.reshape(1, -1)
    out = jax.ShapeDtypeStruct((m, rw), f32)
    ospec = pl.BlockSpec((tm, rw), lambda i, j: (i, 0))
    return pl.pallas_call(
        functools.partial(_prep_kernel, seq_len=seq_len),
        grid=(m // tm, 3),
        in_specs=list(_shift_specs(m, rw, tm, seq_len, p0.shape[0], col))
                 + [pl.BlockSpec((1, rw), lambda i, j: (0, j)),
                    pl.BlockSpec((tm, rw), lambda i, j: (i, 0)),
                    pl.BlockSpec((1, rw), lambda i, j: (0, 0)),
                    pl.BlockSpec((1, rw), lambda i, j: (0, 0))],
        out_specs=[ospec] * 5,
        out_shape=[out] * 5,
        compiler_params=_params("parallel", "arbitrary"),
        name="rwkv_prep",
    )(proj, proj, _pad_first(p0, seq_len, tm), row1(P['mu_rkv']), a_rate, row1(P['k_k']), row1(P['k_a']))


def _rwkv_out_kernel(y_ref, r_ref, k_ref, v_ref, g_ref, om_ref, h_ref, lw_ref, lb_ref, rk_ref, w_ref, gp_ref, o_ref):
    inv_n = 1.0 / RWKV_HEAD_DIM
    y = y_ref[...]
    d = y - _head_sums(y) * inv_n
    var = _head_sums(d * d) * inv_n
    yn = d * lax.rsqrt(var + GN_EPS) * lw_ref[...] + lb_ref[...]
    bonus = _head_sums(r_ref[...] * k_ref[...] * rk_ref[...]) * v_ref[...]
    o = ((yn + bonus) * g_ref[...]).astype(bf16)
    rw = o.shape[1]
    acc = jnp.dot(o, w_ref[:rw, :], preferred_element_type=f32)
    acc += jnp.dot(om_ref[...].astype(bf16), w_ref[rw:, :], preferred_element_type=f32)
    o_ref[...] = h_ref[...] + _rms(acc, gp_ref[...])


def rwkv_out(y, r, k, v, gate, o_mem, h, P):
    m, rw = y.shape
    d = h.shape[1]
    tm = min(ROW_TILE // 2, m)
    wide = pl.BlockSpec((tm, rw), lambda i: (i, 0))
    vec = pl.BlockSpec((1, rw), lambda i: (0, 0))
    return pl.pallas_call(
        _rwkv_out_kernel,
        grid=(m // tm,),
        in_specs=[wide] * 5 + [pl.BlockSpec((tm, o_mem.shape[1]), lambda i: (i, 0)),
                               pl.BlockSpec((tm, d), lambda i: (i, 0)), vec, vec, vec,
                               pl.BlockSpec((rw + o_mem.shape[1], d), lambda i: (0, 0)),
                               pl.BlockSpec((1, d), lambda i: (0, 0))],
        out_specs=pl.BlockSpec((tm, d), lambda i: (i, 0)),
        out_shape=jax.ShapeDtypeStruct((m, d), f32),
        compiler_params=_params("parallel"),
        name="rwkv_out",
    )(y, r, k, v, gate, o_mem, h, P['lnx_w'].reshape(1, rw), P['lnx_b'].reshape(1, rw), P['r_k'].reshape(1, rw),
      P['w_out_a'].astype(bf16), P['g_mix_post'].reshape(1, d))


def _rope_tables(pos):
    half = HEAD_DIM // 2
    inv = jnp.power(ROPE_THETA, -jnp.arange(half, dtype=f32) / half)
    ang = pos.astype(f32)[:, None] * inv[None, :]
    cos, sin = jnp.cos(ang), jnp.sin(ang)
    return jnp.concatenate([cos, cos], axis=-1), jnp.concatenate([-sin, sin], axis=-1)


def _pad_cols(w, n):
    return jnp.pad(w, ((0, 0), (0, n - w.shape[1])))


def _pad_to_rows(x, n):
    return jnp.pad(x, ((0, n - x.shape[0]), (0, 0)))


def rwkv_mem_layer(h, x_prev, s0, mem, P):
    b, t, d = h.shape
    m = b * t
    assert t & (t - 1) == 0 and t % 8 == 0 and (t % ROW_TILE == 0 or ROW_TILE % t == 0)
    h2 = h.reshape(m, d)
    hn = rmsnorm(h2, P['g_mix_pre'])
    rw = P['w0'].shape[0]

    proj = matmul(hn, P['w_in_a'])
    p0 = matmul(_pad_to_rows(x_prev, -(-b // 8) * 8), P['w_in_a'])[:b]
    decay, a_rate, gate = rwkv_lora(hn, x_prev, t, P)
    r, k, kk, kb, v = rwkv_prep(proj, p0, a_rate, t, P)
    as3 = lambda x: x.reshape(b, t, -1)
    y, s_t = wkv_scan(as3(r), as3(decay), as3(k), as3(kk), as3(kb), as3(v), s0)
    o_mem = mem(as3(proj), 3 * rw // MEM_WIDTH)
    h2 = rwkv_out(y.reshape(m, rw), r, k, v, gate, o_mem.reshape(m, MEM_WIDTH), h2, P)
    h2 = ffn_residual(h2, P['g_ffn_pre'], P['w_ff1'], P['w_ff2'], P['g_ffn_post'])
    return h2.reshape(b, t, d), s_t, hn.reshape(b, t, d)[:, -1]


GATE_COL = (NSA_WIDTH + MEM_WIDTH) // LANES


def nsa_mem_layer(h, mem, P, attend):
    b, t, d = h.shape
    m = b * t
    h2 = h.reshape(m, d)
    n_in = (GATE_COL + 1) * LANES
    n_in = -(-n_in // 768) * 768
    proj = matmul(h2, _pad_cols(P['w_in_b'].astype(bf16), n_in), g=P['g_mix_pre'], tn=768).reshape(b, t, -1)
    o_nsa = attend(proj)
    o_mem = mem(proj, NSA_WIDTH // MEM_WIDTH)
    h2 = out_proj_residual(o_nsa.reshape(m, NSA_WIDTH), o_mem.reshape(m, MEM_WIDTH), 0, P['w_out_b'], h2,
                           P['g_mix_post'])
    h2 = ffn_residual(h2, P['g_ffn_pre'], P['w_ff1'], P['w_ff2'], P['g_ffn_post'])
    return h2.reshape(b, t, d)


def kernel(x_prompt, x_sample, mem_prompt, cache_mem_k, cache_mem_v, state_wkv, state_shift, cache_cmp_k, cache_cmp_v, cache_slc_k, cache_slc_v, cache_win_k, cache_win_v, page_table, g_mix_pre, g_mix_post, g_ffn_pre, g_ffn_post, g_mem, w_mem_k, w_mem_v, w_in_a, mu_rkv, mu_wag, w0, w_decay1, w_decay2, a0, w_aaa1, w_aaa2, w_gate1, w_gate2, k_k, k_a, r_k, lnx_w, lnx_b, w_out_a, g_kv, w_kv, cmp_pos, cmp_w1, cmp_w2, w_in_b, w_out_b, w_ff1, w_ff2):
    bp, tp, d = x_prompt.shape
    bs, ts, _ = x_sample.shape
    depth = g_mix_pre.shape[0]
    assert depth == 2 and w_in_a.shape[0] == 1 and w_in_b.shape[0] == 1
    n_pages = page_table.shape[1]
    past = n_pages * PAGE_SIZE
    mem_len = mem_prompt.shape[1]

    P0 = dict(g_mix_pre=g_mix_pre[0], g_mix_post=g_mix_post[0], g_ffn_pre=g_ffn_pre[0], g_ffn_post=g_ffn_post[0],
              w_in_a=w_in_a[0], mu_rkv=mu_rkv[0], mu_wag=mu_wag[0], w0=w0[0], w_decay1=w_decay1[0],
              w_decay2=w_decay2[0], a0=a0[0], w_aaa1=w_aaa1[0], w_aaa2=w_aaa2[0], w_gate1=w_gate1[0],
              w_gate2=w_gate2[0], k_k=k_k[0], k_a=k_a[0], r_k=r_k[0], lnx_w=lnx_w[0], lnx_b=lnx_b[0],
              w_out_a=w_out_a[0], w_ff1=w_ff1[0], w_ff2=w_ff2[0])
    P1 = dict(g_mix_pre=g_mix_pre[1], g_mix_post=g_mix_post[1], g_ffn_pre=g_ffn_pre[1], g_ffn_post=g_ffn_post[1],
              w_in_b=w_in_b[0], w_out_b=w_out_b[0], w_ff1=w_ff1[1], w_ff2=w_ff2[1])
    rows4 = lambda x, bsz: x.reshape(bsz, -1, NSA_KV, HEAD_DIM)

    mem2 = mem_prompt.reshape(bp * mem_len, d)
    mkv = [matmul(mem2, jnp.concatenate([w_mem_k[l], w_mem_v[l]], axis=1), g=g_mem[l]).reshape(bp, mem_len, -1)
           for l in range(depth)]
    mem_k_p = jnp.stack([x[..., :MEM_WIDTH] for x in mkv])
    mem_v_p = jnp.stack([x[..., MEM_WIDTH:] for x in mkv])
    mem_p = lambda l: (lambda q, q_col: mem_attention(q, q_col, mkv[l], mkv[l], k_col=0, v_col=1))

    nh = w0.shape[1] // RWKV_HEAD_DIM
    shift0 = jnp.zeros((bp, d), f32)
    wkv0 = jnp.zeros((bp, nh, RWKV_HEAD_DIM, RWKV_HEAD_DIM), f32)
    h, wkv_p, shift_p = rwkv_mem_layer(x_prompt, shift0, wkv0, mem_p(0), P0)

    c2p, s2p = _rope_tables(jnp.arange(tp, dtype=jnp.int32))
    rows_p, kv_bf = kv_proj(h.reshape(bp * tp, d), g_kv, w_kv, c2p, s2p)
    as_p = lambda x: x.reshape(bp, tp, -1)
    ckc, cvc = compress_prompt(as_p(rows_p[0]), as_p(rows_p[1]), cmp_pos, cmp_w1, cmp_w2)
    n_cmp_p = (tp - CMP_BLOCK) // CMP_STRIDE + 1

    def attend_prompt(proj):
        return nsa_prompt(proj, GATE_COL, ckc, cvc, as_p(kv_bf), c2p, s2p, n_cmp_p)

    y_p = nsa_mem_layer(h, mem_p(1), P1, attend_prompt)
    cmp_k_p, cmp_v_p, slc_k_p, slc_v_p, win_k_p, win_v_p = [rows4(x, bp) for x in rows_p]
    n_keep = min(WINDOW, tp)
    win_k_p, win_v_p = win_k_p[:, tp - n_keep:], win_v_p[:, tp - n_keep:]

    mk_s, mv_s = cache_mem_k.reshape(-1, HEAD_DIM), cache_mem_v.reshape(-1, HEAD_DIM)
    mem_s = lambda l: (lambda q, q_col: mem_attention(q, q_col, mk_s, mv_s, cached=(l, mem_len)))
    h, wkv_s, shift_s = rwkv_mem_layer(x_sample, state_shift[0], state_wkv[0], mem_s(0), P0)

    c2s, s2s = _rope_tables(past + jnp.arange(ts, dtype=jnp.int32))
    rows_s, _ = kv_proj(h.reshape(bs * ts, d), g_kv, w_kv, jnp.tile(c2s, (bs, 1)), jnp.tile(s2s, (bs, 1)))
    as_s = lambda x: x.reshape(bs, ts, -1)
    n_cmp_s = (past + ts - CMP_BLOCK) // CMP_STRIDE + 1
    assert (n_cmp_s - 1) * CMP_STRIDE + CMP_BLOCK <= past
    n_pool = cache_cmp_k.shape[0]
    pool = lambda x: x.reshape(n_pool * PAGE_ROWS, HEAD_DIM)
    ckc_s, cvc_s = compress_paged(pool(cache_cmp_k), pool(cache_cmp_v), page_table, n_cmp_s, cmp_pos, cmp_w1, cmp_w2)
    win_k2, win_v2 = cache_win_k.reshape(-1, HEAD_DIM), cache_win_v.reshape(-1, HEAD_DIM)

    def attend_sample(proj):
        return nsa_sample(proj, GATE_COL, ckc_s, cvc_s, pool(cache_slc_k), pool(cache_slc_v), page_table,
                          [as_s(x) for x in rows_s[2:]], win_k2, win_v2, c2s, s2s, n_cmp_s, past)

    y_s = nsa_mem_layer(h, mem_s(1), P1, attend_sample)
    cmp_k_s, cmp_v_s, slc_k_s, slc_v_s, wk_new, wv_new = [rows4(x, bs) for x in rows_s]
    win_k_s = jnp.concatenate([cache_win_k, wk_new], axis=1)[:, ts:]
    win_v_s = jnp.concatenate([cache_win_v, wv_new], axis=1)[:, ts:]

    return (y_p, y_s, mem_k_p.reshape(depth, bp, mem_len, MEM_HEADS, HEAD_DIM),
            mem_v_p.reshape(depth, bp, mem_len, MEM_HEADS, HEAD_DIM),
            wkv_p[None], shift_p[None], cmp_k_p, cmp_v_p, slc_k_p, slc_v_p, win_k_p, win_v_p,
            wkv_s[None], shift_s[None], cmp_k_s, cmp_v_s, slc_k_s, slc_v_s, win_k_s, win_v_s)
```

```python
import functools

import jax
import jax.numpy as jnp
from jax import lax
from jax.experimental import pallas as pl
from jax.experimental.pallas import tpu as pltpu

f32 = jnp.float32
bf16 = jnp.bfloat16

LANES = 128
VMEM_LIMIT_BYTES = 56 * 1024 * 1024

HEAD_DIM = 128
MEM_HEADS = 4
MEM_WIDTH = MEM_HEADS * HEAD_DIM
RWKV_HEAD_DIM = 64
GN_EPS = 64e-5
NSA_KV = 2
NSA_GROUP = 6
NSA_HEADS = NSA_KV * NSA_GROUP
NSA_WIDTH = NSA_HEADS * HEAD_DIM
KV_COLS = NSA_KV * HEAD_DIM
CMP_BLOCK = 32
CMP_STRIDE = 16
SLC_BLOCK = 64
SLC_SHIFT = 6
N_SELECT = 16
WINDOW = 512
Q_BLOCK = 128
ROPE_THETA = 10000.0
NORM_EPS = 1e-6
NEG_INF = -1e30
FORCE_SCORE = 1e9
PAGE_SIZE = 128

ROW_TILE = 512
FFN_ROW_TILE = 1024
SLC_CHUNK = 512
SEL_LANES = 128
SEQ_PER_STEP = 2
MEM_SEQ_PER_STEP = 4


def _params(*sem):
    return pltpu.CompilerParams(dimension_semantics=sem, vmem_limit_bytes=VMEM_LIMIT_BYTES)


def _rms(x, g):
    return x * lax.rsqrt(jnp.mean(x * x, axis=-1, keepdims=True) + NORM_EPS) * g


def _sigmoid(x):
    return 1.0 / (1.0 + jnp.exp(-x))


def _rmsnorm_kernel(x_ref, g_ref, o_ref):
    o_ref[...] = _rms(x_ref[...], g_ref[...])


def rmsnorm(x, g):
    m, d = x.shape
    tm = min(ROW_TILE, m)
    return pl.pallas_call(
        _rmsnorm_kernel,
        grid=(m // tm,),
        in_specs=[pl.BlockSpec((tm, d), lambda i: (i, 0)), pl.BlockSpec((1, d), lambda i: (0, 0))],
        out_specs=pl.BlockSpec((tm, d), lambda i: (i, 0)),
        out_shape=jax.ShapeDtypeStruct((m, d), f32),
        compiler_params=_params("parallel"),
        name="rmsnorm",
    )(x, g.reshape(1, d))


def _mm_kernel(x_ref, g_ref, w_ref, o_ref, xn_ref, *, norm):
    @pl.when(pl.program_id(1) == 0)
    def _():
        x = x_ref[...]
        if norm:
            x = _rms(x, g_ref[...])
        xn_ref[...] = x.astype(bf16)

    o_ref[...] = jnp.dot(xn_ref[...], w_ref[...], preferred_element_type=f32)


def matmul(x, w, g=None, tn=512):
    m, k = x.shape
    n = w.shape[1]
    tm = min(ROW_TILE, m)
    tn = min(tn, n)
    assert m % tm == 0 and n % tn == 0, (m, n, tm, tn)
    gg = jnp.ones((1, k), f32) if g is None else g.reshape(1, k)
    return pl.pallas_call(
        functools.partial(_mm_kernel, norm=g is not None),
        grid=(m // tm, n // tn),
        in_specs=[pl.BlockSpec((tm, k), lambda i, j: (i, 0)),
                  pl.BlockSpec((1, k), lambda i, j: (0, 0)),
                  pl.BlockSpec((k, tn), lambda i, j: (0, j))],
        out_specs=pl.BlockSpec((tm, tn), lambda i, j: (i, j)),
        out_shape=jax.ShapeDtypeStruct((m, n), f32),
        scratch_shapes=[pltpu.VMEM((tm, k), bf16)],
        compiler_params=_params("parallel", "arbitrary"),
        name="matmul",
    )(x, gg, w.astype(bf16))


def _out_proj_kernel(oa_ref, ob_ref, w_ref, h_ref, g_ref, y_ref):
    ka = oa_ref.shape[1]
    acc = jnp.dot(oa_ref[...].astype(bf16), w_ref[:ka, :], preferred_element_type=f32)
    acc += jnp.dot(ob_ref[...].astype(bf16), w_ref[ka:, :], preferred_element_type=f32)
    y_ref[...] = h_ref[...] + _rms(acc, g_ref[...])


def out_proj_residual(oa, ob, ob_col, w, h, g):
    m, ka = oa.shape
    d = w.shape[1]
    kb = w.shape[0] - ka
    tm = min(ROW_TILE, m)
    return pl.pallas_call(
        _out_proj_kernel,
        grid=(m // tm,),
        in_specs=[pl.BlockSpec((tm, ka), lambda i: (i, 0)),
                  pl.BlockSpec((tm, kb), lambda i: (i, ob_col)),
                  pl.BlockSpec((ka + kb, d), lambda i: (0, 0)),
                  pl.BlockSpec((tm, d), lambda i: (i, 0)),
                  pl.BlockSpec((1, d), lambda i: (0, 0))],
        out_specs=pl.BlockSpec((tm, d), lambda i: (i, 0)),
        out_shape=jax.ShapeDtypeStruct((m, d), f32),
        compiler_params=_params("parallel"),
        name="out_proj",
    )(oa, ob, w.astype(bf16), h, g.reshape(1, d))


def _ffn_kernel(h_ref, gpre_ref, w1_ref, w2_ref, gpost_ref, y_ref, xn_ref):
    j = pl.program_id(1)

    @pl.when(j == 0)
    def _():
        xn_ref[...] = _rms(h_ref[...], gpre_ref[...]).astype(bf16)
        y_ref[...] = jnp.zeros_like(y_ref)

    u = jnp.dot(xn_ref[...], w1_ref[...], preferred_element_type=f32)
    u = jnp.square(jnp.maximum(u, 0.0))
    y_ref[...] += jnp.dot(u.astype(bf16), w2_ref[...], preferred_element_type=f32)

    @pl.when(j == pl.num_programs(1) - 1)
    def _():
        y_ref[...] = h_ref[...] + _rms(y_ref[...], gpost_ref[...])


def ffn_residual(h, g_pre, w1, w2, g_post, tf=512):
    m, d = h.shape
    dff = w1.shape[1]
    tm = min(FFN_ROW_TILE, m)
    return pl.pallas_call(
        _ffn_kernel,
        grid=(m // tm, dff // tf),
        in_specs=[pl.BlockSpec((tm, d), lambda i, j: (i, 0)),
                  pl.BlockSpec((1, d), lambda i, j: (0, 0)),
                  pl.BlockSpec((d, tf), lambda i, j: (0, j)),
                  pl.BlockSpec((tf, d), lambda i, j: (j, 0)),
                  pl.BlockSpec((1, d), lambda i, j: (0, 0))],
        out_specs=pl.BlockSpec((tm, d), lambda i, j: (i, 0)),
        out_shape=jax.ShapeDtypeStruct((m, d), f32),
        scratch_shapes=[pltpu.VMEM((tm, d), bf16)],
        compiler_params=_params("parallel", "arbitrary"),
        name="ffn",
    )(h, g_pre.reshape(1, d), w1.astype(bf16), w2.astype(bf16), g_post.reshape(1, d))


def _rope_tile(x, c2, s2):
    return x * c2 + pltpu.roll(x, HEAD_DIM // 2, 1) * s2


def _rope_heads(q, c2, s2):
    return jnp.concatenate([_rope_tile(q[:, g * HEAD_DIM:(g + 1) * HEAD_DIM], c2, s2)
                            for g in range(NSA_GROUP)], axis=0)


N_KV_BRANCH = 6


def _kv_proj_kernel(h_ref, g_ref, w_ref, c2_ref, s2_ref, *refs):
    outs, bf_ref, xn_ref = refs[:N_KV_BRANCH], refs[N_KV_BRANCH], refs[N_KV_BRANCH + 1]
    j = pl.program_id(1)

    @pl.when(j == 0)
    def _():
        xn_ref[...] = _rms(h_ref[...], g_ref[...]).astype(bf16)

    acc = jnp.dot(xn_ref[...], w_ref[...], preferred_element_type=f32)
    for br in range(N_KV_BRANCH):
        @pl.when(j == br)
        def _(br=br):
            if br in (2, 4):
                c2, s2 = c2_ref[...], s2_ref[...]
                val = jnp.concatenate([_rope_tile(acc[:, kv * HEAD_DIM:(kv + 1) * HEAD_DIM], c2, s2)
                                       for kv in range(NSA_KV)], axis=1)
            else:
                val = acc
            outs[br][...] = val
            bf_ref[...] = val.astype(bf16)


def kv_proj(h, g, w, c2, s2):
    m, d = h.shape
    n = w.shape[1]
    assert n == N_KV_BRANCH * KV_COLS
    tm = min(ROW_TILE, m)
    ntab = c2.shape[0] // tm
    res = pl.pallas_call(
        _kv_proj_kernel,
        grid=(m // tm, N_KV_BRANCH),
        in_specs=[pl.BlockSpec((tm, d), lambda i, j: (i, 0)),
                  pl.BlockSpec((1, d), lambda i, j: (0, 0)),
                  pl.BlockSpec((d, KV_COLS), lambda i, j: (0, j)),
                  pl.BlockSpec((tm, HEAD_DIM), lambda i, j: (i % ntab, 0)),
                  pl.BlockSpec((tm, HEAD_DIM), lambda i, j: (i % ntab, 0))],
        out_specs=[pl.BlockSpec((tm, KV_COLS), lambda i, j: (i, 0))] * N_KV_BRANCH
                  + [pl.BlockSpec((tm, KV_COLS), lambda i, j: (i, j))],
        out_shape=[jax.ShapeDtypeStruct((m, KV_COLS), f32)] * N_KV_BRANCH + [jax.ShapeDtypeStruct((m, n), bf16)],
        scratch_shapes=[pltpu.VMEM((tm, d), bf16)],
        compiler_params=_params("parallel", "arbitrary"),
        name="kv_proj",
    )(h, g.reshape(1, d), w.astype(bf16), c2, s2)
    return res[:N_KV_BRANCH], res[N_KV_BRANCH]


def _mem_attn_kernel(q_ref, k_ref, v_ref, o_ref, *, heads_on_rows):
    scale = HEAD_DIM ** -0.5
    n_seq = q_ref.shape[0]
    for g in range(n_seq):
        for hd in range(MEM_HEADS):
            sl = slice(hd * HEAD_DIM, (hd + 1) * HEAD_DIM)
            if heads_on_rows:
                rows = k_ref.shape[0] // n_seq
                k = k_ref[pl.ds(g * rows + hd, rows // MEM_HEADS, stride=MEM_HEADS), :]
                v = v_ref[pl.ds(g * rows + hd, rows // MEM_HEADS, stride=MEM_HEADS), :]
            else:
                k, v = k_ref[g, :, sl], v_ref[g, :, sl]
            q = (q_ref[g, :, sl] * scale).astype(bf16)
            s = lax.dot_general(q, k.astype(bf16), (((1,), (1,)), ((), ())), preferred_element_type=f32)
            e = jnp.exp(s - jnp.max(s, axis=-1, keepdims=True))
            p = e / jnp.sum(e, axis=-1, keepdims=True)
            o_ref[g, :, sl] = jnp.dot(p.astype(bf16), v.astype(bf16), preferred_element_type=f32)


def mem_attention(q, q_col, mk, mv, k_col=0, v_col=0, cached=None):
    b, t, _ = q.shape
    w = MEM_WIDTH
    tq = min(ROW_TILE, t)
    if cached is None:
        g = 1
        mlen = mk.shape[1]
        kspec = pl.BlockSpec((1, mlen, w), lambda i, j: (i, 0, k_col))
        vspec = pl.BlockSpec((1, mlen, w), lambda i, j: (i, 0, v_col))
    else:
        layer, mlen = cached
        g = MEM_SEQ_PER_STEP if b % MEM_SEQ_PER_STEP == 0 else 1
        kspec = vspec = pl.BlockSpec((g * mlen * MEM_HEADS, HEAD_DIM), lambda i, j: (layer * (b // g) + i, 0))
    return pl.pallas_call(
        functools.partial(_mem_attn_kernel, heads_on_rows=cached is not None),
        grid=(b // g, t // tq),
        in_specs=[pl.BlockSpec((g, tq, w), lambda i, j: (i, j, q_col)), kspec, vspec],
        out_specs=pl.BlockSpec((g, tq, w), lambda i, j: (i, j, 0)),
        out_shape=jax.ShapeDtypeStruct((b, t, w), f32),
        compiler_params=_params("parallel", "arbitrary"),
        name="mem_attn",
    )(q, mk, mv)


WKV_QUAD = 4
WKV_LANES = WKV_QUAD * RWKV_HEAD_DIM
WKV_TB = 64


def _wkv_kernel(r_ref, w_ref, k_ref, kk_ref, b_ref, v_ref, s0_ref, yt_ref, st_ref,
                s_scr, lhs_scr, vd_scr, yl_scr, *, nb, nq, tb):
    n = RWKV_HEAD_DIM
    ti = pl.program_id(1)

    @pl.when(ti == 0)
    def _():
        for ib in range(nb):
            s_scr[ib * nq * n:(ib + 1) * nq * n, :] = s0_ref[ib]

    yt_ref[...] = jnp.zeros(yt_ref.shape, f32)
    ri = lax.broadcasted_iota(jnp.int32, (WKV_LANES, WKV_LANES), 0)
    ci = lax.broadcasted_iota(jnp.int32, (WKV_LANES, WKV_LANES), 1)
    ones_blk = jnp.where(lax.shift_right_logical(ri, 6) == lax.shift_right_logical(ci, 6), 1.0, 0.0).astype(bf16)
    ones_blk2 = jnp.concatenate([ones_blk, ones_blk], axis=0)
    eye_rep = jnp.where(lax.broadcasted_iota(jnp.int32, (n, WKV_LANES), 0)
                        == jnp.bitwise_and(lax.broadcasted_iota(jnp.int32, (n, WKV_LANES), 1), n - 1), 1.0, 0.0)
    step_lane = jnp.bitwise_and(lax.broadcasted_iota(jnp.int32, (nq * n, WKV_LANES), 1), n - 1)
    tiles = [(ib, q) for ib in range(nb) for q in range(nq)]

    def step(t, carry):
        row = lambda ref, ib, q: ref[ib, pl.ds(t, 1), q * WKV_LANES:(q + 1) * WKV_LANES]
        for ib, q in tiles:
            rs = slice((ib * nq + q) * n, (ib * nq + q + 1) * n)
            prod = s_scr[rs, :] * row(kk_ref, ib, q)
            hi = prod.astype(bf16)
            lhs_scr[rs, 0:WKV_LANES] = hi
            lhs_scr[rs, WKV_LANES:2 * WKV_LANES] = (prod - hi.astype(f32)).astype(bf16)
            vd_scr[rs, :] = (eye_rep * row(v_ref, ib, q)).astype(bf16)
        z = jnp.dot(lhs_scr[...], ones_blk2, preferred_element_type=f32)
        vcol = jnp.dot(vd_scr[...], ones_blk, preferred_element_type=f32)
        for ib, q in tiles:
            rs = slice((ib * nq + q) * n, (ib * nq + q + 1) * n)
            s = s_scr[rs, :] * row(w_ref, ib, q) - z[rs] * row(b_ref, ib, q) + vcol[rs] * row(k_ref, ib, q)
            s_scr[rs, :] = s
            yl_scr[rs, :] = (s * row(r_ref, ib, q)).astype(bf16)
        y = jnp.dot(yl_scr[...], ones_blk, preferred_element_type=f32)
        for ib in range(nb):
            rs = slice(ib * nq * n, (ib + 1) * nq * n)
            yt_ref[ib, 0] = jnp.where(step_lane == t, y[rs], yt_ref[ib, 0])
        return carry

    lax.fori_loop(0, tb, step, 0, unroll=4)

    @pl.when(ti == pl.num_programs(1) - 1)
    def _():
        for ib in range(nb):
            st_ref[ib] = s_scr[ib * nq * n:(ib + 1) * nq * n, :]


def wkv_scan(r, w, k, kk, b, v, s0, nb=2):
    bsz, t, width = r.shape
    n = RWKV_HEAD_DIM
    nh = width // n
    nq = nh // WKV_QUAD
    tb = min(WKV_TB, t)
    nblk = t // tb
    to_tiles = lambda s: s.reshape(bsz, nq, WKV_QUAD, n, n).transpose(0, 1, 3, 2, 4).reshape(bsz, nq * n, WKV_LANES)
    row = pl.BlockSpec((nb, tb, width), lambda i, j: (i, j, 0))
    st = pl.BlockSpec((nb, nq * n, WKV_LANES), lambda i, j: (i, 0, 0))
    rows_all = nb * nq * n
    yt, s_t = pl.pallas_call(
        functools.partial(_wkv_kernel, nb=nb, nq=nq, tb=tb),
        grid=(bsz // nb, nblk),
        in_specs=[row, row, row, row, row, row, st],
        out_specs=[pl.BlockSpec((nb, 1, nq * n, WKV_LANES), lambda i, j: (i, j, 0, 0)), st],
        out_shape=[jax.ShapeDtypeStruct((bsz, nblk, nq * n, WKV_LANES), f32),
                   jax.ShapeDtypeStruct((bsz, nq * n, WKV_LANES), f32)],
        scratch_shapes=[pltpu.VMEM((rows_all, WKV_LANES), f32),
                        pltpu.VMEM((rows_all, 2 * WKV_LANES), bf16),
                        pltpu.VMEM((rows_all, WKV_LANES), bf16),
                        pltpu.VMEM((rows_all, WKV_LANES), bf16)],
        compiler_params=_params("parallel", "arbitrary"),
        name="wkv_scan",
    )(r, w, k, kk, b, v, to_tiles(s0))
    y = yt.reshape(bsz, nblk, nq, n, WKV_QUAD, n).transpose(0, 1, 5, 2, 4, 3)[:, :, :tb].reshape(bsz, t, width)
    s_t = s_t.reshape(bsz, nq, n, WKV_QUAD, n).transpose(0, 1, 3, 2, 4).reshape(bsz, nh, n, n)
    return y, s_t


def _gelu_tanh(x):
    return 0.5 * x * (1.0 + jnp.tanh(0.7978845608028654 * (x + 0.044715 * x * x * x)))


def _chunk_rows(x_ref, n_chunks, row0=0, row_stride=1):
    return jnp.concatenate(
        [x_ref[pl.ds(row0 + s * row_stride, n_chunks, stride=CMP_STRIDE * row_stride), :]
         for s in range(CMP_STRIDE)], axis=1).astype(bf16)


def _compress_rows(x2, pos_ref, w1_ref, w2_ref, n_valid, n_heads):
    rows = x2.shape[0]
    n_chunks = rows // n_heads
    pab = jnp.dot(x2, w1_ref[...], preferred_element_type=f32)
    pos = jnp.dot(pos_ref[...], w1_ref[...], preferred_element_type=f32)
    posterm = pos[0:1, :HEAD_DIM] + pos[1:2, HEAD_DIM:]
    hid = pab[:, :HEAD_DIM] + pltpu.roll(pab[:, HEAD_DIM:], rows - 1, 0) + posterm
    out = jnp.dot(_gelu_tanh(hid).astype(bf16), w2_ref[...], preferred_element_type=f32)
    n = jnp.bitwise_and(lax.broadcasted_iota(jnp.int32, out.shape, 0), n_chunks - 1)
    return jnp.where(n < n_valid, out, 0.0)


def _compress_prompt_kernel(k_ref, v_ref, posk_ref, w1k_ref, w2k_ref, posv_ref, w1v_ref, w2v_ref,
                            ok_ref, ov_ref, *, n_chunks, n_valid):
    ok_ref[0, 0] = _compress_rows(_chunk_rows(k_ref.at[0], n_chunks), posk_ref, w1k_ref, w2k_ref, n_valid, 1)
    ov_ref[0, 0] = _compress_rows(_chunk_rows(v_ref.at[0], n_chunks), posv_ref, w1v_ref, w2v_ref, n_valid, 1)


def _cmp_weights(cmp_pos, cmp_w1, cmp_w2):
    ws = []
    for i in range(2):
        half = CMP_STRIDE * HEAD_DIM
        pos = _pad_to_rows(cmp_pos[i].reshape(2, half), 8).astype(bf16)
        w1 = cmp_w1[i].reshape(2, half, HEAD_DIM)
        ws += [pos, jnp.concatenate([w1[0], w1[1]], axis=1).astype(bf16), cmp_w2[i].astype(bf16)]
    return ws


_CMP_WEIGHT_SHAPES = [(8, CMP_STRIDE * HEAD_DIM), (CMP_STRIDE * HEAD_DIM, 2 * HEAD_DIM), (HEAD_DIM, HEAD_DIM)] * 2


def compress_prompt(ck, cv, cmp_pos, cmp_w1, cmp_w2):
    b, t, _ = ck.shape
    n_chunks = t // CMP_STRIDE
    n_valid = (t - CMP_BLOCK) // CMP_STRIDE + 1
    out = jax.ShapeDtypeStruct((b, NSA_KV, n_chunks, HEAD_DIM), f32)
    ospec = pl.BlockSpec((1, 1, n_chunks, HEAD_DIM), lambda i, kv: (i, kv, 0, 0))
    wspecs = [pl.BlockSpec(s, lambda i, kv: (0, 0)) for s in _CMP_WEIGHT_SHAPES]
    return pl.pallas_call(
        functools.partial(_compress_prompt_kernel, n_chunks=n_chunks, n_valid=n_valid),
        grid=(b, NSA_KV),
        in_specs=[pl.BlockSpec((1, t, HEAD_DIM), lambda i, kv: (i, 0, kv)),
                  pl.BlockSpec((1, t, HEAD_DIM), lambda i, kv: (i, 0, kv))] + wspecs,
        out_specs=[ospec, ospec],
        out_shape=[out, out],
        compiler_params=_params("parallel", "parallel"),
        name="compress_prompt",
    )(ck, cv, *_cmp_weights(cmp_pos, cmp_w1, cmp_w2))


PAGE_ROWS = PAGE_SIZE * NSA_KV


def _page_specs(n_pages, n_seq):
    return [pl.BlockSpec((PAGE_ROWS, HEAD_DIM), lambda i, pt, g=g, p=p: (pt[i * n_seq + g, p], 0))
            for g in range(n_seq) for p in range(n_pages)]


def _compress_paged_kernel(pt_ref, *refs, n_pages, n_seq, n_valid):
    np_all = n_seq * n_pages
    k_pages, v_pages = refs[:np_all], refs[np_all:2 * np_all]
    posk_ref, w1k_ref, w2k_ref, posv_ref, w1v_ref, w2v_ref, ok_ref, ov_ref = refs[2 * np_all:]
    per_page = PAGE_SIZE // CMP_STRIDE
    n_chunks = n_pages * per_page
    heads = [(g, kv) for g in range(n_seq) for kv in range(NSA_KV)]
    chunks = lambda pages: jnp.concatenate([_chunk_rows(pg, per_page, kv, NSA_KV)
                                            for g, kv in heads for pg in pages[g * n_pages:(g + 1) * n_pages]], axis=0)
    ok = _compress_rows(chunks(k_pages), posk_ref, w1k_ref, w2k_ref, n_valid, len(heads))
    ov = _compress_rows(chunks(v_pages), posv_ref, w1v_ref, w2v_ref, n_valid, len(heads))
    for i, (g, kv) in enumerate(heads):
        ok_ref[g, kv] = ok[i * n_chunks:(i + 1) * n_chunks]
        ov_ref[g, kv] = ov[i * n_chunks:(i + 1) * n_chunks]


def compress_paged(pool_k, pool_v, page_table, n_valid, cmp_pos, cmp_w1, cmp_w2):
    b, n_pages = page_table.shape
    n_seq = SEQ_PER_STEP if b % SEQ_PER_STEP == 0 else 1
    n_chunks = n_pages * PAGE_SIZE // CMP_STRIDE
    out = jax.ShapeDtypeStruct((b, NSA_KV, n_chunks, HEAD_DIM), f32)
    ospec = pl.BlockSpec((n_seq, NSA_KV, n_chunks, HEAD_DIM), lambda i, pt: (i, 0, 0, 0))
    wspecs = [pl.BlockSpec(s, lambda i, pt: (0, 0)) for s in _CMP_WEIGHT_SHAPES]
    np_all = n_seq * n_pages
    return pl.pallas_call(
        functools.partial(_compress_paged_kernel, n_pages=n_pages, n_seq=n_seq, n_valid=n_valid),
        grid_spec=pltpu.PrefetchScalarGridSpec(
            num_scalar_prefetch=1,
            grid=(b // n_seq,),
            in_specs=_page_specs(n_pages, n_seq) * 2 + wspecs,
            out_specs=[ospec, ospec]),
        out_shape=[out, out],
        compiler_params=_params("parallel"),
        name="compress_paged",
    )(page_table, *([pool_k] * np_all), *([pool_v] * np_all), *_cmp_weights(cmp_pos, cmp_w1, cmp_w2))


def _stack_heads(x):
    return jnp.concatenate([x[:, g * HEAD_DIM:(g + 1) * HEAD_DIM] for g in range(NSA_GROUP)], axis=0)


def _dot_nt(a, b):
    return lax.dot_general(a, b, (((1,), (1,)), ((), ())), preferred_element_type=f32)


def _softmax_heads(s, bias, tq):
    s3 = s.reshape(NSA_GROUP, tq, s.shape[1]) + bias[None]
    e = jnp.exp(s3 - jnp.max(s3, axis=-1, keepdims=True))
    return e, jnp.sum(e, axis=-1, keepdims=True)


def _compressed_branch(qc, ckc, cvc, pos_t, n_cmp, tq):
    s = _dot_nt(qc, ckc)
    n = lax.broadcasted_iota(jnp.int32, (tq, s.shape[1]), 1)
    vis = (n * CMP_STRIDE + (CMP_BLOCK - 1) <= pos_t) & (n < n_cmp)
    e, denom = _softmax_heads(s, jnp.where(vis, 0.0, NEG_INF), tq)
    any_vis = jnp.where(pos_t >= CMP_BLOCK - 1, 1.0, 0.0)
    p = e / denom * any_vis[None]
    o = jnp.dot(p.reshape(s.shape).astype(bf16), cvc, preferred_element_type=f32)
    return o, jnp.sum(p, axis=0)


def _select_blocks(psum, tq, pos0, n_slc, n_j):
    if tq < SEL_LANES:
        psum = jnp.concatenate([psum, jnp.zeros((SEL_LANES - tq, psum.shape[1]), f32)], axis=0)
    n_c = psum.shape[1]
    j = lax.broadcasted_iota(jnp.int32, (n_j, n_c), 0)
    cs = lax.broadcasted_iota(jnp.int32, (n_j, n_c), 1) * CMP_STRIDE
    overlap = jnp.where((cs < j * SLC_BLOCK + SLC_BLOCK) & (cs + (CMP_BLOCK - 1) >= j * SLC_BLOCK), 1.0, 0.0)
    imp_t = lax.dot_general(overlap, psum, (((1,), (1,)), ((), ())),
                            preferred_element_type=f32, precision=lax.Precision.HIGHEST)
    j = lax.broadcasted_iota(jnp.int32, imp_t.shape, 0)
    pos_t = pos0 + lax.broadcasted_iota(jnp.int32, imp_t.shape, 1)
    cur = lax.shift_right_logical(pos_t, SLC_SHIFT)
    causal = j * SLC_BLOCK <= pos_t
    forced = (j == 0) | (j == cur) | (j == cur - 1)
    score = jnp.where(causal, jnp.where(forced, FORCE_SCORE, imp_t), -FORCE_SCORE)
    score = jnp.where(j < n_slc, score, -2.0 * FORCE_SCORE)
    rank = jnp.zeros(imp_t.shape, f32)
    for jp in range(n_slc):
        row = score[jp:jp + 1, :]
        ahead = (row > score) | ((row == score) & (j > jp))
        rank = rank + jnp.where(ahead, 1.0, 0.0)
    sel_t = jnp.where(rank < min(N_SELECT, n_slc), 1.0, 0.0)
    return sel_t.T[0:tq]


def _selection_bias(sel, key0, n_keys):
    nj = sel.shape[1]
    j = lax.broadcasted_iota(jnp.int32, (nj, n_keys), 0)
    kpos = key0 + lax.broadcasted_iota(jnp.int32, (nj, n_keys), 1)
    e = jnp.where(lax.shift_right_logical(kpos, SLC_SHIFT) == j, 1.0, 0.0).astype(bf16)
    return jnp.dot(jnp.where(sel > 0.5, 0.0, NEG_INF).astype(bf16), e, preferred_element_type=f32)


def _window_branch(qr, wk, wv, kpos0, n_keys_valid, pos_t, tq, n_phantom=None):
    s = _dot_nt(qr, wk)
    lane = lax.broadcasted_iota(jnp.int32, (tq, s.shape[1]), 1)
    kpos = kpos0 + lane
    valid = (kpos <= pos_t) & (pos_t - kpos < WINDOW) & (lane < n_keys_valid)
    s3 = s.reshape(NSA_GROUP, tq, s.shape[1]) + jnp.where(valid, 0.0, NEG_INF)[None]
    m = jnp.max(s3, axis=-1, keepdims=True)
    if n_phantom is not None:
        m = jnp.where(n_phantom[None] > 0.0, jnp.maximum(m, 0.0), m)
    e = jnp.exp(s3 - m)
    denom = jnp.sum(e, axis=-1, keepdims=True)
    if n_phantom is not None:
        denom = denom + n_phantom[None] * jnp.exp(-m)
    return jnp.dot((e / denom).reshape(s.shape).astype(bf16), wv, preferred_element_type=f32)


def _gated_sum(gate, cols, tq, o_cmp, o_slc, o_win, g):
    r = slice(g * tq, (g + 1) * tq)
    c, s, w = cols[0] + g, cols[1] + g, cols[2] + g
    return gate[:, c:c + 1] * o_cmp[r] + gate[:, s:s + 1] * o_slc[r] + gate[:, w:w + 1] * o_win[r]


QK_SCALE = HEAD_DIM ** -0.5


def _nsa_prompt_kernel(q_ref, c2_ref, s2_ref, gate_ref, ckc_ref, cvc_ref, sk_ref, sv_ref, wk_ref, wv_ref,
                       o_ref, *, n_cmp, n_slc):
    i = pl.program_id(2)
    tq = Q_BLOCK
    rows = NSA_GROUP * tq
    q0 = i * tq
    q = q_ref[0] * QK_SCALE
    qc = _stack_heads(q).astype(bf16)
    qr = _rope_heads(q, c2_ref[...], s2_ref[...]).astype(bf16)
    pos_t = q0 + lax.broadcasted_iota(jnp.int32, (tq, 1), 0)

    o_cmp, psum = _compressed_branch(qc, ckc_ref[0, 0].astype(bf16), cvc_ref[0, 0].astype(bf16), pos_t, n_cmp, tq)
    sel = _select_blocks(psum, tq, q0, n_slc, n_slc)

    def slc_step(c, carry, causal):
        m, l, acc = carry
        k0 = pl.multiple_of(c * SLC_CHUNK, SLC_CHUNK)
        bias = _selection_bias(sel, k0, SLC_CHUNK)
        if causal:
            kpos = k0 + lax.broadcasted_iota(jnp.int32, bias.shape, 1)
            bias = jnp.where(kpos <= pos_t, bias, NEG_INF)
        s3 = _dot_nt(qr, sk_ref[0, pl.ds(k0, SLC_CHUNK), :]).reshape(NSA_GROUP, tq, SLC_CHUNK) + bias[None]
        m_new = jnp.maximum(m, jnp.max(s3, axis=-1, keepdims=True))
        alpha = jnp.exp(m - m_new)
        e = jnp.exp(s3 - m_new)
        l = alpha * l + jnp.sum(e, axis=-1, keepdims=True)
        pv = jnp.dot(e.reshape(rows, SLC_CHUNK).astype(bf16), sv_ref[0, pl.ds(k0, SLC_CHUNK), :],
                     preferred_element_type=f32)
        return m_new, l, alpha * acc + pv.reshape(NSA_GROUP, tq, HEAD_DIM)

    c_last = q0 // SLC_CHUNK
    init = (jnp.full((NSA_GROUP, tq, 1), NEG_INF, f32), jnp.zeros((NSA_GROUP, tq, 1), f32),
            jnp.zeros((NSA_GROUP, tq, HEAD_DIM), f32))
    carry = lax.fori_loop(0, c_last, functools.partial(slc_step, causal=False), init)
    _, l, acc = slc_step(c_last, carry, causal=True)
    o_slc = (acc / l).reshape(rows, HEAD_DIM)

    span = WINDOW + tq
    w0 = pl.multiple_of(jnp.maximum(q0 - WINDOW, 0), tq)
    n_phantom = jnp.maximum(WINDOW - 1 - pos_t, 0).astype(f32)
    o_win = _window_branch(qr, wk_ref[0, pl.ds(w0, span), :], wv_ref[0, pl.ds(w0, span), :], w0, span, pos_t, tq,
                           n_phantom)

    gate = _sigmoid(gate_ref[0])
    first_kv = pl.program_id(1) == 0
    for g in range(NSA_GROUP):
        head = [_gated_sum(gate, tuple(br * NSA_HEADS + kv * NSA_GROUP for br in range(3)), tq, o_cmp, o_slc, o_win, g)
                for kv in range(NSA_KV)]
        o_ref[0, :, g * HEAD_DIM:(g + 1) * HEAD_DIM] = jnp.where(first_kv, head[0], head[1])


def nsa_prompt(proj, gate_col, ckc, cvc, kv_bf, c2, s2, n_cmp):
    b, t, _ = proj.shape
    assert t % SLC_CHUNK == 0 and t >= WINDOW + Q_BLOCK
    n_slc = t // SLC_BLOCK
    gw = NSA_GROUP * HEAD_DIM
    kvcol = lambda c: pl.BlockSpec((1, t, HEAD_DIM), lambda bi, kv, i, c=c: (bi, 0, c * NSA_KV + kv))
    cmp_spec = pl.BlockSpec((1, 1, ckc.shape[2], HEAD_DIM), lambda bi, kv, i: (bi, kv, 0, 0))
    return pl.pallas_call(
        functools.partial(_nsa_prompt_kernel, n_cmp=n_cmp, n_slc=n_slc),
        grid=(b, NSA_KV, t // Q_BLOCK),
        in_specs=[pl.BlockSpec((1, Q_BLOCK, gw), lambda bi, kv, i: (bi, i, kv)),
                  pl.BlockSpec((Q_BLOCK, HEAD_DIM), lambda bi, kv, i: (i, 0)),
                  pl.BlockSpec((Q_BLOCK, HEAD_DIM), lambda bi, kv, i: (i, 0)),
                  pl.BlockSpec((1, Q_BLOCK, LANES), lambda bi, kv, i: (bi, i, gate_col)),
                  cmp_spec, cmp_spec, kvcol(2), kvcol(3), kvcol(4), kvcol(5)],
        out_specs=pl.BlockSpec((1, Q_BLOCK, gw), lambda bi, kv, i: (bi, i, kv)),
        out_shape=jax.ShapeDtypeStruct((b, t, NSA_WIDTH), f32),
        compiler_params=_params("parallel", "parallel", "arbitrary"),
        name="nsa_prompt",
    )(proj, c2, s2, proj, ckc, cvc, kv_bf, kv_bf, kv_bf, kv_bf)


def _pad_rows(x, n):
    return jnp.concatenate([x, jnp.zeros((n - x.shape[0], x.shape[1]), x.dtype)], axis=0)


def _nsa_sample_kernel(pt_ref, *refs, n_pages, n_seq, n_cmp, n_slc, n_j, past, tq):
    np_all = n_seq * n_pages
    k_pages, v_pages = refs[:np_all], refs[np_all:2 * np_all]
    (q_ref, c2_ref, s2_ref, gate_ref, ckc_ref, cvc_ref, nsk_ref, nsv_ref, nwk_ref, nwv_ref,
     wink_ref, winv_ref, o_ref, owk_ref, owv_ref) = refs[2 * np_all:]
    pos_t = past + lax.broadcasted_iota(jnp.int32, (tq, 1), 0)
    lw2 = wink_ref.shape[0] // n_seq
    lw = lw2 // NSA_KV
    c2, s2 = c2_ref[...], s2_ref[...]
    for g in range(n_seq):
        gate = _sigmoid(gate_ref[g])
        keep = lw2 - tq * NSA_KV
        for cache_ref, new_ref, out_ref in ((wink_ref, nwk_ref, owk_ref), (winv_ref, nwv_ref, owv_ref)):
            out_ref[g * lw2:g * lw2 + keep, :] = cache_ref[g * lw2 + tq * NSA_KV:(g + 1) * lw2, :]
            for kv in range(NSA_KV):
                out_ref[pl.ds(g * lw2 + keep + kv, tq, stride=NSA_KV), :] = new_ref[g, :, kv * HEAD_DIM:(kv + 1) * HEAD_DIM]
        for kv in range(NSA_KV):
            ksl = slice(kv * HEAD_DIM, (kv + 1) * HEAD_DIM)
            q = q_ref[g, :, kv * NSA_GROUP * HEAD_DIM:(kv + 1) * NSA_GROUP * HEAD_DIM] * QK_SCALE
            qc = _stack_heads(q).astype(bf16)
            qr = _rope_heads(q, c2, s2).astype(bf16)
            o_cmp, psum = _compressed_branch(qc, ckc_ref[g, kv].astype(bf16), cvc_ref[g, kv].astype(bf16), pos_t,
                                             n_cmp, tq)
            sel = _select_blocks(psum, tq, past, n_slc, n_j)

            paged = lambda pages: [pg[pl.ds(kv, PAGE_SIZE, stride=NSA_KV), :] for pg in pages[g * n_pages:(g + 1) * n_pages]]
            sk = jnp.concatenate(paged(k_pages) + [_pad_rows(nsk_ref[g, :, ksl], LANES)], axis=0).astype(bf16)
            sv = jnp.concatenate(paged(v_pages) + [_pad_rows(nsv_ref[g, :, ksl], LANES)], axis=0).astype(bf16)
            n_keys = sk.shape[0]
            bias = _selection_bias(sel, 0, n_keys)
            bias = jnp.where(lax.broadcasted_iota(jnp.int32, bias.shape, 1) <= pos_t, bias, NEG_INF)
            e, denom = _softmax_heads(_dot_nt(qr, sk), bias, tq)
            o_slc = jnp.dot((e / denom).reshape(NSA_GROUP * tq, n_keys).astype(bf16), sv, preferred_element_type=f32)

            cached = lambda ref: ref[pl.ds(g * lw2 + kv, lw, stride=NSA_KV), :]
            wk = jnp.concatenate([cached(wink_ref), _pad_rows(nwk_ref[g, :, ksl], LANES)], axis=0).astype(bf16)
            wv = jnp.concatenate([cached(winv_ref), _pad_rows(nwv_ref[g, :, ksl], LANES)], axis=0).astype(bf16)
            o_win = _window_branch(qr, wk, wv, past - lw, lw + tq, pos_t, tq)

            cols = tuple(br * NSA_HEADS + kv * NSA_GROUP for br in range(3))
            for hg in range(NSA_GROUP):
                hd = kv * NSA_GROUP + hg
                o_ref[g, :, hd * HEAD_DIM:(hd + 1) * HEAD_DIM] = _gated_sum(gate, cols, tq, o_cmp, o_slc, o_win, hg)


def nsa_sample(proj, gate_col, ckc, cvc, pool_k, pool_v, page_table, new_rows, win_k, win_v, c2, s2, n_cmp, past):
    b, tq, _ = proj.shape
    n_pages = page_table.shape[1]
    assert past == n_pages * PAGE_SIZE and past % SLC_BLOCK == 0 and tq <= SLC_BLOCK and (tq * NSA_KV) % 8 == 0
    n_seq = SEQ_PER_STEP if b % SEQ_PER_STEP == 0 else 1
    n_slc = past // SLC_BLOCK + 1
    n_j = -(-n_slc // SLC_BLOCK) * SLC_BLOCK
    lw2 = win_k.shape[0] // b
    per_b = lambda shape: pl.BlockSpec((n_seq,) + shape, lambda i, pt: (i,) + (0,) * len(shape))
    tab = pl.BlockSpec((tq, HEAD_DIM), lambda i, pt: (0, 0))
    win = pl.BlockSpec((n_seq * lw2, HEAD_DIM), lambda i, pt: (i, 0))
    np_all = n_seq * n_pages
    return pl.pallas_call(
        functools.partial(_nsa_sample_kernel, n_pages=n_pages, n_seq=n_seq, n_cmp=n_cmp, n_slc=n_slc, n_j=n_j,
                          past=past, tq=tq),
        grid_spec=pltpu.PrefetchScalarGridSpec(
            num_scalar_prefetch=1,
            grid=(b // n_seq,),
            in_specs=_page_specs(n_pages, n_seq) * 2 + [
                per_b((tq, NSA_WIDTH)), tab, tab,
                pl.BlockSpec((n_seq, tq, LANES), lambda i, pt: (i, 0, gate_col)),
                per_b((NSA_KV, ckc.shape[2], HEAD_DIM)), per_b((NSA_KV, ckc.shape[2], HEAD_DIM))]
                + [per_b((tq, KV_COLS))] * 4 + [win, win],
            out_specs=[per_b((tq, NSA_WIDTH)), win, win]),
        out_shape=[jax.ShapeDtypeStruct((b, tq, NSA_WIDTH), f32),
                   jax.ShapeDtypeStruct(win_k.shape, f32), jax.ShapeDtypeStruct(win_v.shape, f32)],
        compiler_params=_params("parallel"),
        name="nsa_sample",
    )(page_table, *([pool_k] * np_all), *([pool_v] * np_all), proj, c2, s2, proj, ckc, cvc, *new_rows, win_k, win_v)


def _prev_rows(x, tile, halo_ref, first_ref, seq_len):
    tm, c = x.shape
    prev = pltpu.roll(x, 1, 0)
    row = lax.broadcasted_iota(jnp.int32, (tm, 1), 0)
    if seq_len >= tm:
        tiles_per_seq = seq_len // tm
        first = first_ref[pl.ds(tile // tiles_per_seq, 1), :]
        edge = jnp.where(tile % tiles_per_seq == 0, first, halo_ref[7:8, :])
        return jnp.where(row == 0, edge, prev)
    pieces = []
    for j in range(tm // seq_len):
        pieces.append(jnp.broadcast_to(first_ref[j:j + 1, :], (8, c)))
        if seq_len > 8:
            pieces.append(jnp.zeros((seq_len - 8, c), f32))
    return jnp.where(jnp.bitwise_and(row, seq_len - 1) == 0, jnp.concatenate(pieces, axis=0), prev)


def _shift_specs(m, c, tm, seq_len, n_seq, col=None):
    cb = (lambda *g: 0) if col is None else col
    tile = pl.BlockSpec((tm, c), lambda *g: (g[0], cb(*g)))
    halo = pl.BlockSpec((8, c), lambda *g: (jnp.maximum(g[0] * (tm // 8) - 1, 0), cb(*g)))
    if seq_len >= tm:
        first = pl.BlockSpec((-(-n_seq // 8) * 8, c), lambda *g: (0, cb(*g)))
    else:
        first = pl.BlockSpec((tm // seq_len, c), lambda *g: (g[0], cb(*g)))
    return tile, halo, first


def _pad_first(first, seq_len, tm):
    return _pad_to_rows(first, -(-first.shape[0] // 8) * 8) if seq_len >= tm else first


def _head_sums(x):
    ri = lax.broadcasted_iota(jnp.int32, (WKV_LANES, WKV_LANES), 0)
    ci = lax.broadcasted_iota(jnp.int32, (WKV_LANES, WKV_LANES), 1)
    ones_blk = jnp.where(lax.shift_right_logical(ri, 6) == lax.shift_right_logical(ci, 6), 1.0, 0.0).astype(bf16)
    hi = x.astype(bf16)
    lo = (x - hi.astype(f32)).astype(bf16)
    out = []
    for c in range(x.shape[1] // WKV_LANES):
        sl = slice(c * WKV_LANES, (c + 1) * WKV_LANES)
        out.append(jnp.dot(hi[:, sl], ones_blk, preferred_element_type=f32)
                   + jnp.dot(lo[:, sl], ones_blk, preferred_element_type=f32))
    return jnp.concatenate(out, axis=1)


def _lora_kernel(hn_ref, halo_ref, first_ref, mu_ref, wd1_ref, wa1_ref, wg1_ref, wd2_ref, wa2_ref, wg2_ref,
                 w0_ref, a0_ref, decay_ref, a_ref, g_ref, *, seq_len):
    hn = hn_ref[...]
    xx = _prev_rows(hn, pl.program_id(0), halo_ref, first_ref, seq_len) - hn
    mix = lambda r: (hn + xx * mu_ref[r:r + 1, :]).astype(bf16)
    dot = lambda x, w_ref: jnp.dot(x, w_ref[...], preferred_element_type=f32)
    w_raw = w0_ref[...] + dot(jnp.tanh(dot(mix(0), wd1_ref)).astype(bf16), wd2_ref)
    softplus = jnp.maximum(-w_raw, 0.0) + jnp.log(1.0 + jnp.exp(-jnp.abs(w_raw)))
    decay_ref[...] = jnp.exp(-jnp.exp(-softplus - 0.5))
    a_ref[...] = _sigmoid(a0_ref[...] + dot(dot(mix(1), wa1_ref).astype(bf16), wa2_ref))
    g_ref[...] = dot(_sigmoid(dot(mix(2), wg1_ref)).astype(bf16), wg2_ref)


def rwkv_lora(hn, x_prev, seq_len, P):
    m, d = hn.shape
    rw = P['w0'].shape[0]
    tm = min(ROW_TILE, m)
    pad128 = lambda w: _pad_cols(w, -(-w.shape[1] // LANES) * LANES).astype(bf16)
    w1s = [pad128(P[k]) for k in ('w_decay1', 'w_aaa1', 'w_gate1')]
    w2s = [_pad_to_rows(P[k], w1.shape[1]).astype(bf16) for k, w1 in zip(('w_decay2', 'w_aaa2', 'w_gate2'), w1s)]
    full = lambda x: pl.BlockSpec(x.shape, lambda i: (0, 0))
    mu = _pad_to_rows(P['mu_wag'], 8)
    vecs = [P['w0'].reshape(1, rw), P['a0'].reshape(1, rw)]
    out = jax.ShapeDtypeStruct((m, rw), f32)
    ospec = pl.BlockSpec((tm, rw), lambda i: (i, 0))
    return pl.pallas_call(
        functools.partial(_lora_kernel, seq_len=seq_len),
        grid=(m // tm,),
        in_specs=list(_shift_specs(m, d, tm, seq_len, x_prev.shape[0])) + [full(mu)]
                 + [full(w) for w in w1s + w2s + vecs],
        out_specs=[ospec] * 3,
        out_shape=[out] * 3,
        compiler_params=_params("parallel"),
        name="rwkv_lora",
    )(hn, hn, _pad_first(x_prev, seq_len, tm), mu, *w1s, *w2s, *vecs)


def _prep_kernel(p_ref, halo_ref, first_ref, mu_ref, a_ref, kkw_ref, kaw_ref,
                 r_ref, k_ref, kk_ref, b_ref, v_ref, *, seq_len):
    j = pl.program_id(1)
    cur = p_ref[...]
    x = cur + (_prev_rows(cur, pl.program_id(0), halo_ref, first_ref, seq_len) - cur) * mu_ref[...]

    @pl.when(j == 0)
    def _():
        r_ref[...] = x

    @pl.when(j == 1)
    def _():
        a = a_ref[...]
        kk = x * kkw_ref[...]
        kk = kk * lax.rsqrt(_head_sums(kk * kk) + 1e-12)
        kk_ref[...] = kk
        b_ref[...] = kk * a
        k_ref[...] = x * (1.0 + (a - 1.0) * kaw_ref[...])

    @pl.when(j == 2)
    def _():
        v_ref[...] = x


def rwkv_prep(proj, p0, a_rate, seq_len, P):
    m = proj.shape[0]
    rw = a_rate.shape[1]
    tm = min(ROW_TILE, m)
    col = lambda i, j: j
    row1 = lambda v: v.reshape(1, -1)
    out = jax.ShapeDtypeStruct((m, rw), f32)
    ospec = pl.BlockSpec((tm, rw), lambda i, j: (i, 0))
    return pl.pallas_call(
        functools.partial(_prep_kernel, seq_len=seq_len),
        grid=(m // tm, 3),
        in_specs=list(_shift_specs(m, rw, tm, seq_len, p0.shape[0], col))
                 + [pl.BlockSpec((1, rw), lambda i, j: (0, j)),
                    pl.BlockSpec((tm, rw), lambda i, j: (i, 0)),
                    pl.BlockSpec((1, rw), lambda i, j: (0, 0)),
                    pl.BlockSpec((1, rw), lambda i, j: (0, 0))],
        out_specs=[ospec] * 5,
        out_shape=[out] * 5,
        compiler_params=_params("parallel", "arbitrary"),
        name="rwkv_prep",
    )(proj, proj, _pad_first(p0, seq_len, tm), row1(P['mu_rkv']), a_rate, row1(P['k_k']), row1(P['k_a']))


def _rwkv_out_kernel(y_ref, r_ref, k_ref, v_ref, g_ref, om_ref, h_ref, lw_ref, lb_ref, rk_ref, w_ref, gp_ref, o_ref):
    inv_n = 1.0 / RWKV_HEAD_DIM
    y = y_ref[...]
    d = y - _head_sums(y) * inv_n
    var = _head_sums(d * d) * inv_n
    yn = d * lax.rsqrt(var + GN_EPS) * lw_ref[...] + lb_ref[...]
    bonus = _head_sums(r_ref[...] * k_ref[...] * rk_ref[...]) * v_ref[...]
    o = ((yn + bonus) * g_ref[...]).astype(bf16)
    rw = o.shape[1]
    acc = jnp.dot(o, w_ref[:rw, :], preferred_element_type=f32)
    acc += jnp.dot(om_ref[...].astype(bf16), w_ref[rw:, :], preferred_element_type=f32)
    o_ref[...] = h_ref[...] + _rms(acc, gp_ref[...])


def rwkv_out(y, r, k, v, gate, o_mem, h, P):
    m, rw = y.shape
    d = h.shape[1]
    tm = min(ROW_TILE // 2, m)
    wide = pl.BlockSpec((tm, rw), lambda i: (i, 0))
    vec = pl.BlockSpec((1, rw), lambda i: (0, 0))
    return pl.pallas_call(
        _rwkv_out_kernel,
        grid=(m // tm,),
        in_specs=[wide] * 5 + [pl.BlockSpec((tm, o_mem.shape[1]), lambda i: (i, 0)),
                               pl.BlockSpec((tm, d), lambda i: (i, 0)), vec, vec, vec,
                               pl.BlockSpec((rw + o_mem.shape[1], d), lambda i: (0, 0)),
                               pl.BlockSpec((1, d), lambda i: (0, 0))],
        out_specs=pl.BlockSpec((tm, d), lambda i: (i, 0)),
        out_shape=jax.ShapeDtypeStruct((m, d), f32),
        compiler_params=_params("parallel"),
        name="rwkv_out",
    )(y, r, k, v, gate, o_mem, h, P['lnx_w'].reshape(1, rw), P['lnx_b'].reshape(1, rw), P['r_k'].reshape(1, rw),
      P['w_out_a'].astype(bf16), P['g_mix_post'].reshape(1, d))


def _rope_tables(pos):
    half = HEAD_DIM // 2
    inv = jnp.power(ROPE_THETA, -jnp.arange(half, dtype=f32) / half)
    ang = pos.astype(f32)[:, None] * inv[None, :]
    cos, sin = jnp.cos(ang), jnp.sin(ang)
    return jnp.concatenate([cos, cos], axis=-1), jnp.concatenate([-sin, sin], axis=-1)


def _pad_cols(w, n):
    return jnp.pad(w, ((0, 0), (0, n - w.shape[1])))


def _pad_to_rows(x, n):
    return jnp.pad(x, ((0, n - x.shape[0]), (0, 0)))


def rwkv_mem_layer(h, x_prev, s0, mem, P):
    b, t, d = h.shape
    m = b * t
    assert t & (t - 1) == 0 and t % 8 == 0 and (t % ROW_TILE == 0 or ROW_TILE % t == 0)
    h2 = h.reshape(m, d)
    hn = rmsnorm(h2, P['g_mix_pre'])
    rw = P['w0'].shape[0]

    proj = matmul(hn, P['w_in_a'])
    p0 = matmul(_pad_to_rows(x_prev, -(-b // 8) * 8), P['w_in_a'])[:b]
    decay, a_rate, gate = rwkv_lora(hn, x_prev, t, P)
    r, k, kk, kb, v = rwkv_prep(proj, p0, a_rate, t, P)
    as3 = lambda x: x.reshape(b, t, -1)
    y, s_t = wkv_scan(as3(r), as3(decay), as3(k), as3(kk), as3(kb), as3(v), s0)
    o_mem = mem(as3(proj), 3 * rw // MEM_WIDTH)
    h2 = rwkv_out(y.reshape(m, rw), r, k, v, gate, o_mem.reshape(m, MEM_WIDTH), h2, P)
    h2 = ffn_residual(h2, P['g_ffn_pre'], P['w_ff1'], P['w_ff2'], P['g_ffn_post'])
    return h2.reshape(b, t, d), s_t, hn.reshape(b, t, d)[:, -1]


GATE_COL = (NSA_WIDTH + MEM_WIDTH) // LANES


def nsa_mem_layer(h, mem, P, attend):
    b, t, d = h.shape
    m = b * t
    h2 = h.reshape(m, d)
    n_in = (GATE_COL + 1) * LANES
    n_in = -(-n_in // 768) * 768
    proj = matmul(h2, _pad_cols(P['w_in_b'].astype(bf16), n_in), g=P['g_mix_pre'], tn=768).reshape(b, t, -1)
    o_nsa = attend(proj)
    o_mem = mem(proj, NSA_WIDTH // MEM_WIDTH)
    h2 = out_proj_residual(o_nsa.reshape(m, NSA_WIDTH), o_mem.reshape(m, MEM_WIDTH), 0, P['w_out_b'], h2,
                           P['g_mix_post'])
    h2 = ffn_residual(h2, P['g_ffn_pre'], P['w_ff1'], P['w_ff2'], P['g_ffn_post'])
    return h2.reshape(b, t, d)


def kernel(x_prompt, x_sample, mem_prompt, cache_mem_k, cache_mem_v, state_wkv, state_shift, cache_cmp_k, cache_cmp_v, cache_slc_k, cache_slc_v, cache_win_k, cache_win_v, page_table, g_mix_pre, g_mix_post, g_ffn_pre, g_ffn_post, g_mem, w_mem_k, w_mem_v, w_in_a, mu_rkv, mu_wag, w0, w_decay1, w_decay2, a0, w_aaa1, w_aaa2, w_gate1, w_gate2, k_k, k_a, r_k, lnx_w, lnx_b, w_out_a, g_kv, w_kv, cmp_pos, cmp_w1, cmp_w2, w_in_b, w_out_b, w_ff1, w_ff2):
    bp, tp, d = x_prompt.shape
    bs, ts, _ = x_sample.shape
    depth = g_mix_pre.shape[0]
    assert depth == 2 and w_in_a.shape[0] == 1 and w_in_b.shape[0] == 1
    n_pages = page_table.shape[1]
    past = n_pages * PAGE_SIZE
    mem_len = mem_prompt.shape[1]

    P0 = dict(g_mix_pre=g_mix_pre[0], g_mix_post=g_mix_post[0], g_ffn_pre=g_ffn_pre[0], g_ffn_post=g_ffn_post[0],
              w_in_a=w_in_a[0], mu_rkv=mu_rkv[0], mu_wag=mu_wag[0], w0=w0[0], w_decay1=w_decay1[0],
              w_decay2=w_decay2[0], a0=a0[0], w_aaa1=w_aaa1[0], w_aaa2=w_aaa2[0], w_gate1=w_gate1[0],
              w_gate2=w_gate2[0], k_k=k_k[0], k_a=k_a[0], r_k=r_k[0], lnx_w=lnx_w[0], lnx_b=lnx_b[0],
              w_out_a=w_out_a[0], w_ff1=w_ff1[0], w_ff2=w_ff2[0])
    P1 = dict(g_mix_pre=g_mix_pre[1], g_mix_post=g_mix_post[1], g_ffn_pre=g_ffn_pre[1], g_ffn_post=g_ffn_post[1],
              w_in_b=w_in_b[0], w_out_b=w_out_b[0], w_ff1=w_ff1[1], w_ff2=w_ff2[1])
    rows4 = lambda x, bsz: x.reshape(bsz, -1, NSA_KV, HEAD_DIM)

    mem2 = mem_prompt.reshape(bp * mem_len, d)
    mkv = [matmul(mem2, jnp.concatenate([w_mem_k[l], w_mem_v[l]], axis=1), g=g_mem[l]).reshape(bp, mem_len, -1)
           for l in range(depth)]
    mem_k_p = jnp.stack([x[..., :MEM_WIDTH] for x in mkv])
    mem_v_p = jnp.stack([x[..., MEM_WIDTH:] for x in mkv])
    mem_p = lambda l: (lambda q, q_col: mem_attention(q, q_col, mkv[l], mkv[l], k_col=0, v_col=1))

    nh = w0.shape[1] // RWKV_HEAD_DIM
    shift0 = jnp.zeros((bp, d), f32)
    wkv0 = jnp.zeros((bp, nh, RWKV_HEAD_DIM, RWKV_HEAD_DIM), f32)
    h, wkv_p, shift_p = rwkv_mem_layer(x_prompt, shift0, wkv0, mem_p(0), P0)

    c2p, s2p = _rope_tables(jnp.arange(tp, dtype=jnp.int32))
    rows_p, kv_bf = kv_proj(h.reshape(bp * tp, d), g_kv, w_kv, c2p, s2p)
    as_p = lambda x: x.reshape(bp, tp, -1)
    ckc, cvc = compress_prompt(as_p(rows_p[0]), as_p(rows_p[1]), cmp_pos, cmp_w1, cmp_w2)
    n_cmp_p = (tp - CMP_BLOCK) // CMP_STRIDE + 1

    def attend_prompt(proj):
        return nsa_prompt(proj, GATE_COL, ckc, cvc, as_p(kv_bf), c2p, s2p, n_cmp_p)

    y_p = nsa_mem_layer(h, mem_p(1), P1, attend_prompt)
    cmp_k_p, cmp_v_p, slc_k_p, slc_v_p, win_k_p, win_v_p = [rows4(x, bp) for x in rows_p]
    n_keep = min(WINDOW, tp)
    win_k_p, win_v_p = win_k_p[:, tp - n_keep:], win_v_p[:, tp - n_keep:]

    mk_s, mv_s = cache_mem_k.reshape(-1, HEAD_DIM), cache_mem_v.reshape(-1, HEAD_DIM)
    mem_s = lambda l: (lambda q, q_col: mem_attention(q, q_col, mk_s, mv_s, cached=(l, mem_len)))
    h, wkv_s, shift_s = rwkv_mem_layer(x_sample, state_shift[0], state_wkv[0], mem_s(0), P0)

    c2s, s2s = _rope_tables(past + jnp.arange(ts, dtype=jnp.int32))
    rows_s, _ = kv_proj(h.reshape(bs * ts, d), g_kv, w_kv, jnp.tile(c2s, (bs, 1)), jnp.tile(s2s, (bs, 1)))
    as_s = lambda x: x.reshape(bs, ts, -1)
    n_cmp_s = (past + ts - CMP_BLOCK) // CMP_STRIDE + 1
    assert (n_cmp_s - 1) * CMP_STRIDE + CMP_BLOCK <= past
    n_pool = cache_cmp_k.shape[0]
    pool = lambda x: x.reshape(n_pool * PAGE_ROWS, HEAD_DIM)
    ckc_s, cvc_s = compress_paged(pool(cache_cmp_k), pool(cache_cmp_v), page_table, n_cmp_s, cmp_pos, cmp_w1, cmp_w2)
    win_k2, win_v2 = cache_win_k.reshape(-1, HEAD_DIM), cache_win_v.reshape(-1, HEAD_DIM)

    new_win = []

    def attend_sample(proj):
        o, wk_out, wv_out = nsa_sample(proj, GATE_COL, ckc_s, cvc_s, pool(cache_slc_k), pool(cache_slc_v), page_table,
                                       [as_s(x) for x in rows_s[2:]], win_k2, win_v2, c2s, s2s, n_cmp_s, past)
        new_win.extend([wk_out, wv_out])
        return o

    y_s = nsa_mem_layer(h, mem_s(1), P1, attend_sample)
    cmp_k_s, cmp_v_s, slc_k_s, slc_v_s = [rows4(x, bs) for x in rows_s[:4]]
    win_k_s, win_v_s = [x.reshape(cache_win_k.shape) for x in new_win]

    return (y_p, y_s, mem_k_p.reshape(depth, bp, mem_len, MEM_HEADS, HEAD_DIM),
            mem_v_p.reshape(depth, bp, mem_len, MEM_HEADS, HEAD_DIM),
            wkv_p[None], shift_p[None], cmp_k_p, cmp_v_p, slc_k_p, slc_v_p, win_k_p, win_v_p,
            wkv_s[None], shift_s[None], cmp_k_s, cmp_v_s, slc_k_s, slc_v_s, win_k_s, win_v_s)
```

```python
import functools

import jax
import jax.numpy as jnp
from jax import lax
from jax.experimental import pallas as pl
from jax.experimental.pallas import tpu as pltpu

f32 = jnp.float32
bf16 = jnp.bfloat16

LANES = 128
VMEM_LIMIT_BYTES = 56 * 1024 * 1024

HEAD_DIM = 128
MEM_HEADS = 4
MEM_WIDTH = MEM_HEADS * HEAD_DIM
RWKV_HEAD_DIM = 64
GN_EPS = 64e-5
NSA_KV = 2
NSA_GROUP = 6
NSA_HEADS = NSA_KV * NSA_GROUP
NSA_WIDTH = NSA_HEADS * HEAD_DIM
KV_COLS = NSA_KV * HEAD_DIM
CMP_BLOCK = 32
CMP_STRIDE = 16
SLC_BLOCK = 64
SLC_SHIFT = 6
N_SELECT = 16
WINDOW = 512
Q_BLOCK = 128
ROPE_THETA = 10000.0
NORM_EPS = 1e-6
NEG_INF = -1e30
FORCE_SCORE = 1e9
PAGE_SIZE = 128

ROW_TILE = 512
FFN_ROW_TILE = 1024
SLC_CHUNK = 512
SEL_LANES = 128
SEQ_PER_STEP = 2
MEM_SEQ_PER_STEP = 4


def _params(*sem):
    return pltpu.CompilerParams(dimension_semantics=sem, vmem_limit_bytes=VMEM_LIMIT_BYTES)


def _rms(x, g):
    return x * lax.rsqrt(jnp.mean(x * x, axis=-1, keepdims=True) + NORM_EPS) * g


def _sigmoid(x):
    return 1.0 / (1.0 + jnp.exp(-x))


def _rmsnorm_kernel(x_ref, g_ref, o_ref):
    o_ref[...] = _rms(x_ref[...], g_ref[...])


def rmsnorm(x, g):
    m, d = x.shape
    tm = min(ROW_TILE, m)
    return pl.pallas_call(
        _rmsnorm_kernel,
        grid=(m // tm,),
        in_specs=[pl.BlockSpec((tm, d), lambda i: (i, 0)), pl.BlockSpec((1, d), lambda i: (0, 0))],
        out_specs=pl.BlockSpec((tm, d), lambda i: (i, 0)),
        out_shape=jax.ShapeDtypeStruct((m, d), f32),
        compiler_params=_params("parallel"),
        name="rmsnorm",
    )(x, g.reshape(1, d))


def _mm_kernel(x_ref, g_ref, w_ref, o_ref, xn_ref, *, norm):
    @pl.when(pl.program_id(1) == 0)
    def _():
        x = x_ref[...]
        if norm:
            x = _rms(x, g_ref[...])
        xn_ref[...] = x.astype(bf16)

    o_ref[...] = jnp.dot(xn_ref[...], w_ref[...], preferred_element_type=f32)


def matmul(x, w, g=None, tn=512):
    m, k = x.shape
    n = w.shape[1]
    tm = min(FFN_ROW_TILE if m % FFN_ROW_TILE == 0 else ROW_TILE, m)
    tn = min(tn, n)
    assert m % tm == 0 and n % tn == 0, (m, n, tm, tn)
    gg = jnp.ones((1, k), f32) if g is None else g.reshape(1, k)
    return pl.pallas_call(
        functools.partial(_mm_kernel, norm=g is not None),
        grid=(m // tm, n // tn),
        in_specs=[pl.BlockSpec((tm, k), lambda i, j: (i, 0)),
                  pl.BlockSpec((1, k), lambda i, j: (0, 0)),
                  pl.BlockSpec((k, tn), lambda i, j: (0, j))],
        out_specs=pl.BlockSpec((tm, tn), lambda i, j: (i, j)),
        out_shape=jax.ShapeDtypeStruct((m, n), f32),
        scratch_shapes=[pltpu.VMEM((tm, k), bf16)],
        compiler_params=_params("parallel", "arbitrary"),
        name="matmul",
    )(x, gg, w.astype(bf16))


def _out_proj_kernel(oa_ref, ob_ref, w_ref, h_ref, g_ref, y_ref):
    ka = oa_ref.shape[1]
    acc = jnp.dot(oa_ref[...].astype(bf16), w_ref[:ka, :], preferred_element_type=f32)
    acc += jnp.dot(ob_ref[...].astype(bf16), w_ref[ka:, :], preferred_element_type=f32)
    y_ref[...] = h_ref[...] + _rms(acc, g_ref[...])


def out_proj_residual(oa, ob, ob_col, w, h, g):
    m, ka = oa.shape
    d = w.shape[1]
    kb = w.shape[0] - ka
    tm = min(ROW_TILE, m)
    return pl.pallas_call(
        _out_proj_kernel,
        grid=(m // tm,),
        in_specs=[pl.BlockSpec((tm, ka), lambda i: (i, 0)),
                  pl.BlockSpec((tm, kb), lambda i: (i, ob_col)),
                  pl.BlockSpec((ka + kb, d), lambda i: (0, 0)),
                  pl.BlockSpec((tm, d), lambda i: (i, 0)),
                  pl.BlockSpec((1, d), lambda i: (0, 0))],
        out_specs=pl.BlockSpec((tm, d), lambda i: (i, 0)),
        out_shape=jax.ShapeDtypeStruct((m, d), f32),
        compiler_params=_params("parallel"),
        name="out_proj",
    )(oa, ob, w.astype(bf16), h, g.reshape(1, d))


def _ffn_kernel(h_ref, gpre_ref, w1_ref, w2_ref, gpost_ref, y_ref, xn_ref):
    j = pl.program_id(1)

    @pl.when(j == 0)
    def _():
        xn_ref[...] = _rms(h_ref[...], gpre_ref[...]).astype(bf16)
        y_ref[...] = jnp.zeros_like(y_ref)

    u = jnp.dot(xn_ref[...], w1_ref[...], preferred_element_type=f32)
    u = jnp.square(jnp.maximum(u, 0.0))
    y_ref[...] += jnp.dot(u.astype(bf16), w2_ref[...], preferred_element_type=f32)

    @pl.when(j == pl.num_programs(1) - 1)
    def _():
        y_ref[...] = h_ref[...] + _rms(y_ref[...], gpost_ref[...])


def ffn_residual(h, g_pre, w1, w2, g_post, tf=512):
    m, d = h.shape
    dff = w1.shape[1]
    tm = min(FFN_ROW_TILE, m)
    return pl.pallas_call(
        _ffn_kernel,
        grid=(m // tm, dff // tf),
        in_specs=[pl.BlockSpec((tm, d), lambda i, j: (i, 0)),
                  pl.BlockSpec((1, d), lambda i, j: (0, 0)),
                  pl.BlockSpec((d, tf), lambda i, j: (0, j)),
                  pl.BlockSpec((tf, d), lambda i, j: (j, 0)),
                  pl.BlockSpec((1, d), lambda i, j: (0, 0))],
        out_specs=pl.BlockSpec((tm, d), lambda i, j: (i, 0)),
        out_shape=jax.ShapeDtypeStruct((m, d), f32),
        scratch_shapes=[pltpu.VMEM((tm, d), bf16)],
        compiler_params=_params("parallel", "arbitrary"),
        name="ffn",
    )(h, g_pre.reshape(1, d), w1.astype(bf16), w2.astype(bf16), g_post.reshape(1, d))


def _rope_tile(x, c2, s2):
    return x * c2 + pltpu.roll(x, HEAD_DIM // 2, 1) * s2


def _rope_heads(q, c2, s2):
    return jnp.concatenate([_rope_tile(q[:, g * HEAD_DIM:(g + 1) * HEAD_DIM], c2, s2)
                            for g in range(NSA_GROUP)], axis=0)


N_KV_BRANCH = 6


def _kv_proj_kernel(h_ref, g_ref, w_ref, c2_ref, s2_ref, *refs):
    outs, bf_ref, xn_ref = refs[:N_KV_BRANCH], refs[N_KV_BRANCH], refs[N_KV_BRANCH + 1]
    j = pl.program_id(1)

    @pl.when(j == 0)
    def _():
        xn_ref[...] = _rms(h_ref[...], g_ref[...]).astype(bf16)

    acc = jnp.dot(xn_ref[...], w_ref[...], preferred_element_type=f32)
    for br in range(N_KV_BRANCH):
        @pl.when(j == br)
        def _(br=br):
            if br in (2, 4):
                c2, s2 = c2_ref[...], s2_ref[...]
                val = jnp.concatenate([_rope_tile(acc[:, kv * HEAD_DIM:(kv + 1) * HEAD_DIM], c2, s2)
                                       for kv in range(NSA_KV)], axis=1)
            else:
                val = acc
            outs[br][...] = val
            bf_ref[...] = val.astype(bf16)


def kv_proj(h, g, w, c2, s2):
    m, d = h.shape
    n = w.shape[1]
    assert n == N_KV_BRANCH * KV_COLS
    tm = min(ROW_TILE, m)
    ntab = c2.shape[0] // tm
    res = pl.pallas_call(
        _kv_proj_kernel,
        grid=(m // tm, N_KV_BRANCH),
        in_specs=[pl.BlockSpec((tm, d), lambda i, j: (i, 0)),
                  pl.BlockSpec((1, d), lambda i, j: (0, 0)),
                  pl.BlockSpec((d, KV_COLS), lambda i, j: (0, j)),
                  pl.BlockSpec((tm, HEAD_DIM), lambda i, j: (i % ntab, 0)),
                  pl.BlockSpec((tm, HEAD_DIM), lambda i, j: (i % ntab, 0))],
        out_specs=[pl.BlockSpec((tm, KV_COLS), lambda i, j: (i, 0))] * N_KV_BRANCH
                  + [pl.BlockSpec((tm, KV_COLS), lambda i, j: (i, j))],
        out_shape=[jax.ShapeDtypeStruct((m, KV_COLS), f32)] * N_KV_BRANCH + [jax.ShapeDtypeStruct((m, n), bf16)],
        scratch_shapes=[pltpu.VMEM((tm, d), bf16)],
        compiler_params=_params("parallel", "arbitrary"),
        name="kv_proj",
    )(h, g.reshape(1, d), w.astype(bf16), c2, s2)
    return res[:N_KV_BRANCH], res[N_KV_BRANCH]


def _mem_attn_kernel(q_ref, k_ref, v_ref, o_ref, *, heads_on_rows):
    scale = HEAD_DIM ** -0.5
    n_seq = q_ref.shape[0]
    for g in range(n_seq):
        for hd in range(MEM_HEADS):
            sl = slice(hd * HEAD_DIM, (hd + 1) * HEAD_DIM)
            if heads_on_rows:
                rows = k_ref.shape[0] // n_seq
                k = k_ref[pl.ds(g * rows + hd, rows // MEM_HEADS, stride=MEM_HEADS), :]
                v = v_ref[pl.ds(g * rows + hd, rows // MEM_HEADS, stride=MEM_HEADS), :]
            else:
                k, v = k_ref[g, :, sl], v_ref[g, :, sl]
            q = (q_ref[g, :, sl] * scale).astype(bf16)
            s = lax.dot_general(q, k.astype(bf16), (((1,), (1,)), ((), ())), preferred_element_type=f32)
            e = jnp.exp(s - jnp.max(s, axis=-1, keepdims=True))
            p = e / jnp.sum(e, axis=-1, keepdims=True)
            o_ref[g, :, sl] = jnp.dot(p.astype(bf16), v.astype(bf16), preferred_element_type=f32)


def mem_attention(q, q_col, mk, mv, k_col=0, v_col=0, cached=None):
    b, t, _ = q.shape
    w = MEM_WIDTH
    tq = min(ROW_TILE, t)
    if cached is None:
        g = 1
        mlen = mk.shape[1]
        kspec = pl.BlockSpec((1, mlen, w), lambda i, j: (i, 0, k_col))
        vspec = pl.BlockSpec((1, mlen, w), lambda i, j: (i, 0, v_col))
    else:
        layer, mlen = cached
        g = MEM_SEQ_PER_STEP if b % MEM_SEQ_PER_STEP == 0 else 1
        kspec = vspec = pl.BlockSpec((g * mlen * MEM_HEADS, HEAD_DIM), lambda i, j: (layer * (b // g) + i, 0))
    return pl.pallas_call(
        functools.partial(_mem_attn_kernel, heads_on_rows=cached is not None),
        grid=(b // g, t // tq),
        in_specs=[pl.BlockSpec((g, tq, w), lambda i, j: (i, j, q_col)), kspec, vspec],
        out_specs=pl.BlockSpec((g, tq, w), lambda i, j: (i, j, 0)),
        out_shape=jax.ShapeDtypeStruct((b, t, w), f32),
        compiler_params=_params("parallel", "arbitrary"),
        name="mem_attn",
    )(q, mk, mv)


WKV_QUAD = 4
WKV_LANES = WKV_QUAD * RWKV_HEAD_DIM
WKV_TB = 64


def _wkv_kernel(r_ref, w_ref, k_ref, kk_ref, b_ref, v_ref, s0_ref, yt_ref, st_ref,
                s_scr, lhs_scr, vd_scr, yl_scr, *, nb, nq, tb):
    n = RWKV_HEAD_DIM
    ti = pl.program_id(1)

    @pl.when(ti == 0)
    def _():
        for ib in range(nb):
            s_scr[ib * nq * n:(ib + 1) * nq * n, :] = s0_ref[ib]

    yt_ref[...] = jnp.zeros(yt_ref.shape, f32)
    ri = lax.broadcasted_iota(jnp.int32, (WKV_LANES, WKV_LANES), 0)
    ci = lax.broadcasted_iota(jnp.int32, (WKV_LANES, WKV_LANES), 1)
    ones_blk = jnp.where(lax.shift_right_logical(ri, 6) == lax.shift_right_logical(ci, 6), 1.0, 0.0).astype(bf16)
    ones_blk2 = jnp.concatenate([ones_blk, ones_blk], axis=0)
    eye_rep = jnp.where(lax.broadcasted_iota(jnp.int32, (n, WKV_LANES), 0)
                        == jnp.bitwise_and(lax.broadcasted_iota(jnp.int32, (n, WKV_LANES), 1), n - 1), 1.0, 0.0)
    step_lane = jnp.bitwise_and(lax.broadcasted_iota(jnp.int32, (nq * n, WKV_LANES), 1), n - 1)
    tiles = [(ib, q) for ib in range(nb) for q in range(nq)]

    def step(t, carry):
        row = lambda ref, ib, q: ref[ib, pl.ds(t, 1), q * WKV_LANES:(q + 1) * WKV_LANES]
        for ib, q in tiles:
            rs = slice((ib * nq + q) * n, (ib * nq + q + 1) * n)
            prod = s_scr[rs, :] * row(kk_ref, ib, q)
            hi = prod.astype(bf16)
            lhs_scr[rs, 0:WKV_LANES] = hi
            lhs_scr[rs, WKV_LANES:2 * WKV_LANES] = (prod - hi.astype(f32)).astype(bf16)
            vd_scr[rs, :] = (eye_rep * row(v_ref, ib, q)).astype(bf16)
        z = jnp.dot(lhs_scr[...], ones_blk2, preferred_element_type=f32)
        vcol = jnp.dot(vd_scr[...], ones_blk, preferred_element_type=f32)
        for ib, q in tiles:
            rs = slice((ib * nq + q) * n, (ib * nq + q + 1) * n)
            s = s_scr[rs, :] * row(w_ref, ib, q) - z[rs] * row(b_ref, ib, q) + vcol[rs] * row(k_ref, ib, q)
            s_scr[rs, :] = s
            yl_scr[rs, :] = (s * row(r_ref, ib, q)).astype(bf16)
        y = jnp.dot(yl_scr[...], ones_blk, preferred_element_type=f32)
        for ib in range(nb):
            rs = slice(ib * nq * n, (ib + 1) * nq * n)
            yt_ref[ib, 0] = jnp.where(step_lane == t, y[rs], yt_ref[ib, 0])
        return carry

    lax.fori_loop(0, tb, step, 0, unroll=4)

    @pl.when(ti == pl.num_programs(1) - 1)
    def _():
        for ib in range(nb):
            st_ref[ib] = s_scr[ib * nq * n:(ib + 1) * nq * n, :]


def wkv_scan(r, w, k, kk, b, v, s0, nb=2):
    bsz, t, width = r.shape
    n = RWKV_HEAD_DIM
    nh = width // n
    nq = nh // WKV_QUAD
    tb = min(WKV_TB, t)
    nblk = t // tb
    to_tiles = lambda s: s.reshape(bsz, nq, WKV_QUAD, n, n).transpose(0, 1, 3, 2, 4).reshape(bsz, nq * n, WKV_LANES)
    row = pl.BlockSpec((nb, tb, width), lambda i, j: (i, j, 0))
    st = pl.BlockSpec((nb, nq * n, WKV_LANES), lambda i, j: (i, 0, 0))
    rows_all = nb * nq * n
    yt, s_t = pl.pallas_call(
        functools.partial(_wkv_kernel, nb=nb, nq=nq, tb=tb),
        grid=(bsz // nb, nblk),
        in_specs=[row, row, row, row, row, row, st],
        out_specs=[pl.BlockSpec((nb, 1, nq * n, WKV_LANES), lambda i, j: (i, j, 0, 0)), st],
        out_shape=[jax.ShapeDtypeStruct((bsz, nblk, nq * n, WKV_LANES), f32),
                   jax.ShapeDtypeStruct((bsz, nq * n, WKV_LANES), f32)],
        scratch_shapes=[pltpu.VMEM((rows_all, WKV_LANES), f32),
                        pltpu.VMEM((rows_all, 2 * WKV_LANES), bf16),
                        pltpu.VMEM((rows_all, WKV_LANES), bf16),
                        pltpu.VMEM((rows_all, WKV_LANES), bf16)],
        compiler_params=_params("parallel", "arbitrary"),
        name="wkv_scan",
    )(r, w, k, kk, b, v, to_tiles(s0))
    y = yt.reshape(bsz, nblk, nq, n, WKV_QUAD, n).transpose(0, 1, 5, 2, 4, 3)[:, :, :tb].reshape(bsz, t, width)
    s_t = s_t.reshape(bsz, nq, n, WKV_QUAD, n).transpose(0, 1, 3, 2, 4).reshape(bsz, nh, n, n)
    return y, s_t


def _gelu_tanh(x):
    return 0.5 * x * (1.0 + jnp.tanh(0.7978845608028654 * (x + 0.044715 * x * x * x)))


def _chunk_rows(x_ref, n_chunks, row0=0, row_stride=1):
    return jnp.concatenate(
        [x_ref[pl.ds(row0 + s * row_stride, n_chunks, stride=CMP_STRIDE * row_stride), :]
         for s in range(CMP_STRIDE)], axis=1).astype(bf16)


def _compress_rows(x2, pos_ref, w1_ref, w2_ref, n_valid, n_heads):
    rows = x2.shape[0]
    n_chunks = rows // n_heads
    pab = jnp.dot(x2, w1_ref[...], preferred_element_type=f32)
    pos = jnp.dot(pos_ref[...], w1_ref[...], preferred_element_type=f32)
    posterm = pos[0:1, :HEAD_DIM] + pos[1:2, HEAD_DIM:]
    hid = pab[:, :HEAD_DIM] + pltpu.roll(pab[:, HEAD_DIM:], rows - 1, 0) + posterm
    out = jnp.dot(_gelu_tanh(hid).astype(bf16), w2_ref[...], preferred_element_type=f32)
    n = jnp.bitwise_and(lax.broadcasted_iota(jnp.int32, out.shape, 0), n_chunks - 1)
    return jnp.where(n < n_valid, out, 0.0)


def _compress_prompt_kernel(k_ref, v_ref, posk_ref, w1k_ref, w2k_ref, posv_ref, w1v_ref, w2v_ref,
                            ok_ref, ov_ref, *, n_chunks, n_valid):
    ok_ref[0, 0] = _compress_rows(_chunk_rows(k_ref.at[0], n_chunks), posk_ref, w1k_ref, w2k_ref, n_valid, 1)
    ov_ref[0, 0] = _compress_rows(_chunk_rows(v_ref.at[0], n_chunks), posv_ref, w1v_ref, w2v_ref, n_valid, 1)


def _cmp_weights(cmp_pos, cmp_w1, cmp_w2):
    ws = []
    for i in range(2):
        half = CMP_STRIDE * HEAD_DIM
        pos = _pad_to_rows(cmp_pos[i].reshape(2, half), 8).astype(bf16)
        w1 = cmp_w1[i].reshape(2, half, HEAD_DIM)
        ws += [pos, jnp.concatenate([w1[0], w1[1]], axis=1).astype(bf16), cmp_w2[i].astype(bf16)]
    return ws


_CMP_WEIGHT_SHAPES = [(8, CMP_STRIDE * HEAD_DIM), (CMP_STRIDE * HEAD_DIM, 2 * HEAD_DIM), (HEAD_DIM, HEAD_DIM)] * 2


def compress_prompt(ck, cv, cmp_pos, cmp_w1, cmp_w2):
    b, t, _ = ck.shape
    n_chunks = t // CMP_STRIDE
    n_valid = (t - CMP_BLOCK) // CMP_STRIDE + 1
    out = jax.ShapeDtypeStruct((b, NSA_KV, n_chunks, HEAD_DIM), f32)
    ospec = pl.BlockSpec((1, 1, n_chunks, HEAD_DIM), lambda i, kv: (i, kv, 0, 0))
    wspecs = [pl.BlockSpec(s, lambda i, kv: (0, 0)) for s in _CMP_WEIGHT_SHAPES]
    return pl.pallas_call(
        functools.partial(_compress_prompt_kernel, n_chunks=n_chunks, n_valid=n_valid),
        grid=(b, NSA_KV),
        in_specs=[pl.BlockSpec((1, t, HEAD_DIM), lambda i, kv: (i, 0, kv)),
                  pl.BlockSpec((1, t, HEAD_DIM), lambda i, kv: (i, 0, kv))] + wspecs,
        out_specs=[ospec, ospec],
        out_shape=[out, out],
        compiler_params=_params("parallel", "parallel"),
        name="compress_prompt",
    )(ck, cv, *_cmp_weights(cmp_pos, cmp_w1, cmp_w2))


PAGE_ROWS = PAGE_SIZE * NSA_KV


def _page_specs(n_pages, n_seq):
    return [pl.BlockSpec((PAGE_ROWS, HEAD_DIM), lambda i, pt, g=g, p=p: (pt[i * n_seq + g, p], 0))
            for g in range(n_seq) for p in range(n_pages)]


def _compress_paged_kernel(pt_ref, *refs, n_pages, n_seq, n_valid):
    np_all = n_seq * n_pages
    k_pages, v_pages = refs[:np_all], refs[np_all:2 * np_all]
    posk_ref, w1k_ref, w2k_ref, posv_ref, w1v_ref, w2v_ref, ok_ref, ov_ref = refs[2 * np_all:]
    per_page = PAGE_SIZE // CMP_STRIDE
    n_chunks = n_pages * per_page
    heads = [(g, kv) for g in range(n_seq) for kv in range(NSA_KV)]
    chunks = lambda pages: jnp.concatenate([_chunk_rows(pg, per_page, kv, NSA_KV)
                                            for g, kv in heads for pg in pages[g * n_pages:(g + 1) * n_pages]], axis=0)
    ok = _compress_rows(chunks(k_pages), posk_ref, w1k_ref, w2k_ref, n_valid, len(heads))
    ov = _compress_rows(chunks(v_pages), posv_ref, w1v_ref, w2v_ref, n_valid, len(heads))
    for i, (g, kv) in enumerate(heads):
        ok_ref[g, kv] = ok[i * n_chunks:(i + 1) * n_chunks]
        ov_ref[g, kv] = ov[i * n_chunks:(i + 1) * n_chunks]


def compress_paged(pool_k, pool_v, page_table, n_valid, cmp_pos, cmp_w1, cmp_w2):
    b, n_pages = page_table.shape
    n_seq = SEQ_PER_STEP if b % SEQ_PER_STEP == 0 else 1
    n_chunks = n_pages * PAGE_SIZE // CMP_STRIDE
    out = jax.ShapeDtypeStruct((b, NSA_KV, n_chunks, HEAD_DIM), f32)
    ospec = pl.BlockSpec((n_seq, NSA_KV, n_chunks, HEAD_DIM), lambda i, pt: (i, 0, 0, 0))
    wspecs = [pl.BlockSpec(s, lambda i, pt: (0, 0)) for s in _CMP_WEIGHT_SHAPES]
    np_all = n_seq * n_pages
    return pl.pallas_call(
        functools.partial(_compress_paged_kernel, n_pages=n_pages, n_seq=n_seq, n_valid=n_valid),
        grid_spec=pltpu.PrefetchScalarGridSpec(
            num_scalar_prefetch=1,
            grid=(b // n_seq,),
            in_specs=_page_specs(n_pages, n_seq) * 2 + wspecs,
            out_specs=[ospec, ospec]),
        out_shape=[out, out],
        compiler_params=_params("parallel"),
        name="compress_paged",
    )(page_table, *([pool_k] * np_all), *([pool_v] * np_all), *_cmp_weights(cmp_pos, cmp_w1, cmp_w2))


def _stack_heads(x):
    return jnp.concatenate([x[:, g * HEAD_DIM:(g + 1) * HEAD_DIM] for g in range(NSA_GROUP)], axis=0)


def _dot_nt(a, b):
    return lax.dot_general(a, b, (((1,), (1,)), ((), ())), preferred_element_type=f32)


def _softmax_heads(s, bias, tq):
    s3 = s.reshape(NSA_GROUP, tq, s.shape[1]) + bias[None]
    e = jnp.exp(s3 - jnp.max(s3, axis=-1, keepdims=True))
    return e, jnp.sum(e, axis=-1, keepdims=True)


def _compressed_branch(qc, ckc, cvc, pos_t, n_cmp, tq):
    s = _dot_nt(qc, ckc)
    n = lax.broadcasted_iota(jnp.int32, (tq, s.shape[1]), 1)
    vis = (n * CMP_STRIDE + (CMP_BLOCK - 1) <= pos_t) & (n < n_cmp)
    e, denom = _softmax_heads(s, jnp.where(vis, 0.0, NEG_INF), tq)
    any_vis = jnp.where(pos_t >= CMP_BLOCK - 1, 1.0, 0.0)
    p = e / denom * any_vis[None]
    o = jnp.dot(p.reshape(s.shape).astype(bf16), cvc, preferred_element_type=f32)
    return o, jnp.sum(p, axis=0)


def _select_blocks(psum, tq, pos0, n_slc, n_j):
    if tq < SEL_LANES:
        psum = jnp.concatenate([psum, jnp.zeros((SEL_LANES - tq, psum.shape[1]), f32)], axis=0)
    n_c = psum.shape[1]
    j = lax.broadcasted_iota(jnp.int32, (n_j, n_c), 0)
    cs = lax.broadcasted_iota(jnp.int32, (n_j, n_c), 1) * CMP_STRIDE
    overlap = jnp.where((cs < j * SLC_BLOCK + SLC_BLOCK) & (cs + (CMP_BLOCK - 1) >= j * SLC_BLOCK), 1.0, 0.0)
    imp_t = lax.dot_general(overlap, psum, (((1,), (1,)), ((), ())),
                            preferred_element_type=f32, precision=lax.Precision.HIGHEST)
    j = lax.broadcasted_iota(jnp.int32, imp_t.shape, 0)
    pos_t = pos0 + lax.broadcasted_iota(jnp.int32, imp_t.shape, 1)
    cur = lax.shift_right_logical(pos_t, SLC_SHIFT)
    causal = j * SLC_BLOCK <= pos_t
    forced = (j == 0) | (j == cur) | (j == cur - 1)
    score = jnp.where(causal, jnp.where(forced, FORCE_SCORE, imp_t), -FORCE_SCORE)
    score = jnp.where(j < n_slc, score, -2.0 * FORCE_SCORE)
    rank = jnp.zeros(imp_t.shape, f32)
    for jp in range(n_slc):
        row = score[jp:jp + 1, :]
        ahead = (row > score) | ((row == score) & (j > jp))
        rank = rank + jnp.where(ahead, 1.0, 0.0)
    sel_t = jnp.where(rank < min(N_SELECT, n_slc), 1.0, 0.0)
    return sel_t.T[0:tq]


def _selection_bias(sel, key0, n_keys):
    nj = sel.shape[1]
    j = lax.broadcasted_iota(jnp.int32, (nj, n_keys), 0)
    kpos = key0 + lax.broadcasted_iota(jnp.int32, (nj, n_keys), 1)
    e = jnp.where(lax.shift_right_logical(kpos, SLC_SHIFT) == j, 1.0, 0.0).astype(bf16)
    return jnp.dot(jnp.where(sel > 0.5, 0.0, NEG_INF).astype(bf16), e, preferred_element_type=f32)


def _window_branch(qr, wk, wv, kpos0, n_keys_valid, pos_t, tq, n_phantom=None):
    s = _dot_nt(qr, wk)
    lane = lax.broadcasted_iota(jnp.int32, (tq, s.shape[1]), 1)
    kpos = kpos0 + lane
    valid = (kpos <= pos_t) & (pos_t - kpos < WINDOW) & (lane < n_keys_valid)
    s3 = s.reshape(NSA_GROUP, tq, s.shape[1]) + jnp.where(valid, 0.0, NEG_INF)[None]
    m = jnp.max(s3, axis=-1, keepdims=True)
    if n_phantom is not None:
        m = jnp.where(n_phantom[None] > 0.0, jnp.maximum(m, 0.0), m)
    e = jnp.exp(s3 - m)
    denom = jnp.sum(e, axis=-1, keepdims=True)
    if n_phantom is not None:
        denom = denom + n_phantom[None] * jnp.exp(-m)
    return jnp.dot((e / denom).reshape(s.shape).astype(bf16), wv, preferred_element_type=f32)


def _gated_sum(gate, cols, tq, o_cmp, o_slc, o_win, g):
    r = slice(g * tq, (g + 1) * tq)
    c, s, w = cols[0] + g, cols[1] + g, cols[2] + g
    return gate[:, c:c + 1] * o_cmp[r] + gate[:, s:s + 1] * o_slc[r] + gate[:, w:w + 1] * o_win[r]


QK_SCALE = HEAD_DIM ** -0.5


def _nsa_prompt_kernel(q_ref, c2_ref, s2_ref, gate_ref, ckc_ref, cvc_ref, sk_ref, sv_ref, wk_ref, wv_ref,
                       o_ref, *, n_cmp, n_slc):
    i = pl.program_id(2)
    tq = Q_BLOCK
    rows = NSA_GROUP * tq
    q0 = i * tq
    q = q_ref[0] * QK_SCALE
    qc = _stack_heads(q).astype(bf16)
    qr = _rope_heads(q, c2_ref[...], s2_ref[...]).astype(bf16)
    pos_t = q0 + lax.broadcasted_iota(jnp.int32, (tq, 1), 0)

    o_cmp, psum = _compressed_branch(qc, ckc_ref[0, 0].astype(bf16), cvc_ref[0, 0].astype(bf16), pos_t, n_cmp, tq)
    sel = _select_blocks(psum, tq, q0, n_slc, n_slc)

    def slc_step(c, carry, causal):
        m, l, acc = carry
        k0 = pl.multiple_of(c * SLC_CHUNK, SLC_CHUNK)
        bias = _selection_bias(sel, k0, SLC_CHUNK)
        if causal:
            kpos = k0 + lax.broadcasted_iota(jnp.int32, bias.shape, 1)
            bias = jnp.where(kpos <= pos_t, bias, NEG_INF)
        s3 = _dot_nt(qr, sk_ref[0, pl.ds(k0, SLC_CHUNK), :]).reshape(NSA_GROUP, tq, SLC_CHUNK) + bias[None]
        m_new = jnp.maximum(m, jnp.max(s3, axis=-1, keepdims=True))
        alpha = jnp.exp(m - m_new)
        e = jnp.exp(s3 - m_new)
        l = alpha * l + jnp.sum(e, axis=-1, keepdims=True)
        pv = jnp.dot(e.reshape(rows, SLC_CHUNK).astype(bf16), sv_ref[0, pl.ds(k0, SLC_CHUNK), :],
                     preferred_element_type=f32)
        return m_new, l, alpha * acc + pv.reshape(NSA_GROUP, tq, HEAD_DIM)

    c_last = q0 // SLC_CHUNK
    init = (jnp.full((NSA_GROUP, tq, 1), NEG_INF, f32), jnp.zeros((NSA_GROUP, tq, 1), f32),
            jnp.zeros((NSA_GROUP, tq, HEAD_DIM), f32))
    carry = lax.fori_loop(0, c_last, functools.partial(slc_step, causal=False), init)
    _, l, acc = slc_step(c_last, carry, causal=True)
    o_slc = (acc / l).reshape(rows, HEAD_DIM)

    span = WINDOW + tq
    w0 = pl.multiple_of(jnp.maximum(q0 - WINDOW, 0), tq)
    n_phantom = jnp.maximum(WINDOW - 1 - pos_t, 0).astype(f32)
    o_win = _window_branch(qr, wk_ref[0, pl.ds(w0, span), :], wv_ref[0, pl.ds(w0, span), :], w0, span, pos_t, tq,
                           n_phantom)

    gate = _sigmoid(gate_ref[0])
    first_kv = pl.program_id(1) == 0
    for g in range(NSA_GROUP):
        head = [_gated_sum(gate, tuple(br * NSA_HEADS + kv * NSA_GROUP for br in range(3)), tq, o_cmp, o_slc, o_win, g)
                for kv in range(NSA_KV)]
        o_ref[0, :, g * HEAD_DIM:(g + 1) * HEAD_DIM] = jnp.where(first_kv, head[0], head[1])


def nsa_prompt(proj, gate_col, ckc, cvc, kv_bf, c2, s2, n_cmp):
    b, t, _ = proj.shape
    assert t % SLC_CHUNK == 0 and t >= WINDOW + Q_BLOCK
    n_slc = t // SLC_BLOCK
    gw = NSA_GROUP * HEAD_DIM
    kvcol = lambda c: pl.BlockSpec((1, t, HEAD_DIM), lambda bi, kv, i, c=c: (bi, 0, c * NSA_KV + kv))
    cmp_spec = pl.BlockSpec((1, 1, ckc.shape[2], HEAD_DIM), lambda bi, kv, i: (bi, kv, 0, 0))
    return pl.pallas_call(
        functools.partial(_nsa_prompt_kernel, n_cmp=n_cmp, n_slc=n_slc),
        grid=(b, NSA_KV, t // Q_BLOCK),
        in_specs=[pl.BlockSpec((1, Q_BLOCK, gw), lambda bi, kv, i: (bi, i, kv)),
                  pl.BlockSpec((Q_BLOCK, HEAD_DIM), lambda bi, kv, i: (i, 0)),
                  pl.BlockSpec((Q_BLOCK, HEAD_DIM), lambda bi, kv, i: (i, 0)),
                  pl.BlockSpec((1, Q_BLOCK, LANES), lambda bi, kv, i: (bi, i, gate_col)),
                  cmp_spec, cmp_spec, kvcol(2), kvcol(3), kvcol(4), kvcol(5)],
        out_specs=pl.BlockSpec((1, Q_BLOCK, gw), lambda bi, kv, i: (bi, i, kv)),
        out_shape=jax.ShapeDtypeStruct((b, t, NSA_WIDTH), f32),
        compiler_params=_params("parallel", "parallel", "arbitrary"),
        name="nsa_prompt",
    )(proj, c2, s2, proj, ckc, cvc, kv_bf, kv_bf, kv_bf, kv_bf)


def _pad_rows(x, n):
    return jnp.concatenate([x, jnp.zeros((n - x.shape[0], x.shape[1]), x.dtype)], axis=0)


def _nsa_sample_kernel(pt_ref, *refs, n_pages, n_seq, n_cmp, n_slc, n_j, past, tq):
    np_all = n_seq * n_pages
    k_pages, v_pages = refs[:np_all], refs[np_all:2 * np_all]
    (q_ref, c2_ref, s2_ref, gate_ref, ckc_ref, cvc_ref, nsk_ref, nsv_ref, nwk_ref, nwv_ref,
     wink_ref, winv_ref, o_ref, owk_ref, owv_ref) = refs[2 * np_all:]
    pos_t = past + lax.broadcasted_iota(jnp.int32, (tq, 1), 0)
    lw2 = wink_ref.shape[0] // n_seq
    lw = lw2 // NSA_KV
    c2, s2 = c2_ref[...], s2_ref[...]
    for g in range(n_seq):
        gate = _sigmoid(gate_ref[g])
        keep = lw2 - tq * NSA_KV
        for cache_ref, new_ref, out_ref in ((wink_ref, nwk_ref, owk_ref), (winv_ref, nwv_ref, owv_ref)):
            out_ref[g * lw2:g * lw2 + keep, :] = cache_ref[g * lw2 + tq * NSA_KV:(g + 1) * lw2, :]
            for kv in range(NSA_KV):
                out_ref[pl.ds(g * lw2 + keep + kv, tq, stride=NSA_KV), :] = new_ref[g, :, kv * HEAD_DIM:(kv + 1) * HEAD_DIM]
        for kv in range(NSA_KV):
            ksl = slice(kv * HEAD_DIM, (kv + 1) * HEAD_DIM)
            q = q_ref[g, :, kv * NSA_GROUP * HEAD_DIM:(kv + 1) * NSA_GROUP * HEAD_DIM] * QK_SCALE
            qc = _stack_heads(q).astype(bf16)
            qr = _rope_heads(q, c2, s2).astype(bf16)
            o_cmp, psum = _compressed_branch(qc, ckc_ref[g, kv].astype(bf16), cvc_ref[g, kv].astype(bf16), pos_t,
                                             n_cmp, tq)
            sel = _select_blocks(psum, tq, past, n_slc, n_j)

            paged = lambda pages: [pg[pl.ds(kv, PAGE_SIZE, stride=NSA_KV), :] for pg in pages[g * n_pages:(g + 1) * n_pages]]
            sk = jnp.concatenate(paged(k_pages) + [_pad_rows(nsk_ref[g, :, ksl], LANES)], axis=0).astype(bf16)
            sv = jnp.concatenate(paged(v_pages) + [_pad_rows(nsv_ref[g, :, ksl], LANES)], axis=0).astype(bf16)
            n_keys = sk.shape[0]
            bias = _selection_bias(sel, 0, n_keys)
            bias = jnp.where(lax.broadcasted_iota(jnp.int32, bias.shape, 1) <= pos_t, bias, NEG_INF)
            e, denom = _softmax_heads(_dot_nt(qr, sk), bias, tq)
            o_slc = jnp.dot((e / denom).reshape(NSA_GROUP * tq, n_keys).astype(bf16), sv, preferred_element_type=f32)

            cached = lambda ref: ref[pl.ds(g * lw2 + kv, lw, stride=NSA_KV), :]
            wk = jnp.concatenate([cached(wink_ref), _pad_rows(nwk_ref[g, :, ksl], LANES)], axis=0).astype(bf16)
            wv = jnp.concatenate([cached(winv_ref), _pad_rows(nwv_ref[g, :, ksl], LANES)], axis=0).astype(bf16)
            o_win = _window_branch(qr, wk, wv, past - lw, lw + tq, pos_t, tq)

            cols = tuple(br * NSA_HEADS + kv * NSA_GROUP for br in range(3))
            for hg in range(NSA_GROUP):
                hd = kv * NSA_GROUP + hg
                o_ref[g, :, hd * HEAD_DIM:(hd + 1) * HEAD_DIM] = _gated_sum(gate, cols, tq, o_cmp, o_slc, o_win, hg)


def nsa_sample(proj, gate_col, ckc, cvc, pool_k, pool_v, page_table, new_rows, win_k, win_v, c2, s2, n_cmp, past):
    b, tq, _ = proj.shape
    n_pages = page_table.shape[1]
    assert past == n_pages * PAGE_SIZE and past % SLC_BLOCK == 0 and tq <= SLC_BLOCK and (tq * NSA_KV) % 8 == 0
    n_seq = SEQ_PER_STEP if b % SEQ_PER_STEP == 0 else 1
    n_slc = past // SLC_BLOCK + 1
    n_j = -(-n_slc // SLC_BLOCK) * SLC_BLOCK
    lw2 = win_k.shape[0] // b
    per_b = lambda shape: pl.BlockSpec((n_seq,) + shape, lambda i, pt: (i,) + (0,) * len(shape))
    tab = pl.BlockSpec((tq, HEAD_DIM), lambda i, pt: (0, 0))
    win = pl.BlockSpec((n_seq * lw2, HEAD_DIM), lambda i, pt: (i, 0))
    np_all = n_seq * n_pages
    return pl.pallas_call(
        functools.partial(_nsa_sample_kernel, n_pages=n_pages, n_seq=n_seq, n_cmp=n_cmp, n_slc=n_slc, n_j=n_j,
                          past=past, tq=tq),
        grid_spec=pltpu.PrefetchScalarGridSpec(
            num_scalar_prefetch=1,
            grid=(b // n_seq,),
            in_specs=_page_specs(n_pages, n_seq) * 2 + [
                per_b((tq, NSA_WIDTH)), tab, tab,
                pl.BlockSpec((n_seq, tq, LANES), lambda i, pt: (i, 0, gate_col)),
                per_b((NSA_KV, ckc.shape[2], HEAD_DIM)), per_b((NSA_KV, ckc.shape[2], HEAD_DIM))]
                + [per_b((tq, KV_COLS))] * 4 + [win, win],
            out_specs=[per_b((tq, NSA_WIDTH)), win, win]),
        out_shape=[jax.ShapeDtypeStruct((b, tq, NSA_WIDTH), f32),
                   jax.ShapeDtypeStruct(win_k.shape, f32), jax.ShapeDtypeStruct(win_v.shape, f32)],
        compiler_params=_params("parallel"),
        name="nsa_sample",
    )(page_table, *([pool_k] * np_all), *([pool_v] * np_all), proj, c2, s2, proj, ckc, cvc, *new_rows, win_k, win_v)


def _prev_rows(x, tile, halo_ref, first_ref, seq_len):
    tm, c = x.shape
    prev = pltpu.roll(x, 1, 0)
    row = lax.broadcasted_iota(jnp.int32, (tm, 1), 0)
    if seq_len >= tm:
        tiles_per_seq = seq_len // tm
        first = first_ref[pl.ds(tile // tiles_per_seq, 1), :]
        edge = jnp.where(tile % tiles_per_seq == 0, first, halo_ref[7:8, :])
        return jnp.where(row == 0, edge, prev)
    pieces = []
    for j in range(tm // seq_len):
        pieces.append(jnp.broadcast_to(first_ref[j:j + 1, :], (8, c)))
        if seq_len > 8:
            pieces.append(jnp.zeros((seq_len - 8, c), f32))
    return jnp.where(jnp.bitwise_and(row, seq_len - 1) == 0, jnp.concatenate(pieces, axis=0), prev)


def _shift_specs(m, c, tm, seq_len, n_seq, col=None):
    cb = (lambda *g: 0) if col is None else col
    tile = pl.BlockSpec((tm, c), lambda *g: (g[0], cb(*g)))
    halo = pl.BlockSpec((8, c), lambda *g: (jnp.maximum(g[0] * (tm // 8) - 1, 0), cb(*g)))
    if seq_len >= tm:
        first = pl.BlockSpec((-(-n_seq // 8) * 8, c), lambda *g: (0, cb(*g)))
    else:
        first = pl.BlockSpec((tm // seq_len, c), lambda *g: (g[0], cb(*g)))
    return tile, halo, first


def _pad_first(first, seq_len, tm):
    return _pad_to_rows(first, -(-first.shape[0] // 8) * 8) if seq_len >= tm else first


def _head_sums(x):
    ri = lax.broadcasted_iota(jnp.int32, (WKV_LANES, WKV_LANES), 0)
    ci = lax.broadcasted_iota(jnp.int32, (WKV_LANES, WKV_LANES), 1)
    ones_blk = jnp.where(lax.shift_right_logical(ri, 6) == lax.shift_right_logical(ci, 6), 1.0, 0.0).astype(bf16)
    hi = x.astype(bf16)
    lo = (x - hi.astype(f32)).astype(bf16)
    out = []
    for c in range(x.shape[1] // WKV_LANES):
        sl = slice(c * WKV_LANES, (c + 1) * WKV_LANES)
        out.append(jnp.dot(hi[:, sl], ones_blk, preferred_element_type=f32)
                   + jnp.dot(lo[:, sl], ones_blk, preferred_element_type=f32))
    return jnp.concatenate(out, axis=1)


def _lora_kernel(hn_ref, halo_ref, first_ref, mu_ref, wd1_ref, wa1_ref, wg1_ref, wd2_ref, wa2_ref, wg2_ref,
                 w0_ref, a0_ref, decay_ref, a_ref, g_ref, *, seq_len):
    hn = hn_ref[...]
    xx = _prev_rows(hn, pl.program_id(0), halo_ref, first_ref, seq_len) - hn
    mix = lambda r: (hn + xx * mu_ref[r:r + 1, :]).astype(bf16)
    dot = lambda x, w_ref: jnp.dot(x, w_ref[...], preferred_element_type=f32)
    w_raw = w0_ref[...] + dot(jnp.tanh(dot(mix(0), wd1_ref)).astype(bf16), wd2_ref)
    softplus = jnp.maximum(-w_raw, 0.0) + jnp.log(1.0 + jnp.exp(-jnp.abs(w_raw)))
    decay_ref[...] = jnp.exp(-jnp.exp(-softplus - 0.5))
    a_ref[...] = _sigmoid(a0_ref[...] + dot(dot(mix(1), wa1_ref).astype(bf16), wa2_ref))
    g_ref[...] = dot(_sigmoid(dot(mix(2), wg1_ref)).astype(bf16), wg2_ref)


def rwkv_lora(hn, x_prev, seq_len, P):
    m, d = hn.shape
    rw = P['w0'].shape[0]
    tm = min(ROW_TILE, m)
    pad128 = lambda w: _pad_cols(w, -(-w.shape[1] // LANES) * LANES).astype(bf16)
    w1s = [pad128(P[k]) for k in ('w_decay1', 'w_aaa1', 'w_gate1')]
    w2s = [_pad_to_rows(P[k], w1.shape[1]).astype(bf16) for k, w1 in zip(('w_decay2', 'w_aaa2', 'w_gate2'), w1s)]
    full = lambda x: pl.BlockSpec(x.shape, lambda i: (0, 0))
    mu = _pad_to_rows(P['mu_wag'], 8)
    vecs = [P['w0'].reshape(1, rw), P['a0'].reshape(1, rw)]
    out = jax.ShapeDtypeStruct((m, rw), f32)
    ospec = pl.BlockSpec((tm, rw), lambda i: (i, 0))
    return pl.pallas_call(
        functools.partial(_lora_kernel, seq_len=seq_len),
        grid=(m // tm,),
        in_specs=list(_shift_specs(m, d, tm, seq_len, x_prev.shape[0])) + [full(mu)]
                 + [full(w) for w in w1s + w2s + vecs],
        out_specs=[ospec] * 3,
        out_shape=[out] * 3,
        compiler_params=_params("parallel"),
        name="rwkv_lora",
    )(hn, hn, _pad_first(x_prev, seq_len, tm), mu, *w1s, *w2s, *vecs)


RKV_GROUPS = 3
IN_COL_TILE = 512


def _rwkv_in_kernel(h_ref, halo_ref, first_ref, g_ref, w_ref, mu_ref, a_ref, kkw_ref, kaw_ref,
                    r_ref, k_ref, kk_ref, b_ref, v_ref, mq_ref, xn_scr, edge_scr, *, seq_len, per_group):
    i, j = pl.program_id(0), pl.program_id(1)
    tm, c = r_ref.shape

    @pl.when(j == 0)
    def _():
        xn_scr[...] = _rms(h_ref[...], g_ref[...]).astype(bf16)
        edge_scr[0:8, :] = _rms(halo_ref[...], g_ref[...])
        edge_scr[8:, :] = first_ref[...]

    cur = jnp.dot(xn_scr[...], w_ref[...], preferred_element_type=f32)
    edge = jnp.dot(edge_scr[...].astype(bf16), w_ref[...], preferred_element_type=f32)
    halo, first = edge[0:8], edge[8:]
    prev = pltpu.roll(cur, 1, 0)
    row = lax.broadcasted_iota(jnp.int32, (tm, 1), 0)
    if seq_len >= tm:
        tiles_per_seq = seq_len // tm
        seq = lax.broadcasted_iota(jnp.int32, (first.shape[0], 1), 0) == i // tiles_per_seq
        first_row = jnp.sum(jnp.where(seq, first, 0.0), axis=0, keepdims=True)
        edge_row = jnp.where(i % tiles_per_seq == 0, first_row, halo[7:8])
        prev = jnp.where(row == 0, edge_row, prev)
    else:
        pieces = []
        for q in range(tm // seq_len):
            pieces.append(jnp.broadcast_to(first[q:q + 1], (8, c)))
            if seq_len > 8:
                pieces.append(jnp.zeros((seq_len - 8, c), f32))
        prev = jnp.where(jnp.bitwise_and(row, seq_len - 1) == 0, jnp.concatenate(pieces, axis=0), prev)
    x = cur + (prev - cur) * mu_ref[...]

    @pl.when(j < per_group)
    def _():
        r_ref[...] = x

    @pl.when((j >= per_group) & (j < 2 * per_group))
    def _():
        a = a_ref[...]
        kk = x * kkw_ref[...]
        kk = kk * lax.rsqrt(_head_sums(kk * kk) + 1e-12)
        kk_ref[...] = kk
        b_ref[...] = kk * a
        k_ref[...] = x * (1.0 + (a - 1.0) * kaw_ref[...])

    @pl.when((j >= 2 * per_group) & (j < 3 * per_group))
    def _():
        v_ref[...] = x

    @pl.when(j >= 3 * per_group)
    def _():
        mq_ref[...] = cur


def rwkv_in(h, x_prev, a_rate, seq_len, P):
    m, d = h.shape
    rw = a_rate.shape[1]
    c = IN_COL_TILE
    tm = min(ROW_TILE, m)
    per_group = rw // c
    n_col = RKV_GROUPS * per_group + MEM_WIDTH // c
    assert rw % c == 0 and MEM_WIDTH == c and P['w_in_a'].shape[1] == n_col * c
    tile, halo, first = _shift_specs(m, d, tm, seq_len, x_prev.shape[0])
    n_first = first.block_shape[0]
    mu = jnp.pad(P['mu_rkv'], (0, MEM_WIDTH)).reshape(1, -1)
    grp = lambda base: (lambda i, j: (i, jnp.clip(j - base * per_group, 0, per_group - 1)))
    vec = lambda base: (lambda i, j: (0, jnp.clip(j - base * per_group, 0, per_group - 1)))
    wide = jax.ShapeDtypeStruct((m, rw), f32)
    return pl.pallas_call(
        functools.partial(_rwkv_in_kernel, seq_len=seq_len, per_group=per_group),
        grid=(m // tm, n_col),
        in_specs=[pl.BlockSpec((tm, d), lambda i, j: (i, 0)),
                  pl.BlockSpec((8, d), lambda i, j: (jnp.maximum(i * (tm // 8) - 1, 0), 0)),
                  pl.BlockSpec(first.block_shape, lambda i, j: (i if seq_len < tm else 0, 0)),
                  pl.BlockSpec((1, d), lambda i, j: (0, 0)),
                  pl.BlockSpec((d, c), lambda i, j: (0, j)),
                  pl.BlockSpec((1, c), lambda i, j: (0, j)),
                  pl.BlockSpec((tm, c), grp(1)),
                  pl.BlockSpec((1, c), vec(1)),
                  pl.BlockSpec((1, c), vec(1))],
        out_specs=[pl.BlockSpec((tm, c), grp(0)), pl.BlockSpec((tm, c), grp(1)), pl.BlockSpec((tm, c), grp(1)),
                   pl.BlockSpec((tm, c), grp(1)), pl.BlockSpec((tm, c), grp(2)),
                   pl.BlockSpec((tm, c), lambda i, j: (i, 0))],
        out_shape=[wide] * 5 + [jax.ShapeDtypeStruct((m, MEM_WIDTH), f32)],
        scratch_shapes=[pltpu.VMEM((tm, d), bf16), pltpu.VMEM((8 + n_first, d), f32)],
        compiler_params=_params("parallel", "arbitrary"),
        name="rwkv_in",
    )(h, h, _pad_first(x_prev, seq_len, tm), P['g_mix_pre'].reshape(1, d), P['w_in_a'].astype(bf16), mu, a_rate,
      P['k_k'].reshape(1, rw), P['k_a'].reshape(1, rw))


def _rwkv_out_kernel(y_ref, r_ref, k_ref, v_ref, g_ref, om_ref, h_ref, lw_ref, lb_ref, rk_ref, w_ref, gp_ref, o_ref):
    inv_n = 1.0 / RWKV_HEAD_DIM
    y = y_ref[...]
    d = y - _head_sums(y) * inv_n
    var = _head_sums(d * d) * inv_n
    yn = d * lax.rsqrt(var + GN_EPS) * lw_ref[...] + lb_ref[...]
    bonus = _head_sums(r_ref[...] * k_ref[...] * rk_ref[...]) * v_ref[...]
    o = ((yn + bonus) * g_ref[...]).astype(bf16)
    rw = o.shape[1]
    acc = jnp.dot(o, w_ref[:rw, :], preferred_element_type=f32)
    acc += jnp.dot(om_ref[...].astype(bf16), w_ref[rw:, :], preferred_element_type=f32)
    o_ref[...] = h_ref[...] + _rms(acc, gp_ref[...])


def rwkv_out(y, r, k, v, gate, o_mem, h, P):
    m, rw = y.shape
    d = h.shape[1]
    tm = min(ROW_TILE // 2, m)
    wide = pl.BlockSpec((tm, rw), lambda i: (i, 0))
    vec = pl.BlockSpec((1, rw), lambda i: (0, 0))
    return pl.pallas_call(
        _rwkv_out_kernel,
        grid=(m // tm,),
        in_specs=[wide] * 5 + [pl.BlockSpec((tm, o_mem.shape[1]), lambda i: (i, 0)),
                               pl.BlockSpec((tm, d), lambda i: (i, 0)), vec, vec, vec,
                               pl.BlockSpec((rw + o_mem.shape[1], d), lambda i: (0, 0)),
                               pl.BlockSpec((1, d), lambda i: (0, 0))],
        out_specs=pl.BlockSpec((tm, d), lambda i: (i, 0)),
        out_shape=jax.ShapeDtypeStruct((m, d), f32),
        compiler_params=_params("parallel"),
        name="rwkv_out",
    )(y, r, k, v, gate, o_mem, h, P['lnx_w'].reshape(1, rw), P['lnx_b'].reshape(1, rw), P['r_k'].reshape(1, rw),
      P['w_out_a'].astype(bf16), P['g_mix_post'].reshape(1, d))


def _rope_tables(pos):
    half = HEAD_DIM // 2
    inv = jnp.power(ROPE_THETA, -jnp.arange(half, dtype=f32) / half)
    ang = pos.astype(f32)[:, None] * inv[None, :]
    cos, sin = jnp.cos(ang), jnp.sin(ang)
    return jnp.concatenate([cos, cos], axis=-1), jnp.concatenate([-sin, sin], axis=-1)


def _pad_cols(w, n):
    return jnp.pad(w, ((0, 0), (0, n - w.shape[1])))


def _pad_to_rows(x, n):
    return jnp.pad(x, ((0, n - x.shape[0]), (0, 0)))


def rwkv_mem_layer(h, x_prev, s0, mem, P):
    b, t, d = h.shape
    m = b * t
    assert t & (t - 1) == 0 and t % 8 == 0 and (t % ROW_TILE == 0 or ROW_TILE % t == 0)
    h2 = h.reshape(m, d)
    hn = rmsnorm(h2, P['g_mix_pre'])
    rw = P['w0'].shape[0]

    decay, a_rate, gate = rwkv_lora(hn, x_prev, t, P)
    r, k, kk, kb, v, mq = rwkv_in(h2, x_prev, a_rate, t, P)
    as3 = lambda x: x.reshape(b, t, -1)
    y, s_t = wkv_scan(as3(r), as3(decay), as3(k), as3(kk), as3(kb), as3(v), s0)
    o_mem = mem(as3(mq), 0)
    h2 = rwkv_out(y.reshape(m, rw), r, k, v, gate, o_mem.reshape(m, MEM_WIDTH), h2, P)
    h2 = ffn_residual(h2, P['g_ffn_pre'], P['w_ff1'], P['w_ff2'], P['g_ffn_post'])
    return h2.reshape(b, t, d), s_t, hn.reshape(b, t, d)[:, -1]


GATE_COL = (NSA_WIDTH + MEM_WIDTH) // LANES


def nsa_mem_layer(h, mem, P, attend):
    b, t, d = h.shape
    m = b * t
    h2 = h.reshape(m, d)
    n_in = (GATE_COL + 1) * LANES
    n_in = -(-n_in // 768) * 768
    proj = matmul(h2, _pad_cols(P['w_in_b'].astype(bf16), n_in), g=P['g_mix_pre'], tn=768).reshape(b, t, -1)
    o_nsa = attend(proj)
    o_mem = mem(proj, NSA_WIDTH // MEM_WIDTH)
    h2 = out_proj_residual(o_nsa.reshape(m, NSA_WIDTH), o_mem.reshape(m, MEM_WIDTH), 0, P['w_out_b'], h2,
                           P['g_mix_post'])
    h2 = ffn_residual(h2, P['g_ffn_pre'], P['w_ff1'], P['w_ff2'], P['g_ffn_post'])
    return h2.reshape(b, t, d)


def kernel(x_prompt, x_sample, mem_prompt, cache_mem_k, cache_mem_v, state_wkv, state_shift, cache_cmp_k, cache_cmp_v, cache_slc_k, cache_slc_v, cache_win_k, cache_win_v, page_table, g_mix_pre, g_mix_post, g_ffn_pre, g_ffn_post, g_mem, w_mem_k, w_mem_v, w_in_a, mu_rkv, mu_wag, w0, w_decay1, w_decay2, a0, w_aaa1, w_aaa2, w_gate1, w_gate2, k_k, k_a, r_k, lnx_w, lnx_b, w_out_a, g_kv, w_kv, cmp_pos, cmp_w1, cmp_w2, w_in_b, w_out_b, w_ff1, w_ff2):
    bp, tp, d = x_prompt.shape
    bs, ts, _ = x_sample.shape
    depth = g_mix_pre.shape[0]
    assert depth == 2 and w_in_a.shape[0] == 1 and w_in_b.shape[0] == 1
    n_pages = page_table.shape[1]
    past = n_pages * PAGE_SIZE
    mem_len = mem_prompt.shape[1]

    P0 = dict(g_mix_pre=g_mix_pre[0], g_mix_post=g_mix_post[0], g_ffn_pre=g_ffn_pre[0], g_ffn_post=g_ffn_post[0],
              w_in_a=w_in_a[0], mu_rkv=mu_rkv[0], mu_wag=mu_wag[0], w0=w0[0], w_decay1=w_decay1[0],
              w_decay2=w_decay2[0], a0=a0[0], w_aaa1=w_aaa1[0], w_aaa2=w_aaa2[0], w_gate1=w_gate1[0],
              w_gate2=w_gate2[0], k_k=k_k[0], k_a=k_a[0], r_k=r_k[0], lnx_w=lnx_w[0], lnx_b=lnx_b[0],
              w_out_a=w_out_a[0], w_ff1=w_ff1[0], w_ff2=w_ff2[0])
    P1 = dict(g_mix_pre=g_mix_pre[1], g_mix_post=g_mix_post[1], g_ffn_pre=g_ffn_pre[1], g_ffn_post=g_ffn_post[1],
              w_in_b=w_in_b[0], w_out_b=w_out_b[0], w_ff1=w_ff1[1], w_ff2=w_ff2[1])
    rows4 = lambda x, bsz: x.reshape(bsz, -1, NSA_KV, HEAD_DIM)

    mem2 = mem_prompt.reshape(bp * mem_len, d)
    mkv = [matmul(mem2, jnp.concatenate([w_mem_k[l], w_mem_v[l]], axis=1), g=g_mem[l]).reshape(bp, mem_len, -1)
           for l in range(depth)]
    mem_k_p = jnp.stack([x[..., :MEM_WIDTH] for x in mkv])
    mem_v_p = jnp.stack([x[..., MEM_WIDTH:] for x in mkv])
    mem_p = lambda l: (lambda q, q_col: mem_attention(q, q_col, mkv[l], mkv[l], k_col=0, v_col=1))

    nh = w0.shape[1] // RWKV_HEAD_DIM
    shift0 = jnp.zeros((bp, d), f32)
    wkv0 = jnp.zeros((bp, nh, RWKV_HEAD_DIM, RWKV_HEAD_DIM), f32)
    h, wkv_p, shift_p = rwkv_mem_layer(x_prompt, shift0, wkv0, mem_p(0), P0)

    c2p, s2p = _rope_tables(jnp.arange(tp, dtype=jnp.int32))
    rows_p, kv_bf = kv_proj(h.reshape(bp * tp, d), g_kv, w_kv, c2p, s2p)
    as_p = lambda x: x.reshape(bp, tp, -1)
    ckc, cvc = compress_prompt(as_p(rows_p[0]), as_p(rows_p[1]), cmp_pos, cmp_w1, cmp_w2)
    n_cmp_p = (tp - CMP_BLOCK) // CMP_STRIDE + 1

    def attend_prompt(proj):
        return nsa_prompt(proj, GATE_COL, ckc, cvc, as_p(kv_bf), c2p, s2p, n_cmp_p)

    y_p = nsa_mem_layer(h, mem_p(1), P1, attend_prompt)
    cmp_k_p, cmp_v_p, slc_k_p, slc_v_p, win_k_p, win_v_p = [rows4(x, bp) for x in rows_p]
    n_keep = min(WINDOW, tp)
    win_k_p, win_v_p = win_k_p[:, tp - n_keep:], win_v_p[:, tp - n_keep:]

    mk_s, mv_s = cache_mem_k.reshape(-1, HEAD_DIM), cache_mem_v.reshape(-1, HEAD_DIM)
    mem_s = lambda l: (lambda q, q_col: mem_attention(q, q_col, mk_s, mv_s, cached=(l, mem_len)))
    h, wkv_s, shift_s = rwkv_mem_layer(x_sample, state_shift[0], state_wkv[0], mem_s(0), P0)

    c2s, s2s = _rope_tables(past + jnp.arange(ts, dtype=jnp.int32))
    rows_s, _ = kv_proj(h.reshape(bs * ts, d), g_kv, w_kv, jnp.tile(c2s, (bs, 1)), jnp.tile(s2s, (bs, 1)))
    as_s = lambda x: x.reshape(bs, ts, -1)
    n_cmp_s = (past + ts - CMP_BLOCK) // CMP_STRIDE + 1
    assert (n_cmp_s - 1) * CMP_STRIDE + CMP_BLOCK <= past
    n_pool = cache_cmp_k.shape[0]
    pool = lambda x: x.reshape(n_pool * PAGE_ROWS, HEAD_DIM)
    ckc_s, cvc_s = compress_paged(pool(cache_cmp_k), pool(cache_cmp_v), page_table, n_cmp_s, cmp_pos, cmp_w1, cmp_w2)
    win_k2, win_v2 = cache_win_k.reshape(-1, HEAD_DIM), cache_win_v.reshape(-1, HEAD_DIM)

    new_win = []

    def attend_sample(proj):
        o, wk_out, wv_out = nsa_sample(proj, GATE_COL, ckc_s, cvc_s, pool(cache_slc_k), pool(cache_slc_v), page_table,
                                       [as_s(x) for x in rows_s[2:]], win_k2, win_v2, c2s, s2s, n_cmp_s, past)
        new_win.extend([wk_out, wv_out])
        return o

    y_s = nsa_mem_layer(h, mem_s(1), P1, attend_sample)
    cmp_k_s, cmp_v_s, slc_k_s, slc_v_s = [rows4(x, bs) for x in rows_s[:4]]
    win_k_s, win_v_s = [x.reshape(cache_win_k.shape) for x in new_win]

    return (y_p, y_s, mem_k_p.reshape(depth, bp, mem_len, MEM_HEADS, HEAD_DIM),
            mem_v_p.reshape(depth, bp, mem_len, MEM_HEADS, HEAD_DIM),
            wkv_p[None], shift_p[None], cmp_k_p, cmp_v_p, slc_k_p, slc_v_p, win_k_p, win_v_p,
            wkv_s[None], shift_s[None], cmp_k_s, cmp_v_s, slc_k_s, slc_v_s, win_k_s, win_v_s)
```

```python
import functools

import jax
import jax.numpy as jnp
from jax import lax
from jax.experimental import pallas as pl
from jax.experimental.pallas import tpu as pltpu

f32 = jnp.float32
bf16 = jnp.bfloat16

LANES = 128
VMEM_LIMIT_BYTES = 56 * 1024 * 1024

HEAD_DIM = 128
MEM_HEADS = 4
MEM_WIDTH = MEM_HEADS * HEAD_DIM
RWKV_HEAD_DIM = 64
GN_EPS = 64e-5
NSA_KV = 2
NSA_GROUP = 6
NSA_HEADS = NSA_KV * NSA_GROUP
NSA_WIDTH = NSA_HEADS * HEAD_DIM
KV_COLS = NSA_KV * HEAD_DIM
CMP_BLOCK = 32
CMP_STRIDE = 16
SLC_BLOCK = 64
SLC_SHIFT = 6
N_SELECT = 16
WINDOW = 512
Q_BLOCK = 128
ROPE_THETA = 10000.0
NORM_EPS = 1e-6
NEG_INF = -1e30
FORCE_SCORE = 1e9
PAGE_SIZE = 128

ROW_TILE = 512
FFN_ROW_TILE = 1024
SLC_CHUNK = 512
SEL_LANES = 128
SEQ_PER_STEP = 2
MEM_SEQ_PER_STEP = 4


def _params(*sem):
    return pltpu.CompilerParams(dimension_semantics=sem, vmem_limit_bytes=VMEM_LIMIT_BYTES)


def _rms(x, g):
    return x * lax.rsqrt(jnp.mean(x * x, axis=-1, keepdims=True) + NORM_EPS) * g


def _sigmoid(x):
    return 1.0 / (1.0 + jnp.exp(-x))


def _rmsnorm_kernel(x_ref, g_ref, o_ref):
    o_ref[...] = _rms(x_ref[...], g_ref[...])


def rmsnorm(x, g):
    m, d = x.shape
    tm = min(ROW_TILE, m)
    return pl.pallas_call(
        _rmsnorm_kernel,
        grid=(m // tm,),
        in_specs=[pl.BlockSpec((tm, d), lambda i: (i, 0)), pl.BlockSpec((1, d), lambda i: (0, 0))],
        out_specs=pl.BlockSpec((tm, d), lambda i: (i, 0)),
        out_shape=jax.ShapeDtypeStruct((m, d), f32),
        compiler_params=_params("parallel"),
        name="rmsnorm",
    )(x, g.reshape(1, d))


def _mm_kernel(x_ref, g_ref, w_ref, o_ref, xn_ref, *, norm):
    @pl.when(pl.program_id(1) == 0)
    def _():
        x = x_ref[...]
        if norm:
            x = _rms(x, g_ref[...])
        xn_ref[...] = x.astype(bf16)

    o_ref[...] = jnp.dot(xn_ref[...], w_ref[...], preferred_element_type=f32)


def matmul(x, w, g=None, tn=512):
    m, k = x.shape
    n = w.shape[1]
    tm = min(FFN_ROW_TILE if m % FFN_ROW_TILE == 0 else ROW_TILE, m)
    tn = min(tn, n)
    assert m % tm == 0 and n % tn == 0, (m, n, tm, tn)
    gg = jnp.ones((1, k), f32) if g is None else g.reshape(1, k)
    return pl.pallas_call(
        functools.partial(_mm_kernel, norm=g is not None),
        grid=(m // tm, n // tn),
        in_specs=[pl.BlockSpec((tm, k), lambda i, j: (i, 0)),
                  pl.BlockSpec((1, k), lambda i, j: (0, 0)),
                  pl.BlockSpec((k, tn), lambda i, j: (0, j))],
        out_specs=pl.BlockSpec((tm, tn), lambda i, j: (i, j)),
        out_shape=jax.ShapeDtypeStruct((m, n), f32),
        scratch_shapes=[pltpu.VMEM((tm, k), bf16)],
        compiler_params=_params("parallel", "arbitrary"),
        name="matmul",
    )(x, gg, w.astype(bf16))


def _out_proj_kernel(oa_ref, ob_ref, w_ref, h_ref, g_ref, y_ref):
    ka = oa_ref.shape[1]
    acc = jnp.dot(oa_ref[...].astype(bf16), w_ref[:ka, :], preferred_element_type=f32)
    acc += jnp.dot(ob_ref[...].astype(bf16), w_ref[ka:, :], preferred_element_type=f32)
    y_ref[...] = h_ref[...] + _rms(acc, g_ref[...])


def out_proj_residual(oa, ob, ob_col, w, h, g):
    m, ka = oa.shape
    d = w.shape[1]
    kb = w.shape[0] - ka
    tm = min(ROW_TILE, m)
    return pl.pallas_call(
        _out_proj_kernel,
        grid=(m // tm,),
        in_specs=[pl.BlockSpec((tm, ka), lambda i: (i, 0)),
                  pl.BlockSpec((tm, kb), lambda i: (i, ob_col)),
                  pl.BlockSpec((ka + kb, d), lambda i: (0, 0)),
                  pl.BlockSpec((tm, d), lambda i: (i, 0)),
                  pl.BlockSpec((1, d), lambda i: (0, 0))],
        out_specs=pl.BlockSpec((tm, d), lambda i: (i, 0)),
        out_shape=jax.ShapeDtypeStruct((m, d), f32),
        compiler_params=_params("parallel"),
        name="out_proj",
    )(oa, ob, w.astype(bf16), h, g.reshape(1, d))


def _ffn_kernel(h_ref, gpre_ref, w1_ref, w2_ref, gpost_ref, y_ref, xn_ref):
    j = pl.program_id(1)

    @pl.when(j == 0)
    def _():
        xn_ref[...] = _rms(h_ref[...], gpre_ref[...]).astype(bf16)
        y_ref[...] = jnp.zeros_like(y_ref)

    u = jnp.dot(xn_ref[...], w1_ref[...], preferred_element_type=f32)
    u = jnp.square(jnp.maximum(u, 0.0))
    y_ref[...] += jnp.dot(u.astype(bf16), w2_ref[...], preferred_element_type=f32)

    @pl.when(j == pl.num_programs(1) - 1)
    def _():
        y_ref[...] = h_ref[...] + _rms(y_ref[...], gpost_ref[...])


def ffn_residual(h, g_pre, w1, w2, g_post, tf=512):
    m, d = h.shape
    dff = w1.shape[1]
    tm = min(FFN_ROW_TILE, m)
    return pl.pallas_call(
        _ffn_kernel,
        grid=(m // tm, dff // tf),
        in_specs=[pl.BlockSpec((tm, d), lambda i, j: (i, 0)),
                  pl.BlockSpec((1, d), lambda i, j: (0, 0)),
                  pl.BlockSpec((d, tf), lambda i, j: (0, j)),
                  pl.BlockSpec((tf, d), lambda i, j: (j, 0)),
                  pl.BlockSpec((1, d), lambda i, j: (0, 0))],
        out_specs=pl.BlockSpec((tm, d), lambda i, j: (i, 0)),
        out_shape=jax.ShapeDtypeStruct((m, d), f32),
        scratch_shapes=[pltpu.VMEM((tm, d), bf16)],
        compiler_params=_params("parallel", "arbitrary"),
        name="ffn",
    )(h, g_pre.reshape(1, d), w1.astype(bf16), w2.astype(bf16), g_post.reshape(1, d))


def _rope_tile(x, c2, s2):
    return x * c2 + pltpu.roll(x, HEAD_DIM // 2, 1) * s2


def _rope_heads(q, c2, s2):
    return jnp.concatenate([_rope_tile(q[:, g * HEAD_DIM:(g + 1) * HEAD_DIM], c2, s2)
                            for g in range(NSA_GROUP)], axis=0)


N_KV_BRANCH = 6


def _kv_proj_kernel(h_ref, g_ref, w_ref, c2_ref, s2_ref, *refs):
    outs, bf_ref, xn_ref = refs[:N_KV_BRANCH], refs[N_KV_BRANCH], refs[N_KV_BRANCH + 1]
    j = pl.program_id(1)

    @pl.when(j == 0)
    def _():
        xn_ref[...] = _rms(h_ref[...], g_ref[...]).astype(bf16)

    acc = jnp.dot(xn_ref[...], w_ref[...], preferred_element_type=f32)
    for br in range(N_KV_BRANCH):
        @pl.when(j == br)
        def _(br=br):
            if br in (2, 4):
                c2, s2 = c2_ref[...], s2_ref[...]
                val = jnp.concatenate([_rope_tile(acc[:, kv * HEAD_DIM:(kv + 1) * HEAD_DIM], c2, s2)
                                       for kv in range(NSA_KV)], axis=1)
            else:
                val = acc
            outs[br][...] = val
            bf_ref[...] = val.astype(bf16)


def kv_proj(h, g, w, c2, s2):
    m, d = h.shape
    n = w.shape[1]
    assert n == N_KV_BRANCH * KV_COLS
    tm = min(ROW_TILE, m)
    ntab = c2.shape[0] // tm
    res = pl.pallas_call(
        _kv_proj_kernel,
        grid=(m // tm, N_KV_BRANCH),
        in_specs=[pl.BlockSpec((tm, d), lambda i, j: (i, 0)),
                  pl.BlockSpec((1, d), lambda i, j: (0, 0)),
                  pl.BlockSpec((d, KV_COLS), lambda i, j: (0, j)),
                  pl.BlockSpec((tm, HEAD_DIM), lambda i, j: (i % ntab, 0)),
                  pl.BlockSpec((tm, HEAD_DIM), lambda i, j: (i % ntab, 0))],
        out_specs=[pl.BlockSpec((tm, KV_COLS), lambda i, j: (i, 0))] * N_KV_BRANCH
                  + [pl.BlockSpec((tm, KV_COLS), lambda i, j: (i, j))],
        out_shape=[jax.ShapeDtypeStruct((m, KV_COLS), f32)] * N_KV_BRANCH + [jax.ShapeDtypeStruct((m, n), bf16)],
        scratch_shapes=[pltpu.VMEM((tm, d), bf16)],
        compiler_params=_params("parallel", "arbitrary"),
        name="kv_proj",
    )(h, g.reshape(1, d), w.astype(bf16), c2, s2)
    return res[:N_KV_BRANCH], res[N_KV_BRANCH]


def _mem_attn_kernel(q_ref, k_ref, v_ref, o_ref, *, heads_on_rows):
    scale = HEAD_DIM ** -0.5
    n_seq = q_ref.shape[0]
    for g in range(n_seq):
        for hd in range(MEM_HEADS):
            sl = slice(hd * HEAD_DIM, (hd + 1) * HEAD_DIM)
            if heads_on_rows:
                rows = k_ref.shape[0] // n_seq
                k = k_ref[pl.ds(g * rows + hd, rows // MEM_HEADS, stride=MEM_HEADS), :]
                v = v_ref[pl.ds(g * rows + hd, rows // MEM_HEADS, stride=MEM_HEADS), :]
            else:
                k, v = k_ref[g, :, sl], v_ref[g, :, sl]
            q = (q_ref[g, :, sl] * scale).astype(bf16)
            s = lax.dot_general(q, k.astype(bf16), (((1,), (1,)), ((), ())), preferred_element_type=f32)
            e = jnp.exp(s - jnp.max(s, axis=-1, keepdims=True))
            p = e / jnp.sum(e, axis=-1, keepdims=True)
            o_ref[g, :, sl] = jnp.dot(p.astype(bf16), v.astype(bf16), preferred_element_type=f32)


def mem_attention(q, q_col, mk, mv, k_col=0, v_col=0, cached=None):
    b, t, _ = q.shape
    w = MEM_WIDTH
    tq = min(ROW_TILE, t)
    if cached is None:
        g = 1
        mlen = mk.shape[1]
        kspec = pl.BlockSpec((1, mlen, w), lambda i, j: (i, 0, k_col))
        vspec = pl.BlockSpec((1, mlen, w), lambda i, j: (i, 0, v_col))
    else:
        layer, mlen = cached
        g = MEM_SEQ_PER_STEP if b % MEM_SEQ_PER_STEP == 0 else 1
        kspec = vspec = pl.BlockSpec((g * mlen * MEM_HEADS, HEAD_DIM), lambda i, j: (layer * (b // g) + i, 0))
    return pl.pallas_call(
        functools.partial(_mem_attn_kernel, heads_on_rows=cached is not None),
        grid=(b // g, t // tq),
        in_specs=[pl.BlockSpec((g, tq, w), lambda i, j: (i, j, q_col)), kspec, vspec],
        out_specs=pl.BlockSpec((g, tq, w), lambda i, j: (i, j, 0)),
        out_shape=jax.ShapeDtypeStruct((b, t, w), f32),
        compiler_params=_params("parallel", "arbitrary"),
        name="mem_attn",
    )(q, mk, mv)


WKV_QUAD = 4
WKV_LANES = WKV_QUAD * RWKV_HEAD_DIM
WKV_TB = 64


def _block_outputs(r, w, k, b, v, s0, zt):
    n = RWKV_HEAD_DIM
    nt = r.shape[0]
    row = lax.broadcasted_iota(jnp.int32, (1, n, WKV_LANES), 1)
    lane = lax.broadcasted_iota(jnp.int32, (1, n, WKV_LANES), 2)
    head = lax.shift_right_logical(lane, 6)
    tri = jnp.where(lax.broadcasted_iota(jnp.int32, (n, n), 0) >= lax.broadcasted_iota(jnp.int32, (n, n), 1),
                    1.0, 0.0).astype(bf16)
    log_w = jnp.concatenate([jnp.log(w[i]) for i in range(nt)], axis=1)
    hi = log_w.astype(bf16)
    log_p = (jnp.dot(tri, hi, preferred_element_type=f32)
             + jnp.dot(tri, (log_w - hi.astype(f32)).astype(bf16), preferred_element_type=f32))
    log_p = jnp.stack([log_p[:, i * WKV_LANES:(i + 1) * WKV_LANES] for i in range(nt)])
    p, p_inv = jnp.exp(log_p), jnp.exp(-log_p)
    zero = jnp.zeros((), bf16)

    def stack(x):
        xb = x.astype(bf16)
        return jnp.concatenate([jnp.where(head == h4, xb, zero) for h4 in range(WKV_QUAD)], axis=1)

    bdot = lambda x, y: jnp.einsum('nil,njl->nij', x, y, preferred_element_type=f32)
    rt = (r * p).astype(bf16)
    y0 = bdot(s0.astype(bf16), stack(rt))
    causal = jnp.bitwise_and(lane, n - 1) <= row
    a_k = jnp.where(causal, bdot(rt, stack(k * p_inv)), 0.0)
    a_b = jnp.where(causal, bdot(rt, stack(b * p_inv)), 0.0)
    eye_rep = jnp.where(row == jnp.bitwise_and(lane, n - 1), 1.0, 0.0).astype(bf16)
    vt = bdot(jnp.broadcast_to(eye_rep, (nt, n, WKV_LANES)), stack(v))
    return y0 + bdot(vt.astype(bf16), stack(a_k)) - bdot(zt.astype(bf16), stack(a_b))


def _wkv_kernel(r_ref, w_ref, k_ref, kk_ref, b_ref, v_ref, s0_ref, yt_ref, st_ref,
                s_scr, lhs_scr, vd_scr, yl_scr, s0_scr, zt_scr, *, nb, nq, tb, defer_y):
    n = RWKV_HEAD_DIM
    ti = pl.program_id(1)

    @pl.when(ti == 0)
    def _():
        for ib in range(nb):
            s_scr[ib * nq * n:(ib + 1) * nq * n, :] = s0_ref[ib]

    if defer_y:
        s0_scr[...] = s_scr[...]
        zt_scr[...] = jnp.zeros(zt_scr.shape, f32)
    else:
        yt_ref[...] = jnp.zeros(yt_ref.shape, f32)
    ri = lax.broadcasted_iota(jnp.int32, (WKV_LANES, WKV_LANES), 0)
    ci = lax.broadcasted_iota(jnp.int32, (WKV_LANES, WKV_LANES), 1)
    ones_blk = jnp.where(lax.shift_right_logical(ri, 6) == lax.shift_right_logical(ci, 6), 1.0, 0.0).astype(bf16)
    ones_blk2 = jnp.concatenate([ones_blk, ones_blk], axis=0)
    eye_rep = jnp.where(lax.broadcasted_iota(jnp.int32, (n, WKV_LANES), 0)
                        == jnp.bitwise_and(lax.broadcasted_iota(jnp.int32, (n, WKV_LANES), 1), n - 1), 1.0, 0.0)
    step_lane = jnp.bitwise_and(lax.broadcasted_iota(jnp.int32, (nq * n, WKV_LANES), 1), n - 1)
    tiles = [(ib, q) for ib in range(nb) for q in range(nq)]

    def step(t, carry):
        row = lambda ref, ib, q: ref[ib, pl.ds(t, 1), q * WKV_LANES:(q + 1) * WKV_LANES]
        for ib, q in tiles:
            rs = slice((ib * nq + q) * n, (ib * nq + q + 1) * n)
            prod = s_scr[rs, :] * row(kk_ref, ib, q)
            hi = prod.astype(bf16)
            lhs_scr[rs, 0:WKV_LANES] = hi
            lhs_scr[rs, WKV_LANES:2 * WKV_LANES] = (prod - hi.astype(f32)).astype(bf16)
            vd_scr[rs, :] = (eye_rep * row(v_ref, ib, q)).astype(bf16)
        z = jnp.dot(lhs_scr[...], ones_blk2, preferred_element_type=f32)
        vcol = jnp.dot(vd_scr[...], ones_blk, preferred_element_type=f32)
        for ib, q in tiles:
            rs = slice((ib * nq + q) * n, (ib * nq + q + 1) * n)
            s = s_scr[rs, :] * row(w_ref, ib, q) - z[rs] * row(b_ref, ib, q) + vcol[rs] * row(k_ref, ib, q)
            s_scr[rs, :] = s
            if not defer_y:
                yl_scr[rs, :] = (s * row(r_ref, ib, q)).astype(bf16)
        if not defer_y:
            y = jnp.dot(yl_scr[...], ones_blk, preferred_element_type=f32)
        for ib in range(nb):
            rs = slice(ib * nq * n, (ib + 1) * nq * n)
            if defer_y:
                zt_scr[rs, :] = jnp.where(step_lane == t, z[rs], zt_scr[rs, :])
            else:
                yt_ref[ib, 0] = jnp.where(step_lane == t, y[rs], yt_ref[ib, 0])
        return carry

    lax.fori_loop(0, tb, step, 0, unroll=4)

    if defer_y:
        tiled = lambda ref: jnp.stack([ref[ib, :, q * WKV_LANES:(q + 1) * WKV_LANES] for ib, q in tiles])
        as_tiles = lambda ref: ref[...].reshape(len(tiles), n, WKV_LANES)
        yt = _block_outputs(tiled(r_ref), tiled(w_ref), tiled(k_ref), tiled(b_ref), tiled(v_ref),
                            as_tiles(s0_scr), as_tiles(zt_scr))
        for i, (ib, q) in enumerate(tiles):
            yt_ref[ib, 0, q * n:(q + 1) * n, :] = yt[i]

    @pl.when(ti == pl.num_programs(1) - 1)
    def _():
        for ib in range(nb):
            st_ref[ib] = s_scr[ib * nq * n:(ib + 1) * nq * n, :]


def wkv_scan(r, w, k, kk, b, v, s0, nb=2):
    bsz, t, width = r.shape
    n = RWKV_HEAD_DIM
    nh = width // n
    nq = nh // WKV_QUAD
    tb = min(WKV_TB, t)
    nblk = t // tb
    to_tiles = lambda s: s.reshape(bsz, nq, WKV_QUAD, n, n).transpose(0, 1, 3, 2, 4).reshape(bsz, nq * n, WKV_LANES)
    row = pl.BlockSpec((nb, tb, width), lambda i, j: (i, j, 0))
    st = pl.BlockSpec((nb, nq * n, WKV_LANES), lambda i, j: (i, 0, 0))
    rows_all = nb * nq * n
    yt, s_t = pl.pallas_call(
        functools.partial(_wkv_kernel, nb=nb, nq=nq, tb=tb, defer_y=tb == n),
        grid=(bsz // nb, nblk),
        in_specs=[row, row, row, row, row, row, st],
        out_specs=[pl.BlockSpec((nb, 1, nq * n, WKV_LANES), lambda i, j: (i, j, 0, 0)), st],
        out_shape=[jax.ShapeDtypeStruct((bsz, nblk, nq * n, WKV_LANES), f32),
                   jax.ShapeDtypeStruct((bsz, nq * n, WKV_LANES), f32)],
        scratch_shapes=[pltpu.VMEM((rows_all, WKV_LANES), f32),
                        pltpu.VMEM((rows_all, 2 * WKV_LANES), bf16),
                        pltpu.VMEM((rows_all, WKV_LANES), bf16),
                        pltpu.VMEM((rows_all, WKV_LANES), bf16),
                        pltpu.VMEM((rows_all, WKV_LANES), f32),
                        pltpu.VMEM((rows_all, WKV_LANES), f32)],
        compiler_params=_params("parallel", "arbitrary"),
        name="wkv_scan",
    )(r, w, k, kk, b, v, to_tiles(s0))
    y = yt.reshape(bsz, nblk, nq, n, WKV_QUAD, n).transpose(0, 1, 5, 2, 4, 3)[:, :, :tb].reshape(bsz, t, width)
    s_t = s_t.reshape(bsz, nq, n, WKV_QUAD, n).transpose(0, 1, 3, 2, 4).reshape(bsz, nh, n, n)
    return y, s_t


def _gelu_tanh(x):
    return 0.5 * x * (1.0 + jnp.tanh(0.7978845608028654 * (x + 0.044715 * x * x * x)))


def _chunk_rows(x_ref, n_chunks, row0=0, row_stride=1):
    return jnp.concatenate(
        [x_ref[pl.ds(row0 + s * row_stride, n_chunks, stride=CMP_STRIDE * row_stride), :]
         for s in range(CMP_STRIDE)], axis=1).astype(bf16)


def _compress_rows(x2, pos_ref, w1_ref, w2_ref, n_valid, n_heads):
    rows = x2.shape[0]
    n_chunks = rows // n_heads
    pab = jnp.dot(x2, w1_ref[...], preferred_element_type=f32)
    pos = jnp.dot(pos_ref[...], w1_ref[...], preferred_element_type=f32)
    posterm = pos[0:1, :HEAD_DIM] + pos[1:2, HEAD_DIM:]
    hid = pab[:, :HEAD_DIM] + pltpu.roll(pab[:, HEAD_DIM:], rows - 1, 0) + posterm
    out = jnp.dot(_gelu_tanh(hid).astype(bf16), w2_ref[...], preferred_element_type=f32)
    n = jnp.bitwise_and(lax.broadcasted_iota(jnp.int32, out.shape, 0), n_chunks - 1)
    return jnp.where(n < n_valid, out, 0.0)


def _compress_prompt_kernel(k_ref, v_ref, posk_ref, w1k_ref, w2k_ref, posv_ref, w1v_ref, w2v_ref,
                            ok_ref, ov_ref, *, n_chunks, n_valid):
    ok_ref[0, 0] = _compress_rows(_chunk_rows(k_ref.at[0], n_chunks), posk_ref, w1k_ref, w2k_ref, n_valid, 1)
    ov_ref[0, 0] = _compress_rows(_chunk_rows(v_ref.at[0], n_chunks), posv_ref, w1v_ref, w2v_ref, n_valid, 1)


def _cmp_weights(cmp_pos, cmp_w1, cmp_w2):
    ws = []
    for i in range(2):
        half = CMP_STRIDE * HEAD_DIM
        pos = _pad_to_rows(cmp_pos[i].reshape(2, half), 8).astype(bf16)
        w1 = cmp_w1[i].reshape(2, half, HEAD_DIM)
        ws += [pos, jnp.concatenate([w1[0], w1[1]], axis=1).astype(bf16), cmp_w2[i].astype(bf16)]
    return ws


_CMP_WEIGHT_SHAPES = [(8, CMP_STRIDE * HEAD_DIM), (CMP_STRIDE * HEAD_DIM, 2 * HEAD_DIM), (HEAD_DIM, HEAD_DIM)] * 2


def compress_prompt(ck, cv, cmp_pos, cmp_w1, cmp_w2):
    b, t, _ = ck.shape
    n_chunks = t // CMP_STRIDE
    n_valid = (t - CMP_BLOCK) // CMP_STRIDE + 1
    out = jax.ShapeDtypeStruct((b, NSA_KV, n_chunks, HEAD_DIM), f32)
    ospec = pl.BlockSpec((1, 1, n_chunks, HEAD_DIM), lambda i, kv: (i, kv, 0, 0))
    wspecs = [pl.BlockSpec(s, lambda i, kv: (0, 0)) for s in _CMP_WEIGHT_SHAPES]
    return pl.pallas_call(
        functools.partial(_compress_prompt_kernel, n_chunks=n_chunks, n_valid=n_valid),
        grid=(b, NSA_KV),
        in_specs=[pl.BlockSpec((1, t, HEAD_DIM), lambda i, kv: (i, 0, kv)),
                  pl.BlockSpec((1, t, HEAD_DIM), lambda i, kv: (i, 0, kv))] + wspecs,
        out_specs=[ospec, ospec],
        out_shape=[out, out],
        compiler_params=_params("parallel", "parallel"),
        name="compress_prompt",
    )(ck, cv, *_cmp_weights(cmp_pos, cmp_w1, cmp_w2))


PAGE_ROWS = PAGE_SIZE * NSA_KV


def _page_specs(n_pages, n_seq):
    return [pl.BlockSpec((PAGE_ROWS, HEAD_DIM), lambda i, pt, g=g, p=p: (pt[i * n_seq + g, p], 0))
            for g in range(n_seq) for p in range(n_pages)]


def _compress_paged_kernel(pt_ref, *refs, n_pages, n_seq, n_valid):
    np_all = n_seq * n_pages
    k_pages, v_pages = refs[:np_all], refs[np_all:2 * np_all]
    posk_ref, w1k_ref, w2k_ref, posv_ref, w1v_ref, w2v_ref, ok_ref, ov_ref = refs[2 * np_all:]
    per_page = PAGE_SIZE // CMP_STRIDE
    n_chunks = n_pages * per_page
    heads = [(g, kv) for g in range(n_seq) for kv in range(NSA_KV)]
    chunks = lambda pages: jnp.concatenate([_chunk_rows(pg, per_page, kv, NSA_KV)
                                            for g, kv in heads for pg in pages[g * n_pages:(g + 1) * n_pages]], axis=0)
    ok = _compress_rows(chunks(k_pages), posk_ref, w1k_ref, w2k_ref, n_valid, len(heads))
    ov = _compress_rows(chunks(v_pages), posv_ref, w1v_ref, w2v_ref, n_valid, len(heads))
    for i, (g, kv) in enumerate(heads):
        ok_ref[g, kv] = ok[i * n_chunks:(i + 1) * n_chunks]
        ov_ref[g, kv] = ov[i * n_chunks:(i + 1) * n_chunks]


def compress_paged(pool_k, pool_v, page_table, n_valid, cmp_pos, cmp_w1, cmp_w2):
    b, n_pages = page_table.shape
    n_seq = SEQ_PER_STEP if b % SEQ_PER_STEP == 0 else 1
    n_chunks = n_pages * PAGE_SIZE // CMP_STRIDE
    out = jax.ShapeDtypeStruct((b, NSA_KV, n_chunks, HEAD_DIM), f32)
    ospec = pl.BlockSpec((n_seq, NSA_KV, n_chunks, HEAD_DIM), lambda i, pt: (i, 0, 0, 0))
    wspecs = [pl.BlockSpec(s, lambda i, pt: (0, 0)) for s in _CMP_WEIGHT_SHAPES]
    np_all = n_seq * n_pages
    return pl.pallas_call(
        functools.partial(_compress_paged_kernel, n_pages=n_pages, n_seq=n_seq, n_valid=n_valid),
        grid_spec=pltpu.PrefetchScalarGridSpec(
            num_scalar_prefetch=1,
            grid=(b // n_seq,),
            in_specs=_page_specs(n_pages, n_seq) * 2 + wspecs,
            out_specs=[ospec, ospec]),
        out_shape=[out, out],
        compiler_params=_params("parallel"),
        name="compress_paged",
    )(page_table, *([pool_k] * np_all), *([pool_v] * np_all), *_cmp_weights(cmp_pos, cmp_w1, cmp_w2))


def _stack_heads(x):
    return jnp.concatenate([x[:, g * HEAD_DIM:(g + 1) * HEAD_DIM] for g in range(NSA_GROUP)], axis=0)


def _dot_nt(a, b):
    return lax.dot_general(a, b, (((1,), (1,)), ((), ())), preferred_element_type=f32)


def _softmax_heads(s, bias, tq):
    s3 = s.reshape(NSA_GROUP, tq, s.shape[1]) + bias[None]
    e = jnp.exp(s3 - jnp.max(s3, axis=-1, keepdims=True))
    return e, jnp.sum(e, axis=-1, keepdims=True)


def _compressed_branch(qc, ckc, cvc, pos_t, n_cmp, tq):
    s = _dot_nt(qc, ckc)
    n = lax.broadcasted_iota(jnp.int32, (tq, s.shape[1]), 1)
    vis = (n * CMP_STRIDE + (CMP_BLOCK - 1) <= pos_t) & (n < n_cmp)
    e, denom = _softmax_heads(s, jnp.where(vis, 0.0, NEG_INF), tq)
    any_vis = jnp.where(pos_t >= CMP_BLOCK - 1, 1.0, 0.0)
    p = e / denom * any_vis[None]
    o = jnp.dot(p.reshape(s.shape).astype(bf16), cvc, preferred_element_type=f32)
    return o, jnp.sum(p, axis=0)


def _select_blocks(psum, tq, pos0, n_slc, n_j):
    if tq < SEL_LANES:
        psum = jnp.concatenate([psum, jnp.zeros((SEL_LANES - tq, psum.shape[1]), f32)], axis=0)
    n_c = psum.shape[1]
    j = lax.broadcasted_iota(jnp.int32, (n_j, n_c), 0)
    cs = lax.broadcasted_iota(jnp.int32, (n_j, n_c), 1) * CMP_STRIDE
    overlap = jnp.where((cs < j * SLC_BLOCK + SLC_BLOCK) & (cs + (CMP_BLOCK - 1) >= j * SLC_BLOCK), 1.0, 0.0)
    imp_t = lax.dot_general(overlap, psum, (((1,), (1,)), ((), ())),
                            preferred_element_type=f32, precision=lax.Precision.HIGHEST)
    j = lax.broadcasted_iota(jnp.int32, imp_t.shape, 0)
    pos_t = pos0 + lax.broadcasted_iota(jnp.int32, imp_t.shape, 1)
    cur = lax.shift_right_logical(pos_t, SLC_SHIFT)
    causal = j * SLC_BLOCK <= pos_t
    forced = (j == 0) | (j == cur) | (j == cur - 1)
    score = jnp.where(causal, jnp.where(forced, FORCE_SCORE, imp_t), -FORCE_SCORE)
    score = jnp.where(j < n_slc, score, -2.0 * FORCE_SCORE)
    rank = jnp.zeros(imp_t.shape, f32)
    for jp in range(n_slc):
        row = score[jp:jp + 1, :]
        ahead = (row > score) | ((row == score) & (j > jp))
        rank = rank + jnp.where(ahead, 1.0, 0.0)
    sel_t = jnp.where(rank < min(N_SELECT, n_slc), 1.0, 0.0)
    return sel_t.T[0:tq]


def _selection_bias(sel, key0, n_keys):
    nj = sel.shape[1]
    j = lax.broadcasted_iota(jnp.int32, (nj, n_keys), 0)
    kpos = key0 + lax.broadcasted_iota(jnp.int32, (nj, n_keys), 1)
    e = jnp.where(lax.shift_right_logical(kpos, SLC_SHIFT) == j, 1.0, 0.0).astype(bf16)
    return jnp.dot(jnp.where(sel > 0.5, 0.0, NEG_INF).astype(bf16), e, preferred_element_type=f32)


def _window_branch(qr, wk, wv, kpos0, n_keys_valid, pos_t, tq, n_phantom=None):
    s = _dot_nt(qr, wk)
    lane = lax.broadcasted_iota(jnp.int32, (tq, s.shape[1]), 1)
    kpos = kpos0 + lane
    valid = (kpos <= pos_t) & (pos_t - kpos < WINDOW) & (lane < n_keys_valid)
    s3 = s.reshape(NSA_GROUP, tq, s.shape[1]) + jnp.where(valid, 0.0, NEG_INF)[None]
    m = jnp.max(s3, axis=-1, keepdims=True)
    if n_phantom is not None:
        m = jnp.where(n_phantom[None] > 0.0, jnp.maximum(m, 0.0), m)
    e = jnp.exp(s3 - m)
    denom = jnp.sum(e, axis=-1, keepdims=True)
    if n_phantom is not None:
        denom = denom + n_phantom[None] * jnp.exp(-m)
    return jnp.dot((e / denom).reshape(s.shape).astype(bf16), wv, preferred_element_type=f32)


def _gated_sum(gate, cols, tq, o_cmp, o_slc, o_win, g):
    r = slice(g * tq, (g + 1) * tq)
    c, s, w = cols[0] + g, cols[1] + g, cols[2] + g
    return gate[:, c:c + 1] * o_cmp[r] + gate[:, s:s + 1] * o_slc[r] + gate[:, w:w + 1] * o_win[r]


QK_SCALE = HEAD_DIM ** -0.5


def _nsa_prompt_kernel(q_ref, c2_ref, s2_ref, gate_ref, ckc_ref, cvc_ref, sk_ref, sv_ref, wk_ref, wv_ref,
                       o_ref, *, n_cmp, n_slc):
    i = pl.program_id(2)
    tq = Q_BLOCK
    rows = NSA_GROUP * tq
    q0 = i * tq
    q = q_ref[0] * QK_SCALE
    qc = _stack_heads(q).astype(bf16)
    qr = _rope_heads(q, c2_ref[...], s2_ref[...]).astype(bf16)
    pos_t = q0 + lax.broadcasted_iota(jnp.int32, (tq, 1), 0)

    o_cmp, psum = _compressed_branch(qc, ckc_ref[0, 0].astype(bf16), cvc_ref[0, 0].astype(bf16), pos_t, n_cmp, tq)
    sel = _select_blocks(psum, tq, q0, n_slc, n_slc)

    def slc_step(c, carry, causal):
        m, l, acc = carry
        k0 = pl.multiple_of(c * SLC_CHUNK, SLC_CHUNK)
        bias = _selection_bias(sel, k0, SLC_CHUNK)
        if causal:
            kpos = k0 + lax.broadcasted_iota(jnp.int32, bias.shape, 1)
            bias = jnp.where(kpos <= pos_t, bias, NEG_INF)
        s3 = _dot_nt(qr, sk_ref[0, pl.ds(k0, SLC_CHUNK), :]).reshape(NSA_GROUP, tq, SLC_CHUNK) + bias[None]
        m_new = jnp.maximum(m, jnp.max(s3, axis=-1, keepdims=True))
        alpha = jnp.exp(m - m_new)
        e = jnp.exp(s3 - m_new)
        l = alpha * l + jnp.sum(e, axis=-1, keepdims=True)
        pv = jnp.dot(e.reshape(rows, SLC_CHUNK).astype(bf16), sv_ref[0, pl.ds(k0, SLC_CHUNK), :],
                     preferred_element_type=f32)
        return m_new, l, alpha * acc + pv.reshape(NSA_GROUP, tq, HEAD_DIM)

    c_last = q0 // SLC_CHUNK
    init = (jnp.full((NSA_GROUP, tq, 1), NEG_INF, f32), jnp.zeros((NSA_GROUP, tq, 1), f32),
            jnp.zeros((NSA_GROUP, tq, HEAD_DIM), f32))
    carry = lax.fori_loop(0, c_last, functools.partial(slc_step, causal=False), init)
    _, l, acc = slc_step(c_last, carry, causal=True)
    o_slc = (acc / l).reshape(rows, HEAD_DIM)

    span = WINDOW + tq
    w0 = pl.multiple_of(jnp.maximum(q0 - WINDOW, 0), tq)
    n_phantom = jnp.maximum(WINDOW - 1 - pos_t, 0).astype(f32)
    o_win = _window_branch(qr, wk_ref[0, pl.ds(w0, span), :], wv_ref[0, pl.ds(w0, span), :], w0, span, pos_t, tq,
                           n_phantom)

    gate = _sigmoid(gate_ref[0])
    first_kv = pl.program_id(1) == 0
    for g in range(NSA_GROUP):
        head = [_gated_sum(gate, tuple(br * NSA_HEADS + kv * NSA_GROUP for br in range(3)), tq, o_cmp, o_slc, o_win, g)
                for kv in range(NSA_KV)]
        o_ref[0, :, g * HEAD_DIM:(g + 1) * HEAD_DIM] = jnp.where(first_kv, head[0], head[1])


def nsa_prompt(proj, gate_col, ckc, cvc, kv_bf, c2, s2, n_cmp):
    b, t, _ = proj.shape
    assert t % SLC_CHUNK == 0 and t >= WINDOW + Q_BLOCK
    n_slc = t // SLC_BLOCK
    gw = NSA_GROUP * HEAD_DIM
    kvcol = lambda c: pl.BlockSpec((1, t, HEAD_DIM), lambda bi, kv, i, c=c: (bi, 0, c * NSA_KV + kv))
    cmp_spec = pl.BlockSpec((1, 1, ckc.shape[2], HEAD_DIM), lambda bi, kv, i: (bi, kv, 0, 0))
    return pl.pallas_call(
        functools.partial(_nsa_prompt_kernel, n_cmp=n_cmp, n_slc=n_slc),
        grid=(b, NSA_KV, t // Q_BLOCK),
        in_specs=[pl.BlockSpec((1, Q_BLOCK, gw), lambda bi, kv, i: (bi, i, kv)),
                  pl.BlockSpec((Q_BLOCK, HEAD_DIM), lambda bi, kv, i: (i, 0)),
                  pl.BlockSpec((Q_BLOCK, HEAD_DIM), lambda bi, kv, i: (i, 0)),
                  pl.BlockSpec((1, Q_BLOCK, LANES), lambda bi, kv, i: (bi, i, gate_col)),
                  cmp_spec, cmp_spec, kvcol(2), kvcol(3), kvcol(4), kvcol(5)],
        out_specs=pl.BlockSpec((1, Q_BLOCK, gw), lambda bi, kv, i: (bi, i, kv)),
        out_shape=jax.ShapeDtypeStruct((b, t, NSA_WIDTH), f32),
        compiler_params=_params("parallel", "parallel", "arbitrary"),
        name="nsa_prompt",
    )(proj, c2, s2, proj, ckc, cvc, kv_bf, kv_bf, kv_bf, kv_bf)


def _pad_rows(x, n):
    return jnp.concatenate([x, jnp.zeros((n - x.shape[0], x.shape[1]), x.dtype)], axis=0)


def _nsa_sample_kernel(pt_ref, *refs, n_pages, n_seq, n_cmp, n_slc, n_j, past, tq):
    np_all = n_seq * n_pages
    k_pages, v_pages = refs[:np_all], refs[np_all:2 * np_all]
    (q_ref, c2_ref, s2_ref, gate_ref, ckc_ref, cvc_ref, nsk_ref, nsv_ref, nwk_ref, nwv_ref,
     wink_ref, winv_ref, o_ref, owk_ref, owv_ref) = refs[2 * np_all:]
    pos_t = past + lax.broadcasted_iota(jnp.int32, (tq, 1), 0)
    lw2 = wink_ref.shape[0] // n_seq
    lw = lw2 // NSA_KV
    c2, s2 = c2_ref[...], s2_ref[...]
    for g in range(n_seq):
        gate = _sigmoid(gate_ref[g])
        keep = lw2 - tq * NSA_KV
        for cache_ref, new_ref, out_ref in ((wink_ref, nwk_ref, owk_ref), (winv_ref, nwv_ref, owv_ref)):
            out_ref[g * lw2:g * lw2 + keep, :] = cache_ref[g * lw2 + tq * NSA_KV:(g + 1) * lw2, :]
            for kv in range(NSA_KV):
                out_ref[pl.ds(g * lw2 + keep + kv, tq, stride=NSA_KV), :] = new_ref[g, :, kv * HEAD_DIM:(kv + 1) * HEAD_DIM]
        for kv in range(NSA_KV):
            ksl = slice(kv * HEAD_DIM, (kv + 1) * HEAD_DIM)
            q = q_ref[g, :, kv * NSA_GROUP * HEAD_DIM:(kv + 1) * NSA_GROUP * HEAD_DIM] * QK_SCALE
            qc = _stack_heads(q).astype(bf16)
            qr = _rope_heads(q, c2, s2).astype(bf16)
            o_cmp, psum = _compressed_branch(qc, ckc_ref[g, kv].astype(bf16), cvc_ref[g, kv].astype(bf16), pos_t,
                                             n_cmp, tq)
            sel = _select_blocks(psum, tq, past, n_slc, n_j)

            paged = lambda pages: [pg[pl.ds(kv, PAGE_SIZE, stride=NSA_KV), :] for pg in pages[g * n_pages:(g + 1) * n_pages]]
            sk = jnp.concatenate(paged(k_pages) + [_pad_rows(nsk_ref[g, :, ksl], LANES)], axis=0).astype(bf16)
            sv = jnp.concatenate(paged(v_pages) + [_pad_rows(nsv_ref[g, :, ksl], LANES)], axis=0).astype(bf16)
            n_keys = sk.shape[0]
            bias = _selection_bias(sel, 0, n_keys)
            bias = jnp.where(lax.broadcasted_iota(jnp.int32, bias.shape, 1) <= pos_t, bias, NEG_INF)
            e, denom = _softmax_heads(_dot_nt(qr, sk), bias, tq)
            o_slc = jnp.dot((e / denom).reshape(NSA_GROUP * tq, n_keys).astype(bf16), sv, preferred_element_type=f32)

            cached = lambda ref: ref[pl.ds(g * lw2 + kv, lw, stride=NSA_KV), :]
            wk = jnp.concatenate([cached(wink_ref), _pad_rows(nwk_ref[g, :, ksl], LANES)], axis=0).astype(bf16)
            wv = jnp.concatenate([cached(winv_ref), _pad_rows(nwv_ref[g, :, ksl], LANES)], axis=0).astype(bf16)
            o_win = _window_branch(qr, wk, wv, past - lw, lw + tq, pos_t, tq)

            cols = tuple(br * NSA_HEADS + kv * NSA_GROUP for br in range(3))
            for hg in range(NSA_GROUP):
                hd = kv * NSA_GROUP + hg
                o_ref[g, :, hd * HEAD_DIM:(hd + 1) * HEAD_DIM] = _gated_sum(gate, cols, tq, o_cmp, o_slc, o_win, hg)


def nsa_sample(proj, gate_col, ckc, cvc, pool_k, pool_v, page_table, new_rows, win_k, win_v, c2, s2, n_cmp, past):
    b, tq, _ = proj.shape
    n_pages = page_table.shape[1]
    assert past == n_pages * PAGE_SIZE and past % SLC_BLOCK == 0 and tq <= SLC_BLOCK and (tq * NSA_KV) % 8 == 0
    n_seq = SEQ_PER_STEP if b % SEQ_PER_STEP == 0 else 1
    n_slc = past // SLC_BLOCK + 1
    n_j = -(-n_slc // SLC_BLOCK) * SLC_BLOCK
    lw2 = win_k.shape[0] // b
    per_b = lambda shape: pl.BlockSpec((n_seq,) + shape, lambda i, pt: (i,) + (0,) * len(shape))
    tab = pl.BlockSpec((tq, HEAD_DIM), lambda i, pt: (0, 0))
    win = pl.BlockSpec((n_seq * lw2, HEAD_DIM), lambda i, pt: (i, 0))
    np_all = n_seq * n_pages
    return pl.pallas_call(
        functools.partial(_nsa_sample_kernel, n_pages=n_pages, n_seq=n_seq, n_cmp=n_cmp, n_slc=n_slc, n_j=n_j,
                          past=past, tq=tq),
        grid_spec=pltpu.PrefetchScalarGridSpec(
            num_scalar_prefetch=1,
            grid=(b // n_seq,),
            in_specs=_page_specs(n_pages, n_seq) * 2 + [
                per_b((tq, NSA_WIDTH)), tab, tab,
                pl.BlockSpec((n_seq, tq, LANES), lambda i, pt: (i, 0, gate_col)),
                per_b((NSA_KV, ckc.shape[2], HEAD_DIM)), per_b((NSA_KV, ckc.shape[2], HEAD_DIM))]
                + [per_b((tq, KV_COLS))] * 4 + [win, win],
            out_specs=[per_b((tq, NSA_WIDTH)), win, win]),
        out_shape=[jax.ShapeDtypeStruct((b, tq, NSA_WIDTH), f32),
                   jax.ShapeDtypeStruct(win_k.shape, f32), jax.ShapeDtypeStruct(win_v.shape, f32)],
        compiler_params=_params("parallel"),
        name="nsa_sample",
    )(page_table, *([pool_k] * np_all), *([pool_v] * np_all), proj, c2, s2, proj, ckc, cvc, *new_rows, win_k, win_v)


def _prev_rows(x, tile, halo_ref, first_ref, seq_len):
    tm, c = x.shape
    prev = pltpu.roll(x, 1, 0)
    row = lax.broadcasted_iota(jnp.int32, (tm, 1), 0)
    if seq_len >= tm:
        tiles_per_seq = seq_len // tm
        first = first_ref[pl.ds(tile // tiles_per_seq, 1), :]
        edge = jnp.where(tile % tiles_per_seq == 0, first, halo_ref[7:8, :])
        return jnp.where(row == 0, edge, prev)
    pieces = []
    for j in range(tm // seq_len):
        pieces.append(jnp.broadcast_to(first_ref[j:j + 1, :], (8, c)))
        if seq_len > 8:
            pieces.append(jnp.zeros((seq_len - 8, c), f32))
    return jnp.where(jnp.bitwise_and(row, seq_len - 1) == 0, jnp.concatenate(pieces, axis=0), prev)


def _shift_specs(m, c, tm, seq_len, n_seq, col=None):
    cb = (lambda *g: 0) if col is None else col
    tile = pl.BlockSpec((tm, c), lambda *g: (g[0], cb(*g)))
    halo = pl.BlockSpec((8, c), lambda *g: (jnp.maximum(g[0] * (tm // 8) - 1, 0), cb(*g)))
    if seq_len >= tm:
        first = pl.BlockSpec((-(-n_seq // 8) * 8, c), lambda *g: (0, cb(*g)))
    else:
        first = pl.BlockSpec((tm // seq_len, c), lambda *g: (g[0], cb(*g)))
    return tile, halo, first


def _pad_first(first, seq_len, tm):
    return _pad_to_rows(first, -(-first.shape[0] // 8) * 8) if seq_len >= tm else first


def _head_sums(x):
    ri = lax.broadcasted_iota(jnp.int32, (WKV_LANES, WKV_LANES), 0)
    ci = lax.broadcasted_iota(jnp.int32, (WKV_LANES, WKV_LANES), 1)
    ones_blk = jnp.where(lax.shift_right_logical(ri, 6) == lax.shift_right_logical(ci, 6), 1.0, 0.0).astype(bf16)
    hi = x.astype(bf16)
    lo = (x - hi.astype(f32)).astype(bf16)
    out = []
    for c in range(x.shape[1] // WKV_LANES):
        sl = slice(c * WKV_LANES, (c + 1) * WKV_LANES)
        out.append(jnp.dot(hi[:, sl], ones_blk, preferred_element_type=f32)
                   + jnp.dot(lo[:, sl], ones_blk, preferred_element_type=f32))
    return jnp.concatenate(out, axis=1)


def _lora_kernel(hn_ref, halo_ref, first_ref, mu_ref, wd1_ref, wa1_ref, wg1_ref, wd2_ref, wa2_ref, wg2_ref,
                 w0_ref, a0_ref, decay_ref, a_ref, g_ref, *, seq_len):
    hn = hn_ref[...]
    xx = _prev_rows(hn, pl.program_id(0), halo_ref, first_ref, seq_len) - hn
    mix = lambda r: (hn + xx * mu_ref[r:r + 1, :]).astype(bf16)
    dot = lambda x, w_ref: jnp.dot(x, w_ref[...], preferred_element_type=f32)
    w_raw = w0_ref[...] + dot(jnp.tanh(dot(mix(0), wd1_ref)).astype(bf16), wd2_ref)
    softplus = jnp.maximum(-w_raw, 0.0) + jnp.log(1.0 + jnp.exp(-jnp.abs(w_raw)))
    decay_ref[...] = jnp.exp(-jnp.exp(-softplus - 0.5))
    a_ref[...] = _sigmoid(a0_ref[...] + dot(dot(mix(1), wa1_ref).astype(bf16), wa2_ref))
    g_ref[...] = dot(_sigmoid(dot(mix(2), wg1_ref)).astype(bf16), wg2_ref)


def rwkv_lora(hn, x_prev, seq_len, P):
    m, d = hn.shape
    rw = P['w0'].shape[0]
    tm = min(ROW_TILE, m)
    pad128 = lambda w: _pad_cols(w, -(-w.shape[1] // LANES) * LANES).astype(bf16)
    w1s = [pad128(P[k]) for k in ('w_decay1', 'w_aaa1', 'w_gate1')]
    w2s = [_pad_to_rows(P[k], w1.shape[1]).astype(bf16) for k, w1 in zip(('w_decay2', 'w_aaa2', 'w_gate2'), w1s)]
    full = lambda x: pl.BlockSpec(x.shape, lambda i: (0, 0))
    mu = _pad_to_rows(P['mu_wag'], 8)
    vecs = [P['w0'].reshape(1, rw), P['a0'].reshape(1, rw)]
    out = jax.ShapeDtypeStruct((m, rw), f32)
    ospec = pl.BlockSpec((tm, rw), lambda i: (i, 0))
    return pl.pallas_call(
        functools.partial(_lora_kernel, seq_len=seq_len),
        grid=(m // tm,),
        in_specs=list(_shift_specs(m, d, tm, seq_len, x_prev.shape[0])) + [full(mu)]
                 + [full(w) for w in w1s + w2s + vecs],
        out_specs=[ospec] * 3,
        out_shape=[out] * 3,
        compiler_params=_params("parallel"),
        name="rwkv_lora",
    )(hn, hn, _pad_first(x_prev, seq_len, tm), mu, *w1s, *w2s, *vecs)


RKV_GROUPS = 3
IN_COL_TILE = 512


def _rwkv_in_kernel(h_ref, halo_ref, first_ref, g_ref, w_ref, mu_ref, a_ref, kkw_ref, kaw_ref,
                    r_ref, k_ref, kk_ref, b_ref, v_ref, mq_ref, xn_scr, edge_scr, *, seq_len, per_group):
    i, j = pl.program_id(0), pl.program_id(1)
    tm, c = r_ref.shape

    @pl.when(j == 0)
    def _():
        xn_scr[...] = _rms(h_ref[...], g_ref[...]).astype(bf16)
        edge_scr[0:8, :] = _rms(halo_ref[...], g_ref[...])
        edge_scr[8:, :] = first_ref[...]

    cur = jnp.dot(xn_scr[...], w_ref[...], preferred_element_type=f32)
    edge = jnp.dot(edge_scr[...].astype(bf16), w_ref[...], preferred_element_type=f32)
    halo, first = edge[0:8], edge[8:]
    prev = pltpu.roll(cur, 1, 0)
    row = lax.broadcasted_iota(jnp.int32, (tm, 1), 0)
    if seq_len >= tm:
        tiles_per_seq = seq_len // tm
        seq = lax.broadcasted_iota(jnp.int32, (first.shape[0], 1), 0) == i // tiles_per_seq
        first_row = jnp.sum(jnp.where(seq, first, 0.0), axis=0, keepdims=True)
        edge_row = jnp.where(i % tiles_per_seq == 0, first_row, halo[7:8])
        prev = jnp.where(row == 0, edge_row, prev)
    else:
        pieces = []
        for q in range(tm // seq_len):
            pieces.append(jnp.broadcast_to(first[q:q + 1], (8, c)))
            if seq_len > 8:
                pieces.append(jnp.zeros((seq_len - 8, c), f32))
        prev = jnp.where(jnp.bitwise_and(row, seq_len - 1) == 0, jnp.concatenate(pieces, axis=0), prev)
    x = cur + (prev - cur) * mu_ref[...]

    @pl.when(j < per_group)
    def _():
        r_ref[...] = x

    @pl.when((j >= per_group) & (j < 2 * per_group))
    def _():
        a = a_ref[...]
        kk = x * kkw_ref[...]
        kk = kk * lax.rsqrt(_head_sums(kk * kk) + 1e-12)
        kk_ref[...] = kk
        b_ref[...] = kk * a
        k_ref[...] = x * (1.0 + (a - 1.0) * kaw_ref[...])

    @pl.when((j >= 2 * per_group) & (j < 3 * per_group))
    def _():
        v_ref[...] = x

    @pl.when(j >= 3 * per_group)
    def _():
        mq_ref[...] = cur


def rwkv_in(h, x_prev, a_rate, seq_len, P):
    m, d = h.shape
    rw = a_rate.shape[1]
    c = IN_COL_TILE
    tm = min(ROW_TILE, m)
    per_group = rw // c
    n_col = RKV_GROUPS * per_group + MEM_WIDTH // c
    assert rw % c == 0 and MEM_WIDTH == c and P['w_in_a'].shape[1] == n_col * c
    tile, halo, first = _shift_specs(m, d, tm, seq_len, x_prev.shape[0])
    n_first = first.block_shape[0]
    mu = jnp.pad(P['mu_rkv'], (0, MEM_WIDTH)).reshape(1, -1)
    grp = lambda base: (lambda i, j: (i, jnp.clip(j - base * per_group, 0, per_group - 1)))
    vec = lambda base: (lambda i, j: (0, jnp.clip(j - base * per_group, 0, per_group - 1)))
    wide = jax.ShapeDtypeStruct((m, rw), f32)
    return pl.pallas_call(
        functools.partial(_rwkv_in_kernel, seq_len=seq_len, per_group=per_group),
        grid=(m // tm, n_col),
        in_specs=[pl.BlockSpec((tm, d), lambda i, j: (i, 0)),
                  pl.BlockSpec((8, d), lambda i, j: (jnp.maximum(i * (tm // 8) - 1, 0), 0)),
                  pl.BlockSpec(first.block_shape, lambda i, j: (i if seq_len < tm else 0, 0)),
                  pl.BlockSpec((1, d), lambda i, j: (0, 0)),
                  pl.BlockSpec((d, c), lambda i, j: (0, j)),
                  pl.BlockSpec((1, c), lambda i, j: (0, j)),
                  pl.BlockSpec((tm, c), grp(1)),
                  pl.BlockSpec((1, c), vec(1)),
                  pl.BlockSpec((1, c), vec(1))],
        out_specs=[pl.BlockSpec((tm, c), grp(0)), pl.BlockSpec((tm, c), grp(1)), pl.BlockSpec((tm, c), grp(1)),
                   pl.BlockSpec((tm, c), grp(1)), pl.BlockSpec((tm, c), grp(2)),
                   pl.BlockSpec((tm, c), lambda i, j: (i, 0))],
        out_shape=[wide] * 5 + [jax.ShapeDtypeStruct((m, MEM_WIDTH), f32)],
        scratch_shapes=[pltpu.VMEM((tm, d), bf16), pltpu.VMEM((8 + n_first, d), f32)],
        compiler_params=_params("parallel", "arbitrary"),
        name="rwkv_in",
    )(h, h, _pad_first(x_prev, seq_len, tm), P['g_mix_pre'].reshape(1, d), P['w_in_a'].astype(bf16), mu, a_rate,
      P['k_k'].reshape(1, rw), P['k_a'].reshape(1, rw))


def _rwkv_out_kernel(y_ref, r_ref, k_ref, v_ref, g_ref, om_ref, h_ref, lw_ref, lb_ref, rk_ref, w_ref, gp_ref, o_ref):
    inv_n = 1.0 / RWKV_HEAD_DIM
    y = y_ref[...]
    d = y - _head_sums(y) * inv_n
    var = _head_sums(d * d) * inv_n
    yn = d * lax.rsqrt(var + GN_EPS) * lw_ref[...] + lb_ref[...]
    bonus = _head_sums(r_ref[...] * k_ref[...] * rk_ref[...]) * v_ref[...]
    o = ((yn + bonus) * g_ref[...]).astype(bf16)
    rw = o.shape[1]
    acc = jnp.dot(o, w_ref[:rw, :], preferred_element_type=f32)
    acc += jnp.dot(om_ref[...].astype(bf16), w_ref[rw:, :], preferred_element_type=f32)
    o_ref[...] = h_ref[...] + _rms(acc, gp_ref[...])


def rwkv_out(y, r, k, v, gate, o_mem, h, P):
    m, rw = y.shape
    d = h.shape[1]
    tm = min(ROW_TILE // 2, m)
    wide = pl.BlockSpec((tm, rw), lambda i: (i, 0))
    vec = pl.BlockSpec((1, rw), lambda i: (0, 0))
    return pl.pallas_call(
        _rwkv_out_kernel,
        grid=(m // tm,),
        in_specs=[wide] * 5 + [pl.BlockSpec((tm, o_mem.shape[1]), lambda i: (i, 0)),
                               pl.BlockSpec((tm, d), lambda i: (i, 0)), vec, vec, vec,
                               pl.BlockSpec((rw + o_mem.shape[1], d), lambda i: (0, 0)),
                               pl.BlockSpec((1, d), lambda i: (0, 0))],
        out_specs=pl.BlockSpec((tm, d), lambda i: (i, 0)),
        out_shape=jax.ShapeDtypeStruct((m, d), f32),
        compiler_params=_params("parallel"),
        name="rwkv_out",
    )(y, r, k, v, gate, o_mem, h, P['lnx_w'].reshape(1, rw), P['lnx_b'].reshape(1, rw), P['r_k'].reshape(1, rw),
      P['w_out_a'].astype(bf16), P['g_mix_post'].reshape(1, d))


def _rope_tables(pos):
    half = HEAD_DIM // 2
    inv = jnp.power(ROPE_THETA, -jnp.arange(half, dtype=f32) / half)
    ang = pos.astype(f32)[:, None] * inv[None, :]
    cos, sin = jnp.cos(ang), jnp.sin(ang)
    return jnp.concatenate([cos, cos], axis=-1), jnp.concatenate([-sin, sin], axis=-1)


def _pad_cols(w, n):
    return jnp.pad(w, ((0, 0), (0, n - w.shape[1])))


def _pad_to_rows(x, n):
    return jnp.pad(x, ((0, n - x.shape[0]), (0, 0)))


def rwkv_mem_layer(h, x_prev, s0, mem, P):
    b, t, d = h.shape
    m = b * t
    assert t & (t - 1) == 0 and t % 8 == 0 and (t % ROW_TILE == 0 or ROW_TILE % t == 0)
    h2 = h.reshape(m, d)
    hn = rmsnorm(h2, P['g_mix_pre'])
    rw = P['w0'].shape[0]

    decay, a_rate, gate = rwkv_lora(hn, x_prev, t, P)
    r, k, kk, kb, v, mq = rwkv_in(h2, x_prev, a_rate, t, P)
    as3 = lambda x: x.reshape(b, t, -1)
    y, s_t = wkv_scan(as3(r), as3(decay), as3(k), as3(kk), as3(kb), as3(v), s0)
    o_mem = mem(as3(mq), 0)
    h2 = rwkv_out(y.reshape(m, rw), r, k, v, gate, o_mem.reshape(m, MEM_WIDTH), h2, P)
    h2 = ffn_residual(h2, P['g_ffn_pre'], P['w_ff1'], P['w_ff2'], P['g_ffn_post'])
    return h2.reshape(b, t, d), s_t, hn.reshape(b, t, d)[:, -1]


GATE_COL = (NSA_WIDTH + MEM_WIDTH) // LANES


def nsa_mem_layer(h, mem, P, attend):
    b, t, d = h.shape
    m = b * t
    h2 = h.reshape(m, d)
    n_in = (GATE_COL + 1) * LANES
    n_in = -(-n_in // 768) * 768
    proj = matmul(h2, _pad_cols(P['w_in_b'].astype(bf16), n_in), g=P['g_mix_pre'], tn=768).reshape(b, t, -1)
    o_nsa = attend(proj)
    o_mem = mem(proj, NSA_WIDTH // MEM_WIDTH)
    h2 = out_proj_residual(o_nsa.reshape(m, NSA_WIDTH), o_mem.reshape(m, MEM_WIDTH), 0, P['w_out_b'], h2,
                           P['g_mix_post'])
    h2 = ffn_residual(h2, P['g_ffn_pre'], P['w_ff1'], P['w_ff2'], P['g_ffn_post'])
    return h2.reshape(b, t, d)


def kernel(x_prompt, x_sample, mem_prompt, cache_mem_k, cache_mem_v, state_wkv, state_shift, cache_cmp_k, cache_cmp_v, cache_slc_k, cache_slc_v, cache_win_k, cache_win_v, page_table, g_mix_pre, g_mix_post, g_ffn_pre, g_ffn_post, g_mem, w_mem_k, w_mem_v, w_in_a, mu_rkv, mu_wag, w0, w_decay1, w_decay2, a0, w_aaa1, w_aaa2, w_gate1, w_gate2, k_k, k_a, r_k, lnx_w, lnx_b, w_out_a, g_kv, w_kv, cmp_pos, cmp_w1, cmp_w2, w_in_b, w_out_b, w_ff1, w_ff2):
    bp, tp, d = x_prompt.shape
    bs, ts, _ = x_sample.shape
    depth = g_mix_pre.shape[0]
    assert depth == 2 and w_in_a.shape[0] == 1 and w_in_b.shape[0] == 1
    n_pages = page_table.shape[1]
    past = n_pages * PAGE_SIZE
    mem_len = mem_prompt.shape[1]

    P0 = dict(g_mix_pre=g_mix_pre[0], g_mix_post=g_mix_post[0], g_ffn_pre=g_ffn_pre[0], g_ffn_post=g_ffn_post[0],
              w_in_a=w_in_a[0], mu_rkv=mu_rkv[0], mu_wag=mu_wag[0], w0=w0[0], w_decay1=w_decay1[0],
              w_decay2=w_decay2[0], a0=a0[0], w_aaa1=w_aaa1[0], w_aaa2=w_aaa2[0], w_gate1=w_gate1[0],
              w_gate2=w_gate2[0], k_k=k_k[0], k_a=k_a[0], r_k=r_k[0], lnx_w=lnx_w[0], lnx_b=lnx_b[0],
              w_out_a=w_out_a[0], w_ff1=w_ff1[0], w_ff2=w_ff2[0])
    P1 = dict(g_mix_pre=g_mix_pre[1], g_mix_post=g_mix_post[1], g_ffn_pre=g_ffn_pre[1], g_ffn_post=g_ffn_post[1],
              w_in_b=w_in_b[0], w_out_b=w_out_b[0], w_ff1=w_ff1[1], w_ff2=w_ff2[1])
    rows4 = lambda x, bsz: x.reshape(bsz, -1, NSA_KV, HEAD_DIM)

    mem2 = mem_prompt.reshape(bp * mem_len, d)
    mkv = [matmul(mem2, jnp.concatenate([w_mem_k[l], w_mem_v[l]], axis=1), g=g_mem[l]).reshape(bp, mem_len, -1)
           for l in range(depth)]
    mem_k_p = jnp.stack([x[..., :MEM_WIDTH] for x in mkv])
    mem_v_p = jnp.stack([x[..., MEM_WIDTH:] for x in mkv])
    mem_p = lambda l: (lambda q, q_col: mem_attention(q, q_col, mkv[l], mkv[l], k_col=0, v_col=1))

    nh = w0.shape[1] // RWKV_HEAD_DIM
    shift0 = jnp.zeros((bp, d), f32)
    wkv0 = jnp.zeros((bp, nh, RWKV_HEAD_DIM, RWKV_HEAD_DIM), f32)
    h, wkv_p, shift_p = rwkv_mem_layer(x_prompt, shift0, wkv0, mem_p(0), P0)

    c2p, s2p = _rope_tables(jnp.arange(tp, dtype=jnp.int32))
    rows_p, kv_bf = kv_proj(h.reshape(bp * tp, d), g_kv, w_kv, c2p, s2p)
    as_p = lambda x: x.reshape(bp, tp, -1)
    ckc, cvc = compress_prompt(as_p(rows_p[0]), as_p(rows_p[1]), cmp_pos, cmp_w1, cmp_w2)
    n_cmp_p = (tp - CMP_BLOCK) // CMP_STRIDE + 1

    def attend_prompt(proj):
        return nsa_prompt(proj, GATE_COL, ckc, cvc, as_p(kv_bf), c2p, s2p, n_cmp_p)

    y_p = nsa_mem_layer(h, mem_p(1), P1, attend_prompt)
    cmp_k_p, cmp_v_p, slc_k_p, slc_v_p, win_k_p, win_v_p = [rows4(x, bp) for x in rows_p]
    n_keep = min(WINDOW, tp)
    win_k_p, win_v_p = win_k_p[:, tp - n_keep:], win_v_p[:, tp - n_keep:]

    mk_s, mv_s = cache_mem_k.reshape(-1, HEAD_DIM), cache_mem_v.reshape(-1, HEAD_DIM)
    mem_s = lambda l: (lambda q, q_col: mem_attention(q, q_col, mk_s, mv_s, cached=(l, mem_len)))
    h, wkv_s, shift_s = rwkv_mem_layer(x_sample, state_shift[0], state_wkv[0], mem_s(0), P0)

    c2s, s2s = _rope_tables(past + jnp.arange(ts, dtype=jnp.int32))
    rows_s, _ = kv_proj(h.reshape(bs * ts, d), g_kv, w_kv, jnp.tile(c2s, (bs, 1)), jnp.tile(s2s, (bs, 1)))
    as_s = lambda x: x.reshape(bs, ts, -1)
    n_cmp_s = (past + ts - CMP_BLOCK) // CMP_STRIDE + 1
    assert (n_cmp_s - 1) * CMP_STRIDE + CMP_BLOCK <= past
    n_pool = cache_cmp_k.shape[0]
    pool = lambda x: x.reshape(n_pool * PAGE_ROWS, HEAD_DIM)
    ckc_s, cvc_s = compress_paged(pool(cache_cmp_k), pool(cache_cmp_v), page_table, n_cmp_s, cmp_pos, cmp_w1, cmp_w2)
    win_k2, win_v2 = cache_win_k.reshape(-1, HEAD_DIM), cache_win_v.reshape(-1, HEAD_DIM)

    new_win = []

    def attend_sample(proj):
        o, wk_out, wv_out = nsa_sample(proj, GATE_COL, ckc_s, cvc_s, pool(cache_slc_k), pool(cache_slc_v), page_table,
                                       [as_s(x) for x in rows_s[2:]], win_k2, win_v2, c2s, s2s, n_cmp_s, past)
        new_win.extend([wk_out, wv_out])
        return o

    y_s = nsa_mem_layer(h, mem_s(1), P1, attend_sample)
    cmp_k_s, cmp_v_s, slc_k_s, slc_v_s = [rows4(x, bs) for x in rows_s[:4]]
    win_k_s, win_v_s = [x.reshape(cache_win_k.shape) for x in new_win]

    return (y_p, y_s, mem_k_p.reshape(depth, bp, mem_len, MEM_HEADS, HEAD_DIM),
            mem_v_p.reshape(depth, bp, mem_len, MEM_HEADS, HEAD_DIM),
            wkv_p[None], shift_p[None], cmp_k_p, cmp_v_p, slc_k_p, slc_v_p, win_k_p, win_v_p,
            wkv_s[None], shift_s[None], cmp_k_s, cmp_v_s, slc_k_s, slc_v_s, win_k_s, win_v_s)
```

```python
import functools

import jax
import jax.numpy as jnp
from jax import lax
from jax.experimental import pallas as pl
from jax.experimental.pallas import tpu as pltpu

f32 = jnp.float32
bf16 = jnp.bfloat16

LANES = 128
VMEM_LIMIT_BYTES = 56 * 1024 * 1024

HEAD_DIM = 128
MEM_HEADS = 4
MEM_WIDTH = MEM_HEADS * HEAD_DIM
RWKV_HEAD_DIM = 64
GN_EPS = 64e-5
NSA_KV = 2
NSA_GROUP = 6
NSA_HEADS = NSA_KV * NSA_GROUP
NSA_WIDTH = NSA_HEADS * HEAD_DIM
KV_COLS = NSA_KV * HEAD_DIM
CMP_BLOCK = 32
CMP_STRIDE = 16
SLC_BLOCK = 64
SLC_SHIFT = 6
N_SELECT = 16
WINDOW = 512
Q_BLOCK = 128
ROPE_THETA = 10000.0
NORM_EPS = 1e-6
NEG_INF = -1e30
FORCE_SCORE = 1e9
PAGE_SIZE = 128

ROW_TILE = 512
FFN_ROW_TILE = 1024
SLC_CHUNK = 512
SEL_LANES = 128
SEQ_PER_STEP = 2
MEM_SEQ_PER_STEP = 4


def _params(*sem):
    return pltpu.CompilerParams(dimension_semantics=sem, vmem_limit_bytes=VMEM_LIMIT_BYTES)


def _rms(x, g):
    return x * lax.rsqrt(jnp.mean(x * x, axis=-1, keepdims=True) + NORM_EPS) * g


def _sigmoid(x):
    return 1.0 / (1.0 + jnp.exp(-x))


def _rmsnorm_kernel(x_ref, g_ref, o_ref):
    o_ref[...] = _rms(x_ref[...], g_ref[...])


def rmsnorm(x, g):
    m, d = x.shape
    tm = min(ROW_TILE, m)
    return pl.pallas_call(
        _rmsnorm_kernel,
        grid=(m // tm,),
        in_specs=[pl.BlockSpec((tm, d), lambda i: (i, 0)), pl.BlockSpec((1, d), lambda i: (0, 0))],
        out_specs=pl.BlockSpec((tm, d), lambda i: (i, 0)),
        out_shape=jax.ShapeDtypeStruct((m, d), f32),
        compiler_params=_params("parallel"),
        name="rmsnorm",
    )(x, g.reshape(1, d))


def _mm_kernel(x_ref, g_ref, w_ref, o_ref, xn_ref, *, norm):
    @pl.when(pl.program_id(1) == 0)
    def _():
        x = x_ref[...]
        if norm:
            x = _rms(x, g_ref[...])
        xn_ref[...] = x.astype(bf16)

    o_ref[...] = jnp.dot(xn_ref[...], w_ref[...], preferred_element_type=f32)


def matmul(x, w, g=None, tn=512):
    m, k = x.shape
    n = w.shape[1]
    tm = min(FFN_ROW_TILE if m % FFN_ROW_TILE == 0 else ROW_TILE, m)
    tn = min(tn, n)
    assert m % tm == 0 and n % tn == 0, (m, n, tm, tn)
    gg = jnp.ones((1, k), f32) if g is None else g.reshape(1, k)
    return pl.pallas_call(
        functools.partial(_mm_kernel, norm=g is not None),
        grid=(m // tm, n // tn),
        in_specs=[pl.BlockSpec((tm, k), lambda i, j: (i, 0)),
                  pl.BlockSpec((1, k), lambda i, j: (0, 0)),
                  pl.BlockSpec((k, tn), lambda i, j: (0, j))],
        out_specs=pl.BlockSpec((tm, tn), lambda i, j: (i, j)),
        out_shape=jax.ShapeDtypeStruct((m, n), f32),
        scratch_shapes=[pltpu.VMEM((tm, k), bf16)],
        compiler_params=_params("parallel", "arbitrary"),
        name="matmul",
    )(x, gg, w.astype(bf16))


def _out_proj_kernel(oa_ref, ob_ref, w_ref, h_ref, g_ref, y_ref):
    ka = oa_ref.shape[1]
    acc = jnp.dot(oa_ref[...].astype(bf16), w_ref[:ka, :], preferred_element_type=f32)
    acc += jnp.dot(ob_ref[...].astype(bf16), w_ref[ka:, :], preferred_element_type=f32)
    y_ref[...] = h_ref[...] + _rms(acc, g_ref[...])


def out_proj_residual(oa, ob, ob_col, w, h, g):
    m, ka = oa.shape
    d = w.shape[1]
    kb = w.shape[0] - ka
    tm = min(ROW_TILE, m)
    return pl.pallas_call(
        _out_proj_kernel,
        grid=(m // tm,),
        in_specs=[pl.BlockSpec((tm, ka), lambda i: (i, 0)),
                  pl.BlockSpec((tm, kb), lambda i: (i, ob_col)),
                  pl.BlockSpec((ka + kb, d), lambda i: (0, 0)),
                  pl.BlockSpec((tm, d), lambda i: (i, 0)),
                  pl.BlockSpec((1, d), lambda i: (0, 0))],
        out_specs=pl.BlockSpec((tm, d), lambda i: (i, 0)),
        out_shape=jax.ShapeDtypeStruct((m, d), f32),
        compiler_params=_params("parallel"),
        name="out_proj",
    )(oa, ob, w.astype(bf16), h, g.reshape(1, d))


def _ffn_kernel(h_ref, gpre_ref, w1_ref, w2_ref, gpost_ref, y_ref, xn_ref):
    j = pl.program_id(1)

    @pl.when(j == 0)
    def _():
        xn_ref[...] = _rms(h_ref[...], gpre_ref[...]).astype(bf16)
        y_ref[...] = jnp.zeros_like(y_ref)

    u = jnp.dot(xn_ref[...], w1_ref[...], preferred_element_type=f32)
    u = jnp.square(jnp.maximum(u, 0.0))
    y_ref[...] += jnp.dot(u.astype(bf16), w2_ref[...], preferred_element_type=f32)

    @pl.when(j == pl.num_programs(1) - 1)
    def _():
        y_ref[...] = h_ref[...] + _rms(y_ref[...], gpost_ref[...])


def ffn_residual(h, g_pre, w1, w2, g_post, tf=512):
    m, d = h.shape
    dff = w1.shape[1]
    tm = min(FFN_ROW_TILE, m)
    return pl.pallas_call(
        _ffn_kernel,
        grid=(m // tm, dff // tf),
        in_specs=[pl.BlockSpec((tm, d), lambda i, j: (i, 0)),
                  pl.BlockSpec((1, d), lambda i, j: (0, 0)),
                  pl.BlockSpec((d, tf), lambda i, j: (0, j)),
                  pl.BlockSpec((tf, d), lambda i, j: (j, 0)),
                  pl.BlockSpec((1, d), lambda i, j: (0, 0))],
        out_specs=pl.BlockSpec((tm, d), lambda i, j: (i, 0)),
        out_shape=jax.ShapeDtypeStruct((m, d), f32),
        scratch_shapes=[pltpu.VMEM((tm, d), bf16)],
        compiler_params=_params("parallel", "arbitrary"),
        name="ffn",
    )(h, g_pre.reshape(1, d), w1.astype(bf16), w2.astype(bf16), g_post.reshape(1, d))


def _rope_tile(x, c2, s2):
    return x * c2 + pltpu.roll(x, HEAD_DIM // 2, 1) * s2


def _rope_heads(q, c2, s2):
    return jnp.concatenate([_rope_tile(q[:, g * HEAD_DIM:(g + 1) * HEAD_DIM], c2, s2)
                            for g in range(NSA_GROUP)], axis=0)


N_KV_BRANCH = 6


def _kv_proj_kernel(h_ref, g_ref, w_ref, c2_ref, s2_ref, *refs):
    outs, bf_ref, xn_ref = refs[:N_KV_BRANCH], refs[N_KV_BRANCH], refs[N_KV_BRANCH + 1]
    j = pl.program_id(1)

    @pl.when(j == 0)
    def _():
        xn_ref[...] = _rms(h_ref[...], g_ref[...]).astype(bf16)

    acc = jnp.dot(xn_ref[...], w_ref[...], preferred_element_type=f32)
    for br in range(N_KV_BRANCH):
        @pl.when(j == br)
        def _(br=br):
            if br in (2, 4):
                c2, s2 = c2_ref[...], s2_ref[...]
                val = jnp.concatenate([_rope_tile(acc[:, kv * HEAD_DIM:(kv + 1) * HEAD_DIM], c2, s2)
                                       for kv in range(NSA_KV)], axis=1)
            else:
                val = acc
            outs[br][...] = val
            bf_ref[...] = val.astype(bf16)


def kv_proj(h, g, w, c2, s2):
    m, d = h.shape
    n = w.shape[1]
    assert n == N_KV_BRANCH * KV_COLS
    tm = min(ROW_TILE, m)
    ntab = c2.shape[0] // tm
    res = pl.pallas_call(
        _kv_proj_kernel,
        grid=(m // tm, N_KV_BRANCH),
        in_specs=[pl.BlockSpec((tm, d), lambda i, j: (i, 0)),
                  pl.BlockSpec((1, d), lambda i, j: (0, 0)),
                  pl.BlockSpec((d, KV_COLS), lambda i, j: (0, j)),
                  pl.BlockSpec((tm, HEAD_DIM), lambda i, j: (i % ntab, 0)),
                  pl.BlockSpec((tm, HEAD_DIM), lambda i, j: (i % ntab, 0))],
        out_specs=[pl.BlockSpec((tm, KV_COLS), lambda i, j: (i, 0))] * N_KV_BRANCH
                  + [pl.BlockSpec((tm, KV_COLS), lambda i, j: (i, j))],
        out_shape=[jax.ShapeDtypeStruct((m, KV_COLS), f32)] * N_KV_BRANCH + [jax.ShapeDtypeStruct((m, n), bf16)],
        scratch_shapes=[pltpu.VMEM((tm, d), bf16)],
        compiler_params=_params("parallel", "arbitrary"),
        name="kv_proj",
    )(h, g.reshape(1, d), w.astype(bf16), c2, s2)
    return res[:N_KV_BRANCH], res[N_KV_BRANCH]


def _mem_attn_kernel(q_ref, k_ref, v_ref, o_ref, *, heads_on_rows):
    scale = HEAD_DIM ** -0.5
    n_seq = q_ref.shape[0]
    for g in range(n_seq):
        for hd in range(MEM_HEADS):
            sl = slice(hd * HEAD_DIM, (hd + 1) * HEAD_DIM)
            if heads_on_rows:
                rows = k_ref.shape[0] // n_seq
                k = k_ref[pl.ds(g * rows + hd, rows // MEM_HEADS, stride=MEM_HEADS), :]
                v = v_ref[pl.ds(g * rows + hd, rows // MEM_HEADS, stride=MEM_HEADS), :]
            else:
                k, v = k_ref[g, :, sl], v_ref[g, :, sl]
            q = (q_ref[g, :, sl] * scale).astype(bf16)
            s = lax.dot_general(q, k.astype(bf16), (((1,), (1,)), ((), ())), preferred_element_type=f32)
            e = jnp.exp(s - jnp.max(s, axis=-1, keepdims=True))
            p = e / jnp.sum(e, axis=-1, keepdims=True)
            o_ref[g, :, sl] = jnp.dot(p.astype(bf16), v.astype(bf16), preferred_element_type=f32)


def mem_attention(q, q_col, mk, mv, k_col=0, v_col=0, cached=None):
    b, t, _ = q.shape
    w = MEM_WIDTH
    tq = min(ROW_TILE, t)
    if cached is None:
        g = 1
        mlen = mk.shape[1]
        kspec = pl.BlockSpec((1, mlen, w), lambda i, j: (i, 0, k_col))
        vspec = pl.BlockSpec((1, mlen, w), lambda i, j: (i, 0, v_col))
    else:
        layer, mlen = cached
        g = MEM_SEQ_PER_STEP if b % MEM_SEQ_PER_STEP == 0 else 1
        kspec = vspec = pl.BlockSpec((g * mlen * MEM_HEADS, HEAD_DIM), lambda i, j: (layer * (b // g) + i, 0))
    return pl.pallas_call(
        functools.partial(_mem_attn_kernel, heads_on_rows=cached is not None),
        grid=(b // g, t // tq),
        in_specs=[pl.BlockSpec((g, tq, w), lambda i, j: (i, j, q_col)), kspec, vspec],
        out_specs=pl.BlockSpec((g, tq, w), lambda i, j: (i, j, 0)),
        out_shape=jax.ShapeDtypeStruct((b, t, w), f32),
        compiler_params=_params("parallel", "arbitrary"),
        name="mem_attn",
    )(q, mk, mv)


WKV_QUAD = 4
WKV_LANES = WKV_QUAD * RWKV_HEAD_DIM
WKV_GROUP = 6
WKV_TB = 64


def _block_outputs(r, w, k, b, v, s0, zt):
    n = RWKV_HEAD_DIM
    nt = r.shape[0]
    row = lax.broadcasted_iota(jnp.int32, (1, n, WKV_LANES), 1)
    lane = lax.broadcasted_iota(jnp.int32, (1, n, WKV_LANES), 2)
    head = lax.shift_right_logical(lane, 6)
    tri = jnp.where(lax.broadcasted_iota(jnp.int32, (n, n), 0) >= lax.broadcasted_iota(jnp.int32, (n, n), 1),
                    1.0, 0.0).astype(bf16)
    log_w = jnp.concatenate([jnp.log(w[i]) for i in range(nt)], axis=1)
    hi = log_w.astype(bf16)
    log_p = (jnp.dot(tri, hi, preferred_element_type=f32)
             + jnp.dot(tri, (log_w - hi.astype(f32)).astype(bf16), preferred_element_type=f32))
    log_p = jnp.stack([log_p[:, i * WKV_LANES:(i + 1) * WKV_LANES] for i in range(nt)])
    p, p_inv = jnp.exp(log_p), jnp.exp(-log_p)
    zero = jnp.zeros((), bf16)

    def stack(x):
        xb = x.astype(bf16)
        return jnp.concatenate([jnp.where(head == h4, xb, zero) for h4 in range(WKV_QUAD)], axis=1)

    bdot = lambda x, y: jnp.einsum('nil,njl->nij', x, y, preferred_element_type=f32)
    rt = (r * p).astype(bf16)
    y0 = bdot(s0.astype(bf16), stack(rt))
    causal = jnp.bitwise_and(lane, n - 1) <= row
    a_k = jnp.where(causal, bdot(rt, stack(k * p_inv)), 0.0)
    a_b = jnp.where(causal, bdot(rt, stack(b * p_inv)), 0.0)
    eye_rep = jnp.where(row == jnp.bitwise_and(lane, n - 1), 1.0, 0.0).astype(bf16)
    vt = bdot(jnp.broadcast_to(eye_rep, (nt, n, WKV_LANES)), stack(v))
    return y0 + bdot(vt.astype(bf16), stack(a_k)) - bdot(zt.astype(bf16), stack(a_b))


def _wkv_kernel(r_ref, w_ref, k_ref, kk_ref, b_ref, v_ref, s0_ref, yt_ref, st_ref,
                s_scr, lhs_scr, vd_scr, yl_scr, s0_scr, zt_scr, *, nb, nq, tb, defer_y):
    n = RWKV_HEAD_DIM
    ti = pl.program_id(1)

    @pl.when(ti == 0)
    def _():
        for ib in range(nb):
            s_scr[ib * nq * n:(ib + 1) * nq * n, :] = s0_ref[ib]

    if defer_y:
        s0_scr[...] = s_scr[...]
        zt_scr[...] = jnp.zeros(zt_scr.shape, f32)
    else:
        yt_ref[...] = jnp.zeros(yt_ref.shape, f32)
    ri = lax.broadcasted_iota(jnp.int32, (WKV_LANES, WKV_LANES), 0)
    ci = lax.broadcasted_iota(jnp.int32, (WKV_LANES, WKV_LANES), 1)
    ones_blk = jnp.where(lax.shift_right_logical(ri, 6) == lax.shift_right_logical(ci, 6), 1.0, 0.0).astype(bf16)
    ones_blk2 = jnp.concatenate([ones_blk, ones_blk], axis=0)
    eye_rep = jnp.where(lax.broadcasted_iota(jnp.int32, (n, WKV_LANES), 0)
                        == jnp.bitwise_and(lax.broadcasted_iota(jnp.int32, (n, WKV_LANES), 1), n - 1),
                        1.0, 0.0).astype(bf16)
    step_lane = jnp.bitwise_and(lax.broadcasted_iota(jnp.int32, (n, WKV_LANES), 1), n - 1)
    tiles = [(ib, q) for ib in range(nb) for q in range(nq)]
    groups = [tiles[i:i + WKV_GROUP] for i in range(0, len(tiles), WKV_GROUP)]

    def step(t, carry):
        row = lambda ref, ib, q: ref[ib, pl.ds(t, 1), q * WKV_LANES:(q + 1) * WKV_LANES]
        for gi, group in enumerate(groups):
            rows_g = slice(gi * WKV_GROUP * n, (gi * WKV_GROUP + len(group)) * n)
            for ib, q in group:
                rs = slice((ib * nq + q) * n, (ib * nq + q + 1) * n)
                prod = s_scr[rs, :] * row(kk_ref, ib, q)
                hi = prod.astype(bf16)
                lhs_scr[rs, 0:WKV_LANES] = hi
                lhs_scr[rs, WKV_LANES:2 * WKV_LANES] = (prod - hi.astype(f32)).astype(bf16)
                vd_scr[rs, :] = eye_rep * row(v_ref, ib, q).astype(bf16)
            z = jnp.dot(lhs_scr[rows_g, :], ones_blk2, preferred_element_type=f32)
            vcol = jnp.dot(vd_scr[rows_g, :], ones_blk, preferred_element_type=f32)
            for i, (ib, q) in enumerate(group):
                rs = slice((ib * nq + q) * n, (ib * nq + q + 1) * n)
                ts = slice(i * n, (i + 1) * n)
                s = s_scr[rs, :] * row(w_ref, ib, q) - z[ts] * row(b_ref, ib, q) + vcol[ts] * row(k_ref, ib, q)
                s_scr[rs, :] = s
                if defer_y:
                    zt_scr[rs, :] = jnp.where(step_lane == t, z[ts], zt_scr[rs, :])
                else:
                    yl_scr[rs, :] = (s * row(r_ref, ib, q)).astype(bf16)
            if not defer_y:
                y = jnp.dot(yl_scr[rows_g, :], ones_blk, preferred_element_type=f32)
                for i, (ib, q) in enumerate(group):
                    yt_ref[ib, 0, q * n:(q + 1) * n, :] = jnp.where(step_lane == t, y[i * n:(i + 1) * n],
                                                                     yt_ref[ib, 0, q * n:(q + 1) * n, :])
        return carry

    lax.fori_loop(0, tb, step, 0, unroll=4)

    if defer_y:
        tiled = lambda ref: jnp.stack([ref[ib, :, q * WKV_LANES:(q + 1) * WKV_LANES] for ib, q in tiles])
        as_tiles = lambda ref: ref[...].reshape(len(tiles), n, WKV_LANES)
        yt = _block_outputs(tiled(r_ref), tiled(w_ref), tiled(k_ref), tiled(b_ref), tiled(v_ref),
                            as_tiles(s0_scr), as_tiles(zt_scr))
        for i, (ib, q) in enumerate(tiles):
            yt_ref[ib, 0, q * n:(q + 1) * n, :] = yt[i]

    @pl.when(ti == pl.num_programs(1) - 1)
    def _():
        for ib in range(nb):
            st_ref[ib] = s_scr[ib * nq * n:(ib + 1) * nq * n, :]


def wkv_scan(r, w, k, kk, b, v, s0, nb=2):
    bsz, t, width = r.shape
    n = RWKV_HEAD_DIM
    nh = width // n
    nq = nh // WKV_QUAD
    tb = min(WKV_TB, t)
    nblk = t // tb
    to_tiles = lambda s: s.reshape(bsz, nq, WKV_QUAD, n, n).transpose(0, 1, 3, 2, 4).reshape(bsz, nq * n, WKV_LANES)
    row = pl.BlockSpec((nb, tb, width), lambda i, j: (i, j, 0))
    st = pl.BlockSpec((nb, nq * n, WKV_LANES), lambda i, j: (i, 0, 0))
    rows_all = nb * nq * n
    yt, s_t = pl.pallas_call(
        functools.partial(_wkv_kernel, nb=nb, nq=nq, tb=tb, defer_y=tb == n),
        grid=(bsz // nb, nblk),
        in_specs=[row, row, row, row, row, row, st],
        out_specs=[pl.BlockSpec((nb, 1, nq * n, WKV_LANES), lambda i, j: (i, j, 0, 0)), st],
        out_shape=[jax.ShapeDtypeStruct((bsz, nblk, nq * n, WKV_LANES), f32),
                   jax.ShapeDtypeStruct((bsz, nq * n, WKV_LANES), f32)],
        scratch_shapes=[pltpu.VMEM((rows_all, WKV_LANES), f32),
                        pltpu.VMEM((rows_all, 2 * WKV_LANES), bf16),
                        pltpu.VMEM((rows_all, WKV_LANES), bf16),
                        pltpu.VMEM((rows_all, WKV_LANES), bf16),
                        pltpu.VMEM((rows_all, WKV_LANES), f32),
                        pltpu.VMEM((rows_all, WKV_LANES), f32)],
        compiler_params=_params("parallel", "arbitrary"),
        name="wkv_scan",
    )(r, w, k, kk, b, v, to_tiles(s0))
    y = yt.reshape(bsz, nblk, nq, n, WKV_QUAD, n).transpose(0, 1, 5, 2, 4, 3)[:, :, :tb].reshape(bsz, t, width)
    s_t = s_t.reshape(bsz, nq, n, WKV_QUAD, n).transpose(0, 1, 3, 2, 4).reshape(bsz, nh, n, n)
    return y, s_t


def _gelu_tanh(x):
    return 0.5 * x * (1.0 + jnp.tanh(0.7978845608028654 * (x + 0.044715 * x * x * x)))


def _chunk_rows(x_ref, n_chunks, row0=0, row_stride=1):
    return jnp.concatenate(
        [x_ref[pl.ds(row0 + s * row_stride, n_chunks, stride=CMP_STRIDE * row_stride), :]
         for s in range(CMP_STRIDE)], axis=1).astype(bf16)


def _compress_rows(x2, pos_ref, w1_ref, w2_ref, n_valid, n_heads):
    rows = x2.shape[0]
    n_chunks = rows // n_heads
    pab = jnp.dot(x2, w1_ref[...], preferred_element_type=f32)
    pos = jnp.dot(pos_ref[...], w1_ref[...], preferred_element_type=f32)
    posterm = pos[0:1, :HEAD_DIM] + pos[1:2, HEAD_DIM:]
    hid = pab[:, :HEAD_DIM] + pltpu.roll(pab[:, HEAD_DIM:], rows - 1, 0) + posterm
    out = jnp.dot(_gelu_tanh(hid).astype(bf16), w2_ref[...], preferred_element_type=f32)
    n = jnp.bitwise_and(lax.broadcasted_iota(jnp.int32, out.shape, 0), n_chunks - 1)
    return jnp.where(n < n_valid, out, 0.0)


def _compress_prompt_kernel(k_ref, v_ref, posk_ref, w1k_ref, w2k_ref, posv_ref, w1v_ref, w2v_ref,
                            ok_ref, ov_ref, *, n_chunks, n_valid):
    ok_ref[0, 0] = _compress_rows(_chunk_rows(k_ref.at[0], n_chunks), posk_ref, w1k_ref, w2k_ref, n_valid, 1)
    ov_ref[0, 0] = _compress_rows(_chunk_rows(v_ref.at[0], n_chunks), posv_ref, w1v_ref, w2v_ref, n_valid, 1)


def _cmp_weights(cmp_pos, cmp_w1, cmp_w2):
    ws = []
    for i in range(2):
        half = CMP_STRIDE * HEAD_DIM
        pos = _pad_to_rows(cmp_pos[i].reshape(2, half), 8).astype(bf16)
        w1 = cmp_w1[i].reshape(2, half, HEAD_DIM)
        ws += [pos, jnp.concatenate([w1[0], w1[1]], axis=1).astype(bf16), cmp_w2[i].astype(bf16)]
    return ws


_CMP_WEIGHT_SHAPES = [(8, CMP_STRIDE * HEAD_DIM), (CMP_STRIDE * HEAD_DIM, 2 * HEAD_DIM), (HEAD_DIM, HEAD_DIM)] * 2


def compress_prompt(ck, cv, cmp_pos, cmp_w1, cmp_w2):
    b, t, _ = ck.shape
    n_chunks = t // CMP_STRIDE
    n_valid = (t - CMP_BLOCK) // CMP_STRIDE + 1
    out = jax.ShapeDtypeStruct((b, NSA_KV, n_chunks, HEAD_DIM), f32)
    ospec = pl.BlockSpec((1, 1, n_chunks, HEAD_DIM), lambda i, kv: (i, kv, 0, 0))
    wspecs = [pl.BlockSpec(s, lambda i, kv: (0, 0)) for s in _CMP_WEIGHT_SHAPES]
    return pl.pallas_call(
        functools.partial(_compress_prompt_kernel, n_chunks=n_chunks, n_valid=n_valid),
        grid=(b, NSA_KV),
        in_specs=[pl.BlockSpec((1, t, HEAD_DIM), lambda i, kv: (i, 0, kv)),
                  pl.BlockSpec((1, t, HEAD_DIM), lambda i, kv: (i, 0, kv))] + wspecs,
        out_specs=[ospec, ospec],
        out_shape=[out, out],
        compiler_params=_params("parallel", "parallel"),
        name="compress_prompt",
    )(ck, cv, *_cmp_weights(cmp_pos, cmp_w1, cmp_w2))


PAGE_ROWS = PAGE_SIZE * NSA_KV


def _page_specs(n_pages, n_seq):
    return [pl.BlockSpec((PAGE_ROWS, HEAD_DIM), lambda i, pt, g=g, p=p: (pt[i * n_seq + g, p], 0))
            for g in range(n_seq) for p in range(n_pages)]


def _compress_paged_kernel(pt_ref, *refs, n_pages, n_seq, n_valid):
    np_all = n_seq * n_pages
    k_pages, v_pages = refs[:np_all], refs[np_all:2 * np_all]
    posk_ref, w1k_ref, w2k_ref, posv_ref, w1v_ref, w2v_ref, ok_ref, ov_ref = refs[2 * np_all:]
    per_page = PAGE_SIZE // CMP_STRIDE
    n_chunks = n_pages * per_page
    heads = [(g, kv) for g in range(n_seq) for kv in range(NSA_KV)]
    chunks = lambda pages: jnp.concatenate([_chunk_rows(pg, per_page, kv, NSA_KV)
                                            for g, kv in heads for pg in pages[g * n_pages:(g + 1) * n_pages]], axis=0)
    ok = _compress_rows(chunks(k_pages), posk_ref, w1k_ref, w2k_ref, n_valid, len(heads))
    ov = _compress_rows(chunks(v_pages), posv_ref, w1v_ref, w2v_ref, n_valid, len(heads))
    for i, (g, kv) in enumerate(heads):
        ok_ref[g, kv] = ok[i * n_chunks:(i + 1) * n_chunks]
        ov_ref[g, kv] = ov[i * n_chunks:(i + 1) * n_chunks]


def compress_paged(pool_k, pool_v, page_table, n_valid, cmp_pos, cmp_w1, cmp_w2):
    b, n_pages = page_table.shape
    n_seq = SEQ_PER_STEP if b % SEQ_PER_STEP == 0 else 1
    n_chunks = n_pages * PAGE_SIZE // CMP_STRIDE
    out = jax.ShapeDtypeStruct((b, NSA_KV, n_chunks, HEAD_DIM), f32)
    ospec = pl.BlockSpec((n_seq, NSA_KV, n_chunks, HEAD_DIM), lambda i, pt: (i, 0, 0, 0))
    wspecs = [pl.BlockSpec(s, lambda i, pt: (0, 0)) for s in _CMP_WEIGHT_SHAPES]
    np_all = n_seq * n_pages
    return pl.pallas_call(
        functools.partial(_compress_paged_kernel, n_pages=n_pages, n_seq=n_seq, n_valid=n_valid),
        grid_spec=pltpu.PrefetchScalarGridSpec(
            num_scalar_prefetch=1,
            grid=(b // n_seq,),
            in_specs=_page_specs(n_pages, n_seq) * 2 + wspecs,
            out_specs=[ospec, ospec]),
        out_shape=[out, out],
        compiler_params=_params("parallel"),
        name="compress_paged",
    )(page_table, *([pool_k] * np_all), *([pool_v] * np_all), *_cmp_weights(cmp_pos, cmp_w1, cmp_w2))


def _stack_heads(x):
    return jnp.concatenate([x[:, g * HEAD_DIM:(g + 1) * HEAD_DIM] for g in range(NSA_GROUP)], axis=0)


def _dot_nt(a, b):
    return lax.dot_general(a, b, (((1,), (1,)), ((), ())), preferred_element_type=f32)


def _softmax_heads(s, bias, tq):
    s3 = s.reshape(NSA_GROUP, tq, s.shape[1]) + bias[None]
    e = jnp.exp(s3 - jnp.max(s3, axis=-1, keepdims=True))
    return e, jnp.sum(e, axis=-1, keepdims=True)


def _compressed_branch(qc, ckc, cvc, pos_t, n_cmp, tq):
    s = _dot_nt(qc, ckc)
    n = lax.broadcasted_iota(jnp.int32, (tq, s.shape[1]), 1)
    vis = (n * CMP_STRIDE + (CMP_BLOCK - 1) <= pos_t) & (n < n_cmp)
    e, denom = _softmax_heads(s, jnp.where(vis, 0.0, NEG_INF), tq)
    any_vis = jnp.where(pos_t >= CMP_BLOCK - 1, 1.0, 0.0)
    p = e / denom * any_vis[None]
    o = jnp.dot(p.reshape(s.shape).astype(bf16), cvc, preferred_element_type=f32)
    return o, jnp.sum(p, axis=0)


def _select_blocks(psum, tq, pos0, n_slc, n_j):
    if tq < SEL_LANES:
        psum = jnp.concatenate([psum, jnp.zeros((SEL_LANES - tq, psum.shape[1]), f32)], axis=0)
    n_c = psum.shape[1]
    j = lax.broadcasted_iota(jnp.int32, (n_j, n_c), 0)
    cs = lax.broadcasted_iota(jnp.int32, (n_j, n_c), 1) * CMP_STRIDE
    overlap = jnp.where((cs < j * SLC_BLOCK + SLC_BLOCK) & (cs + (CMP_BLOCK - 1) >= j * SLC_BLOCK), 1.0, 0.0)
    imp_t = lax.dot_general(overlap, psum, (((1,), (1,)), ((), ())),
                            preferred_element_type=f32, precision=lax.Precision.HIGHEST)
    j = lax.broadcasted_iota(jnp.int32, imp_t.shape, 0)
    pos_t = pos0 + lax.broadcasted_iota(jnp.int32, imp_t.shape, 1)
    cur = lax.shift_right_logical(pos_t, SLC_SHIFT)
    causal = j * SLC_BLOCK <= pos_t
    forced = (j == 0) | (j == cur) | (j == cur - 1)
    score = jnp.where(causal, jnp.where(forced, FORCE_SCORE, imp_t), -FORCE_SCORE)
    score = jnp.where(j < n_slc, score, -2.0 * FORCE_SCORE)
    rank = jnp.zeros(imp_t.shape, f32)
    for jp in range(n_slc):
        row = score[jp:jp + 1, :]
        ahead = (row > score) | ((row == score) & (j > jp))
        rank = rank + jnp.where(ahead, 1.0, 0.0)
    sel_t = jnp.where(rank < min(N_SELECT, n_slc), 1.0, 0.0)
    return sel_t.T[0:tq]


def _selection_bias(sel, key0, n_keys):
    nj = sel.shape[1]
    j = lax.broadcasted_iota(jnp.int32, (nj, n_keys), 0)
    kpos = key0 + lax.broadcasted_iota(jnp.int32, (nj, n_keys), 1)
    e = jnp.where(lax.shift_right_logical(kpos, SLC_SHIFT) == j, 1.0, 0.0).astype(bf16)
    return jnp.dot(jnp.where(sel > 0.5, 0.0, NEG_INF).astype(bf16), e, preferred_element_type=f32)


def _window_branch(qr, wk, wv, kpos0, n_keys_valid, pos_t, tq, n_phantom=None):
    s = _dot_nt(qr, wk)
    lane = lax.broadcasted_iota(jnp.int32, (tq, s.shape[1]), 1)
    kpos = kpos0 + lane
    valid = (kpos <= pos_t) & (pos_t - kpos < WINDOW) & (lane < n_keys_valid)
    s3 = s.reshape(NSA_GROUP, tq, s.shape[1]) + jnp.where(valid, 0.0, NEG_INF)[None]
    m = jnp.max(s3, axis=-1, keepdims=True)
    if n_phantom is not None:
        m = jnp.where(n_phantom[None] > 0.0, jnp.maximum(m, 0.0), m)
    e = jnp.exp(s3 - m)
    denom = jnp.sum(e, axis=-1, keepdims=True)
    if n_phantom is not None:
        denom = denom + n_phantom[None] * jnp.exp(-m)
    return jnp.dot((e / denom).reshape(s.shape).astype(bf16), wv, preferred_element_type=f32)


def _gated_sum(gate, cols, tq, o_cmp, o_slc, o_win, g):
    r = slice(g * tq, (g + 1) * tq)
    c, s, w = cols[0] + g, cols[1] + g, cols[2] + g
    return gate[:, c:c + 1] * o_cmp[r] + gate[:, s:s + 1] * o_slc[r] + gate[:, w:w + 1] * o_win[r]


QK_SCALE = HEAD_DIM ** -0.5


def _nsa_prompt_kernel(q_ref, c2_ref, s2_ref, gate_ref, ckc_ref, cvc_ref, sk_ref, sv_ref, wk_ref, wv_ref,
                       o_ref, *, n_cmp, n_slc):
    i = pl.program_id(2)
    tq = Q_BLOCK
    rows = NSA_GROUP * tq
    q0 = i * tq
    q = q_ref[0] * QK_SCALE
    qc = _stack_heads(q).astype(bf16)
    qr = _rope_heads(q, c2_ref[...], s2_ref[...]).astype(bf16)
    pos_t = q0 + lax.broadcasted_iota(jnp.int32, (tq, 1), 0)

    o_cmp, psum = _compressed_branch(qc, ckc_ref[0, 0].astype(bf16), cvc_ref[0, 0].astype(bf16), pos_t, n_cmp, tq)
    sel = _select_blocks(psum, tq, q0, n_slc, n_slc)

    def slc_step(c, carry, causal):
        m, l, acc = carry
        k0 = pl.multiple_of(c * SLC_CHUNK, SLC_CHUNK)
        bias = _selection_bias(sel, k0, SLC_CHUNK)
        if causal:
            kpos = k0 + lax.broadcasted_iota(jnp.int32, bias.shape, 1)
            bias = jnp.where(kpos <= pos_t, bias, NEG_INF)
        s3 = _dot_nt(qr, sk_ref[0, pl.ds(k0, SLC_CHUNK), :]).reshape(NSA_GROUP, tq, SLC_CHUNK) + bias[None]
        m_new = jnp.maximum(m, jnp.max(s3, axis=-1, keepdims=True))
        alpha = jnp.exp(m - m_new)
        e = jnp.exp(s3 - m_new)
        l = alpha * l + jnp.sum(e, axis=-1, keepdims=True)
        pv = jnp.dot(e.reshape(rows, SLC_CHUNK).astype(bf16), sv_ref[0, pl.ds(k0, SLC_CHUNK), :],
                     preferred_element_type=f32)
        return m_new, l, alpha * acc + pv.reshape(NSA_GROUP, tq, HEAD_DIM)

    c_last = q0 // SLC_CHUNK
    init = (jnp.full((NSA_GROUP, tq, 1), NEG_INF, f32), jnp.zeros((NSA_GROUP, tq, 1), f32),
            jnp.zeros((NSA_GROUP, tq, HEAD_DIM), f32))
    carry = lax.fori_loop(0, c_last, functools.partial(slc_step, causal=False), init)
    _, l, acc = slc_step(c_last, carry, causal=True)
    o_slc = (acc / l).reshape(rows, HEAD_DIM)

    span = WINDOW + tq
    w0 = pl.multiple_of(jnp.maximum(q0 - WINDOW, 0), tq)
    n_phantom = jnp.maximum(WINDOW - 1 - pos_t, 0).astype(f32)
    o_win = _window_branch(qr, wk_ref[0, pl.ds(w0, span), :], wv_ref[0, pl.ds(w0, span), :], w0, span, pos_t, tq,
                           n_phantom)

    gate = _sigmoid(gate_ref[0])
    first_kv = pl.program_id(1) == 0
    for g in range(NSA_GROUP):
        head = [_gated_sum(gate, tuple(br * NSA_HEADS + kv * NSA_GROUP for br in range(3)), tq, o_cmp, o_slc, o_win, g)
                for kv in range(NSA_KV)]
        o_ref[0, :, g * HEAD_DIM:(g + 1) * HEAD_DIM] = jnp.where(first_kv, head[0], head[1])


def nsa_prompt(proj, gate_col, ckc, cvc, kv_bf, c2, s2, n_cmp):
    b, t, _ = proj.shape
    assert t % SLC_CHUNK == 0 and t >= WINDOW + Q_BLOCK
    n_slc = t // SLC_BLOCK
    gw = NSA_GROUP * HEAD_DIM
    kvcol = lambda c: pl.BlockSpec((1, t, HEAD_DIM), lambda bi, kv, i, c=c: (bi, 0, c * NSA_KV + kv))
    cmp_spec = pl.BlockSpec((1, 1, ckc.shape[2], HEAD_DIM), lambda bi, kv, i: (bi, kv, 0, 0))
    return pl.pallas_call(
        functools.partial(_nsa_prompt_kernel, n_cmp=n_cmp, n_slc=n_slc),
        grid=(b, NSA_KV, t // Q_BLOCK),
        in_specs=[pl.BlockSpec((1, Q_BLOCK, gw), lambda bi, kv, i: (bi, i, kv)),
                  pl.BlockSpec((Q_BLOCK, HEAD_DIM), lambda bi, kv, i: (i, 0)),
                  pl.BlockSpec((Q_BLOCK, HEAD_DIM), lambda bi, kv, i: (i, 0)),
                  pl.BlockSpec((1, Q_BLOCK, LANES), lambda bi, kv, i: (bi, i, gate_col)),
                  cmp_spec, cmp_spec, kvcol(2), kvcol(3), kvcol(4), kvcol(5)],
        out_specs=pl.BlockSpec((1, Q_BLOCK, gw), lambda bi, kv, i: (bi, i, kv)),
        out_shape=jax.ShapeDtypeStruct((b, t, NSA_WIDTH), f32),
        compiler_params=_params("parallel", "parallel", "arbitrary"),
        name="nsa_prompt",
    )(proj, c2, s2, proj, ckc, cvc, kv_bf, kv_bf, kv_bf, kv_bf)


def _pad_rows(x, n):
    return jnp.concatenate([x, jnp.zeros((n - x.shape[0], x.shape[1]), x.dtype)], axis=0)


def _nsa_sample_kernel(pt_ref, *refs, n_pages, n_seq, n_cmp, n_slc, n_j, past, tq):
    np_all = n_seq * n_pages
    k_pages, v_pages = refs[:np_all], refs[np_all:2 * np_all]
    (q_ref, c2_ref, s2_ref, gate_ref, ckc_ref, cvc_ref, nsk_ref, nsv_ref, nwk_ref, nwv_ref,
     wink_ref, winv_ref, o_ref, owk_ref, owv_ref) = refs[2 * np_all:]
    pos_t = past + lax.broadcasted_iota(jnp.int32, (tq, 1), 0)
    lw2 = wink_ref.shape[0] // n_seq
    lw = lw2 // NSA_KV
    c2, s2 = c2_ref[...], s2_ref[...]
    for g in range(n_seq):
        gate = _sigmoid(gate_ref[g])
        keep = lw2 - tq * NSA_KV
        for cache_ref, new_ref, out_ref in ((wink_ref, nwk_ref, owk_ref), (winv_ref, nwv_ref, owv_ref)):
            out_ref[g * lw2:g * lw2 + keep, :] = cache_ref[g * lw2 + tq * NSA_KV:(g + 1) * lw2, :]
            for kv in range(NSA_KV):
                out_ref[pl.ds(g * lw2 + keep + kv, tq, stride=NSA_KV), :] = new_ref[g, :, kv * HEAD_DIM:(kv + 1) * HEAD_DIM]
        for kv in range(NSA_KV):
            ksl = slice(kv * HEAD_DIM, (kv + 1) * HEAD_DIM)
            q = q_ref[g, :, kv * NSA_GROUP * HEAD_DIM:(kv + 1) * NSA_GROUP * HEAD_DIM] * QK_SCALE
            qc = _stack_heads(q).astype(bf16)
            qr = _rope_heads(q, c2, s2).astype(bf16)
            o_cmp, psum = _compressed_branch(qc, ckc_ref[g, kv].astype(bf16), cvc_ref[g, kv].astype(bf16), pos_t,
                                             n_cmp, tq)
            sel = _select_blocks(psum, tq, past, n_slc, n_j)

            paged = lambda pages: [pg[pl.ds(kv, PAGE_SIZE, stride=NSA_KV), :] for pg in pages[g * n_pages:(g + 1) * n_pages]]
            sk = jnp.concatenate(paged(k_pages) + [_pad_rows(nsk_ref[g, :, ksl], LANES)], axis=0).astype(bf16)
            sv = jnp.concatenate(paged(v_pages) + [_pad_rows(nsv_ref[g, :, ksl], LANES)], axis=0).astype(bf16)
            n_keys = sk.shape[0]
            bias = _selection_bias(sel, 0, n_keys)
            bias = jnp.where(lax.broadcasted_iota(jnp.int32, bias.shape, 1) <= pos_t, bias, NEG_INF)
            e, denom = _softmax_heads(_dot_nt(qr, sk), bias, tq)
            o_slc = jnp.dot((e / denom).reshape(NSA_GROUP * tq, n_keys).astype(bf16), sv, preferred_element_type=f32)

            cached = lambda ref: ref[pl.ds(g * lw2 + kv, lw, stride=NSA_KV), :]
            wk = jnp.concatenate([cached(wink_ref), _pad_rows(nwk_ref[g, :, ksl], LANES)], axis=0).astype(bf16)
            wv = jnp.concatenate([cached(winv_ref), _pad_rows(nwv_ref[g, :, ksl], LANES)], axis=0).astype(bf16)
            o_win = _window_branch(qr, wk, wv, past - lw, lw + tq, pos_t, tq)

            cols = tuple(br * NSA_HEADS + kv * NSA_GROUP for br in range(3))
            for hg in range(NSA_GROUP):
                hd = kv * NSA_GROUP + hg
                o_ref[g, :, hd * HEAD_DIM:(hd + 1) * HEAD_DIM] = _gated_sum(gate, cols, tq, o_cmp, o_slc, o_win, hg)


def nsa_sample(proj, gate_col, ckc, cvc, pool_k, pool_v, page_table, new_rows, win_k, win_v, c2, s2, n_cmp, past):
    b, tq, _ = proj.shape
    n_pages = page_table.shape[1]
    assert past == n_pages * PAGE_SIZE and past % SLC_BLOCK == 0 and tq <= SLC_BLOCK and (tq * NSA_KV) % 8 == 0
    n_seq = SEQ_PER_STEP if b % SEQ_PER_STEP == 0 else 1
    n_slc = past // SLC_BLOCK + 1
    n_j = -(-n_slc // SLC_BLOCK) * SLC_BLOCK
    lw2 = win_k.shape[0] // b
    per_b = lambda shape: pl.BlockSpec((n_seq,) + shape, lambda i, pt: (i,) + (0,) * len(shape))
    tab = pl.BlockSpec((tq, HEAD_DIM), lambda i, pt: (0, 0))
    win = pl.BlockSpec((n_seq * lw2, HEAD_DIM), lambda i, pt: (i, 0))
    np_all = n_seq * n_pages
    return pl.pallas_call(
        functools.partial(_nsa_sample_kernel, n_pages=n_pages, n_seq=n_seq, n_cmp=n_cmp, n_slc=n_slc, n_j=n_j,
                          past=past, tq=tq),
        grid_spec=pltpu.PrefetchScalarGridSpec(
            num_scalar_prefetch=1,
            grid=(b // n_seq,),
            in_specs=_page_specs(n_pages, n_seq) * 2 + [
                per_b((tq, NSA_WIDTH)), tab, tab,
                pl.BlockSpec((n_seq, tq, LANES), lambda i, pt: (i, 0, gate_col)),
                per_b((NSA_KV, ckc.shape[2], HEAD_DIM)), per_b((NSA_KV, ckc.shape[2], HEAD_DIM))]
                + [per_b((tq, KV_COLS))] * 4 + [win, win],
            out_specs=[per_b((tq, NSA_WIDTH)), win, win]),
        out_shape=[jax.ShapeDtypeStruct((b, tq, NSA_WIDTH), f32),
                   jax.ShapeDtypeStruct(win_k.shape, f32), jax.ShapeDtypeStruct(win_v.shape, f32)],
        compiler_params=_params("parallel"),
        name="nsa_sample",
    )(page_table, *([pool_k] * np_all), *([pool_v] * np_all), proj, c2, s2, proj, ckc, cvc, *new_rows, win_k, win_v)


def _prev_rows(x, tile, halo_ref, first_ref, seq_len, halo_fn=lambda rows: rows):
    tm, c = x.shape
    prev = pltpu.roll(x, 1, 0)
    row = lax.broadcasted_iota(jnp.int32, (tm, 1), 0)
    if seq_len >= tm:
        tiles_per_seq = seq_len // tm
        first = first_ref[pl.ds(tile // tiles_per_seq, 1), :]
        edge = jnp.where(tile % tiles_per_seq == 0, first, halo_fn(halo_ref[...])[7:8, :])
        return jnp.where(row == 0, edge, prev)
    pieces = []
    for j in range(tm // seq_len):
        pieces.append(jnp.broadcast_to(first_ref[j:j + 1, :], (8, c)))
        if seq_len > 8:
            pieces.append(jnp.zeros((seq_len - 8, c), f32))
    return jnp.where(jnp.bitwise_and(row, seq_len - 1) == 0, jnp.concatenate(pieces, axis=0), prev)


def _shift_specs(m, c, tm, seq_len, n_seq, col=None):
    cb = (lambda *g: 0) if col is None else col
    tile = pl.BlockSpec((tm, c), lambda *g: (g[0], cb(*g)))
    halo = pl.BlockSpec((8, c), lambda *g: (jnp.maximum(g[0] * (tm // 8) - 1, 0), cb(*g)))
    if seq_len >= tm:
        first = pl.BlockSpec((-(-n_seq // 8) * 8, c), lambda *g: (0, cb(*g)))
    else:
        first = pl.BlockSpec((tm // seq_len, c), lambda *g: (g[0], cb(*g)))
    return tile, halo, first


def _pad_first(first, seq_len, tm):
    return _pad_to_rows(first, -(-first.shape[0] // 8) * 8) if seq_len >= tm else first


def _head_sums(x):
    ri = lax.broadcasted_iota(jnp.int32, (WKV_LANES, WKV_LANES), 0)
    ci = lax.broadcasted_iota(jnp.int32, (WKV_LANES, WKV_LANES), 1)
    ones_blk = jnp.where(lax.shift_right_logical(ri, 6) == lax.shift_right_logical(ci, 6), 1.0, 0.0).astype(bf16)
    hi = x.astype(bf16)
    lo = (x - hi.astype(f32)).astype(bf16)
    out = []
    for c in range(x.shape[1] // WKV_LANES):
        sl = slice(c * WKV_LANES, (c + 1) * WKV_LANES)
        out.append(jnp.dot(hi[:, sl], ones_blk, preferred_element_type=f32)
                   + jnp.dot(lo[:, sl], ones_blk, preferred_element_type=f32))
    return jnp.concatenate(out, axis=1)


def _lora_kernel(h_ref, halo_ref, first_ref, gn_ref, mu_ref, wd1_ref, wa1_ref, wg1_ref, wd2_ref, wa2_ref, wg2_ref,
                 w0_ref, a0_ref, decay_ref, a_ref, g_ref, *, seq_len):
    norm = lambda rows: _rms(rows, gn_ref[...])
    hn = norm(h_ref[...])
    xx = _prev_rows(hn, pl.program_id(0), halo_ref, first_ref, seq_len, norm) - hn
    mix = lambda r: (hn + xx * mu_ref[r:r + 1, :]).astype(bf16)
    dot = lambda x, w_ref: jnp.dot(x, w_ref[...], preferred_element_type=f32)
    w_raw = w0_ref[...] + dot(jnp.tanh(dot(mix(0), wd1_ref)).astype(bf16), wd2_ref)
    softplus = jnp.maximum(-w_raw, 0.0) + jnp.log(1.0 + jnp.exp(-jnp.abs(w_raw)))
    decay_ref[...] = jnp.exp(-jnp.exp(-softplus - 0.5))
    a_ref[...] = _sigmoid(a0_ref[...] + dot(dot(mix(1), wa1_ref).astype(bf16), wa2_ref))
    g_ref[...] = dot(_sigmoid(dot(mix(2), wg1_ref)).astype(bf16), wg2_ref)


def rwkv_lora(h, x_prev, seq_len, P):
    m, d = h.shape
    rw = P['w0'].shape[0]
    tm = min(ROW_TILE, m)
    pad128 = lambda w: _pad_cols(w, -(-w.shape[1] // LANES) * LANES).astype(bf16)
    w1s = [pad128(P[k]) for k in ('w_decay1', 'w_aaa1', 'w_gate1')]
    w2s = [_pad_to_rows(P[k], w1.shape[1]).astype(bf16) for k, w1 in zip(('w_decay2', 'w_aaa2', 'w_gate2'), w1s)]
    full = lambda x: pl.BlockSpec(x.shape, lambda i: (0, 0))
    mu = _pad_to_rows(P['mu_wag'], 8)
    gn = P['g_mix_pre'].reshape(1, d)
    vecs = [P['w0'].reshape(1, rw), P['a0'].reshape(1, rw)]
    out = jax.ShapeDtypeStruct((m, rw), f32)
    ospec = pl.BlockSpec((tm, rw), lambda i: (i, 0))
    return pl.pallas_call(
        functools.partial(_lora_kernel, seq_len=seq_len),
        grid=(m // tm,),
        in_specs=list(_shift_specs(m, d, tm, seq_len, x_prev.shape[0])) + [full(gn), full(mu)]
                 + [full(w) for w in w1s + w2s + vecs],
        out_specs=[ospec] * 3,
        out_shape=[out] * 3,
        compiler_params=_params("parallel"),
        name="rwkv_lora",
    )(h, h, _pad_first(x_prev, seq_len, tm), gn, mu, *w1s, *w2s, *vecs)


RKV_GROUPS = 3
IN_COL_TILE = 512


def _rwkv_in_kernel(h_ref, halo_ref, first_ref, g_ref, w_ref, mu_ref, a_ref, kkw_ref, kaw_ref,
                    r_ref, k_ref, kk_ref, b_ref, v_ref, mq_ref, xn_scr, edge_scr, *, seq_len, per_group):
    i, j = pl.program_id(0), pl.program_id(1)
    tm, c = r_ref.shape

    @pl.when(j == 0)
    def _():
        xn_scr[...] = _rms(h_ref[...], g_ref[...]).astype(bf16)
        edge_scr[0:8, :] = _rms(halo_ref[...], g_ref[...])
        edge_scr[8:, :] = first_ref[...]

    cur = jnp.dot(xn_scr[...], w_ref[...], preferred_element_type=f32)
    edge = jnp.dot(edge_scr[...].astype(bf16), w_ref[...], preferred_element_type=f32)
    halo, first = edge[0:8], edge[8:]
    prev = pltpu.roll(cur, 1, 0)
    row = lax.broadcasted_iota(jnp.int32, (tm, 1), 0)
    if seq_len >= tm:
        tiles_per_seq = seq_len // tm
        seq = lax.broadcasted_iota(jnp.int32, (first.shape[0], 1), 0) == i // tiles_per_seq
        first_row = jnp.sum(jnp.where(seq, first, 0.0), axis=0, keepdims=True)
        edge_row = jnp.where(i % tiles_per_seq == 0, first_row, halo[7:8])
        prev = jnp.where(row == 0, edge_row, prev)
    else:
        pieces = []
        for q in range(tm // seq_len):
            pieces.append(jnp.broadcast_to(first[q:q + 1], (8, c)))
            if seq_len > 8:
                pieces.append(jnp.zeros((seq_len - 8, c), f32))
        prev = jnp.where(jnp.bitwise_and(row, seq_len - 1) == 0, jnp.concatenate(pieces, axis=0), prev)
    x = cur + (prev - cur) * mu_ref[...]

    @pl.when(j < per_group)
    def _():
        r_ref[...] = x

    @pl.when((j >= per_group) & (j < 2 * per_group))
    def _():
        a = a_ref[...]
        kk = x * kkw_ref[...]
        kk = kk * lax.rsqrt(_head_sums(kk * kk) + 1e-12)
        kk_ref[...] = kk
        b_ref[...] = kk * a
        k_ref[...] = x * (1.0 + (a - 1.0) * kaw_ref[...])

    @pl.when((j >= 2 * per_group) & (j < 3 * per_group))
    def _():
        v_ref[...] = x

    @pl.when(j >= 3 * per_group)
    def _():
        mq_ref[...] = cur


def rwkv_in(h, x_prev, a_rate, seq_len, P):
    m, d = h.shape
    rw = a_rate.shape[1]
    c = IN_COL_TILE
    tm = min(ROW_TILE, m)
    per_group = rw // c
    n_col = RKV_GROUPS * per_group + MEM_WIDTH // c
    assert rw % c == 0 and MEM_WIDTH == c and P['w_in_a'].shape[1] == n_col * c
    tile, halo, first = _shift_specs(m, d, tm, seq_len, x_prev.shape[0])
    n_first = first.block_shape[0]
    mu = jnp.pad(P['mu_rkv'], (0, MEM_WIDTH)).reshape(1, -1)
    grp = lambda base: (lambda i, j: (i, jnp.clip(j - base * per_group, 0, per_group - 1)))
    vec = lambda base: (lambda i, j: (0, jnp.clip(j - base * per_group, 0, per_group - 1)))
    wide = jax.ShapeDtypeStruct((m, rw), f32)
    return pl.pallas_call(
        functools.partial(_rwkv_in_kernel, seq_len=seq_len, per_group=per_group),
        grid=(m // tm, n_col),
        in_specs=[pl.BlockSpec((tm, d), lambda i, j: (i, 0)),
                  pl.BlockSpec((8, d), lambda i, j: (jnp.maximum(i * (tm // 8) - 1, 0), 0)),
                  pl.BlockSpec(first.block_shape, lambda i, j: (i if seq_len < tm else 0, 0)),
                  pl.BlockSpec((1, d), lambda i, j: (0, 0)),
                  pl.BlockSpec((d, c), lambda i, j: (0, j)),
                  pl.BlockSpec((1, c), lambda i, j: (0, j)),
                  pl.BlockSpec((tm, c), grp(1)),
                  pl.BlockSpec((1, c), vec(1)),
                  pl.BlockSpec((1, c), vec(1))],
        out_specs=[pl.BlockSpec((tm, c), grp(0)), pl.BlockSpec((tm, c), grp(1)), pl.BlockSpec((tm, c), grp(1)),
                   pl.BlockSpec((tm, c), grp(1)), pl.BlockSpec((tm, c), grp(2)),
                   pl.BlockSpec((tm, c), lambda i, j: (i, 0))],
        out_shape=[wide] * 5 + [jax.ShapeDtypeStruct((m, MEM_WIDTH), f32)],
        scratch_shapes=[pltpu.VMEM((tm, d), bf16), pltpu.VMEM((8 + n_first, d), f32)],
        compiler_params=_params("parallel", "arbitrary"),
        name="rwkv_in",
    )(h, h, _pad_first(x_prev, seq_len, tm), P['g_mix_pre'].reshape(1, d), P['w_in_a'].astype(bf16), mu, a_rate,
      P['k_k'].reshape(1, rw), P['k_a'].reshape(1, rw))


def _rwkv_out_kernel(y_ref, r_ref, k_ref, v_ref, g_ref, om_ref, h_ref, lw_ref, lb_ref, rk_ref, w_ref, gp_ref, o_ref):
    inv_n = 1.0 / RWKV_HEAD_DIM
    y = y_ref[...]
    d = y - _head_sums(y) * inv_n
    var = _head_sums(d * d) * inv_n
    yn = d * lax.rsqrt(var + GN_EPS) * lw_ref[...] + lb_ref[...]
    bonus = _head_sums(r_ref[...] * k_ref[...] * rk_ref[...]) * v_ref[...]
    o = ((yn + bonus) * g_ref[...]).astype(bf16)
    rw = o.shape[1]
    acc = jnp.dot(o, w_ref[:rw, :], preferred_element_type=f32)
    acc += jnp.dot(om_ref[...].astype(bf16), w_ref[rw:, :], preferred_element_type=f32)
    o_ref[...] = h_ref[...] + _rms(acc, gp_ref[...])


def rwkv_out(y, r, k, v, gate, o_mem, h, P):
    m, rw = y.shape
    d = h.shape[1]
    tm = min(ROW_TILE // 2, m)
    wide = pl.BlockSpec((tm, rw), lambda i: (i, 0))
    vec = pl.BlockSpec((1, rw), lambda i: (0, 0))
    return pl.pallas_call(
        _rwkv_out_kernel,
        grid=(m // tm,),
        in_specs=[wide] * 5 + [pl.BlockSpec((tm, o_mem.shape[1]), lambda i: (i, 0)),
                               pl.BlockSpec((tm, d), lambda i: (i, 0)), vec, vec, vec,
                               pl.BlockSpec((rw + o_mem.shape[1], d), lambda i: (0, 0)),
                               pl.BlockSpec((1, d), lambda i: (0, 0))],
        out_specs=pl.BlockSpec((tm, d), lambda i: (i, 0)),
        out_shape=jax.ShapeDtypeStruct((m, d), f32),
        compiler_params=_params("parallel"),
        name="rwkv_out",
    )(y, r, k, v, gate, o_mem, h, P['lnx_w'].reshape(1, rw), P['lnx_b'].reshape(1, rw), P['r_k'].reshape(1, rw),
      P['w_out_a'].astype(bf16), P['g_mix_post'].reshape(1, d))


def _rope_tables(pos):
    half = HEAD_DIM // 2
    inv = jnp.power(ROPE_THETA, -jnp.arange(half, dtype=f32) / half)
    ang = pos.astype(f32)[:, None] * inv[None, :]
    cos, sin = jnp.cos(ang), jnp.sin(ang)
    return jnp.concatenate([cos, cos], axis=-1), jnp.concatenate([-sin, sin], axis=-1)


def _pad_cols(w, n):
    return jnp.pad(w, ((0, 0), (0, n - w.shape[1])))


def _pad_to_rows(x, n):
    return jnp.pad(x, ((0, n - x.shape[0]), (0, 0)))


def rwkv_mem_layer(h, x_prev, s0, mem, P):
    b, t, d = h.shape
    m = b * t
    assert t & (t - 1) == 0 and t % 8 == 0 and (t % ROW_TILE == 0 or ROW_TILE % t == 0)
    h2 = h.reshape(m, d)
    rw = P['w0'].shape[0]
    last = rmsnorm(_pad_to_rows(h[:, -1], -(-b // 8) * 8), P['g_mix_pre'])[:b]

    decay, a_rate, gate = rwkv_lora(h2, x_prev, t, P)
    r, k, kk, kb, v, mq = rwkv_in(h2, x_prev, a_rate, t, P)
    as3 = lambda x: x.reshape(b, t, -1)
    y, s_t = wkv_scan(as3(r), as3(decay), as3(k), as3(kk), as3(kb), as3(v), s0)
    o_mem = mem(as3(mq), 0)
    h2 = rwkv_out(y.reshape(m, rw), r, k, v, gate, o_mem.reshape(m, MEM_WIDTH), h2, P)
    h2 = ffn_residual(h2, P['g_ffn_pre'], P['w_ff1'], P['w_ff2'], P['g_ffn_post'])
    return h2.reshape(b, t, d), s_t, last


GATE_COL = (NSA_WIDTH + MEM_WIDTH) // LANES


def nsa_mem_layer(h, mem, P, attend):
    b, t, d = h.shape
    m = b * t
    h2 = h.reshape(m, d)
    n_in = (GATE_COL + 1) * LANES
    n_in = -(-n_in // 768) * 768
    proj = matmul(h2, _pad_cols(P['w_in_b'].astype(bf16), n_in), g=P['g_mix_pre'], tn=768).reshape(b, t, -1)
    o_nsa = attend(proj)
    o_mem = mem(proj, NSA_WIDTH // MEM_WIDTH)
    h2 = out_proj_residual(o_nsa.reshape(m, NSA_WIDTH), o_mem.reshape(m, MEM_WIDTH), 0, P['w_out_b'], h2,
                           P['g_mix_post'])
    h2 = ffn_residual(h2, P['g_ffn_pre'], P['w_ff1'], P['w_ff2'], P['g_ffn_post'])
    return h2.reshape(b, t, d)


def kernel(x_prompt, x_sample, mem_prompt, cache_mem_k, cache_mem_v, state_wkv, state_shift, cache_cmp_k, cache_cmp_v, cache_slc_k, cache_slc_v, cache_win_k, cache_win_v, page_table, g_mix_pre, g_mix_post, g_ffn_pre, g_ffn_post, g_mem, w_mem_k, w_mem_v, w_in_a, mu_rkv, mu_wag, w0, w_decay1, w_decay2, a0, w_aaa1, w_aaa2, w_gate1, w_gate2, k_k, k_a, r_k, lnx_w, lnx_b, w_out_a, g_kv, w_kv, cmp_pos, cmp_w1, cmp_w2, w_in_b, w_out_b, w_ff1, w_ff2):
    bp, tp, d = x_prompt.shape
    bs, ts, _ = x_sample.shape
    depth = g_mix_pre.shape[0]
    assert depth == 2 and w_in_a.shape[0] == 1 and w_in_b.shape[0] == 1
    n_pages = page_table.shape[1]
    past = n_pages * PAGE_SIZE
    mem_len = mem_prompt.shape[1]

    P0 = dict(g_mix_pre=g_mix_pre[0], g_mix_post=g_mix_post[0], g_ffn_pre=g_ffn_pre[0], g_ffn_post=g_ffn_post[0],
              w_in_a=w_in_a[0], mu_rkv=mu_rkv[0], mu_wag=mu_wag[0], w0=w0[0], w_decay1=w_decay1[0],
              w_decay2=w_decay2[0], a0=a0[0], w_aaa1=w_aaa1[0], w_aaa2=w_aaa2[0], w_gate1=w_gate1[0],
              w_gate2=w_gate2[0], k_k=k_k[0], k_a=k_a[0], r_k=r_k[0], lnx_w=lnx_w[0], lnx_b=lnx_b[0],
              w_out_a=w_out_a[0], w_ff1=w_ff1[0], w_ff2=w_ff2[0])
    P1 = dict(g_mix_pre=g_mix_pre[1], g_mix_post=g_mix_post[1], g_ffn_pre=g_ffn_pre[1], g_ffn_post=g_ffn_post[1],
              w_in_b=w_in_b[0], w_out_b=w_out_b[0], w_ff1=w_ff1[1], w_ff2=w_ff2[1])
    rows4 = lambda x, bsz: x.reshape(bsz, -1, NSA_KV, HEAD_DIM)

    mem2 = mem_prompt.reshape(bp * mem_len, d)
    mkv = [matmul(mem2, jnp.concatenate([w_mem_k[l], w_mem_v[l]], axis=1), g=g_mem[l]).reshape(bp, mem_len, -1)
           for l in range(depth)]
    mem_k_p = jnp.stack([x[..., :MEM_WIDTH] for x in mkv])
    mem_v_p = jnp.stack([x[..., MEM_WIDTH:] for x in mkv])
    mem_p = lambda l: (lambda q, q_col: mem_attention(q, q_col, mkv[l], mkv[l], k_col=0, v_col=1))

    nh = w0.shape[1] // RWKV_HEAD_DIM
    shift0 = jnp.zeros((bp, d), f32)
    wkv0 = jnp.zeros((bp, nh, RWKV_HEAD_DIM, RWKV_HEAD_DIM), f32)
    h, wkv_p, shift_p = rwkv_mem_layer(x_prompt, shift0, wkv0, mem_p(0), P0)

    c2p, s2p = _rope_tables(jnp.arange(tp, dtype=jnp.int32))
    rows_p, kv_bf = kv_proj(h.reshape(bp * tp, d), g_kv, w_kv, c2p, s2p)
    as_p = lambda x: x.reshape(bp, tp, -1)
    ckc, cvc = compress_prompt(as_p(rows_p[0]), as_p(rows_p[1]), cmp_pos, cmp_w1, cmp_w2)
    n_cmp_p = (tp - CMP_BLOCK) // CMP_STRIDE + 1

    def attend_prompt(proj):
        return nsa_prompt(proj, GATE_COL, ckc, cvc, as_p(kv_bf), c2p, s2p, n_cmp_p)

    y_p = nsa_mem_layer(h, mem_p(1), P1, attend_prompt)
    cmp_k_p, cmp_v_p, slc_k_p, slc_v_p, win_k_p, win_v_p = [rows4(x, bp) for x in rows_p]
    n_keep = min(WINDOW, tp)
    win_k_p, win_v_p = win_k_p[:, tp - n_keep:], win_v_p[:, tp - n_keep:]

    mk_s, mv_s = cache_mem_k.reshape(-1, HEAD_DIM), cache_mem_v.reshape(-1, HEAD_DIM)
    mem_s = lambda l: (lambda q, q_col: mem_attention(q, q_col, mk_s, mv_s, cached=(l, mem_len)))
    h, wkv_s, shift_s = rwkv_mem_layer(x_sample, state_shift[0], state_wkv[0], mem_s(0), P0)

    c2s, s2s = _rope_tables(past + jnp.arange(ts, dtype=jnp.int32))
    rows_s, _ = kv_proj(h.reshape(bs * ts, d), g_kv, w_kv, jnp.tile(c2s, (bs, 1)), jnp.tile(s2s, (bs, 1)))
    as_s = lambda x: x.reshape(bs, ts, -1)
    n_cmp_s = (past + ts - CMP_BLOCK) // CMP_STRIDE + 1
    assert (n_cmp_s - 1) * CMP_STRIDE + CMP_BLOCK <= past
    n_pool = cache_cmp_k.shape[0]
    pool = lambda x: x.reshape(n_pool * PAGE_ROWS, HEAD_DIM)
    ckc_s, cvc_s = compress_paged(pool(cache_cmp_k), pool(cache_cmp_v), page_table, n_cmp_s, cmp_pos, cmp_w1, cmp_w2)
    win_k2, win_v2 = cache_win_k.reshape(-1, HEAD_DIM), cache_win_v.reshape(-1, HEAD_DIM)

    new_win = []

    def attend_sample(proj):
        o, wk_out, wv_out = nsa_sample(proj, GATE_COL, ckc_s, cvc_s, pool(cache_slc_k), pool(cache_slc_v), page_table,
                                       [as_s(x) for x in rows_s[2:]], win_k2, win_v2, c2s, s2s, n_cmp_s, past)
        new_win.extend([wk_out, wv_out])
        return o

    y_s = nsa_mem_layer(h, mem_s(1), P1, attend_sample)
    cmp_k_s, cmp_v_s, slc_k_s, slc_v_s = [rows4(x, bs) for x in rows_s[:4]]
    win_k_s, win_v_s = [x.reshape(cache_win_k.shape) for x in new_win]

    return (y_p, y_s, mem_k_p.reshape(depth, bp, mem_len, MEM_HEADS, HEAD_DIM),
            mem_v_p.reshape(depth, bp, mem_len, MEM_HEADS, HEAD_DIM),
            wkv_p[None], shift_p[None], cmp_k_p, cmp_v_p, slc_k_p, slc_v_p, win_k_p, win_v_p,
            wkv_s[None], shift_s[None], cmp_k_s, cmp_v_s, slc_k_s, slc_v_s, win_k_s, win_v_s)
```

```python
import functools

import jax
import jax.numpy as jnp
from jax import lax
from jax.experimental import pallas as pl
from jax.experimental.pallas import tpu as pltpu

f32 = jnp.float32
bf16 = jnp.bfloat16

LANES = 128
VMEM_LIMIT_BYTES = 56 * 1024 * 1024

HEAD_DIM = 128
MEM_HEADS = 4
MEM_WIDTH = MEM_HEADS * HEAD_DIM
RWKV_HEAD_DIM = 64
GN_EPS = 64e-5
NSA_KV = 2
NSA_GROUP = 6
NSA_HEADS = NSA_KV * NSA_GROUP
NSA_WIDTH = NSA_HEADS * HEAD_DIM
KV_COLS = NSA_KV * HEAD_DIM
CMP_BLOCK = 32
CMP_STRIDE = 16
SLC_BLOCK = 64
SLC_SHIFT = 6
N_SELECT = 16
WINDOW = 512
Q_BLOCK = 128
ROPE_THETA = 10000.0
NORM_EPS = 1e-6
NEG_INF = -1e30
FORCE_SCORE = 1e9
PAGE_SIZE = 128

ROW_TILE = 512
FFN_ROW_TILE = 1024
SLC_CHUNK = 512
SEL_LANES = 128
SEQ_PER_STEP = 2
MEM_SEQ_PER_STEP = 4


def _params(*sem):
    return pltpu.CompilerParams(dimension_semantics=sem, vmem_limit_bytes=VMEM_LIMIT_BYTES)


def _rms(x, g):
    return x * lax.rsqrt(jnp.mean(x * x, axis=-1, keepdims=True) + NORM_EPS) * g


def _sigmoid(x):
    return 1.0 / (1.0 + jnp.exp(-x))


def _rmsnorm_kernel(x_ref, g_ref, o_ref):
    o_ref[...] = _rms(x_ref[...], g_ref[...])


def rmsnorm(x, g):
    m, d = x.shape
    tm = min(ROW_TILE, m)
    return pl.pallas_call(
        _rmsnorm_kernel,
        grid=(m // tm,),
        in_specs=[pl.BlockSpec((tm, d), lambda i: (i, 0)), pl.BlockSpec((1, d), lambda i: (0, 0))],
        out_specs=pl.BlockSpec((tm, d), lambda i: (i, 0)),
        out_shape=jax.ShapeDtypeStruct((m, d), f32),
        compiler_params=_params("parallel"),
        name="rmsnorm",
    )(x, g.reshape(1, d))


def _mm_kernel(x_ref, g_ref, w_ref, o_ref, xn_ref, *, norm):
    @pl.when(pl.program_id(1) == 0)
    def _():
        x = x_ref[...]
        if norm:
            x = _rms(x, g_ref[...])
        xn_ref[...] = x.astype(bf16)

    o_ref[...] = jnp.dot(xn_ref[...], w_ref[...], preferred_element_type=f32)


def matmul(x, w, g=None, tn=512):
    m, k = x.shape
    n = w.shape[1]
    tm = min(FFN_ROW_TILE if m % FFN_ROW_TILE == 0 else ROW_TILE, m)
    tn = min(tn, n)
    assert m % tm == 0 and n % tn == 0, (m, n, tm, tn)
    gg = jnp.ones((1, k), f32) if g is None else g.reshape(1, k)
    return pl.pallas_call(
        functools.partial(_mm_kernel, norm=g is not None),
        grid=(m // tm, n // tn),
        in_specs=[pl.BlockSpec((tm, k), lambda i, j: (i, 0)),
                  pl.BlockSpec((1, k), lambda i, j: (0, 0)),
                  pl.BlockSpec((k, tn), lambda i, j: (0, j))],
        out_specs=pl.BlockSpec((tm, tn), lambda i, j: (i, j)),
        out_shape=jax.ShapeDtypeStruct((m, n), f32),
        scratch_shapes=[pltpu.VMEM((tm, k), bf16)],
        compiler_params=_params("parallel", "arbitrary"),
        name="matmul",
    )(x, gg, w.astype(bf16))


def _out_proj_kernel(oa_ref, ob_ref, w_ref, h_ref, g_ref, y_ref):
    ka = oa_ref.shape[1]
    acc = jnp.dot(oa_ref[...].astype(bf16), w_ref[:ka, :], preferred_element_type=f32)
    acc += jnp.dot(ob_ref[...].astype(bf16), w_ref[ka:, :], preferred_element_type=f32)
    y_ref[...] = h_ref[...] + _rms(acc, g_ref[...])


def out_proj_residual(oa, ob, ob_col, w, h, g):
    m, ka = oa.shape
    d = w.shape[1]
    kb = w.shape[0] - ka
    tm = min(ROW_TILE, m)
    return pl.pallas_call(
        _out_proj_kernel,
        grid=(m // tm,),
        in_specs=[pl.BlockSpec((tm, ka), lambda i: (i, 0)),
                  pl.BlockSpec((tm, kb), lambda i: (i, ob_col)),
                  pl.BlockSpec((ka + kb, d), lambda i: (0, 0)),
                  pl.BlockSpec((tm, d), lambda i: (i, 0)),
                  pl.BlockSpec((1, d), lambda i: (0, 0))],
        out_specs=pl.BlockSpec((tm, d), lambda i: (i, 0)),
        out_shape=jax.ShapeDtypeStruct((m, d), f32),
        compiler_params=_params("parallel"),
        name="out_proj",
    )(oa, ob, w.astype(bf16), h, g.reshape(1, d))


def _ffn_kernel(h_ref, gpre_ref, w1_ref, w2_ref, gpost_ref, y_ref, xn_ref):
    j = pl.program_id(1)

    @pl.when(j == 0)
    def _():
        xn_ref[...] = _rms(h_ref[...], gpre_ref[...]).astype(bf16)
        y_ref[...] = jnp.zeros_like(y_ref)

    u = jnp.dot(xn_ref[...], w1_ref[...], preferred_element_type=f32)
    u = jnp.square(jnp.maximum(u, 0.0))
    y_ref[...] += jnp.dot(u.astype(bf16), w2_ref[...], preferred_element_type=f32)

    @pl.when(j == pl.num_programs(1) - 1)
    def _():
        y_ref[...] = h_ref[...] + _rms(y_ref[...], gpost_ref[...])


def ffn_residual(h, g_pre, w1, w2, g_post, tf=512):
    m, d = h.shape
    dff = w1.shape[1]
    tm = min(FFN_ROW_TILE, m)
    return pl.pallas_call(
        _ffn_kernel,
        grid=(m // tm, dff // tf),
        in_specs=[pl.BlockSpec((tm, d), lambda i, j: (i, 0)),
                  pl.BlockSpec((1, d), lambda i, j: (0, 0)),
                  pl.BlockSpec((d, tf), lambda i, j: (0, j)),
                  pl.BlockSpec((tf, d), lambda i, j: (j, 0)),
                  pl.BlockSpec((1, d), lambda i, j: (0, 0))],
        out_specs=pl.BlockSpec((tm, d), lambda i, j: (i, 0)),
        out_shape=jax.ShapeDtypeStruct((m, d), f32),
        scratch_shapes=[pltpu.VMEM((tm, d), bf16)],
        compiler_params=_params("parallel", "arbitrary"),
        name="ffn",
    )(h, g_pre.reshape(1, d), w1.astype(bf16), w2.astype(bf16), g_post.reshape(1, d))


def _rope_tile(x, c2, s2):
    return x * c2 + pltpu.roll(x, HEAD_DIM // 2, 1) * s2


def _rope_heads(q, c2, s2):
    return jnp.concatenate([_rope_tile(q[:, g * HEAD_DIM:(g + 1) * HEAD_DIM], c2, s2)
                            for g in range(NSA_GROUP)], axis=0)


N_KV_BRANCH = 6


def _kv_proj_kernel(h_ref, g_ref, w_ref, c2_ref, s2_ref, *refs):
    outs, bf_ref, xn_ref = refs[:N_KV_BRANCH], refs[N_KV_BRANCH], refs[N_KV_BRANCH + 1]
    j = pl.program_id(1)

    @pl.when(j == 0)
    def _():
        xn_ref[...] = _rms(h_ref[...], g_ref[...]).astype(bf16)

    acc = jnp.dot(xn_ref[...], w_ref[...], preferred_element_type=f32)
    for br in range(N_KV_BRANCH):
        @pl.when(j == br)
        def _(br=br):
            if br in (2, 4):
                c2, s2 = c2_ref[...], s2_ref[...]
                val = jnp.concatenate([_rope_tile(acc[:, kv * HEAD_DIM:(kv + 1) * HEAD_DIM], c2, s2)
                                       for kv in range(NSA_KV)], axis=1)
            else:
                val = acc
            outs[br][...] = val
            bf_ref[...] = val.astype(bf16)


def kv_proj(h, g, w, c2, s2):
    m, d = h.shape
    n = w.shape[1]
    assert n == N_KV_BRANCH * KV_COLS
    tm = min(ROW_TILE, m)
    ntab = c2.shape[0] // tm
    res = pl.pallas_call(
        _kv_proj_kernel,
        grid=(m // tm, N_KV_BRANCH),
        in_specs=[pl.BlockSpec((tm, d), lambda i, j: (i, 0)),
                  pl.BlockSpec((1, d), lambda i, j: (0, 0)),
                  pl.BlockSpec((d, KV_COLS), lambda i, j: (0, j)),
                  pl.BlockSpec((tm, HEAD_DIM), lambda i, j: (i % ntab, 0)),
                  pl.BlockSpec((tm, HEAD_DIM), lambda i, j: (i % ntab, 0))],
        out_specs=[pl.BlockSpec((tm, KV_COLS), lambda i, j: (i, 0))] * N_KV_BRANCH
                  + [pl.BlockSpec((tm, KV_COLS), lambda i, j: (i, j))],
        out_shape=[jax.ShapeDtypeStruct((m, KV_COLS), f32)] * N_KV_BRANCH + [jax.ShapeDtypeStruct((m, n), bf16)],
        scratch_shapes=[pltpu.VMEM((tm, d), bf16)],
        compiler_params=_params("parallel", "arbitrary"),
        name="kv_proj",
    )(h, g.reshape(1, d), w.astype(bf16), c2, s2)
    return res[:N_KV_BRANCH], res[N_KV_BRANCH]


def _mem_attn_kernel(q_ref, k_ref, v_ref, o_ref, *, heads_on_rows):
    scale = HEAD_DIM ** -0.5
    n_seq = q_ref.shape[0]
    for g in range(n_seq):
        for hd in range(MEM_HEADS):
            sl = slice(hd * HEAD_DIM, (hd + 1) * HEAD_DIM)
            if heads_on_rows:
                rows = k_ref.shape[0] // n_seq
                k = k_ref[pl.ds(g * rows + hd, rows // MEM_HEADS, stride=MEM_HEADS), :]
                v = v_ref[pl.ds(g * rows + hd, rows // MEM_HEADS, stride=MEM_HEADS), :]
            else:
                k, v = k_ref[g, :, sl], v_ref[g, :, sl]
            q = (q_ref[g, :, sl] * scale).astype(bf16)
            s = lax.dot_general(q, k.astype(bf16), (((1,), (1,)), ((), ())), preferred_element_type=f32)
            e = jnp.exp(s - jnp.max(s, axis=-1, keepdims=True))
            p = e / jnp.sum(e, axis=-1, keepdims=True)
            o_ref[g, :, sl] = jnp.dot(p.astype(bf16), v.astype(bf16), preferred_element_type=f32)


def mem_attention(q, q_col, mk, mv, k_col=0, v_col=0, cached=None):
    b, t, _ = q.shape
    w = MEM_WIDTH
    tq = min(ROW_TILE, t)
    if cached is None:
        g = 1
        mlen = mk.shape[1]
        kspec = pl.BlockSpec((1, mlen, w), lambda i, j: (i, 0, k_col))
        vspec = pl.BlockSpec((1, mlen, w), lambda i, j: (i, 0, v_col))
    else:
        layer, mlen = cached
        g = MEM_SEQ_PER_STEP if b % MEM_SEQ_PER_STEP == 0 else 1
        kspec = vspec = pl.BlockSpec((g * mlen * MEM_HEADS, HEAD_DIM), lambda i, j: (layer * (b // g) + i, 0))
    return pl.pallas_call(
        functools.partial(_mem_attn_kernel, heads_on_rows=cached is not None),
        grid=(b // g, t // tq),
        in_specs=[pl.BlockSpec((g, tq, w), lambda i, j: (i, j, q_col)), kspec, vspec],
        out_specs=pl.BlockSpec((g, tq, w), lambda i, j: (i, j, 0)),
        out_shape=jax.ShapeDtypeStruct((b, t, w), f32),
        compiler_params=_params("parallel", "arbitrary"),
        name="mem_attn",
    )(q, mk, mv)


WKV_QUAD = 4
WKV_LANES = WKV_QUAD * RWKV_HEAD_DIM
WKV_GROUP = 6
WKV_TB = 64


def _block_outputs(r, w, k, b, v, s0, zt):
    n = RWKV_HEAD_DIM
    nt = r.shape[0]
    row = lax.broadcasted_iota(jnp.int32, (1, n, WKV_LANES), 1)
    lane = lax.broadcasted_iota(jnp.int32, (1, n, WKV_LANES), 2)
    head = lax.shift_right_logical(lane, 6)
    tri = jnp.where(lax.broadcasted_iota(jnp.int32, (n, n), 0) >= lax.broadcasted_iota(jnp.int32, (n, n), 1),
                    1.0, 0.0).astype(bf16)
    log_w = jnp.concatenate([jnp.log(w[i]) for i in range(nt)], axis=1)
    hi = log_w.astype(bf16)
    log_p = (jnp.dot(tri, hi, preferred_element_type=f32)
             + jnp.dot(tri, (log_w - hi.astype(f32)).astype(bf16), preferred_element_type=f32))
    log_p = jnp.stack([log_p[:, i * WKV_LANES:(i + 1) * WKV_LANES] for i in range(nt)])
    p, p_inv = jnp.exp(log_p), jnp.exp(-log_p)
    zero = jnp.zeros((), bf16)

    def stack(x):
        xb = x.astype(bf16)
        return jnp.concatenate([jnp.where(head == h4, xb, zero) for h4 in range(WKV_QUAD)], axis=1)

    bdot = lambda x, y: jnp.einsum('nil,njl->nij', x, y, preferred_element_type=f32)
    rt = (r * p).astype(bf16)
    y0 = bdot(rt, stack(s0))
    causal = jnp.bitwise_and(lane, n - 1) <= row
    a_k = jnp.where(causal, bdot(rt, stack(k * p_inv)), 0.0).astype(bf16)
    a_b = jnp.where(causal, bdot(rt, stack(b * p_inv)), 0.0).astype(bf16)
    y_v = jnp.einsum('nij,njl->nil', a_k, stack(v), preferred_element_type=f32)
    return y0 + y_v - bdot(a_b, stack(zt))


def _wkv_kernel(r_ref, w_ref, k_ref, kk_ref, b_ref, v_ref, s0_ref, yt_ref, st_ref,
                s_scr, lhs_scr, vd_scr, yl_scr, s0_scr, zt_scr, *, nb, nq, tb, defer_y):
    n = RWKV_HEAD_DIM
    ti = pl.program_id(1)

    @pl.when(ti == 0)
    def _():
        for ib in range(nb):
            s_scr[ib * nq * n:(ib + 1) * nq * n, :] = s0_ref[ib]

    if defer_y:
        s0_scr[...] = s_scr[...]
        zt_scr[...] = jnp.zeros(zt_scr.shape, f32)
    else:
        yt_ref[...] = jnp.zeros(yt_ref.shape, f32)
    ri = lax.broadcasted_iota(jnp.int32, (WKV_LANES, WKV_LANES), 0)
    ci = lax.broadcasted_iota(jnp.int32, (WKV_LANES, WKV_LANES), 1)
    ones_blk = jnp.where(lax.shift_right_logical(ri, 6) == lax.shift_right_logical(ci, 6), 1.0, 0.0).astype(bf16)
    ones_blk2 = jnp.concatenate([ones_blk, ones_blk], axis=0)
    eye_rep = jnp.where(lax.broadcasted_iota(jnp.int32, (n, WKV_LANES), 0)
                        == jnp.bitwise_and(lax.broadcasted_iota(jnp.int32, (n, WKV_LANES), 1), n - 1),
                        1.0, 0.0).astype(bf16)
    step_lane = jnp.bitwise_and(lax.broadcasted_iota(jnp.int32, (n, WKV_LANES), 1), n - 1)
    tiles = [(ib, q) for ib in range(nb) for q in range(nq)]
    groups = [tiles[i:i + WKV_GROUP] for i in range(0, len(tiles), WKV_GROUP)]

    def step(t, carry):
        row = lambda ref, ib, q: ref[ib, pl.ds(t, 1), q * WKV_LANES:(q + 1) * WKV_LANES]
        for gi, group in enumerate(groups):
            rows_g = slice(gi * WKV_GROUP * n, (gi * WKV_GROUP + len(group)) * n)
            for ib, q in group:
                rs = slice((ib * nq + q) * n, (ib * nq + q + 1) * n)
                prod = s_scr[rs, :] * row(kk_ref, ib, q)
                hi = prod.astype(bf16)
                lhs_scr[rs, 0:WKV_LANES] = hi
                lhs_scr[rs, WKV_LANES:2 * WKV_LANES] = (prod - hi.astype(f32)).astype(bf16)
                vd_scr[rs, :] = eye_rep * row(v_ref, ib, q).astype(bf16)
            z = jnp.dot(lhs_scr[rows_g, :], ones_blk2, preferred_element_type=f32)
            vcol = jnp.dot(vd_scr[rows_g, :], ones_blk, preferred_element_type=f32)
            for i, (ib, q) in enumerate(group):
                rs = slice((ib * nq + q) * n, (ib * nq + q + 1) * n)
                ts = slice(i * n, (i + 1) * n)
                s = s_scr[rs, :] * row(w_ref, ib, q) - z[ts] * row(b_ref, ib, q) + vcol[ts] * row(k_ref, ib, q)
                s_scr[rs, :] = s
                if defer_y:
                    zt_scr[rs, :] = jnp.where(step_lane == t, z[ts], zt_scr[rs, :])
                else:
                    yl_scr[rs, :] = (s * row(r_ref, ib, q)).astype(bf16)
            if not defer_y:
                y = jnp.dot(yl_scr[rows_g, :], ones_blk, preferred_element_type=f32)
                for i, (ib, q) in enumerate(group):
                    yt_ref[ib, 0, q * n:(q + 1) * n, :] = jnp.where(step_lane == t, y[i * n:(i + 1) * n],
                                                                     yt_ref[ib, 0, q * n:(q + 1) * n, :])
        return carry

    lax.fori_loop(0, tb, step, 0, unroll=8)

    if defer_y:
        tiled = lambda ref: jnp.stack([ref[ib, :, q * WKV_LANES:(q + 1) * WKV_LANES] for ib, q in tiles])
        as_tiles = lambda ref: ref[...].reshape(len(tiles), n, WKV_LANES)
        yt = _block_outputs(tiled(r_ref), tiled(w_ref), tiled(k_ref), tiled(b_ref), tiled(v_ref),
                            as_tiles(s0_scr), as_tiles(zt_scr))
        for i, (ib, q) in enumerate(tiles):
            yt_ref[ib, :, q * WKV_LANES:(q + 1) * WKV_LANES] = yt[i]

    @pl.when(ti == pl.num_programs(1) - 1)
    def _():
        for ib in range(nb):
            st_ref[ib] = s_scr[ib * nq * n:(ib + 1) * nq * n, :]


def wkv_scan(r, w, k, kk, b, v, s0, nb=2):
    bsz, t, width = r.shape
    n = RWKV_HEAD_DIM
    nh = width // n
    nq = nh // WKV_QUAD
    tb = min(WKV_TB, t)
    nblk = t // tb
    to_tiles = lambda s: s.reshape(bsz, nq, WKV_QUAD, n, n).transpose(0, 1, 3, 2, 4).reshape(bsz, nq * n, WKV_LANES)
    row = pl.BlockSpec((nb, tb, width), lambda i, j: (i, j, 0))
    st = pl.BlockSpec((nb, nq * n, WKV_LANES), lambda i, j: (i, 0, 0))
    rows_all = nb * nq * n
    defer_y = tb == n
    if defer_y:
        y_spec, y_shape = row, jax.ShapeDtypeStruct((bsz, t, width), f32)
    else:
        y_spec = pl.BlockSpec((nb, 1, nq * n, WKV_LANES), lambda i, j: (i, j, 0, 0))
        y_shape = jax.ShapeDtypeStruct((bsz, nblk, nq * n, WKV_LANES), f32)
    yt, s_t = pl.pallas_call(
        functools.partial(_wkv_kernel, nb=nb, nq=nq, tb=tb, defer_y=defer_y),
        grid=(bsz // nb, nblk),
        in_specs=[row, row, row, row, row, row, st],
        out_specs=[y_spec, st],
        out_shape=[y_shape, jax.ShapeDtypeStruct((bsz, nq * n, WKV_LANES), f32)],
        scratch_shapes=[pltpu.VMEM((rows_all, WKV_LANES), f32),
                        pltpu.VMEM((rows_all, 2 * WKV_LANES), bf16),
                        pltpu.VMEM((rows_all, WKV_LANES), bf16),
                        pltpu.VMEM((rows_all, WKV_LANES), bf16),
                        pltpu.VMEM((rows_all, WKV_LANES), f32),
                        pltpu.VMEM((rows_all, WKV_LANES), f32)],
        compiler_params=_params("parallel", "arbitrary"),
        name="wkv_scan",
    )(r, w, k, kk, b, v, to_tiles(s0))
    if defer_y:
        y = yt
    else:
        y = yt.reshape(bsz, nblk, nq, n, WKV_QUAD, n).transpose(0, 1, 5, 2, 4, 3)[:, :, :tb].reshape(bsz, t, width)
    s_t = s_t.reshape(bsz, nq, n, WKV_QUAD, n).transpose(0, 1, 3, 2, 4).reshape(bsz, nh, n, n)
    return y, s_t


def _gelu_tanh(x):
    return 0.5 * x * (1.0 + jnp.tanh(0.7978845608028654 * (x + 0.044715 * x * x * x)))


def _chunk_rows(x_ref, n_chunks, row0=0, row_stride=1):
    return jnp.concatenate(
        [x_ref[pl.ds(row0 + s * row_stride, n_chunks, stride=CMP_STRIDE * row_stride), :]
         for s in range(CMP_STRIDE)], axis=1).astype(bf16)


def _compress_rows(x2, pos_ref, w1_ref, w2_ref, n_valid, n_heads):
    rows = x2.shape[0]
    n_chunks = rows // n_heads
    pab = jnp.dot(x2, w1_ref[...], preferred_element_type=f32)
    pos = jnp.dot(pos_ref[...], w1_ref[...], preferred_element_type=f32)
    posterm = pos[0:1, :HEAD_DIM] + pos[1:2, HEAD_DIM:]
    hid = pab[:, :HEAD_DIM] + pltpu.roll(pab[:, HEAD_DIM:], rows - 1, 0) + posterm
    out = jnp.dot(_gelu_tanh(hid).astype(bf16), w2_ref[...], preferred_element_type=f32)
    n = jnp.bitwise_and(lax.broadcasted_iota(jnp.int32, out.shape, 0), n_chunks - 1)
    return jnp.where(n < n_valid, out, 0.0)


def _compress_prompt_kernel(k_ref, v_ref, posk_ref, w1k_ref, w2k_ref, posv_ref, w1v_ref, w2v_ref,
                            ok_ref, ov_ref, *, n_chunks, n_valid):
    ok_ref[0, 0] = _compress_rows(_chunk_rows(k_ref.at[0], n_chunks), posk_ref, w1k_ref, w2k_ref, n_valid, 1)
    ov_ref[0, 0] = _compress_rows(_chunk_rows(v_ref.at[0], n_chunks), posv_ref, w1v_ref, w2v_ref, n_valid, 1)


def _cmp_weights(cmp_pos, cmp_w1, cmp_w2):
    ws = []
    for i in range(2):
        half = CMP_STRIDE * HEAD_DIM
        pos = _pad_to_rows(cmp_pos[i].reshape(2, half), 8).astype(bf16)
        w1 = cmp_w1[i].reshape(2, half, HEAD_DIM)
        ws += [pos, jnp.concatenate([w1[0], w1[1]], axis=1).astype(bf16), cmp_w2[i].astype(bf16)]
    return ws


_CMP_WEIGHT_SHAPES = [(8, CMP_STRIDE * HEAD_DIM), (CMP_STRIDE * HEAD_DIM, 2 * HEAD_DIM), (HEAD_DIM, HEAD_DIM)] * 2


def compress_prompt(ck, cv, cmp_pos, cmp_w1, cmp_w2):
    b, t, _ = ck.shape
    n_chunks = t // CMP_STRIDE
    n_valid = (t - CMP_BLOCK) // CMP_STRIDE + 1
    out = jax.ShapeDtypeStruct((b, NSA_KV, n_chunks, HEAD_DIM), f32)
    ospec = pl.BlockSpec((1, 1, n_chunks, HEAD_DIM), lambda i, kv: (i, kv, 0, 0))
    wspecs = [pl.BlockSpec(s, lambda i, kv: (0, 0)) for s in _CMP_WEIGHT_SHAPES]
    return pl.pallas_call(
        functools.partial(_compress_prompt_kernel, n_chunks=n_chunks, n_valid=n_valid),
        grid=(b, NSA_KV),
        in_specs=[pl.BlockSpec((1, t, HEAD_DIM), lambda i, kv: (i, 0, kv)),
                  pl.BlockSpec((1, t, HEAD_DIM), lambda i, kv: (i, 0, kv))] + wspecs,
        out_specs=[ospec, ospec],
        out_shape=[out, out],
        compiler_params=_params("parallel", "parallel"),
        name="compress_prompt",
    )(ck, cv, *_cmp_weights(cmp_pos, cmp_w1, cmp_w2))


PAGE_ROWS = PAGE_SIZE * NSA_KV


def _page_specs(n_pages, n_seq):
    return [pl.BlockSpec((PAGE_ROWS, HEAD_DIM), lambda i, pt, g=g, p=p: (pt[i * n_seq + g, p], 0))
            for g in range(n_seq) for p in range(n_pages)]


def _compress_paged_kernel(pt_ref, *refs, n_pages, n_seq, n_valid):
    np_all = n_seq * n_pages
    k_pages, v_pages = refs[:np_all], refs[np_all:2 * np_all]
    posk_ref, w1k_ref, w2k_ref, posv_ref, w1v_ref, w2v_ref, ok_ref, ov_ref = refs[2 * np_all:]
    per_page = PAGE_SIZE // CMP_STRIDE
    n_chunks = n_pages * per_page
    heads = [(g, kv) for g in range(n_seq) for kv in range(NSA_KV)]
    chunks = lambda pages: jnp.concatenate([_chunk_rows(pg, per_page, kv, NSA_KV)
                                            for g, kv in heads for pg in pages[g * n_pages:(g + 1) * n_pages]], axis=0)
    ok = _compress_rows(chunks(k_pages), posk_ref, w1k_ref, w2k_ref, n_valid, len(heads))
    ov = _compress_rows(chunks(v_pages), posv_ref, w1v_ref, w2v_ref, n_valid, len(heads))
    for i, (g, kv) in enumerate(heads):
        ok_ref[g, kv] = ok[i * n_chunks:(i + 1) * n_chunks]
        ov_ref[g, kv] = ov[i * n_chunks:(i + 1) * n_chunks]


def compress_paged(pool_k, pool_v, page_table, n_valid, cmp_pos, cmp_w1, cmp_w2):
    b, n_pages = page_table.shape
    n_seq = SEQ_PER_STEP if b % SEQ_PER_STEP == 0 else 1
    n_chunks = n_pages * PAGE_SIZE // CMP_STRIDE
    out = jax.ShapeDtypeStruct((b, NSA_KV, n_chunks, HEAD_DIM), f32)
    ospec = pl.BlockSpec((n_seq, NSA_KV, n_chunks, HEAD_DIM), lambda i, pt: (i, 0, 0, 0))
    wspecs = [pl.BlockSpec(s, lambda i, pt: (0, 0)) for s in _CMP_WEIGHT_SHAPES]
    np_all = n_seq * n_pages
    return pl.pallas_call(
        functools.partial(_compress_paged_kernel, n_pages=n_pages, n_seq=n_seq, n_valid=n_valid),
        grid_spec=pltpu.PrefetchScalarGridSpec(
            num_scalar_prefetch=1,
            grid=(b // n_seq,),
            in_specs=_page_specs(n_pages, n_seq) * 2 + wspecs,
            out_specs=[ospec, ospec]),
        out_shape=[out, out],
        compiler_params=_params("parallel"),
        name="compress_paged",
    )(page_table, *([pool_k] * np_all), *([pool_v] * np_all), *_cmp_weights(cmp_pos, cmp_w1, cmp_w2))


def _stack_heads(x):
    return jnp.concatenate([x[:, g * HEAD_DIM:(g + 1) * HEAD_DIM] for g in range(NSA_GROUP)], axis=0)


def _dot_nt(a, b):
    return lax.dot_general(a, b, (((1,), (1,)), ((), ())), preferred_element_type=f32)


def _softmax_heads(s, bias, tq):
    s3 = s.reshape(NSA_GROUP, tq, s.shape[1]) + bias[None]
    e = jnp.exp(s3 - jnp.max(s3, axis=-1, keepdims=True))
    return e, jnp.sum(e, axis=-1, keepdims=True)


def _compressed_branch(qc, ckc, cvc, pos_t, n_cmp, tq):
    s = _dot_nt(qc, ckc)
    n = lax.broadcasted_iota(jnp.int32, (tq, s.shape[1]), 1)
    vis = (n * CMP_STRIDE + (CMP_BLOCK - 1) <= pos_t) & (n < n_cmp)
    e, denom = _softmax_heads(s, jnp.where(vis, 0.0, NEG_INF), tq)
    any_vis = jnp.where(pos_t >= CMP_BLOCK - 1, 1.0, 0.0)
    p = e / denom * any_vis[None]
    o = jnp.dot(p.reshape(s.shape).astype(bf16), cvc, preferred_element_type=f32)
    return o, jnp.sum(p, axis=0)


def _select_blocks(psum, tq, pos0, n_slc, n_j):
    if tq < SEL_LANES:
        psum = jnp.concatenate([psum, jnp.zeros((SEL_LANES - tq, psum.shape[1]), f32)], axis=0)
    n_c = psum.shape[1]
    j = lax.broadcasted_iota(jnp.int32, (n_j, n_c), 0)
    cs = lax.broadcasted_iota(jnp.int32, (n_j, n_c), 1) * CMP_STRIDE
    overlap = jnp.where((cs < j * SLC_BLOCK + SLC_BLOCK) & (cs + (CMP_BLOCK - 1) >= j * SLC_BLOCK), 1.0, 0.0)
    imp_t = lax.dot_general(overlap, psum, (((1,), (1,)), ((), ())),
                            preferred_element_type=f32, precision=lax.Precision.HIGHEST)
    j = lax.broadcasted_iota(jnp.int32, imp_t.shape, 0)
    pos_t = pos0 + lax.broadcasted_iota(jnp.int32, imp_t.shape, 1)
    cur = lax.shift_right_logical(pos_t, SLC_SHIFT)
    causal = j * SLC_BLOCK <= pos_t
    forced = (j == 0) | (j == cur) | (j == cur - 1)
    score = jnp.where(causal, jnp.where(forced, FORCE_SCORE, imp_t), -FORCE_SCORE)
    score = jnp.where(j < n_slc, score, -2.0 * FORCE_SCORE)
    rank = jnp.zeros(imp_t.shape, f32)
    for jp in range(n_slc):
        row = score[jp:jp + 1, :]
        ahead = (row > score) | ((row == score) & (j > jp))
        rank = rank + jnp.where(ahead, 1.0, 0.0)
    sel_t = jnp.where(rank < min(N_SELECT, n_slc), 1.0, 0.0)
    return sel_t.T[0:tq]


def _selection_bias(sel, key0, n_keys):
    nj = sel.shape[1]
    j = lax.broadcasted_iota(jnp.int32, (nj, n_keys), 0)
    kpos = key0 + lax.broadcasted_iota(jnp.int32, (nj, n_keys), 1)
    e = jnp.where(lax.shift_right_logical(kpos, SLC_SHIFT) == j, 1.0, 0.0).astype(bf16)
    return jnp.dot(jnp.where(sel > 0.5, 0.0, NEG_INF).astype(bf16), e, preferred_element_type=f32)


def _window_branch(qr, wk, wv, kpos0, n_keys_valid, pos_t, tq, n_phantom=None):
    s = _dot_nt(qr, wk)
    lane = lax.broadcasted_iota(jnp.int32, (tq, s.shape[1]), 1)
    kpos = kpos0 + lane
    valid = (kpos <= pos_t) & (pos_t - kpos < WINDOW) & (lane < n_keys_valid)
    s3 = s.reshape(NSA_GROUP, tq, s.shape[1]) + jnp.where(valid, 0.0, NEG_INF)[None]
    m = jnp.max(s3, axis=-1, keepdims=True)
    if n_phantom is not None:
        m = jnp.where(n_phantom[None] > 0.0, jnp.maximum(m, 0.0), m)
    e = jnp.exp(s3 - m)
    denom = jnp.sum(e, axis=-1, keepdims=True)
    if n_phantom is not None:
        denom = denom + n_phantom[None] * jnp.exp(-m)
    return jnp.dot((e / denom).reshape(s.shape).astype(bf16), wv, preferred_element_type=f32)


def _gated_sum(gate, cols, tq, o_cmp, o_slc, o_win, g):
    r = slice(g * tq, (g + 1) * tq)
    c, s, w = cols[0] + g, cols[1] + g, cols[2] + g
    return gate[:, c:c + 1] * o_cmp[r] + gate[:, s:s + 1] * o_slc[r] + gate[:, w:w + 1] * o_win[r]


QK_SCALE = HEAD_DIM ** -0.5


def _nsa_prompt_kernel(q_ref, c2_ref, s2_ref, gate_ref, ckc_ref, cvc_ref, sk_ref, sv_ref, wk_ref, wv_ref,
                       o_ref, *, n_cmp, n_slc):
    i = pl.program_id(2)
    tq = Q_BLOCK
    rows = NSA_GROUP * tq
    q0 = i * tq
    q = q_ref[0] * QK_SCALE
    qc = _stack_heads(q).astype(bf16)
    qr = _rope_heads(q, c2_ref[...], s2_ref[...]).astype(bf16)
    pos_t = q0 + lax.broadcasted_iota(jnp.int32, (tq, 1), 0)

    o_cmp, psum = _compressed_branch(qc, ckc_ref[0, 0].astype(bf16), cvc_ref[0, 0].astype(bf16), pos_t, n_cmp, tq)
    sel = _select_blocks(psum, tq, q0, n_slc, n_slc)

    def slc_step(c, carry, causal):
        m, l, acc = carry
        k0 = pl.multiple_of(c * SLC_CHUNK, SLC_CHUNK)
        bias = _selection_bias(sel, k0, SLC_CHUNK)
        if causal:
            kpos = k0 + lax.broadcasted_iota(jnp.int32, bias.shape, 1)
            bias = jnp.where(kpos <= pos_t, bias, NEG_INF)
        s3 = _dot_nt(qr, sk_ref[0, pl.ds(k0, SLC_CHUNK), :]).reshape(NSA_GROUP, tq, SLC_CHUNK) + bias[None]
        m_new = jnp.maximum(m, jnp.max(s3, axis=-1, keepdims=True))
        alpha = jnp.exp(m - m_new)
        e = jnp.exp(s3 - m_new)
        l = alpha * l + jnp.sum(e, axis=-1, keepdims=True)
        pv = jnp.dot(e.reshape(rows, SLC_CHUNK).astype(bf16), sv_ref[0, pl.ds(k0, SLC_CHUNK), :],
                     preferred_element_type=f32)
        return m_new, l, alpha * acc + pv.reshape(NSA_GROUP, tq, HEAD_DIM)

    c_last = q0 // SLC_CHUNK
    init = (jnp.full((NSA_GROUP, tq, 1), NEG_INF, f32), jnp.zeros((NSA_GROUP, tq, 1), f32),
            jnp.zeros((NSA_GROUP, tq, HEAD_DIM), f32))
    carry = lax.fori_loop(0, c_last, functools.partial(slc_step, causal=False), init)
    _, l, acc = slc_step(c_last, carry, causal=True)
    o_slc = (acc / l).reshape(rows, HEAD_DIM)

    span = WINDOW + tq
    w0 = pl.multiple_of(jnp.maximum(q0 - WINDOW, 0), tq)
    n_phantom = jnp.maximum(WINDOW - 1 - pos_t, 0).astype(f32)
    o_win = _window_branch(qr, wk_ref[0, pl.ds(w0, span), :], wv_ref[0, pl.ds(w0, span), :], w0, span, pos_t, tq,
                           n_phantom)

    gate = _sigmoid(gate_ref[0])
    first_kv = pl.program_id(1) == 0
    for g in range(NSA_GROUP):
        head = [_gated_sum(gate, tuple(br * NSA_HEADS + kv * NSA_GROUP for br in range(3)), tq, o_cmp, o_slc, o_win, g)
                for kv in range(NSA_KV)]
        o_ref[0, :, g * HEAD_DIM:(g + 1) * HEAD_DIM] = jnp.where(first_kv, head[0], head[1])


def nsa_prompt(proj, gate_col, ckc, cvc, kv_bf, c2, s2, n_cmp):
    b, t, _ = proj.shape
    assert t % SLC_CHUNK == 0 and t >= WINDOW + Q_BLOCK
    n_slc = t // SLC_BLOCK
    gw = NSA_GROUP * HEAD_DIM
    kvcol = lambda c: pl.BlockSpec((1, t, HEAD_DIM), lambda bi, kv, i, c=c: (bi, 0, c * NSA_KV + kv))
    cmp_spec = pl.BlockSpec((1, 1, ckc.shape[2], HEAD_DIM), lambda bi, kv, i: (bi, kv, 0, 0))
    return pl.pallas_call(
        functools.partial(_nsa_prompt_kernel, n_cmp=n_cmp, n_slc=n_slc),
        grid=(b, NSA_KV, t // Q_BLOCK),
        in_specs=[pl.BlockSpec((1, Q_BLOCK, gw), lambda bi, kv, i: (bi, i, kv)),
                  pl.BlockSpec((Q_BLOCK, HEAD_DIM), lambda bi, kv, i: (i, 0)),
                  pl.BlockSpec((Q_BLOCK, HEAD_DIM), lambda bi, kv, i: (i, 0)),
                  pl.BlockSpec((1, Q_BLOCK, LANES), lambda bi, kv, i: (bi, i, gate_col)),
                  cmp_spec, cmp_spec, kvcol(2), kvcol(3), kvcol(4), kvcol(5)],
        out_specs=pl.BlockSpec((1, Q_BLOCK, gw), lambda bi, kv, i: (bi, i, kv)),
        out_shape=jax.ShapeDtypeStruct((b, t, NSA_WIDTH), f32),
        compiler_params=_params("parallel", "parallel", "arbitrary"),
        name="nsa_prompt",
    )(proj, c2, s2, proj, ckc, cvc, kv_bf, kv_bf, kv_bf, kv_bf)


def _pad_rows(x, n):
    return jnp.concatenate([x, jnp.zeros((n - x.shape[0], x.shape[1]), x.dtype)], axis=0)


def _nsa_sample_kernel(pt_ref, *refs, n_pages, n_seq, n_cmp, n_slc, n_j, past, tq):
    np_all = n_seq * n_pages
    k_pages, v_pages = refs[:np_all], refs[np_all:2 * np_all]
    (q_ref, c2_ref, s2_ref, gate_ref, ckc_ref, cvc_ref, nsk_ref, nsv_ref, nwk_ref, nwv_ref,
     wink_ref, winv_ref, o_ref, owk_ref, owv_ref) = refs[2 * np_all:]
    pos_t = past + lax.broadcasted_iota(jnp.int32, (tq, 1), 0)
    lw2 = wink_ref.shape[0] // n_seq
    lw = lw2 // NSA_KV
    c2, s2 = c2_ref[...], s2_ref[...]
    for g in range(n_seq):
        gate = _sigmoid(gate_ref[g])
        keep = lw2 - tq * NSA_KV
        for cache_ref, new_ref, out_ref in ((wink_ref, nwk_ref, owk_ref), (winv_ref, nwv_ref, owv_ref)):
            out_ref[g * lw2:g * lw2 + keep, :] = cache_ref[g * lw2 + tq * NSA_KV:(g + 1) * lw2, :]
            for kv in range(NSA_KV):
                out_ref[pl.ds(g * lw2 + keep + kv, tq, stride=NSA_KV), :] = new_ref[g, :, kv * HEAD_DIM:(kv + 1) * HEAD_DIM]
        for kv in range(NSA_KV):
            ksl = slice(kv * HEAD_DIM, (kv + 1) * HEAD_DIM)
            q = q_ref[g, :, kv * NSA_GROUP * HEAD_DIM:(kv + 1) * NSA_GROUP * HEAD_DIM] * QK_SCALE
            qc = _stack_heads(q).astype(bf16)
            qr = _rope_heads(q, c2, s2).astype(bf16)
            o_cmp, psum = _compressed_branch(qc, ckc_ref[g, kv].astype(bf16), cvc_ref[g, kv].astype(bf16), pos_t,
                                             n_cmp, tq)
            sel = _select_blocks(psum, tq, past, n_slc, n_j)

            paged = lambda pages: [pg[pl.ds(kv, PAGE_SIZE, stride=NSA_KV), :] for pg in pages[g * n_pages:(g + 1) * n_pages]]
            sk = jnp.concatenate(paged(k_pages) + [_pad_rows(nsk_ref[g, :, ksl], LANES)], axis=0).astype(bf16)
            sv = jnp.concatenate(paged(v_pages) + [_pad_rows(nsv_ref[g, :, ksl], LANES)], axis=0).astype(bf16)
            n_keys = sk.shape[0]
            bias = _selection_bias(sel, 0, n_keys)
            bias = jnp.where(lax.broadcasted_iota(jnp.int32, bias.shape, 1) <= pos_t, bias, NEG_INF)
            e, denom = _softmax_heads(_dot_nt(qr, sk), bias, tq)
            o_slc = jnp.dot((e / denom).reshape(NSA_GROUP * tq, n_keys).astype(bf16), sv, preferred_element_type=f32)

            cached = lambda ref: ref[pl.ds(g * lw2 + kv, lw, stride=NSA_KV), :]
            wk = jnp.concatenate([cached(wink_ref), _pad_rows(nwk_ref[g, :, ksl], LANES)], axis=0).astype(bf16)
            wv = jnp.concatenate([cached(winv_ref), _pad_rows(nwv_ref[g, :, ksl], LANES)], axis=0).astype(bf16)
            o_win = _window_branch(qr, wk, wv, past - lw, lw + tq, pos_t, tq)

            cols = tuple(br * NSA_HEADS + kv * NSA_GROUP for br in range(3))
            for hg in range(NSA_GROUP):
                hd = kv * NSA_GROUP + hg
                o_ref[g, :, hd * HEAD_DIM:(hd + 1) * HEAD_DIM] = _gated_sum(gate, cols, tq, o_cmp, o_slc, o_win, hg)


def nsa_sample(proj, gate_col, ckc, cvc, pool_k, pool_v, page_table, new_rows, win_k, win_v, c2, s2, n_cmp, past):
    b, tq, _ = proj.shape
    n_pages = page_table.shape[1]
    assert past == n_pages * PAGE_SIZE and past % SLC_BLOCK == 0 and tq <= SLC_BLOCK and (tq * NSA_KV) % 8 == 0
    n_seq = SEQ_PER_STEP if b % SEQ_PER_STEP == 0 else 1
    n_slc = past // SLC_BLOCK + 1
    n_j = -(-n_slc // SLC_BLOCK) * SLC_BLOCK
    lw2 = win_k.shape[0] // b
    per_b = lambda shape: pl.BlockSpec((n_seq,) + shape, lambda i, pt: (i,) + (0,) * len(shape))
    tab = pl.BlockSpec((tq, HEAD_DIM), lambda i, pt: (0, 0))
    win = pl.BlockSpec((n_seq * lw2, HEAD_DIM), lambda i, pt: (i, 0))
    np_all = n_seq * n_pages
    return pl.pallas_call(
        functools.partial(_nsa_sample_kernel, n_pages=n_pages, n_seq=n_seq, n_cmp=n_cmp, n_slc=n_slc, n_j=n_j,
                          past=past, tq=tq),
        grid_spec=pltpu.PrefetchScalarGridSpec(
            num_scalar_prefetch=1,
            grid=(b // n_seq,),
            in_specs=_page_specs(n_pages, n_seq) * 2 + [
                per_b((tq, NSA_WIDTH)), tab, tab,
                pl.BlockSpec((n_seq, tq, LANES), lambda i, pt: (i, 0, gate_col)),
                per_b((NSA_KV, ckc.shape[2], HEAD_DIM)), per_b((NSA_KV, ckc.shape[2], HEAD_DIM))]
                + [per_b((tq, KV_COLS))] * 4 + [win, win],
            out_specs=[per_b((tq, NSA_WIDTH)), win, win]),
        out_shape=[jax.ShapeDtypeStruct((b, tq, NSA_WIDTH), f32),
                   jax.ShapeDtypeStruct(win_k.shape, f32), jax.ShapeDtypeStruct(win_v.shape, f32)],
        compiler_params=_params("parallel"),
        name="nsa_sample",
    )(page_table, *([pool_k] * np_all), *([pool_v] * np_all), proj, c2, s2, proj, ckc, cvc, *new_rows, win_k, win_v)


def _prev_rows(x, tile, halo_ref, first_ref, seq_len, halo_fn=lambda rows: rows):
    tm, c = x.shape
    prev = pltpu.roll(x, 1, 0)
    row = lax.broadcasted_iota(jnp.int32, (tm, 1), 0)
    if seq_len >= tm:
        tiles_per_seq = seq_len // tm
        first = first_ref[pl.ds(tile // tiles_per_seq, 1), :]
        edge = jnp.where(tile % tiles_per_seq == 0, first, halo_fn(halo_ref[...])[7:8, :])
        return jnp.where(row == 0, edge, prev)
    pieces = []
    for j in range(tm // seq_len):
        pieces.append(jnp.broadcast_to(first_ref[j:j + 1, :], (8, c)))
        if seq_len > 8:
            pieces.append(jnp.zeros((seq_len - 8, c), f32))
    return jnp.where(jnp.bitwise_and(row, seq_len - 1) == 0, jnp.concatenate(pieces, axis=0), prev)


def _shift_specs(m, c, tm, seq_len, n_seq, col=None):
    cb = (lambda *g: 0) if col is None else col
    tile = pl.BlockSpec((tm, c), lambda *g: (g[0], cb(*g)))
    halo = pl.BlockSpec((8, c), lambda *g: (jnp.maximum(g[0] * (tm // 8) - 1, 0), cb(*g)))
    if seq_len >= tm:
        first = pl.BlockSpec((-(-n_seq // 8) * 8, c), lambda *g: (0, cb(*g)))
    else:
        first = pl.BlockSpec((tm // seq_len, c), lambda *g: (g[0], cb(*g)))
    return tile, halo, first


def _pad_first(first, seq_len, tm):
    return _pad_to_rows(first, -(-first.shape[0] // 8) * 8) if seq_len >= tm else first


def _head_sums(x):
    ri = lax.broadcasted_iota(jnp.int32, (WKV_LANES, WKV_LANES), 0)
    ci = lax.broadcasted_iota(jnp.int32, (WKV_LANES, WKV_LANES), 1)
    ones_blk = jnp.where(lax.shift_right_logical(ri, 6) == lax.shift_right_logical(ci, 6), 1.0, 0.0).astype(bf16)
    hi = x.astype(bf16)
    lo = (x - hi.astype(f32)).astype(bf16)
    out = []
    for c in range(x.shape[1] // WKV_LANES):
        sl = slice(c * WKV_LANES, (c + 1) * WKV_LANES)
        out.append(jnp.dot(hi[:, sl], ones_blk, preferred_element_type=f32)
                   + jnp.dot(lo[:, sl], ones_blk, preferred_element_type=f32))
    return jnp.concatenate(out, axis=1)


def _lora_kernel(h_ref, halo_ref, first_ref, gn_ref, mu_ref, wd1_ref, wa1_ref, wg1_ref, wd2_ref, wa2_ref, wg2_ref,
                 w0_ref, a0_ref, decay_ref, a_ref, g_ref, *, seq_len):
    norm = lambda rows: _rms(rows, gn_ref[...])
    hn = norm(h_ref[...])
    xx = _prev_rows(hn, pl.program_id(0), halo_ref, first_ref, seq_len, norm) - hn
    mix = lambda r: (hn + xx * mu_ref[r:r + 1, :]).astype(bf16)
    dot = lambda x, w_ref: jnp.dot(x, w_ref[...], preferred_element_type=f32)
    w_raw = w0_ref[...] + dot(jnp.tanh(dot(mix(0), wd1_ref)).astype(bf16), wd2_ref)
    softplus = jnp.maximum(-w_raw, 0.0) + jnp.log(1.0 + jnp.exp(-jnp.abs(w_raw)))
    decay_ref[...] = jnp.exp(-jnp.exp(-softplus - 0.5))
    a_ref[...] = _sigmoid(a0_ref[...] + dot(dot(mix(1), wa1_ref).astype(bf16), wa2_ref))
    g_ref[...] = dot(_sigmoid(dot(mix(2), wg1_ref)).astype(bf16), wg2_ref)


def rwkv_lora(h, x_prev, seq_len, P):
    m, d = h.shape
    rw = P['w0'].shape[0]
    tm = min(ROW_TILE, m)
    pad128 = lambda w: _pad_cols(w, -(-w.shape[1] // LANES) * LANES).astype(bf16)
    w1s = [pad128(P[k]) for k in ('w_decay1', 'w_aaa1', 'w_gate1')]
    w2s = [_pad_to_rows(P[k], w1.shape[1]).astype(bf16) for k, w1 in zip(('w_decay2', 'w_aaa2', 'w_gate2'), w1s)]
    full = lambda x: pl.BlockSpec(x.shape, lambda i: (0, 0))
    mu = _pad_to_rows(P['mu_wag'], 8)
    gn = P['g_mix_pre'].reshape(1, d)
    vecs = [P['w0'].reshape(1, rw), P['a0'].reshape(1, rw)]
    out = jax.ShapeDtypeStruct((m, rw), f32)
    ospec = pl.BlockSpec((tm, rw), lambda i: (i, 0))
    return pl.pallas_call(
        functools.partial(_lora_kernel, seq_len=seq_len),
        grid=(m // tm,),
        in_specs=list(_shift_specs(m, d, tm, seq_len, x_prev.shape[0])) + [full(gn), full(mu)]
                 + [full(w) for w in w1s + w2s + vecs],
        out_specs=[ospec] * 3,
        out_shape=[out] * 3,
        compiler_params=_params("parallel"),
        name="rwkv_lora",
    )(h, h, _pad_first(x_prev, seq_len, tm), gn, mu, *w1s, *w2s, *vecs)


RKV_GROUPS = 3
IN_COL_TILE = 512


def _rwkv_in_kernel(h_ref, halo_ref, first_ref, g_ref, w_ref, mu_ref, a_ref, kkw_ref, kaw_ref,
                    r_ref, k_ref, kk_ref, b_ref, v_ref, mq_ref, xn_scr, edge_scr, *, seq_len, per_group):
    i, j = pl.program_id(0), pl.program_id(1)
    tm, c = r_ref.shape

    @pl.when(j == 0)
    def _():
        xn_scr[...] = _rms(h_ref[...], g_ref[...]).astype(bf16)
        edge_scr[0:8, :] = _rms(halo_ref[...], g_ref[...])
        edge_scr[8:, :] = first_ref[...]

    cur = jnp.dot(xn_scr[...], w_ref[...], preferred_element_type=f32)
    edge = jnp.dot(edge_scr[...].astype(bf16), w_ref[...], preferred_element_type=f32)
    halo, first = edge[0:8], edge[8:]
    prev = pltpu.roll(cur, 1, 0)
    row = lax.broadcasted_iota(jnp.int32, (tm, 1), 0)
    if seq_len >= tm:
        tiles_per_seq = seq_len // tm
        seq = lax.broadcasted_iota(jnp.int32, (first.shape[0], 1), 0) == i // tiles_per_seq
        first_row = jnp.sum(jnp.where(seq, first, 0.0), axis=0, keepdims=True)
        edge_row = jnp.where(i % tiles_per_seq == 0, first_row, halo[7:8])
        prev = jnp.where(row == 0, edge_row, prev)
    else:
        pieces = []
        for q in range(tm // seq_len):
            pieces.append(jnp.broadcast_to(first[q:q + 1], (8, c)))
            if seq_len > 8:
                pieces.append(jnp.zeros((seq_len - 8, c), f32))
        prev = jnp.where(jnp.bitwise_and(row, seq_len - 1) == 0, jnp.concatenate(pieces, axis=0), prev)
    x = cur + (prev - cur) * mu_ref[...]

    @pl.when(j < per_group)
    def _():
        r_ref[...] = x

    @pl.when((j >= per_group) & (j < 2 * per_group))
    def _():
        a = a_ref[...]
        kk = x * kkw_ref[...]
        kk = kk * lax.rsqrt(_head_sums(kk * kk) + 1e-12)
        kk_ref[...] = kk
        b_ref[...] = kk * a
        k_ref[...] = x * (1.0 + (a - 1.0) * kaw_ref[...])

    @pl.when((j >= 2 * per_group) & (j < 3 * per_group))
    def _():
        v_ref[...] = x

    @pl.when(j >= 3 * per_group)
    def _():
        mq_ref[...] = cur


def rwkv_in(h, x_prev, a_rate, seq_len, P):
    m, d = h.shape
    rw = a_rate.shape[1]
    c = IN_COL_TILE
    tm = min(ROW_TILE, m)
    per_group = rw // c
    n_col = RKV_GROUPS * per_group + MEM_WIDTH // c
    assert rw % c == 0 and MEM_WIDTH == c and P['w_in_a'].shape[1] == n_col * c
    tile, halo, first = _shift_specs(m, d, tm, seq_len, x_prev.shape[0])
    n_first = first.block_shape[0]
    mu = jnp.pad(P['mu_rkv'], (0, MEM_WIDTH)).reshape(1, -1)
    grp = lambda base: (lambda i, j: (i, jnp.clip(j - base * per_group, 0, per_group - 1)))
    vec = lambda base: (lambda i, j: (0, jnp.clip(j - base * per_group, 0, per_group - 1)))
    wide = jax.ShapeDtypeStruct((m, rw), f32)
    return pl.pallas_call(
        functools.partial(_rwkv_in_kernel, seq_len=seq_len, per_group=per_group),
        grid=(m // tm, n_col),
        in_specs=[pl.BlockSpec((tm, d), lambda i, j: (i, 0)),
                  pl.BlockSpec((8, d), lambda i, j: (jnp.maximum(i * (tm // 8) - 1, 0), 0)),
                  pl.BlockSpec(first.block_shape, lambda i, j: (i if seq_len < tm else 0, 0)),
                  pl.BlockSpec((1, d), lambda i, j: (0, 0)),
                  pl.BlockSpec((d, c), lambda i, j: (0, j)),
                  pl.BlockSpec((1, c), lambda i, j: (0, j)),
                  pl.BlockSpec((tm, c), grp(1)),
                  pl.BlockSpec((1, c), vec(1)),
                  pl.BlockSpec((1, c), vec(1))],
        out_specs=[pl.BlockSpec((tm, c), grp(0)), pl.BlockSpec((tm, c), grp(1)), pl.BlockSpec((tm, c), grp(1)),
                   pl.BlockSpec((tm, c), grp(1)), pl.BlockSpec((tm, c), grp(2)),
                   pl.BlockSpec((tm, c), lambda i, j: (i, 0))],
        out_shape=[wide] * 5 + [jax.ShapeDtypeStruct((m, MEM_WIDTH), f32)],
        scratch_shapes=[pltpu.VMEM((tm, d), bf16), pltpu.VMEM((8 + n_first, d), f32)],
        compiler_params=_params("parallel", "arbitrary"),
        name="rwkv_in",
    )(h, h, _pad_first(x_prev, seq_len, tm), P['g_mix_pre'].reshape(1, d), P['w_in_a'].astype(bf16), mu, a_rate,
      P['k_k'].reshape(1, rw), P['k_a'].reshape(1, rw))


def _rwkv_out_kernel(y_ref, r_ref, k_ref, v_ref, g_ref, om_ref, h_ref, lw_ref, lb_ref, rk_ref, w_ref, gp_ref, o_ref):
    inv_n = 1.0 / RWKV_HEAD_DIM
    y = y_ref[...]
    d = y - _head_sums(y) * inv_n
    var = _head_sums(d * d) * inv_n
    yn = d * lax.rsqrt(var + GN_EPS) * lw_ref[...] + lb_ref[...]
    bonus = _head_sums(r_ref[...] * k_ref[...] * rk_ref[...]) * v_ref[...]
    o = ((yn + bonus) * g_ref[...]).astype(bf16)
    rw = o.shape[1]
    acc = jnp.dot(o, w_ref[:rw, :], preferred_element_type=f32)
    acc += jnp.dot(om_ref[...].astype(bf16), w_ref[rw:, :], preferred_element_type=f32)
    o_ref[...] = h_ref[...] + _rms(acc, gp_ref[...])


def rwkv_out(y, r, k, v, gate, o_mem, h, P):
    m, rw = y.shape
    d = h.shape[1]
    tm = min(ROW_TILE // 2, m)
    wide = pl.BlockSpec((tm, rw), lambda i: (i, 0))
    vec = pl.BlockSpec((1, rw), lambda i: (0, 0))
    return pl.pallas_call(
        _rwkv_out_kernel,
        grid=(m // tm,),
        in_specs=[wide] * 5 + [pl.BlockSpec((tm, o_mem.shape[1]), lambda i: (i, 0)),
                               pl.BlockSpec((tm, d), lambda i: (i, 0)), vec, vec, vec,
                               pl.BlockSpec((rw + o_mem.shape[1], d), lambda i: (0, 0)),
                               pl.BlockSpec((1, d), lambda i: (0, 0))],
        out_specs=pl.BlockSpec((tm, d), lambda i: (i, 0)),
        out_shape=jax.ShapeDtypeStruct((m, d), f32),
        compiler_params=_params("parallel"),
        name="rwkv_out",
    )(y, r, k, v, gate, o_mem, h, P['lnx_w'].reshape(1, rw), P['lnx_b'].reshape(1, rw), P['r_k'].reshape(1, rw),
      P['w_out_a'].astype(bf16), P['g_mix_post'].reshape(1, d))


def _rope_tables(pos):
    half = HEAD_DIM // 2
    inv = jnp.power(ROPE_THETA, -jnp.arange(half, dtype=f32) / half)
    ang = pos.astype(f32)[:, None] * inv[None, :]
    cos, sin = jnp.cos(ang), jnp.sin(ang)
    return jnp.concatenate([cos, cos], axis=-1), jnp.concatenate([-sin, sin], axis=-1)


def _pad_cols(w, n):
    return jnp.pad(w, ((0, 0), (0, n - w.shape[1])))


def _pad_to_rows(x, n):
    return jnp.pad(x, ((0, n - x.shape[0]), (0, 0)))


def rwkv_mem_layer(h, x_prev, s0, mem, P):
    b, t, d = h.shape
    m = b * t
    assert t & (t - 1) == 0 and t % 8 == 0 and (t % ROW_TILE == 0 or ROW_TILE % t == 0)
    h2 = h.reshape(m, d)
    rw = P['w0'].shape[0]
    last = rmsnorm(_pad_to_rows(h[:, -1], -(-b // 8) * 8), P['g_mix_pre'])[:b]

    decay, a_rate, gate = rwkv_lora(h2, x_prev, t, P)
    r, k, kk, kb, v, mq = rwkv_in(h2, x_prev, a_rate, t, P)
    as3 = lambda x: x.reshape(b, t, -1)
    y, s_t = wkv_scan(as3(r), as3(decay), as3(k), as3(kk), as3(kb), as3(v), s0)
    o_mem = mem(as3(mq), 0)
    h2 = rwkv_out(y.reshape(m, rw), r, k, v, gate, o_mem.reshape(m, MEM_WIDTH), h2, P)
    h2 = ffn_residual(h2, P['g_ffn_pre'], P['w_ff1'], P['w_ff2'], P['g_ffn_post'])
    return h2.reshape(b, t, d), s_t, last


GATE_COL = (NSA_WIDTH + MEM_WIDTH) // LANES


def nsa_mem_layer(h, mem, P, attend):
    b, t, d = h.shape
    m = b * t
    h2 = h.reshape(m, d)
    n_in = (GATE_COL + 1) * LANES
    n_in = -(-n_in // 768) * 768
    proj = matmul(h2, _pad_cols(P['w_in_b'].astype(bf16), n_in), g=P['g_mix_pre'], tn=768).reshape(b, t, -1)
    o_nsa = attend(proj)
    o_mem = mem(proj, NSA_WIDTH // MEM_WIDTH)
    h2 = out_proj_residual(o_nsa.reshape(m, NSA_WIDTH), o_mem.reshape(m, MEM_WIDTH), 0, P['w_out_b'], h2,
                           P['g_mix_post'])
    h2 = ffn_residual(h2, P['g_ffn_pre'], P['w_ff1'], P['w_ff2'], P['g_ffn_post'])
    return h2.reshape(b, t, d)


def kernel(x_prompt, x_sample, mem_prompt, cache_mem_k, cache_mem_v, state_wkv, state_shift, cache_cmp_k, cache_cmp_v, cache_slc_k, cache_slc_v, cache_win_k, cache_win_v, page_table, g_mix_pre, g_mix_post, g_ffn_pre, g_ffn_post, g_mem, w_mem_k, w_mem_v, w_in_a, mu_rkv, mu_wag, w0, w_decay1, w_decay2, a0, w_aaa1, w_aaa2, w_gate1, w_gate2, k_k, k_a, r_k, lnx_w, lnx_b, w_out_a, g_kv, w_kv, cmp_pos, cmp_w1, cmp_w2, w_in_b, w_out_b, w_ff1, w_ff2):
    bp, tp, d = x_prompt.shape
    bs, ts, _ = x_sample.shape
    depth = g_mix_pre.shape[0]
    assert depth == 2 and w_in_a.shape[0] == 1 and w_in_b.shape[0] == 1
    n_pages = page_table.shape[1]
    past = n_pages * PAGE_SIZE
    mem_len = mem_prompt.shape[1]

    P0 = dict(g_mix_pre=g_mix_pre[0], g_mix_post=g_mix_post[0], g_ffn_pre=g_ffn_pre[0], g_ffn_post=g_ffn_post[0],
              w_in_a=w_in_a[0], mu_rkv=mu_rkv[0], mu_wag=mu_wag[0], w0=w0[0], w_decay1=w_decay1[0],
              w_decay2=w_decay2[0], a0=a0[0], w_aaa1=w_aaa1[0], w_aaa2=w_aaa2[0], w_gate1=w_gate1[0],
              w_gate2=w_gate2[0], k_k=k_k[0], k_a=k_a[0], r_k=r_k[0], lnx_w=lnx_w[0], lnx_b=lnx_b[0],
              w_out_a=w_out_a[0], w_ff1=w_ff1[0], w_ff2=w_ff2[0])
    P1 = dict(g_mix_pre=g_mix_pre[1], g_mix_post=g_mix_post[1], g_ffn_pre=g_ffn_pre[1], g_ffn_post=g_ffn_post[1],
              w_in_b=w_in_b[0], w_out_b=w_out_b[0], w_ff1=w_ff1[1], w_ff2=w_ff2[1])
    rows4 = lambda x, bsz: x.reshape(bsz, -1, NSA_KV, HEAD_DIM)

    mem2 = mem_prompt.reshape(bp * mem_len, d)
    mkv = [matmul(mem2, jnp.concatenate([w_mem_k[l], w_mem_v[l]], axis=1), g=g_mem[l]).reshape(bp, mem_len, -1)
           for l in range(depth)]
    mem_k_p = jnp.stack([x[..., :MEM_WIDTH] for x in mkv])
    mem_v_p = jnp.stack([x[..., MEM_WIDTH:] for x in mkv])
    mem_p = lambda l: (lambda q, q_col: mem_attention(q, q_col, mkv[l], mkv[l], k_col=0, v_col=1))

    nh = w0.shape[1] // RWKV_HEAD_DIM
    shift0 = jnp.zeros((bp, d), f32)
    wkv0 = jnp.zeros((bp, nh, RWKV_HEAD_DIM, RWKV_HEAD_DIM), f32)
    h, wkv_p, shift_p = rwkv_mem_layer(x_prompt, shift0, wkv0, mem_p(0), P0)

    c2p, s2p = _rope_tables(jnp.arange(tp, dtype=jnp.int32))
    rows_p, kv_bf = kv_proj(h.reshape(bp * tp, d), g_kv, w_kv, c2p, s2p)
    as_p = lambda x: x.reshape(bp, tp, -1)
    ckc, cvc = compress_prompt(as_p(rows_p[0]), as_p(rows_p[1]), cmp_pos, cmp_w1, cmp_w2)
    n_cmp_p = (tp - CMP_BLOCK) // CMP_STRIDE + 1

    def attend_prompt(proj):
        return nsa_prompt(proj, GATE_COL, ckc, cvc, as_p(kv_bf), c2p, s2p, n_cmp_p)

    y_p = nsa_mem_layer(h, mem_p(1), P1, attend_prompt)
    cmp_k_p, cmp_v_p, slc_k_p, slc_v_p, win_k_p, win_v_p = [rows4(x, bp) for x in rows_p]
    n_keep = min(WINDOW, tp)
    win_k_p, win_v_p = win_k_p[:, tp - n_keep:], win_v_p[:, tp - n_keep:]

    mk_s, mv_s = cache_mem_k.reshape(-1, HEAD_DIM), cache_mem_v.reshape(-1, HEAD_DIM)
    mem_s = lambda l: (lambda q, q_col: mem_attention(q, q_col, mk_s, mv_s, cached=(l, mem_len)))
    h, wkv_s, shift_s = rwkv_mem_layer(x_sample, state_shift[0], state_wkv[0], mem_s(0), P0)

    c2s, s2s = _rope_tables(past + jnp.arange(ts, dtype=jnp.int32))
    rows_s, _ = kv_proj(h.reshape(bs * ts, d), g_kv, w_kv, jnp.tile(c2s, (bs, 1)), jnp.tile(s2s, (bs, 1)))
    as_s = lambda x: x.reshape(bs, ts, -1)
    n_cmp_s = (past + ts - CMP_BLOCK) // CMP_STRIDE + 1
    assert (n_cmp_s - 1) * CMP_STRIDE + CMP_BLOCK <= past
    n_pool = cache_cmp_k.shape[0]
    pool = lambda x: x.reshape(n_pool * PAGE_ROWS, HEAD_DIM)
    ckc_s, cvc_s = compress_paged(pool(cache_cmp_k), pool(cache_cmp_v), page_table, n_cmp_s, cmp_pos, cmp_w1, cmp_w2)
    win_k2, win_v2 = cache_win_k.reshape(-1, HEAD_DIM), cache_win_v.reshape(-1, HEAD_DIM)

    new_win = []

    def attend_sample(proj):
        o, wk_out, wv_out = nsa_sample(proj, GATE_COL, ckc_s, cvc_s, pool(cache_slc_k), pool(cache_slc_v), page_table,
                                       [as_s(x) for x in rows_s[2:]], win_k2, win_v2, c2s, s2s, n_cmp_s, past)
        new_win.extend([wk_out, wv_out])
        return o

    y_s = nsa_mem_layer(h, mem_s(1), P1, attend_sample)
    cmp_k_s, cmp_v_s, slc_k_s, slc_v_s = [rows4(x, bs) for x in rows_s[:4]]
    win_k_s, win_v_s = [x.reshape(cache_win_k.shape) for x in new_win]

    return (y_p, y_s, mem_k_p.reshape(depth, bp, mem_len, MEM_HEADS, HEAD_DIM),
            mem_v_p.reshape(depth, bp, mem_len, MEM_HEADS, HEAD_DIM),
            wkv_p[None], shift_p[None], cmp_k_p, cmp_v_p, slc_k_p, slc_v_p, win_k_p, win_v_p,
            wkv_s[None], shift_s[None], cmp_k_s, cmp_v_s, slc_k_s, slc_v_s, win_k_s, win_v_s)
```

```python
import functools

import jax
import jax.numpy as jnp
from jax import lax
from jax.experimental import pallas as pl
from jax.experimental.pallas import tpu as pltpu

f32 = jnp.float32
bf16 = jnp.bfloat16

LANES = 128
VMEM_LIMIT_BYTES = 56 * 1024 * 1024

HEAD_DIM = 128
MEM_HEADS = 4
MEM_WIDTH = MEM_HEADS * HEAD_DIM
RWKV_HEAD_DIM = 64
GN_EPS = 64e-5
NSA_KV = 2
NSA_GROUP = 6
NSA_HEADS = NSA_KV * NSA_GROUP
NSA_WIDTH = NSA_HEADS * HEAD_DIM
KV_COLS = NSA_KV * HEAD_DIM
CMP_BLOCK = 32
CMP_STRIDE = 16
SLC_BLOCK = 64
SLC_SHIFT = 6
N_SELECT = 16
WINDOW = 512
Q_BLOCK = 128
ROPE_THETA = 10000.0
NORM_EPS = 1e-6
NEG_INF = -1e30
FORCE_SCORE = 1e9
PAGE_SIZE = 128

ROW_TILE = 512
FFN_ROW_TILE = 1024
SLC_CHUNK = 512
SEL_LANES = 128
SEQ_PER_STEP = 2
MEM_SEQ_PER_STEP = 4


def _params(*sem):
    return pltpu.CompilerParams(dimension_semantics=sem, vmem_limit_bytes=VMEM_LIMIT_BYTES)


def _rms(x, g):
    return x * lax.rsqrt(jnp.mean(x * x, axis=-1, keepdims=True) + NORM_EPS) * g


def _sigmoid(x):
    return 1.0 / (1.0 + jnp.exp(-x))


def _rmsnorm_kernel(x_ref, g_ref, o_ref):
    o_ref[...] = _rms(x_ref[...], g_ref[...])


def rmsnorm(x, g):
    m, d = x.shape
    tm = min(ROW_TILE, m)
    return pl.pallas_call(
        _rmsnorm_kernel,
        grid=(m // tm,),
        in_specs=[pl.BlockSpec((tm, d), lambda i: (i, 0)), pl.BlockSpec((1, d), lambda i: (0, 0))],
        out_specs=pl.BlockSpec((tm, d), lambda i: (i, 0)),
        out_shape=jax.ShapeDtypeStruct((m, d), f32),
        compiler_params=_params("parallel"),
        name="rmsnorm",
    )(x, g.reshape(1, d))


def _mm_kernel(x_ref, g_ref, w_ref, o_ref, xn_ref, *, norm):
    @pl.when(pl.program_id(1) == 0)
    def _():
        x = x_ref[...]
        if norm:
            x = _rms(x, g_ref[...])
        xn_ref[...] = x.astype(bf16)

    o_ref[...] = jnp.dot(xn_ref[...], w_ref[...], preferred_element_type=f32)


def matmul(x, w, g=None, tn=512):
    m, k = x.shape
    n = w.shape[1]
    tm = min(FFN_ROW_TILE if m % FFN_ROW_TILE == 0 else ROW_TILE, m)
    tn = min(tn, n)
    assert m % tm == 0 and n % tn == 0, (m, n, tm, tn)
    gg = jnp.ones((1, k), f32) if g is None else g.reshape(1, k)
    return pl.pallas_call(
        functools.partial(_mm_kernel, norm=g is not None),
        grid=(m // tm, n // tn),
        in_specs=[pl.BlockSpec((tm, k), lambda i, j: (i, 0)),
                  pl.BlockSpec((1, k), lambda i, j: (0, 0)),
                  pl.BlockSpec((k, tn), lambda i, j: (0, j))],
        out_specs=pl.BlockSpec((tm, tn), lambda i, j: (i, j)),
        out_shape=jax.ShapeDtypeStruct((m, n), f32),
        scratch_shapes=[pltpu.VMEM((tm, k), bf16)],
        compiler_params=_params("parallel", "arbitrary"),
        name="matmul",
    )(x, gg, w.astype(bf16))


def _out_proj_kernel(oa_ref, ob_ref, w_ref, h_ref, g_ref, y_ref):
    ka = oa_ref.shape[1]
    acc = jnp.dot(oa_ref[...].astype(bf16), w_ref[:ka, :], preferred_element_type=f32)
    acc += jnp.dot(ob_ref[...].astype(bf16), w_ref[ka:, :], preferred_element_type=f32)
    y_ref[...] = h_ref[...] + _rms(acc, g_ref[...])


def out_proj_residual(oa, ob, ob_col, w, h, g):
    m, ka = oa.shape
    d = w.shape[1]
    kb = w.shape[0] - ka
    tm = min(ROW_TILE, m)
    return pl.pallas_call(
        _out_proj_kernel,
        grid=(m // tm,),
        in_specs=[pl.BlockSpec((tm, ka), lambda i: (i, 0)),
                  pl.BlockSpec((tm, kb), lambda i: (i, ob_col)),
                  pl.BlockSpec((ka + kb, d), lambda i: (0, 0)),
                  pl.BlockSpec((tm, d), lambda i: (i, 0)),
                  pl.BlockSpec((1, d), lambda i: (0, 0))],
        out_specs=pl.BlockSpec((tm, d), lambda i: (i, 0)),
        out_shape=jax.ShapeDtypeStruct((m, d), f32),
        compiler_params=_params("parallel"),
        name="out_proj",
    )(oa, ob, w.astype(bf16), h, g.reshape(1, d))


def _ffn_kernel(h_ref, gpre_ref, w1_ref, w2_ref, gpost_ref, y_ref, xn_ref):
    j = pl.program_id(1)

    @pl.when(j == 0)
    def _():
        xn_ref[...] = _rms(h_ref[...], gpre_ref[...]).astype(bf16)
        y_ref[...] = jnp.zeros_like(y_ref)

    u = jnp.dot(xn_ref[...], w1_ref[...], preferred_element_type=f32)
    u = jnp.square(jnp.maximum(u, 0.0))
    y_ref[...] += jnp.dot(u.astype(bf16), w2_ref[...], preferred_element_type=f32)

    @pl.when(j == pl.num_programs(1) - 1)
    def _():
        y_ref[...] = h_ref[...] + _rms(y_ref[...], gpost_ref[...])


def ffn_residual(h, g_pre, w1, w2, g_post, tf=512):
    m, d = h.shape
    dff = w1.shape[1]
    tm = min(FFN_ROW_TILE, m)
    return pl.pallas_call(
        _ffn_kernel,
        grid=(m // tm, dff // tf),
        in_specs=[pl.BlockSpec((tm, d), lambda i, j: (i, 0)),
                  pl.BlockSpec((1, d), lambda i, j: (0, 0)),
                  pl.BlockSpec((d, tf), lambda i, j: (0, j)),
                  pl.BlockSpec((tf, d), lambda i, j: (j, 0)),
                  pl.BlockSpec((1, d), lambda i, j: (0, 0))],
        out_specs=pl.BlockSpec((tm, d), lambda i, j: (i, 0)),
        out_shape=jax.ShapeDtypeStruct((m, d), f32),
        scratch_shapes=[pltpu.VMEM((tm, d), bf16)],
        compiler_params=_params("parallel", "arbitrary"),
        name="ffn",
    )(h, g_pre.reshape(1, d), w1.astype(bf16), w2.astype(bf16), g_post.reshape(1, d))


def _rope_tile(x, c2, s2):
    return x * c2 + pltpu.roll(x, HEAD_DIM // 2, 1) * s2


def _rope_heads(q, c2, s2):
    return jnp.concatenate([_rope_tile(q[:, g * HEAD_DIM:(g + 1) * HEAD_DIM], c2, s2)
                            for g in range(NSA_GROUP)], axis=0)


N_KV_BRANCH = 6


def _kv_proj_kernel(h_ref, g_ref, w_ref, c2_ref, s2_ref, *refs):
    outs, bf_ref, xn_ref = refs[:N_KV_BRANCH], refs[N_KV_BRANCH], refs[N_KV_BRANCH + 1]
    j = pl.program_id(1)

    @pl.when(j == 0)
    def _():
        xn_ref[...] = _rms(h_ref[...], g_ref[...]).astype(bf16)

    acc = jnp.dot(xn_ref[...], w_ref[...], preferred_element_type=f32)
    for br in range(N_KV_BRANCH):
        @pl.when(j == br)
        def _(br=br):
            if br in (2, 4):
                c2, s2 = c2_ref[...], s2_ref[...]
                val = jnp.concatenate([_rope_tile(acc[:, kv * HEAD_DIM:(kv + 1) * HEAD_DIM], c2, s2)
                                       for kv in range(NSA_KV)], axis=1)
            else:
                val = acc
            outs[br][...] = val
            bf_ref[...] = val.astype(bf16)


def kv_proj(h, g, w, c2, s2):
    m, d = h.shape
    n = w.shape[1]
    assert n == N_KV_BRANCH * KV_COLS
    tm = min(ROW_TILE, m)
    ntab = c2.shape[0] // tm
    res = pl.pallas_call(
        _kv_proj_kernel,
        grid=(m // tm, N_KV_BRANCH),
        in_specs=[pl.BlockSpec((tm, d), lambda i, j: (i, 0)),
                  pl.BlockSpec((1, d), lambda i, j: (0, 0)),
                  pl.BlockSpec((d, KV_COLS), lambda i, j: (0, j)),
                  pl.BlockSpec((tm, HEAD_DIM), lambda i, j: (i % ntab, 0)),
                  pl.BlockSpec((tm, HEAD_DIM), lambda i, j: (i % ntab, 0))],
        out_specs=[pl.BlockSpec((tm, KV_COLS), lambda i, j: (i, 0))] * N_KV_BRANCH
                  + [pl.BlockSpec((tm, KV_COLS), lambda i, j: (i, j))],
        out_shape=[jax.ShapeDtypeStruct((m, KV_COLS), f32)] * N_KV_BRANCH + [jax.ShapeDtypeStruct((m, n), bf16)],
        scratch_shapes=[pltpu.VMEM((tm, d), bf16)],
        compiler_params=_params("parallel", "arbitrary"),
        name="kv_proj",
    )(h, g.reshape(1, d), w.astype(bf16), c2, s2)
    return res[:N_KV_BRANCH], res[N_KV_BRANCH]


def _mem_attn_kernel(q_ref, k_ref, v_ref, o_ref):
    scale = HEAD_DIM ** -0.5
    for hd in range(MEM_HEADS):
        sl = slice(hd * HEAD_DIM, (hd + 1) * HEAD_DIM)
        q = (q_ref[0, :, sl] * scale).astype(bf16)
        s = _dot_nt(q, k_ref[0, :, sl].astype(bf16))
        e = jnp.exp(s - jnp.max(s, axis=-1, keepdims=True))
        p = e / jnp.sum(e, axis=-1, keepdims=True)
        o_ref[0, :, sl] = jnp.dot(p.astype(bf16), v_ref[0, :, sl].astype(bf16), preferred_element_type=f32)


def _mem_attn_cached_kernel(q_ref, k_ref, v_ref, o_ref):
    n_seq, tq, _ = q_ref.shape
    n_keys = k_ref.shape[0]
    q = jnp.concatenate([q_ref[g, :, hd * HEAD_DIM:(hd + 1) * HEAD_DIM]
                         for g in range(n_seq) for hd in range(MEM_HEADS)], axis=0)
    s = _dot_nt((q * HEAD_DIM ** -0.5).astype(bf16), k_ref[...].astype(bf16))
    shift = lambda x, n: lax.shift_right_logical(x, n.bit_length() - 1)
    r = lax.broadcasted_iota(jnp.int32, (s.shape[0], 1), 0)
    c = lax.broadcasted_iota(jnp.int32, (1, n_keys), 1)
    row_code = shift(r, tq)
    col_code = shift(c, n_keys // n_seq) * MEM_HEADS + jnp.bitwise_and(c, MEM_HEADS - 1)
    s = jnp.where(row_code == col_code, s, NEG_INF)
    e = jnp.exp(s - jnp.max(s, axis=-1, keepdims=True))
    p = e / jnp.sum(e, axis=-1, keepdims=True)
    o = jnp.dot(p.astype(bf16), v_ref[...].astype(bf16), preferred_element_type=f32)
    for g in range(n_seq):
        for hd in range(MEM_HEADS):
            r0 = (g * MEM_HEADS + hd) * tq
            o_ref[g, :, hd * HEAD_DIM:(hd + 1) * HEAD_DIM] = o[r0:r0 + tq]


def mem_attention(q, q_col, mk, mv, k_col=0, v_col=0, cached=None):
    b, t, _ = q.shape
    w = MEM_WIDTH
    tq = min(ROW_TILE, t)
    if cached is None:
        g, body = 1, _mem_attn_kernel
        mlen = mk.shape[1]
        kspec = pl.BlockSpec((1, mlen, w), lambda i, j: (i, 0, k_col))
        vspec = pl.BlockSpec((1, mlen, w), lambda i, j: (i, 0, v_col))
    else:
        layer, mlen = cached
        g, body = (MEM_SEQ_PER_STEP if b % MEM_SEQ_PER_STEP == 0 else 1), _mem_attn_cached_kernel
        pow2 = lambda x: x & (x - 1) == 0
        assert tq == t and pow2(tq) and pow2(mlen * MEM_HEADS) and pow2(MEM_HEADS)
        kspec = vspec = pl.BlockSpec((g * mlen * MEM_HEADS, HEAD_DIM), lambda i, j: (layer * (b // g) + i, 0))
    return pl.pallas_call(
        body,
        grid=(b // g, t // tq),
        in_specs=[pl.BlockSpec((g, tq, w), lambda i, j: (i, j, q_col)), kspec, vspec],
        out_specs=pl.BlockSpec((g, tq, w), lambda i, j: (i, j, 0)),
        out_shape=jax.ShapeDtypeStruct((b, t, w), f32),
        compiler_params=_params("parallel", "arbitrary"),
        name="mem_attn",
    )(q, mk, mv)


WKV_QUAD = 4
WKV_LANES = WKV_QUAD * RWKV_HEAD_DIM
WKV_GROUP = 6
WKV_TB = 64


def _block_outputs(r, w, k, b, v, s0, zt):
    n = RWKV_HEAD_DIM
    nt = r.shape[0]
    row = lax.broadcasted_iota(jnp.int32, (1, n, WKV_LANES), 1)
    lane = lax.broadcasted_iota(jnp.int32, (1, n, WKV_LANES), 2)
    head = lax.shift_right_logical(lane, 6)
    tri = jnp.where(lax.broadcasted_iota(jnp.int32, (n, n), 0) >= lax.broadcasted_iota(jnp.int32, (n, n), 1),
                    1.0, 0.0).astype(bf16)
    log_w = jnp.concatenate([jnp.log(w[i]) for i in range(nt)], axis=1)
    hi = log_w.astype(bf16)
    log_p = (jnp.dot(tri, hi, preferred_element_type=f32)
             + jnp.dot(tri, (log_w - hi.astype(f32)).astype(bf16), preferred_element_type=f32))
    log_p = jnp.stack([log_p[:, i * WKV_LANES:(i + 1) * WKV_LANES] for i in range(nt)])
    p, p_inv = jnp.exp(log_p), jnp.exp(-log_p)
    zero = jnp.zeros((), bf16)

    def stack(x):
        xb = x.astype(bf16)
        return jnp.concatenate([jnp.where(head == h4, xb, zero) for h4 in range(WKV_QUAD)], axis=1)

    bdot = lambda x, y: jnp.einsum('nil,njl->nij', x, y, preferred_element_type=f32)
    rt = (r * p).astype(bf16)
    y0 = bdot(rt, stack(s0))
    causal = jnp.bitwise_and(lane, n - 1) <= row
    a_k = jnp.where(causal, bdot(rt, stack(k * p_inv)), 0.0).astype(bf16)
    a_b = jnp.where(causal, bdot(rt, stack(b * p_inv)), 0.0).astype(bf16)
    y_v = jnp.einsum('nij,njl->nil', a_k, stack(v), preferred_element_type=f32)
    return y0 + y_v - bdot(a_b, stack(zt))


def _wkv_kernel(r_ref, w_ref, k_ref, kk_ref, b_ref, v_ref, s0_ref, yt_ref, st_ref,
                s_scr, lhs_scr, vd_scr, yl_scr, s0_scr, zt_scr, *, nb, nq, tb, defer_y):
    n = RWKV_HEAD_DIM
    ti = pl.program_id(1)

    @pl.when(ti == 0)
    def _():
        for ib in range(nb):
            s_scr[ib * nq * n:(ib + 1) * nq * n, :] = s0_ref[ib]

    if defer_y:
        s0_scr[...] = s_scr[...]
        zt_scr[...] = jnp.zeros(zt_scr.shape, f32)
    else:
        yt_ref[...] = jnp.zeros(yt_ref.shape, f32)
    ri = lax.broadcasted_iota(jnp.int32, (WKV_LANES, WKV_LANES), 0)
    ci = lax.broadcasted_iota(jnp.int32, (WKV_LANES, WKV_LANES), 1)
    ones_blk = jnp.where(lax.shift_right_logical(ri, 6) == lax.shift_right_logical(ci, 6), 1.0, 0.0).astype(bf16)
    ones_blk2 = jnp.concatenate([ones_blk, ones_blk], axis=0)
    eye_rep = jnp.where(lax.broadcasted_iota(jnp.int32, (n, WKV_LANES), 0)
                        == jnp.bitwise_and(lax.broadcasted_iota(jnp.int32, (n, WKV_LANES), 1), n - 1),
                        1.0, 0.0).astype(bf16)
    step_lane = jnp.bitwise_and(lax.broadcasted_iota(jnp.int32, (n, WKV_LANES), 1), n - 1)
    tiles = [(ib, q) for ib in range(nb) for q in range(nq)]
    groups = [tiles[i:i + WKV_GROUP] for i in range(0, len(tiles), WKV_GROUP)]

    def step(t, carry):
        row = lambda ref, ib, q: ref[ib, pl.ds(t, 1), q * WKV_LANES:(q + 1) * WKV_LANES]
        for gi, group in enumerate(groups):
            rows_g = slice(gi * WKV_GROUP * n, (gi * WKV_GROUP + len(group)) * n)
            for ib, q in group:
                rs = slice((ib * nq + q) * n, (ib * nq + q + 1) * n)
                prod = s_scr[rs, :] * row(kk_ref, ib, q)
                hi = prod.astype(bf16)
                lhs_scr[rs, 0:WKV_LANES] = hi
                lhs_scr[rs, WKV_LANES:2 * WKV_LANES] = (prod - hi.astype(f32)).astype(bf16)
                vd_scr[rs, :] = eye_rep * row(v_ref, ib, q).astype(bf16)
            z = jnp.dot(lhs_scr[rows_g, :], ones_blk2, preferred_element_type=f32)
            vcol = jnp.dot(vd_scr[rows_g, :], ones_blk, preferred_element_type=f32)
            for i, (ib, q) in enumerate(group):
                rs = slice((ib * nq + q) * n, (ib * nq + q + 1) * n)
                ts = slice(i * n, (i + 1) * n)
                s = s_scr[rs, :] * row(w_ref, ib, q) - z[ts] * row(b_ref, ib, q) + vcol[ts] * row(k_ref, ib, q)
                s_scr[rs, :] = s
                if defer_y:
                    zt_scr[rs, :] = jnp.where(step_lane == t, z[ts], zt_scr[rs, :])
                else:
                    yl_scr[rs, :] = (s * row(r_ref, ib, q)).astype(bf16)
            if not defer_y:
                y = jnp.dot(yl_scr[rows_g, :], ones_blk, preferred_element_type=f32)
                for i, (ib, q) in enumerate(group):
                    yt_ref[ib, 0, q * n:(q + 1) * n, :] = jnp.where(step_lane == t, y[i * n:(i + 1) * n],
                                                                     yt_ref[ib, 0, q * n:(q + 1) * n, :])
        return carry

    lax.fori_loop(0, tb, step, 0, unroll=8)

    if defer_y:
        tiled = lambda ref: jnp.stack([ref[ib, :, q * WKV_LANES:(q + 1) * WKV_LANES] for ib, q in tiles])
        as_tiles = lambda ref: ref[...].reshape(len(tiles), n, WKV_LANES)
        yt = _block_outputs(tiled(r_ref), tiled(w_ref), tiled(k_ref), tiled(b_ref), tiled(v_ref),
                            as_tiles(s0_scr), as_tiles(zt_scr))
        for i, (ib, q) in enumerate(tiles):
            yt_ref[ib, :, q * WKV_LANES:(q + 1) * WKV_LANES] = yt[i]

    @pl.when(ti == pl.num_programs(1) - 1)
    def _():
        for ib in range(nb):
            st_ref[ib] = s_scr[ib * nq * n:(ib + 1) * nq * n, :]


def wkv_scan(r, w, k, kk, b, v, s0, nb=2):
    bsz, t, width = r.shape
    n = RWKV_HEAD_DIM
    nh = width // n
    nq = nh // WKV_QUAD
    tb = min(WKV_TB, t)
    nblk = t // tb
    to_tiles = lambda s: s.reshape(bsz, nq, WKV_QUAD, n, n).transpose(0, 1, 3, 2, 4).reshape(bsz, nq * n, WKV_LANES)
    row = pl.BlockSpec((nb, tb, width), lambda i, j: (i, j, 0))
    st = pl.BlockSpec((nb, nq * n, WKV_LANES), lambda i, j: (i, 0, 0))
    rows_all = nb * nq * n
    defer_y = tb == n
    if defer_y:
        y_spec, y_shape = row, jax.ShapeDtypeStruct((bsz, t, width), f32)
    else:
        y_spec = pl.BlockSpec((nb, 1, nq * n, WKV_LANES), lambda i, j: (i, j, 0, 0))
        y_shape = jax.ShapeDtypeStruct((bsz, nblk, nq * n, WKV_LANES), f32)
    yt, s_t = pl.pallas_call(
        functools.partial(_wkv_kernel, nb=nb, nq=nq, tb=tb, defer_y=defer_y),
        grid=(bsz // nb, nblk),
        in_specs=[row, row, row, row, row, row, st],
        out_specs=[y_spec, st],
        out_shape=[y_shape, jax.ShapeDtypeStruct((bsz, nq * n, WKV_LANES), f32)],
        scratch_shapes=[pltpu.VMEM((rows_all, WKV_LANES), f32),
                        pltpu.VMEM((rows_all, 2 * WKV_LANES), bf16),
                        pltpu.VMEM((rows_all, WKV_LANES), bf16),
                        pltpu.VMEM((rows_all, WKV_LANES), bf16),
                        pltpu.VMEM((rows_all, WKV_LANES), f32),
                        pltpu.VMEM((rows_all, WKV_LANES), f32)],
        compiler_params=_params("parallel", "arbitrary"),
        name="wkv_scan",
    )(r, w, k, kk, b, v, to_tiles(s0))
    if defer_y:
        y = yt
    else:
        y = yt.reshape(bsz, nblk, nq, n, WKV_QUAD, n).transpose(0, 1, 5, 2, 4, 3)[:, :, :tb].reshape(bsz, t, width)
    s_t = s_t.reshape(bsz, nq, n, WKV_QUAD, n).transpose(0, 1, 3, 2, 4).reshape(bsz, nh, n, n)
    return y, s_t


def _gelu_tanh(x):
    return 0.5 * x * (1.0 + jnp.tanh(0.7978845608028654 * (x + 0.044715 * x * x * x)))


def _chunk_rows(x_ref, n_chunks, row0=0, row_stride=1):
    return jnp.concatenate(
        [x_ref[pl.ds(row0 + s * row_stride, n_chunks, stride=CMP_STRIDE * row_stride), :]
         for s in range(CMP_STRIDE)], axis=1).astype(bf16)


def _compress_rows(x2, pos_ref, w1_ref, w2_ref, n_valid, n_heads):
    rows = x2.shape[0]
    n_chunks = rows // n_heads
    pab = jnp.dot(x2, w1_ref[...], preferred_element_type=f32)
    pos = jnp.dot(pos_ref[...], w1_ref[...], preferred_element_type=f32)
    posterm = pos[0:1, :HEAD_DIM] + pos[1:2, HEAD_DIM:]
    hid = pab[:, :HEAD_DIM] + pltpu.roll(pab[:, HEAD_DIM:], rows - 1, 0) + posterm
    out = jnp.dot(_gelu_tanh(hid).astype(bf16), w2_ref[...], preferred_element_type=f32)
    n = jnp.bitwise_and(lax.broadcasted_iota(jnp.int32, out.shape, 0), n_chunks - 1)
    return jnp.where(n < n_valid, out, 0.0)


def _compress_prompt_kernel(k_ref, v_ref, posk_ref, w1k_ref, w2k_ref, posv_ref, w1v_ref, w2v_ref,
                            ok_ref, ov_ref, *, n_chunks, n_valid):
    ok_ref[0, 0] = _compress_rows(_chunk_rows(k_ref.at[0], n_chunks), posk_ref, w1k_ref, w2k_ref, n_valid, 1)
    ov_ref[0, 0] = _compress_rows(_chunk_rows(v_ref.at[0], n_chunks), posv_ref, w1v_ref, w2v_ref, n_valid, 1)


def _cmp_weights(cmp_pos, cmp_w1, cmp_w2):
    ws = []
    for i in range(2):
        half = CMP_STRIDE * HEAD_DIM
        pos = _pad_to_rows(cmp_pos[i].reshape(2, half), 8).astype(bf16)
        w1 = cmp_w1[i].reshape(2, half, HEAD_DIM)
        ws += [pos, jnp.concatenate([w1[0], w1[1]], axis=1).astype(bf16), cmp_w2[i].astype(bf16)]
    return ws


_CMP_WEIGHT_SHAPES = [(8, CMP_STRIDE * HEAD_DIM), (CMP_STRIDE * HEAD_DIM, 2 * HEAD_DIM), (HEAD_DIM, HEAD_DIM)] * 2


def compress_prompt(ck, cv, cmp_pos, cmp_w1, cmp_w2):
    b, t, _ = ck.shape
    n_chunks = t // CMP_STRIDE
    n_valid = (t - CMP_BLOCK) // CMP_STRIDE + 1
    out = jax.ShapeDtypeStruct((b, NSA_KV, n_chunks, HEAD_DIM), f32)
    ospec = pl.BlockSpec((1, 1, n_chunks, HEAD_DIM), lambda i, kv: (i, kv, 0, 0))
    wspecs = [pl.BlockSpec(s, lambda i, kv: (0, 0)) for s in _CMP_WEIGHT_SHAPES]
    return pl.pallas_call(
        functools.partial(_compress_prompt_kernel, n_chunks=n_chunks, n_valid=n_valid),
        grid=(b, NSA_KV),
        in_specs=[pl.BlockSpec((1, t, HEAD_DIM), lambda i, kv: (i, 0, kv)),
                  pl.BlockSpec((1, t, HEAD_DIM), lambda i, kv: (i, 0, kv))] + wspecs,
        out_specs=[ospec, ospec],
        out_shape=[out, out],
        compiler_params=_params("parallel", "parallel"),
        name="compress_prompt",
    )(ck, cv, *_cmp_weights(cmp_pos, cmp_w1, cmp_w2))


PAGE_ROWS = PAGE_SIZE * NSA_KV


def _page_specs(n_pages, n_seq):
    return [pl.BlockSpec((PAGE_ROWS, HEAD_DIM), lambda i, pt, g=g, p=p: (pt[i * n_seq + g, p], 0))
            for g in range(n_seq) for p in range(n_pages)]


def _compress_paged_kernel(pt_ref, *refs, n_pages, n_seq, n_valid):
    np_all = n_seq * n_pages
    k_pages, v_pages = refs[:np_all], refs[np_all:2 * np_all]
    posk_ref, w1k_ref, w2k_ref, posv_ref, w1v_ref, w2v_ref, ok_ref, ov_ref = refs[2 * np_all:]
    per_page = PAGE_SIZE // CMP_STRIDE
    n_chunks = n_pages * per_page
    heads = [(g, kv) for g in range(n_seq) for kv in range(NSA_KV)]
    chunks = lambda pages: jnp.concatenate([_chunk_rows(pg, per_page, kv, NSA_KV)
                                            for g, kv in heads for pg in pages[g * n_pages:(g + 1) * n_pages]], axis=0)
    ok = _compress_rows(chunks(k_pages), posk_ref, w1k_ref, w2k_ref, n_valid, len(heads))
    ov = _compress_rows(chunks(v_pages), posv_ref, w1v_ref, w2v_ref, n_valid, len(heads))
    for i, (g, kv) in enumerate(heads):
        ok_ref[g, kv] = ok[i * n_chunks:(i + 1) * n_chunks]
        ov_ref[g, kv] = ov[i * n_chunks:(i + 1) * n_chunks]


def compress_paged(pool_k, pool_v, page_table, n_valid, cmp_pos, cmp_w1, cmp_w2):
    b, n_pages = page_table.shape
    n_seq = SEQ_PER_STEP if b % SEQ_PER_STEP == 0 else 1
    n_chunks = n_pages * PAGE_SIZE // CMP_STRIDE
    out = jax.ShapeDtypeStruct((b, NSA_KV, n_chunks, HEAD_DIM), f32)
    ospec = pl.BlockSpec((n_seq, NSA_KV, n_chunks, HEAD_DIM), lambda i, pt: (i, 0, 0, 0))
    wspecs = [pl.BlockSpec(s, lambda i, pt: (0, 0)) for s in _CMP_WEIGHT_SHAPES]
    np_all = n_seq * n_pages
    return pl.pallas_call(
        functools.partial(_compress_paged_kernel, n_pages=n_pages, n_seq=n_seq, n_valid=n_valid),
        grid_spec=pltpu.PrefetchScalarGridSpec(
            num_scalar_prefetch=1,
            grid=(b // n_seq,),
            in_specs=_page_specs(n_pages, n_seq) * 2 + wspecs,
            out_specs=[ospec, ospec]),
        out_shape=[out, out],
        compiler_params=_params("parallel"),
        name="compress_paged",
    )(page_table, *([pool_k] * np_all), *([pool_v] * np_all), *_cmp_weights(cmp_pos, cmp_w1, cmp_w2))


def _stack_heads(x):
    return jnp.concatenate([x[:, g * HEAD_DIM:(g + 1) * HEAD_DIM] for g in range(NSA_GROUP)], axis=0)


def _dot_nt(a, b):
    return lax.dot_general(a, b, (((1,), (1,)), ((), ())), preferred_element_type=f32)


def _softmax_heads(s, bias, tq):
    s3 = s.reshape(NSA_GROUP, tq, s.shape[1]) + bias[None]
    e = jnp.exp(s3 - jnp.max(s3, axis=-1, keepdims=True))
    return e, jnp.sum(e, axis=-1, keepdims=True)


def _compressed_branch(qc, ckc, cvc, pos_t, n_cmp, tq):
    s = _dot_nt(qc, ckc)
    n = lax.broadcasted_iota(jnp.int32, (tq, s.shape[1]), 1)
    vis = (n * CMP_STRIDE + (CMP_BLOCK - 1) <= pos_t) & (n < n_cmp)
    e, denom = _softmax_heads(s, jnp.where(vis, 0.0, NEG_INF), tq)
    any_vis = jnp.where(pos_t >= CMP_BLOCK - 1, 1.0, 0.0)
    p = e / denom * any_vis[None]
    o = jnp.dot(p.reshape(s.shape).astype(bf16), cvc, preferred_element_type=f32)
    return o, jnp.sum(p, axis=0)


def _select_blocks(psum, tq, pos0, n_slc, n_j):
    if tq < SEL_LANES:
        psum = jnp.concatenate([psum, jnp.zeros((SEL_LANES - tq, psum.shape[1]), f32)], axis=0)
    n_c = psum.shape[1]
    j = lax.broadcasted_iota(jnp.int32, (n_j, n_c), 0)
    cs = lax.broadcasted_iota(jnp.int32, (n_j, n_c), 1) * CMP_STRIDE
    overlap = jnp.where((cs < j * SLC_BLOCK + SLC_BLOCK) & (cs + (CMP_BLOCK - 1) >= j * SLC_BLOCK), 1.0, 0.0)
    imp_t = lax.dot_general(overlap, psum, (((1,), (1,)), ((), ())),
                            preferred_element_type=f32, precision=lax.Precision.HIGHEST)
    j = lax.broadcasted_iota(jnp.int32, imp_t.shape, 0)
    pos_t = pos0 + lax.broadcasted_iota(jnp.int32, imp_t.shape, 1)
    cur = lax.shift_right_logical(pos_t, SLC_SHIFT)
    causal = j * SLC_BLOCK <= pos_t
    forced = (j == 0) | (j == cur) | (j == cur - 1)
    score = jnp.where(causal, jnp.where(forced, FORCE_SCORE, imp_t), -FORCE_SCORE)
    score = jnp.where(j < n_slc, score, -2.0 * FORCE_SCORE)
    rank = jnp.zeros(imp_t.shape, f32)
    for jp in range(n_slc):
        row = score[jp:jp + 1, :]
        ahead = (row > score) | ((row == score) & (j > jp))
        rank = rank + jnp.where(ahead, 1.0, 0.0)
    sel_t = jnp.where(rank < min(N_SELECT, n_slc), 1.0, 0.0)
    return sel_t.T[0:tq]


def _selection_bias(sel, key0, n_keys):
    nj = sel.shape[1]
    j = lax.broadcasted_iota(jnp.int32, (nj, n_keys), 0)
    kpos = key0 + lax.broadcasted_iota(jnp.int32, (nj, n_keys), 1)
    e = jnp.where(lax.shift_right_logical(kpos, SLC_SHIFT) == j, 1.0, 0.0).astype(bf16)
    return jnp.dot(jnp.where(sel > 0.5, 0.0, NEG_INF).astype(bf16), e, preferred_element_type=f32)


def _window_branch(qr, wk, wv, kpos0, n_keys_valid, pos_t, tq, n_phantom=None):
    s = _dot_nt(qr, wk)
    lane = lax.broadcasted_iota(jnp.int32, (tq, s.shape[1]), 1)
    kpos = kpos0 + lane
    valid = (kpos <= pos_t) & (pos_t - kpos < WINDOW) & (lane < n_keys_valid)
    s3 = s.reshape(NSA_GROUP, tq, s.shape[1]) + jnp.where(valid, 0.0, NEG_INF)[None]
    m = jnp.max(s3, axis=-1, keepdims=True)
    if n_phantom is not None:
        m = jnp.where(n_phantom[None] > 0.0, jnp.maximum(m, 0.0), m)
    e = jnp.exp(s3 - m)
    denom = jnp.sum(e, axis=-1, keepdims=True)
    if n_phantom is not None:
        denom = denom + n_phantom[None] * jnp.exp(-m)
    return jnp.dot((e / denom).reshape(s.shape).astype(bf16), wv, preferred_element_type=f32)


def _gated_sum(gate, cols, tq, o_cmp, o_slc, o_win, g):
    r = slice(g * tq, (g + 1) * tq)
    c, s, w = cols[0] + g, cols[1] + g, cols[2] + g
    return gate[:, c:c + 1] * o_cmp[r] + gate[:, s:s + 1] * o_slc[r] + gate[:, w:w + 1] * o_win[r]


QK_SCALE = HEAD_DIM ** -0.5


def _nsa_prompt_kernel(q_ref, c2_ref, s2_ref, gate_ref, ckc_ref, cvc_ref, sk_ref, sv_ref, wk_ref, wv_ref,
                       o_ref, *, n_cmp, n_slc):
    i = pl.program_id(2)
    tq = Q_BLOCK
    rows = NSA_GROUP * tq
    q0 = i * tq
    q = q_ref[0] * QK_SCALE
    qc = _stack_heads(q).astype(bf16)
    qr = _rope_heads(q, c2_ref[...], s2_ref[...]).astype(bf16)
    pos_t = q0 + lax.broadcasted_iota(jnp.int32, (tq, 1), 0)

    o_cmp, psum = _compressed_branch(qc, ckc_ref[0, 0].astype(bf16), cvc_ref[0, 0].astype(bf16), pos_t, n_cmp, tq)
    sel = _select_blocks(psum, tq, q0, n_slc, n_slc)

    def slc_step(c, carry, causal):
        m, l, acc = carry
        k0 = pl.multiple_of(c * SLC_CHUNK, SLC_CHUNK)
        bias = _selection_bias(sel, k0, SLC_CHUNK)
        if causal:
            kpos = k0 + lax.broadcasted_iota(jnp.int32, bias.shape, 1)
            bias = jnp.where(kpos <= pos_t, bias, NEG_INF)
        s3 = _dot_nt(qr, sk_ref[0, pl.ds(k0, SLC_CHUNK), :]).reshape(NSA_GROUP, tq, SLC_CHUNK) + bias[None]
        m_new = jnp.maximum(m, jnp.max(s3, axis=-1, keepdims=True))
        alpha = jnp.exp(m - m_new)
        e = jnp.exp(s3 - m_new)
        l = alpha * l + jnp.sum(e, axis=-1, keepdims=True)
        pv = jnp.dot(e.reshape(rows, SLC_CHUNK).astype(bf16), sv_ref[0, pl.ds(k0, SLC_CHUNK), :],
                     preferred_element_type=f32)
        return m_new, l, alpha * acc + pv.reshape(NSA_GROUP, tq, HEAD_DIM)

    c_last = q0 // SLC_CHUNK
    init = (jnp.full((NSA_GROUP, tq, 1), NEG_INF, f32), jnp.zeros((NSA_GROUP, tq, 1), f32),
            jnp.zeros((NSA_GROUP, tq, HEAD_DIM), f32))
    carry = lax.fori_loop(0, c_last, functools.partial(slc_step, causal=False), init)
    _, l, acc = slc_step(c_last, carry, causal=True)
    o_slc = (acc / l).reshape(rows, HEAD_DIM)

    span = WINDOW + tq
    w0 = pl.multiple_of(jnp.maximum(q0 - WINDOW, 0), tq)
    n_phantom = jnp.maximum(WINDOW - 1 - pos_t, 0).astype(f32)
    o_win = _window_branch(qr, wk_ref[0, pl.ds(w0, span), :], wv_ref[0, pl.ds(w0, span), :], w0, span, pos_t, tq,
                           n_phantom)

    gate = _sigmoid(gate_ref[0])
    first_kv = pl.program_id(1) == 0
    for g in range(NSA_GROUP):
        head = [_gated_sum(gate, tuple(br * NSA_HEADS + kv * NSA_GROUP for br in range(3)), tq, o_cmp, o_slc, o_win, g)
                for kv in range(NSA_KV)]
        o_ref[0, :, g * HEAD_DIM:(g + 1) * HEAD_DIM] = jnp.where(first_kv, head[0], head[1])


def nsa_prompt(proj, gate_col, ckc, cvc, kv_bf, c2, s2, n_cmp):
    b, t, _ = proj.shape
    assert t % SLC_CHUNK == 0 and t >= WINDOW + Q_BLOCK
    n_slc = t // SLC_BLOCK
    gw = NSA_GROUP * HEAD_DIM
    kvcol = lambda c: pl.BlockSpec((1, t, HEAD_DIM), lambda bi, kv, i, c=c: (bi, 0, c * NSA_KV + kv))
    cmp_spec = pl.BlockSpec((1, 1, ckc.shape[2], HEAD_DIM), lambda bi, kv, i: (bi, kv, 0, 0))
    return pl.pallas_call(
        functools.partial(_nsa_prompt_kernel, n_cmp=n_cmp, n_slc=n_slc),
        grid=(b, NSA_KV, t // Q_BLOCK),
        in_specs=[pl.BlockSpec((1, Q_BLOCK, gw), lambda bi, kv, i: (bi, i, kv)),
                  pl.BlockSpec((Q_BLOCK, HEAD_DIM), lambda bi, kv, i: (i, 0)),
                  pl.BlockSpec((Q_BLOCK, HEAD_DIM), lambda bi, kv, i: (i, 0)),
                  pl.BlockSpec((1, Q_BLOCK, LANES), lambda bi, kv, i: (bi, i, gate_col)),
                  cmp_spec, cmp_spec, kvcol(2), kvcol(3), kvcol(4), kvcol(5)],
        out_specs=pl.BlockSpec((1, Q_BLOCK, gw), lambda bi, kv, i: (bi, i, kv)),
        out_shape=jax.ShapeDtypeStruct((b, t, NSA_WIDTH), f32),
        compiler_params=_params("parallel", "parallel", "arbitrary"),
        name="nsa_prompt",
    )(proj, c2, s2, proj, ckc, cvc, kv_bf, kv_bf, kv_bf, kv_bf)


def _pad_rows(x, n):
    return jnp.concatenate([x, jnp.zeros((n - x.shape[0], x.shape[1]), x.dtype)], axis=0)


def _nsa_sample_kernel(pt_ref, *refs, n_pages, n_seq, n_cmp, n_slc, n_j, past, tq):
    np_all = n_seq * n_pages
    k_pages, v_pages = refs[:np_all], refs[np_all:2 * np_all]
    (q_ref, c2_ref, s2_ref, gate_ref, ckc_ref, cvc_ref, nsk_ref, nsv_ref, nwk_ref, nwv_ref,
     wink_ref, winv_ref, o_ref, owk_ref, owv_ref) = refs[2 * np_all:]
    pos_t = past + lax.broadcasted_iota(jnp.int32, (tq, 1), 0)
    lw2 = wink_ref.shape[0] // n_seq
    lw = lw2 // NSA_KV
    c2, s2 = c2_ref[...], s2_ref[...]
    for g in range(n_seq):
        gate = _sigmoid(gate_ref[g])
        keep = lw2 - tq * NSA_KV
        for cache_ref, new_ref, out_ref in ((wink_ref, nwk_ref, owk_ref), (winv_ref, nwv_ref, owv_ref)):
            out_ref[g * lw2:g * lw2 + keep, :] = cache_ref[g * lw2 + tq * NSA_KV:(g + 1) * lw2, :]
            for kv in range(NSA_KV):
                out_ref[pl.ds(g * lw2 + keep + kv, tq, stride=NSA_KV), :] = new_ref[g, :, kv * HEAD_DIM:(kv + 1) * HEAD_DIM]
        for kv in range(NSA_KV):
            ksl = slice(kv * HEAD_DIM, (kv + 1) * HEAD_DIM)
            q = q_ref[g, :, kv * NSA_GROUP * HEAD_DIM:(kv + 1) * NSA_GROUP * HEAD_DIM] * QK_SCALE
            qc = _stack_heads(q).astype(bf16)
            qr = _rope_heads(q, c2, s2).astype(bf16)
            o_cmp, psum = _compressed_branch(qc, ckc_ref[g, kv].astype(bf16), cvc_ref[g, kv].astype(bf16), pos_t,
                                             n_cmp, tq)
            sel = _select_blocks(psum, tq, past, n_slc, n_j)

            paged = lambda pages: [pg[pl.ds(kv, PAGE_SIZE, stride=NSA_KV), :] for pg in pages[g * n_pages:(g + 1) * n_pages]]
            sk = jnp.concatenate(paged(k_pages) + [_pad_rows(nsk_ref[g, :, ksl], LANES)], axis=0).astype(bf16)
            sv = jnp.concatenate(paged(v_pages) + [_pad_rows(nsv_ref[g, :, ksl], LANES)], axis=0).astype(bf16)
            n_keys = sk.shape[0]
            bias = _selection_bias(sel, 0, n_keys)
            bias = jnp.where(lax.broadcasted_iota(jnp.int32, bias.shape, 1) <= pos_t, bias, NEG_INF)
            e, denom = _softmax_heads(_dot_nt(qr, sk), bias, tq)
            o_slc = jnp.dot((e / denom).reshape(NSA_GROUP * tq, n_keys).astype(bf16), sv, preferred_element_type=f32)

            cached = lambda ref: ref[pl.ds(g * lw2 + kv, lw, stride=NSA_KV), :]
            wk = jnp.concatenate([cached(wink_ref), _pad_rows(nwk_ref[g, :, ksl], LANES)], axis=0).astype(bf16)
            wv = jnp.concatenate([cached(winv_ref), _pad_rows(nwv_ref[g, :, ksl], LANES)], axis=0).astype(bf16)
            o_win = _window_branch(qr, wk, wv, past - lw, lw + tq, pos_t, tq)

            cols = tuple(br * NSA_HEADS + kv * NSA_GROUP for br in range(3))
            for hg in range(NSA_GROUP):
                hd = kv * NSA_GROUP + hg
                o_ref[g, :, hd * HEAD_DIM:(hd + 1) * HEAD_DIM] = _gated_sum(gate, cols, tq, o_cmp, o_slc, o_win, hg)


def nsa_sample(proj, gate_col, ckc, cvc, pool_k, pool_v, page_table, new_rows, win_k, win_v, c2, s2, n_cmp, past):
    b, tq, _ = proj.shape
    n_pages = page_table.shape[1]
    assert past == n_pages * PAGE_SIZE and past % SLC_BLOCK == 0 and tq <= SLC_BLOCK and (tq * NSA_KV) % 8 == 0
    n_seq = SEQ_PER_STEP if b % SEQ_PER_STEP == 0 else 1
    n_slc = past // SLC_BLOCK + 1
    n_j = -(-n_slc // SLC_BLOCK) * SLC_BLOCK
    lw2 = win_k.shape[0] // b
    per_b = lambda shape: pl.BlockSpec((n_seq,) + shape, lambda i, pt: (i,) + (0,) * len(shape))
    tab = pl.BlockSpec((tq, HEAD_DIM), lambda i, pt: (0, 0))
    win = pl.BlockSpec((n_seq * lw2, HEAD_DIM), lambda i, pt: (i, 0))
    np_all = n_seq * n_pages
    return pl.pallas_call(
        functools.partial(_nsa_sample_kernel, n_pages=n_pages, n_seq=n_seq, n_cmp=n_cmp, n_slc=n_slc, n_j=n_j,
                          past=past, tq=tq),
        grid_spec=pltpu.PrefetchScalarGridSpec(
            num_scalar_prefetch=1,
            grid=(b // n_seq,),
            in_specs=_page_specs(n_pages, n_seq) * 2 + [
                per_b((tq, NSA_WIDTH)), tab, tab,
                pl.BlockSpec((n_seq, tq, LANES), lambda i, pt: (i, 0, gate_col)),
                per_b((NSA_KV, ckc.shape[2], HEAD_DIM)), per_b((NSA_KV, ckc.shape[2], HEAD_DIM))]
                + [per_b((tq, KV_COLS))] * 4 + [win, win],
            out_specs=[per_b((tq, NSA_WIDTH)), win, win]),
        out_shape=[jax.ShapeDtypeStruct((b, tq, NSA_WIDTH), f32),
                   jax.ShapeDtypeStruct(win_k.shape, f32), jax.ShapeDtypeStruct(win_v.shape, f32)],
        compiler_params=_params("parallel"),
        name="nsa_sample",
    )(page_table, *([pool_k] * np_all), *([pool_v] * np_all), proj, c2, s2, proj, ckc, cvc, *new_rows, win_k, win_v)


def _prev_rows(x, tile, halo_ref, first_ref, seq_len, halo_fn=lambda rows: rows):
    tm, c = x.shape
    prev = pltpu.roll(x, 1, 0)
    row = lax.broadcasted_iota(jnp.int32, (tm, 1), 0)
    if seq_len >= tm:
        tiles_per_seq = seq_len // tm
        first = first_ref[pl.ds(tile // tiles_per_seq, 1), :]
        edge = jnp.where(tile % tiles_per_seq == 0, first, halo_fn(halo_ref[...])[7:8, :])
        return jnp.where(row == 0, edge, prev)
    pieces = []
    for j in range(tm // seq_len):
        pieces.append(jnp.broadcast_to(first_ref[j:j + 1, :], (8, c)))
        if seq_len > 8:
            pieces.append(jnp.zeros((seq_len - 8, c), f32))
    return jnp.where(jnp.bitwise_and(row, seq_len - 1) == 0, jnp.concatenate(pieces, axis=0), prev)


def _shift_specs(m, c, tm, seq_len, n_seq, col=None):
    cb = (lambda *g: 0) if col is None else col
    tile = pl.BlockSpec((tm, c), lambda *g: (g[0], cb(*g)))
    halo = pl.BlockSpec((8, c), lambda *g: (jnp.maximum(g[0] * (tm // 8) - 1, 0), cb(*g)))
    if seq_len >= tm:
        first = pl.BlockSpec((-(-n_seq // 8) * 8, c), lambda *g: (0, cb(*g)))
    else:
        first = pl.BlockSpec((tm // seq_len, c), lambda *g: (g[0], cb(*g)))
    return tile, halo, first


def _pad_first(first, seq_len, tm):
    return _pad_to_rows(first, -(-first.shape[0] // 8) * 8) if seq_len >= tm else first


def _head_sums(x):
    ri = lax.broadcasted_iota(jnp.int32, (WKV_LANES, WKV_LANES), 0)
    ci = lax.broadcasted_iota(jnp.int32, (WKV_LANES, WKV_LANES), 1)
    ones_blk = jnp.where(lax.shift_right_logical(ri, 6) == lax.shift_right_logical(ci, 6), 1.0, 0.0).astype(bf16)
    hi = x.astype(bf16)
    lo = (x - hi.astype(f32)).astype(bf16)
    out = []
    for c in range(x.shape[1] // WKV_LANES):
        sl = slice(c * WKV_LANES, (c + 1) * WKV_LANES)
        out.append(jnp.dot(hi[:, sl], ones_blk, preferred_element_type=f32)
                   + jnp.dot(lo[:, sl], ones_blk, preferred_element_type=f32))
    return jnp.concatenate(out, axis=1)


def _lora_kernel(h_ref, halo_ref, first_ref, gn_ref, mu_ref, wd1_ref, wa1_ref, wg1_ref, wd2_ref, wa2_ref, wg2_ref,
                 w0_ref, a0_ref, decay_ref, a_ref, g_ref, *, seq_len):
    norm = lambda rows: _rms(rows, gn_ref[...])
    hn = norm(h_ref[...])
    xx = _prev_rows(hn, pl.program_id(0), halo_ref, first_ref, seq_len, norm) - hn
    mix = lambda r: (hn + xx * mu_ref[r:r + 1, :]).astype(bf16)
    dot = lambda x, w_ref: jnp.dot(x, w_ref[...], preferred_element_type=f32)
    w_raw = w0_ref[...] + dot(jnp.tanh(dot(mix(0), wd1_ref)).astype(bf16), wd2_ref)
    softplus = jnp.maximum(-w_raw, 0.0) + jnp.log(1.0 + jnp.exp(-jnp.abs(w_raw)))
    decay_ref[...] = jnp.exp(-jnp.exp(-softplus - 0.5))
    a_ref[...] = _sigmoid(a0_ref[...] + dot(dot(mix(1), wa1_ref).astype(bf16), wa2_ref))
    g_ref[...] = dot(_sigmoid(dot(mix(2), wg1_ref)).astype(bf16), wg2_ref)


def rwkv_lora(h, x_prev, seq_len, P):
    m, d = h.shape
    rw = P['w0'].shape[0]
    tm = min(ROW_TILE, m)
    pad128 = lambda w: _pad_cols(w, -(-w.shape[1] // LANES) * LANES).astype(bf16)
    w1s = [pad128(P[k]) for k in ('w_decay1', 'w_aaa1', 'w_gate1')]
    w2s = [_pad_to_rows(P[k], w1.shape[1]).astype(bf16) for k, w1 in zip(('w_decay2', 'w_aaa2', 'w_gate2'), w1s)]
    full = lambda x: pl.BlockSpec(x.shape, lambda i: (0, 0))
    mu = _pad_to_rows(P['mu_wag'], 8)
    gn = P['g_mix_pre'].reshape(1, d)
    vecs = [P['w0'].reshape(1, rw), P['a0'].reshape(1, rw)]
    out = jax.ShapeDtypeStruct((m, rw), f32)
    ospec = pl.BlockSpec((tm, rw), lambda i: (i, 0))
    return pl.pallas_call(
        functools.partial(_lora_kernel, seq_len=seq_len),
        grid=(m // tm,),
        in_specs=list(_shift_specs(m, d, tm, seq_len, x_prev.shape[0])) + [full(gn), full(mu)]
                 + [full(w) for w in w1s + w2s + vecs],
        out_specs=[ospec] * 3,
        out_shape=[out] * 3,
        compiler_params=_params("parallel"),
        name="rwkv_lora",
    )(h, h, _pad_first(x_prev, seq_len, tm), gn, mu, *w1s, *w2s, *vecs)


RKV_GROUPS = 3
IN_COL_TILE = 512


def _rwkv_in_kernel(h_ref, halo_ref, first_ref, g_ref, w_ref, mu_ref, a_ref, kkw_ref, kaw_ref,
                    r_ref, k_ref, kk_ref, b_ref, v_ref, mq_ref, xn_scr, edge_scr, *, seq_len, per_group):
    i, j = pl.program_id(0), pl.program_id(1)
    tm, c = r_ref.shape

    @pl.when(j == 0)
    def _():
        xn_scr[...] = _rms(h_ref[...], g_ref[...]).astype(bf16)
        edge_scr[0:8, :] = _rms(halo_ref[...], g_ref[...])
        edge_scr[8:, :] = first_ref[...]

    cur = jnp.dot(xn_scr[...], w_ref[...], preferred_element_type=f32)
    edge = jnp.dot(edge_scr[...].astype(bf16), w_ref[...], preferred_element_type=f32)
    halo, first = edge[0:8], edge[8:]
    prev = pltpu.roll(cur, 1, 0)
    row = lax.broadcasted_iota(jnp.int32, (tm, 1), 0)
    if seq_len >= tm:
        tiles_per_seq = seq_len // tm
        seq = lax.broadcasted_iota(jnp.int32, (first.shape[0], 1), 0) == i // tiles_per_seq
        first_row = jnp.sum(jnp.where(seq, first, 0.0), axis=0, keepdims=True)
        edge_row = jnp.where(i % tiles_per_seq == 0, first_row, halo[7:8])
        prev = jnp.where(row == 0, edge_row, prev)
    else:
        pieces = []
        for q in range(tm // seq_len):
            pieces.append(jnp.broadcast_to(first[q:q + 1], (8, c)))
            if seq_len > 8:
                pieces.append(jnp.zeros((seq_len - 8, c), f32))
        prev = jnp.where(jnp.bitwise_and(row, seq_len - 1) == 0, jnp.concatenate(pieces, axis=0), prev)
    x = cur + (prev - cur) * mu_ref[...]

    @pl.when(j < per_group)
    def _():
        r_ref[...] = x

    @pl.when((j >= per_group) & (j < 2 * per_group))
    def _():
        a = a_ref[...]
        kk = x * kkw_ref[...]
        kk = kk * lax.rsqrt(_head_sums(kk * kk) + 1e-12)
        kk_ref[...] = kk
        b_ref[...] = kk * a
        k_ref[...] = x * (1.0 + (a - 1.0) * kaw_ref[...])

    @pl.when((j >= 2 * per_group) & (j < 3 * per_group))
    def _():
        v_ref[...] = x

    @pl.when(j >= 3 * per_group)
    def _():
        mq_ref[...] = cur


def rwkv_in(h, x_prev, a_rate, seq_len, P):
    m, d = h.shape
    rw = a_rate.shape[1]
    c = IN_COL_TILE
    tm = min(ROW_TILE, m)
    per_group = rw // c
    n_col = RKV_GROUPS * per_group + MEM_WIDTH // c
    assert rw % c == 0 and MEM_WIDTH == c and P['w_in_a'].shape[1] == n_col * c
    tile, halo, first = _shift_specs(m, d, tm, seq_len, x_prev.shape[0])
    n_first = first.block_shape[0]
    mu = jnp.pad(P['mu_rkv'], (0, MEM_WIDTH)).reshape(1, -1)
    grp = lambda base: (lambda i, j: (i, jnp.clip(j - base * per_group, 0, per_group - 1)))
    vec = lambda base: (lambda i, j: (0, jnp.clip(j - base * per_group, 0, per_group - 1)))
    wide = jax.ShapeDtypeStruct((m, rw), f32)
    return pl.pallas_call(
        functools.partial(_rwkv_in_kernel, seq_len=seq_len, per_group=per_group),
        grid=(m // tm, n_col),
        in_specs=[pl.BlockSpec((tm, d), lambda i, j: (i, 0)),
                  pl.BlockSpec((8, d), lambda i, j: (jnp.maximum(i * (tm // 8) - 1, 0), 0)),
                  pl.BlockSpec(first.block_shape, lambda i, j: (i if seq_len < tm else 0, 0)),
                  pl.BlockSpec((1, d), lambda i, j: (0, 0)),
                  pl.BlockSpec((d, c), lambda i, j: (0, j)),
                  pl.BlockSpec((1, c), lambda i, j: (0, j)),
                  pl.BlockSpec((tm, c), grp(1)),
                  pl.BlockSpec((1, c), vec(1)),
                  pl.BlockSpec((1, c), vec(1))],
        out_specs=[pl.BlockSpec((tm, c), grp(0)), pl.BlockSpec((tm, c), grp(1)), pl.BlockSpec((tm, c), grp(1)),
                   pl.BlockSpec((tm, c), grp(1)), pl.BlockSpec((tm, c), grp(2)),
                   pl.BlockSpec((tm, c), lambda i, j: (i, 0))],
        out_shape=[wide] * 5 + [jax.ShapeDtypeStruct((m, MEM_WIDTH), f32)],
        scratch_shapes=[pltpu.VMEM((tm, d), bf16), pltpu.VMEM((8 + n_first, d), f32)],
        compiler_params=_params("parallel", "arbitrary"),
        name="rwkv_in",
    )(h, h, _pad_first(x_prev, seq_len, tm), P['g_mix_pre'].reshape(1, d), P['w_in_a'].astype(bf16), mu, a_rate,
      P['k_k'].reshape(1, rw), P['k_a'].reshape(1, rw))


def _rwkv_out_kernel(y_ref, r_ref, k_ref, v_ref, g_ref, om_ref, h_ref, lw_ref, lb_ref, rk_ref, w_ref, gp_ref, o_ref):
    inv_n = 1.0 / RWKV_HEAD_DIM
    y = y_ref[...]
    d = y - _head_sums(y) * inv_n
    var = _head_sums(d * d) * inv_n
    yn = d * lax.rsqrt(var + GN_EPS) * lw_ref[...] + lb_ref[...]
    bonus = _head_sums(r_ref[...] * k_ref[...] * rk_ref[...]) * v_ref[...]
    o = ((yn + bonus) * g_ref[...]).astype(bf16)
    rw = o.shape[1]
    acc = jnp.dot(o, w_ref[:rw, :], preferred_element_type=f32)
    acc += jnp.dot(om_ref[...].astype(bf16), w_ref[rw:, :], preferred_element_type=f32)
    o_ref[...] = h_ref[...] + _rms(acc, gp_ref[...])


def rwkv_out(y, r, k, v, gate, o_mem, h, P):
    m, rw = y.shape
    d = h.shape[1]
    tm = min(ROW_TILE // 2, m)
    wide = pl.BlockSpec((tm, rw), lambda i: (i, 0))
    vec = pl.BlockSpec((1, rw), lambda i: (0, 0))
    return pl.pallas_call(
        _rwkv_out_kernel,
        grid=(m // tm,),
        in_specs=[wide] * 5 + [pl.BlockSpec((tm, o_mem.shape[1]), lambda i: (i, 0)),
                               pl.BlockSpec((tm, d), lambda i: (i, 0)), vec, vec, vec,
                               pl.BlockSpec((rw + o_mem.shape[1], d), lambda i: (0, 0)),
                               pl.BlockSpec((1, d), lambda i: (0, 0))],
        out_specs=pl.BlockSpec((tm, d), lambda i: (i, 0)),
        out_shape=jax.ShapeDtypeStruct((m, d), f32),
        compiler_params=_params("parallel"),
        name="rwkv_out",
    )(y, r, k, v, gate, o_mem, h, P['lnx_w'].reshape(1, rw), P['lnx_b'].reshape(1, rw), P['r_k'].reshape(1, rw),
      P['w_out_a'].astype(bf16), P['g_mix_post'].reshape(1, d))


def _rope_tables(pos):
    half = HEAD_DIM // 2
    inv = jnp.power(ROPE_THETA, -jnp.arange(half, dtype=f32) / half)
    ang = pos.astype(f32)[:, None] * inv[None, :]
    cos, sin = jnp.cos(ang), jnp.sin(ang)
    return jnp.concatenate([cos, cos], axis=-1), jnp.concatenate([-sin, sin], axis=-1)


def _pad_cols(w, n):
    return jnp.pad(w, ((0, 0), (0, n - w.shape[1])))


def _pad_to_rows(x, n):
    return jnp.pad(x, ((0, n - x.shape[0]), (0, 0)))


def rwkv_mem_layer(h, x_prev, s0, mem, P):
    b, t, d = h.shape
    m = b * t
    assert t & (t - 1) == 0 and t % 8 == 0 and (t % ROW_TILE == 0 or ROW_TILE % t == 0)
    h2 = h.reshape(m, d)
    rw = P['w0'].shape[0]
    last = rmsnorm(_pad_to_rows(h[:, -1], -(-b // 8) * 8), P['g_mix_pre'])[:b]

    decay, a_rate, gate = rwkv_lora(h2, x_prev, t, P)
    r, k, kk, kb, v, mq = rwkv_in(h2, x_prev, a_rate, t, P)
    as3 = lambda x: x.reshape(b, t, -1)
    y, s_t = wkv_scan(as3(r), as3(decay), as3(k), as3(kk), as3(kb), as3(v), s0)
    o_mem = mem(as3(mq), 0)
    h2 = rwkv_out(y.reshape(m, rw), r, k, v, gate, o_mem.reshape(m, MEM_WIDTH), h2, P)
    h2 = ffn_residual(h2, P['g_ffn_pre'], P['w_ff1'], P['w_ff2'], P['g_ffn_post'])
    return h2.reshape(b, t, d), s_t, last


GATE_COL = (NSA_WIDTH + MEM_WIDTH) // LANES


def nsa_mem_layer(h, mem, P, attend):
    b, t, d = h.shape
    m = b * t
    h2 = h.reshape(m, d)
    n_in = (GATE_COL + 1) * LANES
    n_in = -(-n_in // 768) * 768
    proj = matmul(h2, _pad_cols(P['w_in_b'].astype(bf16), n_in), g=P['g_mix_pre'], tn=768).reshape(b, t, -1)
    o_nsa = attend(proj)
    o_mem = mem(proj, NSA_WIDTH // MEM_WIDTH)
    h2 = out_proj_residual(o_nsa.reshape(m, NSA_WIDTH), o_mem.reshape(m, MEM_WIDTH), 0, P['w_out_b'], h2,
                           P['g_mix_post'])
    h2 = ffn_residual(h2, P['g_ffn_pre'], P['w_ff1'], P['w_ff2'], P['g_ffn_post'])
    return h2.reshape(b, t, d)


def kernel(x_prompt, x_sample, mem_prompt, cache_mem_k, cache_mem_v, state_wkv, state_shift, cache_cmp_k, cache_cmp_v, cache_slc_k, cache_slc_v, cache_win_k, cache_win_v, page_table, g_mix_pre, g_mix_post, g_ffn_pre, g_ffn_post, g_mem, w_mem_k, w_mem_v, w_in_a, mu_rkv, mu_wag, w0, w_decay1, w_decay2, a0, w_aaa1, w_aaa2, w_gate1, w_gate2, k_k, k_a, r_k, lnx_w, lnx_b, w_out_a, g_kv, w_kv, cmp_pos, cmp_w1, cmp_w2, w_in_b, w_out_b, w_ff1, w_ff2):
    bp, tp, d = x_prompt.shape
    bs, ts, _ = x_sample.shape
    depth = g_mix_pre.shape[0]
    assert depth == 2 and w_in_a.shape[0] == 1 and w_in_b.shape[0] == 1
    n_pages = page_table.shape[1]
    past = n_pages * PAGE_SIZE
    mem_len = mem_prompt.shape[1]

    P0 = dict(g_mix_pre=g_mix_pre[0], g_mix_post=g_mix_post[0], g_ffn_pre=g_ffn_pre[0], g_ffn_post=g_ffn_post[0],
              w_in_a=w_in_a[0], mu_rkv=mu_rkv[0], mu_wag=mu_wag[0], w0=w0[0], w_decay1=w_decay1[0],
              w_decay2=w_decay2[0], a0=a0[0], w_aaa1=w_aaa1[0], w_aaa2=w_aaa2[0], w_gate1=w_gate1[0],
              w_gate2=w_gate2[0], k_k=k_k[0], k_a=k_a[0], r_k=r_k[0], lnx_w=lnx_w[0], lnx_b=lnx_b[0],
              w_out_a=w_out_a[0], w_ff1=w_ff1[0], w_ff2=w_ff2[0])
    P1 = dict(g_mix_pre=g_mix_pre[1], g_mix_post=g_mix_post[1], g_ffn_pre=g_ffn_pre[1], g_ffn_post=g_ffn_post[1],
              w_in_b=w_in_b[0], w_out_b=w_out_b[0], w_ff1=w_ff1[1], w_ff2=w_ff2[1])
    rows4 = lambda x, bsz: x.reshape(bsz, -1, NSA_KV, HEAD_DIM)

    mem2 = mem_prompt.reshape(bp * mem_len, d)
    mkv = [matmul(mem2, jnp.concatenate([w_mem_k[l], w_mem_v[l]], axis=1), g=g_mem[l]).reshape(bp, mem_len, -1)
           for l in range(depth)]
    mem_k_p = jnp.stack([x[..., :MEM_WIDTH] for x in mkv])
    mem_v_p = jnp.stack([x[..., MEM_WIDTH:] for x in mkv])
    mem_p = lambda l: (lambda q, q_col: mem_attention(q, q_col, mkv[l], mkv[l], k_col=0, v_col=1))

    nh = w0.shape[1] // RWKV_HEAD_DIM
    shift0 = jnp.zeros((bp, d), f32)
    wkv0 = jnp.zeros((bp, nh, RWKV_HEAD_DIM, RWKV_HEAD_DIM), f32)
    h, wkv_p, shift_p = rwkv_mem_layer(x_prompt, shift0, wkv0, mem_p(0), P0)

    c2p, s2p = _rope_tables(jnp.arange(tp, dtype=jnp.int32))
    rows_p, kv_bf = kv_proj(h.reshape(bp * tp, d), g_kv, w_kv, c2p, s2p)
    as_p = lambda x: x.reshape(bp, tp, -1)
    ckc, cvc = compress_prompt(as_p(rows_p[0]), as_p(rows_p[1]), cmp_pos, cmp_w1, cmp_w2)
    n_cmp_p = (tp - CMP_BLOCK) // CMP_STRIDE + 1

    def attend_prompt(proj):
        return nsa_prompt(proj, GATE_COL, ckc, cvc, as_p(kv_bf), c2p, s2p, n_cmp_p)

    y_p = nsa_mem_layer(h, mem_p(1), P1, attend_prompt)
    cmp_k_p, cmp_v_p, slc_k_p, slc_v_p, win_k_p, win_v_p = [rows4(x, bp) for x in rows_p]
    n_keep = min(WINDOW, tp)
    win_k_p, win_v_p = win_k_p[:, tp - n_keep:], win_v_p[:, tp - n_keep:]

    mk_s, mv_s = cache_mem_k.reshape(-1, HEAD_DIM), cache_mem_v.reshape(-1, HEAD_DIM)
    mem_s = lambda l: (lambda q, q_col: mem_attention(q, q_col, mk_s, mv_s, cached=(l, mem_len)))
    h, wkv_s, shift_s = rwkv_mem_layer(x_sample, state_shift[0], state_wkv[0], mem_s(0), P0)

    c2s, s2s = _rope_tables(past + jnp.arange(ts, dtype=jnp.int32))
    rows_s, _ = kv_proj(h.reshape(bs * ts, d), g_kv, w_kv, jnp.tile(c2s, (bs, 1)), jnp.tile(s2s, (bs, 1)))
    as_s = lambda x: x.reshape(bs, ts, -1)
    n_cmp_s = (past + ts - CMP_BLOCK) // CMP_STRIDE + 1
    assert (n_cmp_s - 1) * CMP_STRIDE + CMP_BLOCK <= past
    n_pool = cache_cmp_k.shape[0]
    pool = lambda x: x.reshape(n_pool * PAGE_ROWS, HEAD_DIM)
    ckc_s, cvc_s = compress_paged(pool(cache_cmp_k), pool(cache_cmp_v), page_table, n_cmp_s, cmp_pos, cmp_w1, cmp_w2)
    win_k2, win_v2 = cache_win_k.reshape(-1, HEAD_DIM), cache_win_v.reshape(-1, HEAD_DIM)

    new_win = []

    def attend_sample(proj):
        o, wk_out, wv_out = nsa_sample(proj, GATE_COL, ckc_s, cvc_s, pool(cache_slc_k), pool(cache_slc_v), page_table,
                                       [as_s(x) for x in rows_s[2:]], win_k2, win_v2, c2s, s2s, n_cmp_s, past)
        new_win.extend([wk_out, wv_out])
        return o

    y_s = nsa_mem_layer(h, mem_s(1), P1, attend_sample)
    cmp_k_s, cmp_v_s, slc_k_s, slc_v_s = [rows4(x, bs) for x in rows_s[:4]]
    win_k_s, win_v_s = [x.reshape(cache_win_k.shape) for x in new_win]

    return (y_p, y_s, mem_k_p.reshape(depth, bp, mem_len, MEM_HEADS, HEAD_DIM),
            mem_v_p.reshape(depth, bp, mem_len, MEM_HEADS, HEAD_DIM),
            wkv_p[None], shift_p[None], cmp_k_p, cmp_v_p, slc_k_p, slc_v_p, win_k_p, win_v_p,
            wkv_s[None], shift_s[None], cmp_k_s, cmp_v_s, slc_k_s, slc_v_s, win_k_s, win_v_s)
```

```python
import functools

import jax
import jax.numpy as jnp
from jax import lax
from jax.experimental import pallas as pl
from jax.experimental.pallas import tpu as pltpu

f32 = jnp.float32
bf16 = jnp.bfloat16

LANES = 128
VMEM_LIMIT_BYTES = 56 * 1024 * 1024

HEAD_DIM = 128
MEM_HEADS = 4
MEM_WIDTH = MEM_HEADS * HEAD_DIM
RWKV_HEAD_DIM = 64
GN_EPS = 64e-5
NSA_KV = 2
NSA_GROUP = 6
NSA_HEADS = NSA_KV * NSA_GROUP
NSA_WIDTH = NSA_HEADS * HEAD_DIM
KV_COLS = NSA_KV * HEAD_DIM
CMP_BLOCK = 32
CMP_STRIDE = 16
SLC_BLOCK = 64
SLC_SHIFT = 6
N_SELECT = 16
WINDOW = 512
Q_BLOCK = 128
ROPE_THETA = 10000.0
NORM_EPS = 1e-6
NEG_INF = -1e30
FORCE_SCORE = 1e9
PAGE_SIZE = 128

ROW_TILE = 512
FFN_ROW_TILE = 1024
SLC_CHUNK = 512
SEL_LANES = 128
SEQ_PER_STEP = 2
MEM_SEQ_PER_STEP = 4


def _params(*sem):
    return pltpu.CompilerParams(dimension_semantics=sem, vmem_limit_bytes=VMEM_LIMIT_BYTES)


def _rms(x, g):
    return x * lax.rsqrt(jnp.mean(x * x, axis=-1, keepdims=True) + NORM_EPS) * g


def _sigmoid(x):
    return 1.0 / (1.0 + jnp.exp(-x))


def _rmsnorm_kernel(x_ref, g_ref, o_ref):
    o_ref[...] = _rms(x_ref[...], g_ref[...])


def rmsnorm(x, g):
    m, d = x.shape
    tm = min(ROW_TILE, m)
    return pl.pallas_call(
        _rmsnorm_kernel,
        grid=(m // tm,),
        in_specs=[pl.BlockSpec((tm, d), lambda i: (i, 0)), pl.BlockSpec((1, d), lambda i: (0, 0))],
        out_specs=pl.BlockSpec((tm, d), lambda i: (i, 0)),
        out_shape=jax.ShapeDtypeStruct((m, d), f32),
        compiler_params=_params("parallel"),
        name="rmsnorm",
    )(x, g.reshape(1, d))


def _mm_kernel(x_ref, g_ref, w_ref, o_ref, xn_ref, *, norm):
    @pl.when(pl.program_id(1) == 0)
    def _():
        x = x_ref[...]
        if norm:
            x = _rms(x, g_ref[...])
        xn_ref[...] = x.astype(bf16)

    o_ref[...] = jnp.dot(xn_ref[...], w_ref[...], preferred_element_type=f32)


def matmul(x, w, g=None, tn=512):
    m, k = x.shape
    n = w.shape[1]
    tm = min(FFN_ROW_TILE if m % FFN_ROW_TILE == 0 else ROW_TILE, m)
    tn = min(tn, n)
    assert m % tm == 0 and n % tn == 0, (m, n, tm, tn)
    gg = jnp.ones((1, k), f32) if g is None else g.reshape(1, k)
    return pl.pallas_call(
        functools.partial(_mm_kernel, norm=g is not None),
        grid=(m // tm, n // tn),
        in_specs=[pl.BlockSpec((tm, k), lambda i, j: (i, 0)),
                  pl.BlockSpec((1, k), lambda i, j: (0, 0)),
                  pl.BlockSpec((k, tn), lambda i, j: (0, j))],
        out_specs=pl.BlockSpec((tm, tn), lambda i, j: (i, j)),
        out_shape=jax.ShapeDtypeStruct((m, n), f32),
        scratch_shapes=[pltpu.VMEM((tm, k), bf16)],
        compiler_params=_params("parallel", "arbitrary"),
        name="matmul",
    )(x, gg, w.astype(bf16))


def _out_proj_kernel(oa_ref, ob_ref, w_ref, h_ref, g_ref, y_ref):
    ka = oa_ref.shape[1]
    acc = jnp.dot(oa_ref[...].astype(bf16), w_ref[:ka, :], preferred_element_type=f32)
    acc += jnp.dot(ob_ref[...].astype(bf16), w_ref[ka:, :], preferred_element_type=f32)
    y_ref[...] = h_ref[...] + _rms(acc, g_ref[...])


def out_proj_residual(oa, ob, ob_col, w, h, g):
    m, ka = oa.shape
    d = w.shape[1]
    kb = w.shape[0] - ka
    tm = min(ROW_TILE, m)
    return pl.pallas_call(
        _out_proj_kernel,
        grid=(m // tm,),
        in_specs=[pl.BlockSpec((tm, ka), lambda i: (i, 0)),
                  pl.BlockSpec((tm, kb), lambda i: (i, ob_col)),
                  pl.BlockSpec((ka + kb, d), lambda i: (0, 0)),
                  pl.BlockSpec((tm, d), lambda i: (i, 0)),
                  pl.BlockSpec((1, d), lambda i: (0, 0))],
        out_specs=pl.BlockSpec((tm, d), lambda i: (i, 0)),
        out_shape=jax.ShapeDtypeStruct((m, d), f32),
        compiler_params=_params("parallel"),
        name="out_proj",
    )(oa, ob, w.astype(bf16), h, g.reshape(1, d))


def _ffn_kernel(h_ref, gpre_ref, w1_ref, w2_ref, gpost_ref, y_ref, xn_ref):
    j = pl.program_id(1)

    @pl.when(j == 0)
    def _():
        xn_ref[...] = _rms(h_ref[...], gpre_ref[...]).astype(bf16)
        y_ref[...] = jnp.zeros_like(y_ref)

    u = jnp.dot(xn_ref[...], w1_ref[0], preferred_element_type=f32)
    u = jnp.square(jnp.maximum(u, 0.0))
    y_ref[...] += jnp.dot(u.astype(bf16), w2_ref[0], preferred_element_type=f32)

    @pl.when(j == pl.num_programs(1) - 1)
    def _():
        y_ref[...] = h_ref[...] + _rms(y_ref[...], gpost_ref[...])


def ffn_residual(h, g_pre, w1, w2, g_post, layer, tf=512):
    m, d = h.shape
    dff = w1.shape[2]
    tm = min(FFN_ROW_TILE, m)
    return pl.pallas_call(
        _ffn_kernel,
        grid=(m // tm, dff // tf),
        in_specs=[pl.BlockSpec((tm, d), lambda i, j: (i, 0)),
                  pl.BlockSpec((1, d), lambda i, j: (0, 0)),
                  pl.BlockSpec((1, d, tf), lambda i, j: (layer, 0, j)),
                  pl.BlockSpec((1, tf, d), lambda i, j: (layer, j, 0)),
                  pl.BlockSpec((1, d), lambda i, j: (0, 0))],
        out_specs=pl.BlockSpec((tm, d), lambda i, j: (i, 0)),
        out_shape=jax.ShapeDtypeStruct((m, d), f32),
        scratch_shapes=[pltpu.VMEM((tm, d), bf16)],
        compiler_params=_params("parallel", "arbitrary"),
        name="ffn",
    )(h, g_pre.reshape(1, d), w1, w2, g_post.reshape(1, d))


def _rope_tile(x, c2, s2):
    return x * c2 + pltpu.roll(x, HEAD_DIM // 2, 1) * s2


def _rope_heads(q, c2, s2):
    return jnp.concatenate([_rope_tile(q[:, g * HEAD_DIM:(g + 1) * HEAD_DIM], c2, s2)
                            for g in range(NSA_GROUP)], axis=0)


N_KV_BRANCH = 6


def _kv_proj_kernel(h_ref, g_ref, w_ref, c2_ref, s2_ref, *refs):
    outs, bf_ref, xn_ref = refs[:N_KV_BRANCH], refs[N_KV_BRANCH], refs[N_KV_BRANCH + 1]
    j = pl.program_id(1)

    @pl.when(j == 0)
    def _():
        xn_ref[...] = _rms(h_ref[...], g_ref[...]).astype(bf16)

    acc = jnp.dot(xn_ref[...], w_ref[...], preferred_element_type=f32)
    for br in range(N_KV_BRANCH):
        @pl.when(j == br)
        def _(br=br):
            if br in (2, 4):
                c2, s2 = c2_ref[...], s2_ref[...]
                val = jnp.concatenate([_rope_tile(acc[:, kv * HEAD_DIM:(kv + 1) * HEAD_DIM], c2, s2)
                                       for kv in range(NSA_KV)], axis=1)
            else:
                val = acc
            outs[br][...] = val
            bf_ref[...] = val.astype(bf16)


def kv_proj(h, g, w, c2, s2):
    m, d = h.shape
    n = w.shape[1]
    assert n == N_KV_BRANCH * KV_COLS
    tm = min(FFN_ROW_TILE if m % FFN_ROW_TILE == 0 else ROW_TILE, m)
    ntab = c2.shape[0] // tm
    res = pl.pallas_call(
        _kv_proj_kernel,
        grid=(m // tm, N_KV_BRANCH),
        in_specs=[pl.BlockSpec((tm, d), lambda i, j: (i, 0)),
                  pl.BlockSpec((1, d), lambda i, j: (0, 0)),
                  pl.BlockSpec((d, KV_COLS), lambda i, j: (0, j)),
                  pl.BlockSpec((tm, HEAD_DIM), lambda i, j: (i % ntab, 0)),
                  pl.BlockSpec((tm, HEAD_DIM), lambda i, j: (i % ntab, 0))],
        out_specs=[pl.BlockSpec((tm, KV_COLS), lambda i, j: (i, 0))] * N_KV_BRANCH
                  + [pl.BlockSpec((tm, KV_COLS), lambda i, j: (i, j))],
        out_shape=[jax.ShapeDtypeStruct((m, KV_COLS), f32)] * N_KV_BRANCH + [jax.ShapeDtypeStruct((m, n), bf16)],
        scratch_shapes=[pltpu.VMEM((tm, d), bf16)],
        compiler_params=_params("parallel", "arbitrary"),
        name="kv_proj",
    )(h, g.reshape(1, d), w.astype(bf16), c2, s2)
    return res[:N_KV_BRANCH], res[N_KV_BRANCH]


def _mem_attn_kernel(q_ref, k_ref, v_ref, o_ref):
    scale = HEAD_DIM ** -0.5
    for hd in range(MEM_HEADS):
        sl = slice(hd * HEAD_DIM, (hd + 1) * HEAD_DIM)
        q = (q_ref[0, :, sl] * scale).astype(bf16)
        s = _dot_nt(q, k_ref[0, :, sl].astype(bf16))
        e = jnp.exp(s - jnp.max(s, axis=-1, keepdims=True))
        p = e / jnp.sum(e, axis=-1, keepdims=True)
        o_ref[0, :, sl] = jnp.dot(p.astype(bf16), v_ref[0, :, sl].astype(bf16), preferred_element_type=f32)


def _mem_attn_cached_kernel(q_ref, k_ref, v_ref, o_ref):
    n_seq, tq, _ = q_ref.shape
    n_keys = k_ref.shape[0]
    q = jnp.concatenate([q_ref[g, :, hd * HEAD_DIM:(hd + 1) * HEAD_DIM]
                         for g in range(n_seq) for hd in range(MEM_HEADS)], axis=0)
    s = _dot_nt((q * HEAD_DIM ** -0.5).astype(bf16), k_ref[...].astype(bf16))
    shift = lambda x, n: lax.shift_right_logical(x, n.bit_length() - 1)
    r = lax.broadcasted_iota(jnp.int32, (s.shape[0], 1), 0)
    c = lax.broadcasted_iota(jnp.int32, (1, n_keys), 1)
    row_code = shift(r, tq)
    col_code = shift(c, n_keys // n_seq) * MEM_HEADS + jnp.bitwise_and(c, MEM_HEADS - 1)
    s = jnp.where(row_code == col_code, s, NEG_INF)
    e = jnp.exp(s - jnp.max(s, axis=-1, keepdims=True))
    p = e / jnp.sum(e, axis=-1, keepdims=True)
    o = jnp.dot(p.astype(bf16), v_ref[...].astype(bf16), preferred_element_type=f32)
    for g in range(n_seq):
        for hd in range(MEM_HEADS):
            r0 = (g * MEM_HEADS + hd) * tq
            o_ref[g, :, hd * HEAD_DIM:(hd + 1) * HEAD_DIM] = o[r0:r0 + tq]


def mem_attention(q, q_col, mk, mv, k_col=0, v_col=0, cached=None):
    b, t, _ = q.shape
    w = MEM_WIDTH
    tq = min(ROW_TILE, t)
    if cached is None:
        g, body = 1, _mem_attn_kernel
        mlen = mk.shape[1]
        kspec = pl.BlockSpec((1, mlen, w), lambda i, j: (i, 0, k_col))
        vspec = pl.BlockSpec((1, mlen, w), lambda i, j: (i, 0, v_col))
    else:
        layer, mlen = cached
        g, body = (MEM_SEQ_PER_STEP if b % MEM_SEQ_PER_STEP == 0 else 1), _mem_attn_cached_kernel
        pow2 = lambda x: x & (x - 1) == 0
        assert tq == t and pow2(tq) and pow2(mlen * MEM_HEADS) and pow2(MEM_HEADS)
        kspec = vspec = pl.BlockSpec((g * mlen * MEM_HEADS, HEAD_DIM), lambda i, j: (layer * (b // g) + i, 0))
    return pl.pallas_call(
        body,
        grid=(b // g, t // tq),
        in_specs=[pl.BlockSpec((g, tq, w), lambda i, j: (i, j, q_col)), kspec, vspec],
        out_specs=pl.BlockSpec((g, tq, w), lambda i, j: (i, j, 0)),
        out_shape=jax.ShapeDtypeStruct((b, t, w), f32),
        compiler_params=_params("parallel", "arbitrary"),
        name="mem_attn",
    )(q, mk, mv)


WKV_QUAD = 4
WKV_LANES = WKV_QUAD * RWKV_HEAD_DIM
WKV_GROUP = 6
WKV_TB = 64


def _block_outputs(r, w, k, b, v, s0, zt):
    n = RWKV_HEAD_DIM
    nt = r.shape[0]
    row = lax.broadcasted_iota(jnp.int32, (1, n, WKV_LANES), 1)
    lane = lax.broadcasted_iota(jnp.int32, (1, n, WKV_LANES), 2)
    head = lax.shift_right_logical(lane, 6)
    tri = jnp.where(lax.broadcasted_iota(jnp.int32, (n, n), 0) >= lax.broadcasted_iota(jnp.int32, (n, n), 1),
                    1.0, 0.0).astype(bf16)
    log_w = jnp.concatenate([jnp.log(w[i]) for i in range(nt)], axis=1)
    hi = log_w.astype(bf16)
    log_p = (jnp.dot(tri, hi, preferred_element_type=f32)
             + jnp.dot(tri, (log_w - hi.astype(f32)).astype(bf16), preferred_element_type=f32))
    log_p = jnp.stack([log_p[:, i * WKV_LANES:(i + 1) * WKV_LANES] for i in range(nt)])
    p, p_inv = jnp.exp(log_p), jnp.exp(-log_p)
    zero = jnp.zeros((), bf16)

    def stack(x):
        xb = x.astype(bf16)
        return jnp.concatenate([jnp.where(head == h4, xb, zero) for h4 in range(WKV_QUAD)], axis=1)

    bdot = lambda x, y: jnp.einsum('nil,njl->nij', x, y, preferred_element_type=f32)
    rt = (r * p).astype(bf16)
    y0 = bdot(rt, stack(s0))
    causal = jnp.bitwise_and(lane, n - 1) <= row
    a_k = jnp.where(causal, bdot(rt, stack(k * p_inv)), 0.0).astype(bf16)
    a_b = jnp.where(causal, bdot(rt, stack(b * p_inv)), 0.0).astype(bf16)
    y_v = jnp.einsum('nij,njl->nil', a_k, stack(v), preferred_element_type=f32)
    return y0 + y_v - bdot(a_b, stack(zt))


def _wkv_kernel(r_ref, w_ref, k_ref, kk_ref, b_ref, v_ref, s0_ref, yt_ref, st_ref,
                s_scr, lhs_scr, vd_scr, yl_scr, s0_scr, zt_scr, *, nb, nq, tb, defer_y):
    n = RWKV_HEAD_DIM
    ti = pl.program_id(1)

    @pl.when(ti == 0)
    def _():
        for ib in range(nb):
            s_scr[ib * nq * n:(ib + 1) * nq * n, :] = s0_ref[ib]

    if defer_y:
        s0_scr[...] = s_scr[...]
        zt_scr[...] = jnp.zeros(zt_scr.shape, f32)
    else:
        yt_ref[...] = jnp.zeros(yt_ref.shape, f32)
    ri = lax.broadcasted_iota(jnp.int32, (WKV_LANES, WKV_LANES), 0)
    ci = lax.broadcasted_iota(jnp.int32, (WKV_LANES, WKV_LANES), 1)
    ones_blk = jnp.where(lax.shift_right_logical(ri, 6) == lax.shift_right_logical(ci, 6), 1.0, 0.0).astype(bf16)
    ones_blk2 = jnp.concatenate([ones_blk, ones_blk], axis=0)
    eye_rep = jnp.where(lax.broadcasted_iota(jnp.int32, (n, WKV_LANES), 0)
                        == jnp.bitwise_and(lax.broadcasted_iota(jnp.int32, (n, WKV_LANES), 1), n - 1),
                        1.0, 0.0).astype(bf16)
    step_lane = jnp.bitwise_and(lax.broadcasted_iota(jnp.int32, (n, WKV_LANES), 1), n - 1)
    tiles = [(ib, q) for ib in range(nb) for q in range(nq)]
    groups = [tiles[i:i + WKV_GROUP] for i in range(0, len(tiles), WKV_GROUP)]

    def step(t, carry):
        row = lambda ref, ib, q: ref[ib, pl.ds(t, 1), q * WKV_LANES:(q + 1) * WKV_LANES]
        for gi, group in enumerate(groups):
            rows_g = slice(gi * WKV_GROUP * n, (gi * WKV_GROUP + len(group)) * n)
            for ib, q in group:
                rs = slice((ib * nq + q) * n, (ib * nq + q + 1) * n)
                prod = s_scr[rs, :] * row(kk_ref, ib, q)
                hi = prod.astype(bf16)
                lhs_scr[rs, 0:WKV_LANES] = hi
                lhs_scr[rs, WKV_LANES:2 * WKV_LANES] = (prod - hi.astype(f32)).astype(bf16)
                vd_scr[rs, :] = eye_rep * row(v_ref, ib, q).astype(bf16)
            z = jnp.dot(lhs_scr[rows_g, :], ones_blk2, preferred_element_type=f32)
            vcol = jnp.dot(vd_scr[rows_g, :], ones_blk, preferred_element_type=f32)
            for i, (ib, q) in enumerate(group):
                rs = slice((ib * nq + q) * n, (ib * nq + q + 1) * n)
                ts = slice(i * n, (i + 1) * n)
                s = s_scr[rs, :] * row(w_ref, ib, q) - z[ts] * row(b_ref, ib, q) + vcol[ts] * row(k_ref, ib, q)
                s_scr[rs, :] = s
                if defer_y:
                    zt_scr[rs, :] = jnp.where(step_lane == t, z[ts], zt_scr[rs, :])
                else:
                    yl_scr[rs, :] = (s * row(r_ref, ib, q)).astype(bf16)
            if not defer_y:
                y = jnp.dot(yl_scr[rows_g, :], ones_blk, preferred_element_type=f32)
                for i, (ib, q) in enumerate(group):
                    yt_ref[ib, 0, q * n:(q + 1) * n, :] = jnp.where(step_lane == t, y[i * n:(i + 1) * n],
                                                                     yt_ref[ib, 0, q * n:(q + 1) * n, :])
        return carry

    lax.fori_loop(0, tb, step, 0, unroll=8)

    if defer_y:
        tiled = lambda ref: jnp.stack([ref[ib, :, q * WKV_LANES:(q + 1) * WKV_LANES] for ib, q in tiles])
        as_tiles = lambda ref: ref[...].reshape(len(tiles), n, WKV_LANES)
        yt = _block_outputs(tiled(r_ref), tiled(w_ref), tiled(k_ref), tiled(b_ref), tiled(v_ref),
                            as_tiles(s0_scr), as_tiles(zt_scr))
        for i, (ib, q) in enumerate(tiles):
            yt_ref[ib, :, q * WKV_LANES:(q + 1) * WKV_LANES] = yt[i]

    @pl.when(ti == pl.num_programs(1) - 1)
    def _():
        for ib in range(nb):
            st_ref[ib] = s_scr[ib * nq * n:(ib + 1) * nq * n, :]


def wkv_scan(r, w, k, kk, b, v, s0, nb=2):
    bsz, t, width = r.shape
    n = RWKV_HEAD_DIM
    nh = width // n
    nq = nh // WKV_QUAD
    tb = min(WKV_TB, t)
    nblk = t // tb
    to_tiles = lambda s: s.reshape(bsz, nq, WKV_QUAD, n, n).transpose(0, 1, 3, 2, 4).reshape(bsz, nq * n, WKV_LANES)
    row = pl.BlockSpec((nb, tb, width), lambda i, j: (i, j, 0))
    st = pl.BlockSpec((nb, nq * n, WKV_LANES), lambda i, j: (i, 0, 0))
    rows_all = nb * nq * n
    defer_y = tb == n
    if defer_y:
        y_spec, y_shape = row, jax.ShapeDtypeStruct((bsz, t, width), f32)
    else:
        y_spec = pl.BlockSpec((nb, 1, nq * n, WKV_LANES), lambda i, j: (i, j, 0, 0))
        y_shape = jax.ShapeDtypeStruct((bsz, nblk, nq * n, WKV_LANES), f32)
    yt, s_t = pl.pallas_call(
        functools.partial(_wkv_kernel, nb=nb, nq=nq, tb=tb, defer_y=defer_y),
        grid=(bsz // nb, nblk),
        in_specs=[row, row, row, row, row, row, st],
        out_specs=[y_spec, st],
        out_shape=[y_shape, jax.ShapeDtypeStruct((bsz, nq * n, WKV_LANES), f32)],
        scratch_shapes=[pltpu.VMEM((rows_all, WKV_LANES), f32),
                        pltpu.VMEM((rows_all, 2 * WKV_LANES), bf16),
                        pltpu.VMEM((rows_all, WKV_LANES), bf16),
                        pltpu.VMEM((rows_all, WKV_LANES), bf16),
                        pltpu.VMEM((rows_all, WKV_LANES), f32),
                        pltpu.VMEM((rows_all, WKV_LANES), f32)],
        compiler_params=_params("parallel", "arbitrary"),
        name="wkv_scan",
    )(r, w, k, kk, b, v, to_tiles(s0))
    if defer_y:
        y = yt
    else:
        y = yt.reshape(bsz, nblk, nq, n, WKV_QUAD, n)[..., :tb].transpose(0, 1, 5, 2, 4, 3).reshape(bsz, t, width)
    s_t = s_t.reshape(bsz, nq, n, WKV_QUAD, n).transpose(0, 1, 3, 2, 4).reshape(bsz, nh, n, n)
    return y, s_t


def _gelu_tanh(x):
    return 0.5 * x * (1.0 + jnp.tanh(0.7978845608028654 * (x + 0.044715 * x * x * x)))


def _chunk_rows(x_ref, n_chunks, row0=0, row_stride=1):
    return jnp.concatenate(
        [x_ref[pl.ds(row0 + s * row_stride, n_chunks, stride=CMP_STRIDE * row_stride), :]
         for s in range(CMP_STRIDE)], axis=1).astype(bf16)


def _compress_rows(x2, pos_ref, w1_ref, w2_ref, n_valid, n_heads):
    rows = x2.shape[0]
    n_chunks = rows // n_heads
    pab = jnp.dot(x2, w1_ref[...], preferred_element_type=f32)
    pos = jnp.dot(pos_ref[...], w1_ref[...], preferred_element_type=f32)
    posterm = pos[0:1, :HEAD_DIM] + pos[1:2, HEAD_DIM:]
    hid = pab[:, :HEAD_DIM] + pltpu.roll(pab[:, HEAD_DIM:], rows - 1, 0) + posterm
    out = jnp.dot(_gelu_tanh(hid).astype(bf16), w2_ref[...], preferred_element_type=f32)
    n = jnp.bitwise_and(lax.broadcasted_iota(jnp.int32, out.shape, 0), n_chunks - 1)
    return jnp.where(n < n_valid, out, 0.0)


def _compress_prompt_kernel(k_ref, v_ref, posk_ref, w1k_ref, w2k_ref, posv_ref, w1v_ref, w2v_ref,
                            ok_ref, ov_ref, *, n_chunks, n_valid):
    ok_ref[0, 0] = _compress_rows(_chunk_rows(k_ref.at[0], n_chunks), posk_ref, w1k_ref, w2k_ref, n_valid, 1)
    ov_ref[0, 0] = _compress_rows(_chunk_rows(v_ref.at[0], n_chunks), posv_ref, w1v_ref, w2v_ref, n_valid, 1)


def _cmp_weights(cmp_pos, cmp_w1, cmp_w2):
    ws = []
    for i in range(2):
        half = CMP_STRIDE * HEAD_DIM
        pos = _pad_to_rows(cmp_pos[i].reshape(2, half), 8).astype(bf16)
        w1 = cmp_w1[i].reshape(2, half, HEAD_DIM)
        ws += [pos, jnp.concatenate([w1[0], w1[1]], axis=1).astype(bf16), cmp_w2[i].astype(bf16)]
    return ws


_CMP_WEIGHT_SHAPES = [(8, CMP_STRIDE * HEAD_DIM), (CMP_STRIDE * HEAD_DIM, 2 * HEAD_DIM), (HEAD_DIM, HEAD_DIM)] * 2


def compress_prompt(ck, cv, cmp_pos, cmp_w1, cmp_w2):
    b, t, _ = ck.shape
    n_chunks = t // CMP_STRIDE
    n_valid = (t - CMP_BLOCK) // CMP_STRIDE + 1
    out = jax.ShapeDtypeStruct((b, NSA_KV, n_chunks, HEAD_DIM), f32)
    ospec = pl.BlockSpec((1, 1, n_chunks, HEAD_DIM), lambda i, kv: (i, kv, 0, 0))
    wspecs = [pl.BlockSpec(s, lambda i, kv: (0, 0)) for s in _CMP_WEIGHT_SHAPES]
    return pl.pallas_call(
        functools.partial(_compress_prompt_kernel, n_chunks=n_chunks, n_valid=n_valid),
        grid=(b, NSA_KV),
        in_specs=[pl.BlockSpec((1, t, HEAD_DIM), lambda i, kv: (i, 0, kv)),
                  pl.BlockSpec((1, t, HEAD_DIM), lambda i, kv: (i, 0, kv))] + wspecs,
        out_specs=[ospec, ospec],
        out_shape=[out, out],
        compiler_params=_params("parallel", "parallel"),
        name="compress_prompt",
    )(ck, cv, *_cmp_weights(cmp_pos, cmp_w1, cmp_w2))


PAGE_ROWS = PAGE_SIZE * NSA_KV


def _page_specs(n_pages, n_seq):
    return [pl.BlockSpec((PAGE_ROWS, HEAD_DIM), lambda i, pt, g=g, p=p: (pt[i * n_seq + g, p], 0))
            for g in range(n_seq) for p in range(n_pages)]


def _compress_paged_kernel(pt_ref, *refs, n_pages, n_seq, n_valid):
    np_all = n_seq * n_pages
    k_pages, v_pages = refs[:np_all], refs[np_all:2 * np_all]
    posk_ref, w1k_ref, w2k_ref, posv_ref, w1v_ref, w2v_ref, ok_ref, ov_ref = refs[2 * np_all:]
    per_page = PAGE_SIZE // CMP_STRIDE
    n_chunks = n_pages * per_page
    heads = [(g, kv) for g in range(n_seq) for kv in range(NSA_KV)]
    chunks = lambda pages: jnp.concatenate([_chunk_rows(pg, per_page, kv, NSA_KV)
                                            for g, kv in heads for pg in pages[g * n_pages:(g + 1) * n_pages]], axis=0)
    ok = _compress_rows(chunks(k_pages), posk_ref, w1k_ref, w2k_ref, n_valid, len(heads))
    ov = _compress_rows(chunks(v_pages), posv_ref, w1v_ref, w2v_ref, n_valid, len(heads))
    for i, (g, kv) in enumerate(heads):
        ok_ref[g, kv] = ok[i * n_chunks:(i + 1) * n_chunks]
        ov_ref[g, kv] = ov[i * n_chunks:(i + 1) * n_chunks]


def compress_paged(pool_k, pool_v, page_table, n_valid, cmp_pos, cmp_w1, cmp_w2):
    b, n_pages = page_table.shape
    n_seq = SEQ_PER_STEP if b % SEQ_PER_STEP == 0 else 1
    n_chunks = n_pages * PAGE_SIZE // CMP_STRIDE
    out = jax.ShapeDtypeStruct((b, NSA_KV, n_chunks, HEAD_DIM), f32)
    ospec = pl.BlockSpec((n_seq, NSA_KV, n_chunks, HEAD_DIM), lambda i, pt: (i, 0, 0, 0))
    wspecs = [pl.BlockSpec(s, lambda i, pt: (0, 0)) for s in _CMP_WEIGHT_SHAPES]
    np_all = n_seq * n_pages
    return pl.pallas_call(
        functools.partial(_compress_paged_kernel, n_pages=n_pages, n_seq=n_seq, n_valid=n_valid),
        grid_spec=pltpu.PrefetchScalarGridSpec(
            num_scalar_prefetch=1,
            grid=(b // n_seq,),
            in_specs=_page_specs(n_pages, n_seq) * 2 + wspecs,
            out_specs=[ospec, ospec]),
        out_shape=[out, out],
        compiler_params=_params("parallel"),
        name="compress_paged",
    )(page_table, *([pool_k] * np_all), *([pool_v] * np_all), *_cmp_weights(cmp_pos, cmp_w1, cmp_w2))


def _stack_heads(x):
    return jnp.concatenate([x[:, g * HEAD_DIM:(g + 1) * HEAD_DIM] for g in range(NSA_GROUP)], axis=0)


def _dot_nt(a, b):
    return lax.dot_general(a, b, (((1,), (1,)), ((), ())), preferred_element_type=f32)


def _softmax_heads(s, bias, tq):
    s3 = s.reshape(NSA_GROUP, tq, s.shape[1]) + bias[None]
    e = jnp.exp(s3 - jnp.max(s3, axis=-1, keepdims=True))
    return e, jnp.sum(e, axis=-1, keepdims=True)


def _compressed_branch(qc, ckc, cvc, pos_t, n_cmp, tq):
    s = _dot_nt(qc, ckc)
    n = lax.broadcasted_iota(jnp.int32, (tq, s.shape[1]), 1)
    vis = (n * CMP_STRIDE + (CMP_BLOCK - 1) <= pos_t) & (n < n_cmp)
    e, denom = _softmax_heads(s, jnp.where(vis, 0.0, NEG_INF), tq)
    any_vis = jnp.where(pos_t >= CMP_BLOCK - 1, 1.0, 0.0)
    p = e / denom * any_vis[None]
    o = jnp.dot(p.reshape(s.shape).astype(bf16), cvc, preferred_element_type=f32)
    return o, jnp.sum(p, axis=0)


def _select_blocks(psum, tq, pos0, n_slc, n_j):
    if tq < SEL_LANES:
        psum = jnp.concatenate([psum, jnp.zeros((SEL_LANES - tq, psum.shape[1]), f32)], axis=0)
    n_c = psum.shape[1]
    j = lax.broadcasted_iota(jnp.int32, (n_j, n_c), 0)
    cs = lax.broadcasted_iota(jnp.int32, (n_j, n_c), 1) * CMP_STRIDE
    overlap = jnp.where((cs < j * SLC_BLOCK + SLC_BLOCK) & (cs + (CMP_BLOCK - 1) >= j * SLC_BLOCK), 1.0, 0.0)
    imp_t = lax.dot_general(overlap, psum, (((1,), (1,)), ((), ())),
                            preferred_element_type=f32, precision=lax.Precision.HIGHEST)
    j = lax.broadcasted_iota(jnp.int32, imp_t.shape, 0)
    pos_t = pos0 + lax.broadcasted_iota(jnp.int32, imp_t.shape, 1)
    cur = lax.shift_right_logical(pos_t, SLC_SHIFT)
    causal = j * SLC_BLOCK <= pos_t
    forced = (j == 0) | (j == cur) | (j == cur - 1)
    score = jnp.where(causal, jnp.where(forced, FORCE_SCORE, imp_t), -FORCE_SCORE)
    score = jnp.where(j < n_slc, score, -2.0 * FORCE_SCORE)
    rank = jnp.zeros(imp_t.shape, f32)
    for jp in range(n_slc):
        row = score[jp:jp + 1, :]
        ahead = (row > score) | ((row == score) & (j > jp))
        rank = rank + jnp.where(ahead, 1.0, 0.0)
    sel_t = jnp.where(rank < min(N_SELECT, n_slc), 1.0, 0.0)
    return sel_t.T[0:tq]


def _selection_bias(sel, key0, n_keys):
    nj = sel.shape[1]
    j = lax.broadcasted_iota(jnp.int32, (nj, n_keys), 0)
    kpos = key0 + lax.broadcasted_iota(jnp.int32, (nj, n_keys), 1)
    e = jnp.where(lax.shift_right_logical(kpos, SLC_SHIFT) == j, 1.0, 0.0).astype(bf16)
    return jnp.dot(jnp.where(sel > 0.5, 0.0, NEG_INF).astype(bf16), e, preferred_element_type=f32)


def _window_branch(qr, wk, wv, kpos0, n_keys_valid, pos_t, tq, n_phantom=None):
    s = _dot_nt(qr, wk)
    lane = lax.broadcasted_iota(jnp.int32, (tq, s.shape[1]), 1)
    kpos = kpos0 + lane
    valid = (kpos <= pos_t) & (pos_t - kpos < WINDOW) & (lane < n_keys_valid)
    s3 = s.reshape(NSA_GROUP, tq, s.shape[1]) + jnp.where(valid, 0.0, NEG_INF)[None]
    m = jnp.max(s3, axis=-1, keepdims=True)
    if n_phantom is not None:
        m = jnp.where(n_phantom[None] > 0.0, jnp.maximum(m, 0.0), m)
    e = jnp.exp(s3 - m)
    denom = jnp.sum(e, axis=-1, keepdims=True)
    if n_phantom is not None:
        denom = denom + n_phantom[None] * jnp.exp(-m)
    return jnp.dot((e / denom).reshape(s.shape).astype(bf16), wv, preferred_element_type=f32)


def _gated_sum(gate, cols, tq, o_cmp, o_slc, o_win, g):
    r = slice(g * tq, (g + 1) * tq)
    c, s, w = cols[0] + g, cols[1] + g, cols[2] + g
    return gate[:, c:c + 1] * o_cmp[r] + gate[:, s:s + 1] * o_slc[r] + gate[:, w:w + 1] * o_win[r]


QK_SCALE = HEAD_DIM ** -0.5


def _nsa_prompt_kernel(q_ref, c2_ref, s2_ref, gate_ref, ckc_ref, cvc_ref, sk_ref, sv_ref, wk_ref, wv_ref,
                       o_ref, *, n_cmp, n_slc):
    i = pl.program_id(2)
    tq = Q_BLOCK
    rows = NSA_GROUP * tq
    q0 = i * tq
    q = q_ref[0] * QK_SCALE
    qc = _stack_heads(q).astype(bf16)
    qr = _rope_heads(q, c2_ref[...], s2_ref[...]).astype(bf16)
    pos_t = q0 + lax.broadcasted_iota(jnp.int32, (tq, 1), 0)

    o_cmp, psum = _compressed_branch(qc, ckc_ref[0, 0].astype(bf16), cvc_ref[0, 0].astype(bf16), pos_t, n_cmp, tq)
    sel = _select_blocks(psum, tq, q0, n_slc, n_slc)

    def slc_step(c, carry, causal):
        m, l, acc = carry
        k0 = pl.multiple_of(c * SLC_CHUNK, SLC_CHUNK)
        bias = _selection_bias(sel, k0, SLC_CHUNK)
        if causal:
            kpos = k0 + lax.broadcasted_iota(jnp.int32, bias.shape, 1)
            bias = jnp.where(kpos <= pos_t, bias, NEG_INF)
        s3 = _dot_nt(qr, sk_ref[0, pl.ds(k0, SLC_CHUNK), :]).reshape(NSA_GROUP, tq, SLC_CHUNK) + bias[None]
        m_new = jnp.maximum(m, jnp.max(s3, axis=-1, keepdims=True))
        alpha = jnp.exp(m - m_new)
        e = jnp.exp(s3 - m_new)
        l = alpha * l + jnp.sum(e, axis=-1, keepdims=True)
        pv = jnp.dot(e.reshape(rows, SLC_CHUNK).astype(bf16), sv_ref[0, pl.ds(k0, SLC_CHUNK), :],
                     preferred_element_type=f32)
        return m_new, l, alpha * acc + pv.reshape(NSA_GROUP, tq, HEAD_DIM)

    c_last = q0 // SLC_CHUNK
    init = (jnp.full((NSA_GROUP, tq, 1), NEG_INF, f32), jnp.zeros((NSA_GROUP, tq, 1), f32),
            jnp.zeros((NSA_GROUP, tq, HEAD_DIM), f32))
    carry = lax.fori_loop(0, c_last, functools.partial(slc_step, causal=False), init)
    _, l, acc = slc_step(c_last, carry, causal=True)
    o_slc = (acc / l).reshape(rows, HEAD_DIM)

    span = WINDOW + tq
    w0 = pl.multiple_of(jnp.maximum(q0 - WINDOW, 0), tq)
    n_phantom = jnp.maximum(WINDOW - 1 - pos_t, 0).astype(f32)
    o_win = _window_branch(qr, wk_ref[0, pl.ds(w0, span), :], wv_ref[0, pl.ds(w0, span), :], w0, span, pos_t, tq,
                           n_phantom)

    gate = _sigmoid(gate_ref[0])
    first_kv = pl.program_id(1) == 0
    for g in range(NSA_GROUP):
        head = [_gated_sum(gate, tuple(br * NSA_HEADS + kv * NSA_GROUP for br in range(3)), tq, o_cmp, o_slc, o_win, g)
                for kv in range(NSA_KV)]
        o_ref[0, :, g * HEAD_DIM:(g + 1) * HEAD_DIM] = jnp.where(first_kv, head[0], head[1])


def nsa_prompt(proj, gate_col, ckc, cvc, kv_bf, c2, s2, n_cmp):
    b, t, _ = proj.shape
    assert t % SLC_CHUNK == 0 and t >= WINDOW + Q_BLOCK
    n_slc = t // SLC_BLOCK
    gw = NSA_GROUP * HEAD_DIM
    kvcol = lambda c: pl.BlockSpec((1, t, HEAD_DIM), lambda bi, kv, i, c=c: (bi, 0, c * NSA_KV + kv))
    cmp_spec = pl.BlockSpec((1, 1, ckc.shape[2], HEAD_DIM), lambda bi, kv, i: (bi, kv, 0, 0))
    return pl.pallas_call(
        functools.partial(_nsa_prompt_kernel, n_cmp=n_cmp, n_slc=n_slc),
        grid=(b, NSA_KV, t // Q_BLOCK),
        in_specs=[pl.BlockSpec((1, Q_BLOCK, gw), lambda bi, kv, i: (bi, i, kv)),
                  pl.BlockSpec((Q_BLOCK, HEAD_DIM), lambda bi, kv, i: (i, 0)),
                  pl.BlockSpec((Q_BLOCK, HEAD_DIM), lambda bi, kv, i: (i, 0)),
                  pl.BlockSpec((1, Q_BLOCK, LANES), lambda bi, kv, i: (bi, i, gate_col)),
                  cmp_spec, cmp_spec, kvcol(2), kvcol(3), kvcol(4), kvcol(5)],
        out_specs=pl.BlockSpec((1, Q_BLOCK, gw), lambda bi, kv, i: (bi, i, kv)),
        out_shape=jax.ShapeDtypeStruct((b, t, NSA_WIDTH), f32),
        compiler_params=_params("parallel", "parallel", "arbitrary"),
        name="nsa_prompt",
    )(proj, c2, s2, proj, ckc, cvc, kv_bf, kv_bf, kv_bf, kv_bf)


def _pad_rows(x, n):
    return jnp.concatenate([x, jnp.zeros((n - x.shape[0], x.shape[1]), x.dtype)], axis=0)


def _nsa_sample_kernel(pt_ref, *refs, n_pages, n_seq, n_cmp, n_slc, n_j, past, tq):
    np_all = n_seq * n_pages
    k_pages, v_pages = refs[:np_all], refs[np_all:2 * np_all]
    (q_ref, c2_ref, s2_ref, gate_ref, ckc_ref, cvc_ref, nsk_ref, nsv_ref, nwk_ref, nwv_ref,
     wink_ref, winv_ref, o_ref, owk_ref, owv_ref) = refs[2 * np_all:]
    pos_t = past + lax.broadcasted_iota(jnp.int32, (tq, 1), 0)
    lw2 = wink_ref.shape[0] // n_seq
    lw = lw2 // NSA_KV
    c2, s2 = c2_ref[...], s2_ref[...]
    for g in range(n_seq):
        gate = _sigmoid(gate_ref[g])
        keep = lw2 - tq * NSA_KV
        for cache_ref, new_ref, out_ref in ((wink_ref, nwk_ref, owk_ref), (winv_ref, nwv_ref, owv_ref)):
            out_ref[g * lw2:g * lw2 + keep, :] = cache_ref[g * lw2 + tq * NSA_KV:(g + 1) * lw2, :]
            for kv in range(NSA_KV):
                out_ref[pl.ds(g * lw2 + keep + kv, tq, stride=NSA_KV), :] = new_ref[g, :, kv * HEAD_DIM:(kv + 1) * HEAD_DIM]
        for kv in range(NSA_KV):
            ksl = slice(kv * HEAD_DIM, (kv + 1) * HEAD_DIM)
            q = q_ref[g, :, kv * NSA_GROUP * HEAD_DIM:(kv + 1) * NSA_GROUP * HEAD_DIM] * QK_SCALE
            qc = _stack_heads(q).astype(bf16)
            qr = _rope_heads(q, c2, s2).astype(bf16)
            o_cmp, psum = _compressed_branch(qc, ckc_ref[g, kv].astype(bf16), cvc_ref[g, kv].astype(bf16), pos_t,
                                             n_cmp, tq)
            sel = _select_blocks(psum, tq, past, n_slc, n_j)

            paged = lambda pages: [pg[pl.ds(kv, PAGE_SIZE, stride=NSA_KV), :] for pg in pages[g * n_pages:(g + 1) * n_pages]]
            sk = jnp.concatenate(paged(k_pages) + [_pad_rows(nsk_ref[g, :, ksl], LANES)], axis=0).astype(bf16)
            sv = jnp.concatenate(paged(v_pages) + [_pad_rows(nsv_ref[g, :, ksl], LANES)], axis=0).astype(bf16)
            n_keys = sk.shape[0]
            bias = _selection_bias(sel, 0, n_keys)
            bias = jnp.where(lax.broadcasted_iota(jnp.int32, bias.shape, 1) <= pos_t, bias, NEG_INF)
            e, denom = _softmax_heads(_dot_nt(qr, sk), bias, tq)
            o_slc = jnp.dot((e / denom).reshape(NSA_GROUP * tq, n_keys).astype(bf16), sv, preferred_element_type=f32)

            cached = lambda ref: ref[pl.ds(g * lw2 + kv, lw, stride=NSA_KV), :]
            wk = jnp.concatenate([cached(wink_ref), _pad_rows(nwk_ref[g, :, ksl], LANES)], axis=0).astype(bf16)
            wv = jnp.concatenate([cached(winv_ref), _pad_rows(nwv_ref[g, :, ksl], LANES)], axis=0).astype(bf16)
            o_win = _window_branch(qr, wk, wv, past - lw, lw + tq, pos_t, tq)

            cols = tuple(br * NSA_HEADS + kv * NSA_GROUP for br in range(3))
            for hg in range(NSA_GROUP):
                hd = kv * NSA_GROUP + hg
                o_ref[g, :, hd * HEAD_DIM:(hd + 1) * HEAD_DIM] = _gated_sum(gate, cols, tq, o_cmp, o_slc, o_win, hg)


def nsa_sample(proj, gate_col, ckc, cvc, pool_k, pool_v, page_table, new_rows, win_k, win_v, c2, s2, n_cmp, past):
    b, tq, _ = proj.shape
    n_pages = page_table.shape[1]
    assert past == n_pages * PAGE_SIZE and past % SLC_BLOCK == 0 and tq <= SLC_BLOCK and (tq * NSA_KV) % 8 == 0
    n_seq = SEQ_PER_STEP if b % SEQ_PER_STEP == 0 else 1
    n_slc = past // SLC_BLOCK + 1
    n_j = -(-n_slc // SLC_BLOCK) * SLC_BLOCK
    lw2 = win_k.shape[0] // b
    per_b = lambda shape: pl.BlockSpec((n_seq,) + shape, lambda i, pt: (i,) + (0,) * len(shape))
    tab = pl.BlockSpec((tq, HEAD_DIM), lambda i, pt: (0, 0))
    win = pl.BlockSpec((n_seq * lw2, HEAD_DIM), lambda i, pt: (i, 0))
    np_all = n_seq * n_pages
    return pl.pallas_call(
        functools.partial(_nsa_sample_kernel, n_pages=n_pages, n_seq=n_seq, n_cmp=n_cmp, n_slc=n_slc, n_j=n_j,
                          past=past, tq=tq),
        grid_spec=pltpu.PrefetchScalarGridSpec(
            num_scalar_prefetch=1,
            grid=(b // n_seq,),
            in_specs=_page_specs(n_pages, n_seq) * 2 + [
                per_b((tq, NSA_WIDTH)), tab, tab,
                pl.BlockSpec((n_seq, tq, LANES), lambda i, pt: (i, 0, gate_col)),
                per_b((NSA_KV, ckc.shape[2], HEAD_DIM)), per_b((NSA_KV, ckc.shape[2], HEAD_DIM))]
                + [per_b((tq, KV_COLS))] * 4 + [win, win],
            out_specs=[per_b((tq, NSA_WIDTH)), win, win]),
        out_shape=[jax.ShapeDtypeStruct((b, tq, NSA_WIDTH), f32),
                   jax.ShapeDtypeStruct(win_k.shape, f32), jax.ShapeDtypeStruct(win_v.shape, f32)],
        compiler_params=_params("parallel"),
        name="nsa_sample",
    )(page_table, *([pool_k] * np_all), *([pool_v] * np_all), proj, c2, s2, proj, ckc, cvc, *new_rows, win_k, win_v)


def _prev_rows(x, tile, halo_ref, first_ref, seq_len, halo_fn=lambda rows: rows):
    tm, c = x.shape
    prev = pltpu.roll(x, 1, 0)
    row = lax.broadcasted_iota(jnp.int32, (tm, 1), 0)
    if seq_len >= tm:
        tiles_per_seq = seq_len // tm
        first = first_ref[pl.ds(tile // tiles_per_seq, 1), :]
        edge = jnp.where(tile % tiles_per_seq == 0, first, halo_fn(halo_ref[...])[7:8, :])
        return jnp.where(row == 0, edge, prev)
    pieces = []
    for j in range(tm // seq_len):
        pieces.append(jnp.broadcast_to(first_ref[j:j + 1, :], (8, c)))
        if seq_len > 8:
            pieces.append(jnp.zeros((seq_len - 8, c), f32))
    return jnp.where(jnp.bitwise_and(row, seq_len - 1) == 0, jnp.concatenate(pieces, axis=0), prev)


def _shift_specs(m, c, tm, seq_len, n_seq, col=None):
    cb = (lambda *g: 0) if col is None else col
    tile = pl.BlockSpec((tm, c), lambda *g: (g[0], cb(*g)))
    halo = pl.BlockSpec((8, c), lambda *g: (jnp.maximum(g[0] * (tm // 8) - 1, 0), cb(*g)))
    if seq_len >= tm:
        first = pl.BlockSpec((-(-n_seq // 8) * 8, c), lambda *g: (0, cb(*g)))
    else:
        first = pl.BlockSpec((tm // seq_len, c), lambda *g: (g[0], cb(*g)))
    return tile, halo, first


def _pad_first(first, seq_len, tm):
    return _pad_to_rows(first, -(-first.shape[0] // 8) * 8) if seq_len >= tm else first


def _head_sums(x):
    ri = lax.broadcasted_iota(jnp.int32, (WKV_LANES, WKV_LANES), 0)
    ci = lax.broadcasted_iota(jnp.int32, (WKV_LANES, WKV_LANES), 1)
    ones_blk = jnp.where(lax.shift_right_logical(ri, 6) == lax.shift_right_logical(ci, 6), 1.0, 0.0).astype(bf16)
    hi = x.astype(bf16)
    lo = (x - hi.astype(f32)).astype(bf16)
    out = []
    for c in range(x.shape[1] // WKV_LANES):
        sl = slice(c * WKV_LANES, (c + 1) * WKV_LANES)
        out.append(jnp.dot(hi[:, sl], ones_blk, preferred_element_type=f32)
                   + jnp.dot(lo[:, sl], ones_blk, preferred_element_type=f32))
    return jnp.concatenate(out, axis=1)


def _lora_kernel(h_ref, halo_ref, first_ref, gn_ref, mu_ref, wd1_ref, wa1_ref, wg1_ref, wd2_ref, wa2_ref, wg2_ref,
                 w0_ref, a0_ref, decay_ref, a_ref, g_ref, *, seq_len):
    norm = lambda rows: _rms(rows, gn_ref[...])
    hn = norm(h_ref[...])
    xx = _prev_rows(hn, pl.program_id(0), halo_ref, first_ref, seq_len, norm) - hn
    mix = lambda r: (hn + xx * mu_ref[r:r + 1, :]).astype(bf16)
    dot = lambda x, w_ref: jnp.dot(x, w_ref[...], preferred_element_type=f32)
    w_raw = w0_ref[...] + dot(jnp.tanh(dot(mix(0), wd1_ref)).astype(bf16), wd2_ref)
    softplus = jnp.maximum(-w_raw, 0.0) + jnp.log(1.0 + jnp.exp(-jnp.abs(w_raw)))
    decay_ref[...] = jnp.exp(-jnp.exp(-softplus - 0.5))
    a_ref[...] = _sigmoid(a0_ref[...] + dot(dot(mix(1), wa1_ref).astype(bf16), wa2_ref))
    g_ref[...] = dot(_sigmoid(dot(mix(2), wg1_ref)).astype(bf16), wg2_ref)


def rwkv_lora(h, x_prev, seq_len, P):
    m, d = h.shape
    rw = P['w0'].shape[0]
    tm = min(ROW_TILE, m)
    pad128 = lambda w: _pad_cols(w, -(-w.shape[1] // LANES) * LANES).astype(bf16)
    w1s = [pad128(P[k]) for k in ('w_decay1', 'w_aaa1', 'w_gate1')]
    w2s = [_pad_to_rows(P[k], w1.shape[1]).astype(bf16) for k, w1 in zip(('w_decay2', 'w_aaa2', 'w_gate2'), w1s)]
    full = lambda x: pl.BlockSpec(x.shape, lambda i: (0, 0))
    mu = _pad_to_rows(P['mu_wag'], 8)
    gn = P['g_mix_pre'].reshape(1, d)
    vecs = [P['w0'].reshape(1, rw), P['a0'].reshape(1, rw)]
    out = jax.ShapeDtypeStruct((m, rw), f32)
    ospec = pl.BlockSpec((tm, rw), lambda i: (i, 0))
    return pl.pallas_call(
        functools.partial(_lora_kernel, seq_len=seq_len),
        grid=(m // tm,),
        in_specs=list(_shift_specs(m, d, tm, seq_len, x_prev.shape[0])) + [full(gn), full(mu)]
                 + [full(w) for w in w1s + w2s + vecs],
        out_specs=[ospec] * 3,
        out_shape=[out] * 3,
        compiler_params=_params("parallel"),
        name="rwkv_lora",
    )(h, h, _pad_first(x_prev, seq_len, tm), gn, mu, *w1s, *w2s, *vecs)


RKV_GROUPS = 3
IN_COL_TILE = 512


def _rwkv_in_kernel(h_ref, halo_ref, first_ref, g_ref, w_ref, mu_ref, a_ref, kkw_ref, kaw_ref,
                    r_ref, k_ref, kk_ref, b_ref, v_ref, mq_ref, xn_scr, edge_scr, *, seq_len, per_group):
    i, j = pl.program_id(0), pl.program_id(1)
    tm, c = r_ref.shape

    @pl.when(j == 0)
    def _():
        xn_scr[...] = _rms(h_ref[...], g_ref[...]).astype(bf16)
        edge_scr[0:8, :] = _rms(halo_ref[...], g_ref[...])
        edge_scr[8:, :] = first_ref[...]

    cur = jnp.dot(xn_scr[...], w_ref[...], preferred_element_type=f32)
    edge = jnp.dot(edge_scr[...].astype(bf16), w_ref[...], preferred_element_type=f32)
    halo, first = edge[0:8], edge[8:]
    prev = pltpu.roll(cur, 1, 0)
    row = lax.broadcasted_iota(jnp.int32, (tm, 1), 0)
    if seq_len >= tm:
        tiles_per_seq = seq_len // tm
        seq = lax.broadcasted_iota(jnp.int32, (first.shape[0], 1), 0) == i // tiles_per_seq
        first_row = jnp.sum(jnp.where(seq, first, 0.0), axis=0, keepdims=True)
        edge_row = jnp.where(i % tiles_per_seq == 0, first_row, halo[7:8])
        prev = jnp.where(row == 0, edge_row, prev)
    else:
        pieces = []
        for q in range(tm // seq_len):
            pieces.append(jnp.broadcast_to(first[q:q + 1], (8, c)))
            if seq_len > 8:
                pieces.append(jnp.zeros((seq_len - 8, c), f32))
        prev = jnp.where(jnp.bitwise_and(row, seq_len - 1) == 0, jnp.concatenate(pieces, axis=0), prev)
    x = cur + (prev - cur) * mu_ref[...]

    @pl.when(j < per_group)
    def _():
        r_ref[...] = x

    @pl.when((j >= per_group) & (j < 2 * per_group))
    def _():
        a = a_ref[...]
        kk = x * kkw_ref[...]
        kk = kk * lax.rsqrt(_head_sums(kk * kk) + 1e-12)
        kk_ref[...] = kk
        b_ref[...] = kk * a
        k_ref[...] = x * (1.0 + (a - 1.0) * kaw_ref[...])

    @pl.when((j >= 2 * per_group) & (j < 3 * per_group))
    def _():
        v_ref[...] = x

    @pl.when(j >= 3 * per_group)
    def _():
        mq_ref[...] = cur


def rwkv_in(h, x_prev, a_rate, seq_len, P):
    m, d = h.shape
    rw = a_rate.shape[1]
    c = IN_COL_TILE
    tm = min(ROW_TILE, m)
    per_group = rw // c
    n_col = RKV_GROUPS * per_group + MEM_WIDTH // c
    assert rw % c == 0 and MEM_WIDTH == c and P['w_in_a'].shape[1] == n_col * c
    tile, halo, first = _shift_specs(m, d, tm, seq_len, x_prev.shape[0])
    n_first = first.block_shape[0]
    mu = jnp.pad(P['mu_rkv'], (0, MEM_WIDTH)).reshape(1, -1)
    grp = lambda base: (lambda i, j: (i, jnp.clip(j - base * per_group, 0, per_group - 1)))
    vec = lambda base: (lambda i, j: (0, jnp.clip(j - base * per_group, 0, per_group - 1)))
    wide = jax.ShapeDtypeStruct((m, rw), f32)
    return pl.pallas_call(
        functools.partial(_rwkv_in_kernel, seq_len=seq_len, per_group=per_group),
        grid=(m // tm, n_col),
        in_specs=[pl.BlockSpec((tm, d), lambda i, j: (i, 0)),
                  pl.BlockSpec((8, d), lambda i, j: (jnp.maximum(i * (tm // 8) - 1, 0), 0)),
                  pl.BlockSpec(first.block_shape, lambda i, j: (i if seq_len < tm else 0, 0)),
                  pl.BlockSpec((1, d), lambda i, j: (0, 0)),
                  pl.BlockSpec((d, c), lambda i, j: (0, j)),
                  pl.BlockSpec((1, c), lambda i, j: (0, j)),
                  pl.BlockSpec((tm, c), grp(1)),
                  pl.BlockSpec((1, c), vec(1)),
                  pl.BlockSpec((1, c), vec(1))],
        out_specs=[pl.BlockSpec((tm, c), grp(0)), pl.BlockSpec((tm, c), grp(1)), pl.BlockSpec((tm, c), grp(1)),
                   pl.BlockSpec((tm, c), grp(1)), pl.BlockSpec((tm, c), grp(2)),
                   pl.BlockSpec((tm, c), lambda i, j: (i, 0))],
        out_shape=[wide] * 5 + [jax.ShapeDtypeStruct((m, MEM_WIDTH), f32)],
        scratch_shapes=[pltpu.VMEM((tm, d), bf16), pltpu.VMEM((8 + n_first, d), f32)],
        compiler_params=_params("parallel", "arbitrary"),
        name="rwkv_in",
    )(h, h, _pad_first(x_prev, seq_len, tm), P['g_mix_pre'].reshape(1, d), P['w_in_a'].astype(bf16), mu, a_rate,
      P['k_k'].reshape(1, rw), P['k_a'].reshape(1, rw))


def _rwkv_out_kernel(y_ref, r_ref, k_ref, v_ref, g_ref, om_ref, h_ref, lw_ref, lb_ref, rk_ref, w_ref, gp_ref, o_ref):
    inv_n = 1.0 / RWKV_HEAD_DIM
    y = y_ref[...]
    d = y - _head_sums(y) * inv_n
    var = _head_sums(d * d) * inv_n
    yn = d * lax.rsqrt(var + GN_EPS) * lw_ref[...] + lb_ref[...]
    bonus = _head_sums(r_ref[...] * k_ref[...] * rk_ref[...]) * v_ref[...]
    o = ((yn + bonus) * g_ref[...]).astype(bf16)
    rw = o.shape[1]
    acc = jnp.dot(o, w_ref[:rw, :], preferred_element_type=f32)
    acc += jnp.dot(om_ref[...].astype(bf16), w_ref[rw:, :], preferred_element_type=f32)
    o_ref[...] = h_ref[...] + _rms(acc, gp_ref[...])


def rwkv_out(y, r, k, v, gate, o_mem, h, P):
    m, rw = y.shape
    d = h.shape[1]
    tm = min(ROW_TILE // 2, m)
    wide = pl.BlockSpec((tm, rw), lambda i: (i, 0))
    vec = pl.BlockSpec((1, rw), lambda i: (0, 0))
    return pl.pallas_call(
        _rwkv_out_kernel,
        grid=(m // tm,),
        in_specs=[wide] * 5 + [pl.BlockSpec((tm, o_mem.shape[1]), lambda i: (i, 0)),
                               pl.BlockSpec((tm, d), lambda i: (i, 0)), vec, vec, vec,
                               pl.BlockSpec((rw + o_mem.shape[1], d), lambda i: (0, 0)),
                               pl.BlockSpec((1, d), lambda i: (0, 0))],
        out_specs=pl.BlockSpec((tm, d), lambda i: (i, 0)),
        out_shape=jax.ShapeDtypeStruct((m, d), f32),
        compiler_params=_params("parallel"),
        name="rwkv_out",
    )(y, r, k, v, gate, o_mem, h, P['lnx_w'].reshape(1, rw), P['lnx_b'].reshape(1, rw), P['r_k'].reshape(1, rw),
      P['w_out_a'].astype(bf16), P['g_mix_post'].reshape(1, d))


def _rope_tables(pos):
    half = HEAD_DIM // 2
    inv = jnp.power(ROPE_THETA, -jnp.arange(half, dtype=f32) / half)
    ang = pos.astype(f32)[:, None] * inv[None, :]
    cos, sin = jnp.cos(ang), jnp.sin(ang)
    return jnp.concatenate([cos, cos], axis=-1), jnp.concatenate([-sin, sin], axis=-1)


def _pad_cols(w, n):
    return jnp.pad(w, ((0, 0), (0, n - w.shape[1])))


def _pad_to_rows(x, n):
    return jnp.pad(x, ((0, n - x.shape[0]), (0, 0)))


def rwkv_mem_layer(h, x_prev, s0, mem, P):
    b, t, d = h.shape
    m = b * t
    assert t & (t - 1) == 0 and t % 8 == 0 and (t % ROW_TILE == 0 or ROW_TILE % t == 0)
    h2 = h.reshape(m, d)
    rw = P['w0'].shape[0]
    last = rmsnorm(_pad_to_rows(h[:, -1], -(-b // 8) * 8), P['g_mix_pre'])[:b]

    decay, a_rate, gate = rwkv_lora(h2, x_prev, t, P)
    r, k, kk, kb, v, mq = rwkv_in(h2, x_prev, a_rate, t, P)
    as3 = lambda x: x.reshape(b, t, -1)
    y, s_t = wkv_scan(as3(r), as3(decay), as3(k), as3(kk), as3(kb), as3(v), s0)
    o_mem = mem(as3(mq), 0)
    h2 = rwkv_out(y.reshape(m, rw), r, k, v, gate, o_mem.reshape(m, MEM_WIDTH), h2, P)
    h2 = ffn_residual(h2, P['g_ffn_pre'], P['w_ff1'], P['w_ff2'], P['g_ffn_post'], P['layer'])
    return h2.reshape(b, t, d), s_t, last


GATE_COL = (NSA_WIDTH + MEM_WIDTH) // LANES


def nsa_mem_layer(h, mem, P, attend):
    b, t, d = h.shape
    m = b * t
    h2 = h.reshape(m, d)
    n_in = (GATE_COL + 1) * LANES
    n_in = -(-n_in // 768) * 768
    proj = matmul(h2, _pad_cols(P['w_in_b'].astype(bf16), n_in), g=P['g_mix_pre'], tn=768).reshape(b, t, -1)
    o_nsa = attend(proj)
    o_mem = mem(proj, NSA_WIDTH // MEM_WIDTH)
    h2 = out_proj_residual(o_nsa.reshape(m, NSA_WIDTH), o_mem.reshape(m, MEM_WIDTH), 0, P['w_out_b'], h2,
                           P['g_mix_post'])
    h2 = ffn_residual(h2, P['g_ffn_pre'], P['w_ff1'], P['w_ff2'], P['g_ffn_post'], P['layer'])
    return h2.reshape(b, t, d)


def kernel(x_prompt, x_sample, mem_prompt, cache_mem_k, cache_mem_v, state_wkv, state_shift, cache_cmp_k, cache_cmp_v, cache_slc_k, cache_slc_v, cache_win_k, cache_win_v, page_table, g_mix_pre, g_mix_post, g_ffn_pre, g_ffn_post, g_mem, w_mem_k, w_mem_v, w_in_a, mu_rkv, mu_wag, w0, w_decay1, w_decay2, a0, w_aaa1, w_aaa2, w_gate1, w_gate2, k_k, k_a, r_k, lnx_w, lnx_b, w_out_a, g_kv, w_kv, cmp_pos, cmp_w1, cmp_w2, w_in_b, w_out_b, w_ff1, w_ff2):
    bp, tp, d = x_prompt.shape
    bs, ts, _ = x_sample.shape
    depth = g_mix_pre.shape[0]
    assert depth == 2 and w_in_a.shape[0] == 1 and w_in_b.shape[0] == 1
    n_pages = page_table.shape[1]
    past = n_pages * PAGE_SIZE
    mem_len = mem_prompt.shape[1]

    P0 = dict(g_mix_pre=g_mix_pre[0], g_mix_post=g_mix_post[0], g_ffn_pre=g_ffn_pre[0], g_ffn_post=g_ffn_post[0],
              w_in_a=w_in_a[0], mu_rkv=mu_rkv[0], mu_wag=mu_wag[0], w0=w0[0], w_decay1=w_decay1[0],
              w_decay2=w_decay2[0], a0=a0[0], w_aaa1=w_aaa1[0], w_aaa2=w_aaa2[0], w_gate1=w_gate1[0],
              w_gate2=w_gate2[0], k_k=k_k[0], k_a=k_a[0], r_k=r_k[0], lnx_w=lnx_w[0], lnx_b=lnx_b[0],
              w_out_a=w_out_a[0], w_ff1=w_ff1.astype(bf16), w_ff2=w_ff2.astype(bf16), layer=0)
    P1 = dict(g_mix_pre=g_mix_pre[1], g_mix_post=g_mix_post[1], g_ffn_pre=g_ffn_pre[1], g_ffn_post=g_ffn_post[1],
              w_in_b=w_in_b[0], w_out_b=w_out_b[0], w_ff1=P0['w_ff1'], w_ff2=P0['w_ff2'], layer=1)
    rows4 = lambda x, bsz: x.reshape(bsz, -1, NSA_KV, HEAD_DIM)

    mem2 = mem_prompt.reshape(bp * mem_len, d)
    mkv = [matmul(mem2, jnp.concatenate([w_mem_k[l], w_mem_v[l]], axis=1), g=g_mem[l]).reshape(bp, mem_len, -1)
           for l in range(depth)]
    mem_k_p = jnp.stack([x[..., :MEM_WIDTH] for x in mkv])
    mem_v_p = jnp.stack([x[..., MEM_WIDTH:] for x in mkv])
    mem_p = lambda l: (lambda q, q_col: mem_attention(q, q_col, mkv[l], mkv[l], k_col=0, v_col=1))

    nh = w0.shape[1] // RWKV_HEAD_DIM
    shift0 = jnp.zeros((bp, d), f32)
    wkv0 = jnp.zeros((bp, nh, RWKV_HEAD_DIM, RWKV_HEAD_DIM), f32)
    h, wkv_p, shift_p = rwkv_mem_layer(x_prompt, shift0, wkv0, mem_p(0), P0)

    c2p, s2p = _rope_tables(jnp.arange(tp, dtype=jnp.int32))
    rows_p, kv_bf = kv_proj(h.reshape(bp * tp, d), g_kv, w_kv, c2p, s2p)
    as_p = lambda x: x.reshape(bp, tp, -1)
    ckc, cvc = compress_prompt(as_p(rows_p[0]), as_p(rows_p[1]), cmp_pos, cmp_w1, cmp_w2)
    n_cmp_p = (tp - CMP_BLOCK) // CMP_STRIDE + 1

    def attend_prompt(proj):
        return nsa_prompt(proj, GATE_COL, ckc, cvc, as_p(kv_bf), c2p, s2p, n_cmp_p)

    y_p = nsa_mem_layer(h, mem_p(1), P1, attend_prompt)
    cmp_k_p, cmp_v_p, slc_k_p, slc_v_p, win_k_p, win_v_p = [rows4(x, bp) for x in rows_p]
    n_keep = min(WINDOW, tp)
    win_k_p, win_v_p = win_k_p[:, tp - n_keep:], win_v_p[:, tp - n_keep:]

    mk_s, mv_s = cache_mem_k.reshape(-1, HEAD_DIM), cache_mem_v.reshape(-1, HEAD_DIM)
    mem_s = lambda l: (lambda q, q_col: mem_attention(q, q_col, mk_s, mv_s, cached=(l, mem_len)))
    h, wkv_s, shift_s = rwkv_mem_layer(x_sample, state_shift[0], state_wkv[0], mem_s(0), P0)

    c2s, s2s = _rope_tables(past + jnp.arange(ts, dtype=jnp.int32))
    rows_s, _ = kv_proj(h.reshape(bs * ts, d), g_kv, w_kv, jnp.tile(c2s, (bs, 1)), jnp.tile(s2s, (bs, 1)))
    as_s = lambda x: x.reshape(bs, ts, -1)
    n_cmp_s = (past + ts - CMP_BLOCK) // CMP_STRIDE + 1
    assert (n_cmp_s - 1) * CMP_STRIDE + CMP_BLOCK <= past
    n_pool = cache_cmp_k.shape[0]
    pool = lambda x: x.reshape(n_pool * PAGE_ROWS, HEAD_DIM)
    ckc_s, cvc_s = compress_paged(pool(cache_cmp_k), pool(cache_cmp_v), page_table, n_cmp_s, cmp_pos, cmp_w1, cmp_w2)
    win_k2, win_v2 = cache_win_k.reshape(-1, HEAD_DIM), cache_win_v.reshape(-1, HEAD_DIM)

    new_win = []

    def attend_sample(proj):
        o, wk_out, wv_out = nsa_sample(proj, GATE_COL, ckc_s, cvc_s, pool(cache_slc_k), pool(cache_slc_v), page_table,
                                       [as_s(x) for x in rows_s[2:]], win_k2, win_v2, c2s, s2s, n_cmp_s, past)
        new_win.extend([wk_out, wv_out])
        return o

    y_s = nsa_mem_layer(h, mem_s(1), P1, attend_sample)
    cmp_k_s, cmp_v_s, slc_k_s, slc_v_s = [rows4(x, bs) for x in rows_s[:4]]
    win_k_s, win_v_s = [x.reshape(cache_win_k.shape) for x in new_win]

    return (y_p, y_s, mem_k_p.reshape(depth, bp, mem_len, MEM_HEADS, HEAD_DIM),
            mem_v_p.reshape(depth, bp, mem_len, MEM_HEADS, HEAD_DIM),
            wkv_p[None], shift_p[None], cmp_k_p, cmp_v_p, slc_k_p, slc_v_p, win_k_p, win_v_p,
            wkv_s[None], shift_s[None], cmp_k_s, cmp_v_s, slc_k_s, slc_v_s, win_k_s, win_v_s)
```
